```python
import math
import jax, jax.numpy as jnp
from jax import lax
import numpy as np

D_MODEL = 1024
BATCH = 8
SEQ = 8192
DEPTH = 2

N_META = 16
BLOCK = 128
EPS = 1e-6

SSD_HEADS = 8
SSD_HEAD_DIM = 64
SSD_WIDTH = SSD_HEADS * SSD_HEAD_DIM
SSD_GROUPS = 2
SSD_HEADS_PER_GROUP = SSD_HEADS // SSD_GROUPS
SSD_STATE = 128
SSD_CONV = 4
SSD_XBC = SSD_WIDTH + 2 * SSD_GROUPS * SSD_STATE

SB_HEADS = 4
SB_HEAD_DIM = 64
SB_WIDTH = SB_HEADS * SB_HEAD_DIM

MLA_HEADS = 4
MLA_NOPE = 64
MLA_ROPE = 32
MLA_V = 64
MLA_WIDTH = MLA_HEADS * MLA_V
MLA_Q_RANK = 192
MLA_KV_RANK = 128
ROPE_BASE = 10000.0

MIX_WIDTH = SSD_WIDTH + SB_WIDTH + MLA_WIDTH
IN_SIZES = (SSD_WIDTH, SSD_XBC, SSD_HEADS, SB_WIDTH, SB_WIDTH, SB_WIDTH, MLA_Q_RANK, MLA_KV_RANK, MLA_ROPE)
IN_COLS = 2664

D_FF = 2816
FFN_CONV = 3

kernel_name = "hymba_ssd_stickbreak_mla_convffn"


def rmsnorm(x, g):
    xf = x.astype(jnp.float32)
    xf = xf * lax.rsqrt(jnp.mean(xf * xf, axis=-1, keepdims=True) + EPS)
    return xf.astype(x.dtype) * g


def causal_dwconv(u, w):
    k_w, c = w.shape
    return lax.conv_general_dilated(
        u, w.reshape(k_w, 1, c).astype(u.dtype), window_strides=(1,), padding=[(k_w - 1, 0)],
        dimension_numbers=("NWC", "WIO", "NWC"), feature_group_count=c)


def rope_tables(length, dtype):
    pos = jnp.arange(length, dtype=jnp.float32)
    inv = 1.0 / (ROPE_BASE ** (jnp.arange(0, MLA_ROPE, 2, dtype=jnp.float32) / MLA_ROPE))
    ang = pos[:, None] * inv[None, :]
    ang = jnp.concatenate([ang, ang], axis=-1)
    return jnp.cos(ang).astype(dtype), jnp.sin(ang).astype(dtype)


def apply_rope(x, cos, sin):
    half = MLA_ROPE // 2
    rot = jnp.concatenate([-x[..., half:], x[..., :half]], axis=-1)
    return x * cos[None, :, None, :] + rot * sin[None, :, None, :]


def sweep_query_blocks(q, block_fn):
    bsz, length = q.shape[0], q.shape[1]
    pad = BLOCK - N_META
    qp = jnp.pad(q, ((0, 0), (pad, 0), (0, 0), (0, 0)))
    nb = qp.shape[1] // BLOCK
    qb = jnp.moveaxis(qp.reshape(bsz, nb, BLOCK, q.shape[2], q.shape[3]), 1, 0)
    starts = jnp.arange(nb, dtype=jnp.int32) * BLOCK - pad
    offs = jnp.arange(BLOCK, dtype=jnp.int32)
    out = lax.map(lambda a: block_fn(a[0], a[1] + offs), (qb, starts))
    out = jnp.moveaxis(out, 0, 1).reshape(bsz, nb * BLOCK, -1)
    return out[:, pad:]


def ssd_mixer(z, xbc, dt_raw, conv_w, conv_b, dt_bias, a_log, d_skip, norm_g):
    f32 = jnp.float32
    bsz, length, _ = z.shape
    G, R, P, N = SSD_GROUPS, SSD_HEADS_PER_GROUP, SSD_HEAD_DIM, SSD_STATE
    gn = G * N
    xbc = jax.nn.silu(causal_dwconv(xbc, conv_w) + conv_b)
    xs = xbc[..., :SSD_WIDTH].reshape(bsz, length, SSD_HEADS, P).astype(f32)
    b_in = xbc[..., SSD_WIDTH:SSD_WIDTH + gn].reshape(bsz, length, G, N).astype(f32)
    c_in = xbc[..., SSD_WIDTH + gn:].reshape(bsz, length, G, N).astype(f32)
    dt = jax.nn.softplus((dt_raw + dt_bias).astype(f32))
    a = -jnp.exp(a_log.astype(f32))

    pad = BLOCK - N_META
    nc = (length + pad) // BLOCK

    def chunked(t):
        t = jnp.pad(t, ((0, 0), (pad, 0)) + ((0, 0),) * (t.ndim - 2))
        return t.reshape((bsz, nc, BLOCK) + t.shape[2:])

    X = chunked(xs * dt[..., None]).reshape(bsz, nc, BLOCK, G, R, P)
    dA = chunked(dt * a).reshape(bsz, nc, BLOCK, G, R)
    Bc = chunked(b_in)
    Cc = chunked(c_in)
    a_cs = jnp.cumsum(dA, axis=2)

    causal = jnp.tril(jnp.ones((BLOCK, BLOCK), dtype=bool))
    seg = a_cs[:, :, :, None] - a_cs[:, :, None, :]
    decay_ls = jnp.exp(jnp.where(causal[:, :, None, None], seg, -jnp.inf))
    cb = jnp.einsum('bclgn,bcsgn->bclsg', Cc, Bc)
    y_diag = jnp.einsum('bclsg,bclsgr,bcsgrp->bclgrp', cb, decay_ls, X)

    decay_to_end = jnp.exp(a_cs[:, :, -1:] - a_cs)
    states = jnp.einsum('bclgn,bclgr,bclgrp->bcgrpn', Bc, decay_to_end, X)
    chunk_decay = jnp.exp(a_cs[:, :, -1])

    def step(h, inp):
        st, dec = inp
        return h * dec[..., None, None] + st, h

    h0 = jnp.zeros((bsz, G, R, P, N), f32)
    _, h_prev = lax.scan(step, h0, (jnp.moveaxis(states, 1, 0), jnp.moveaxis(chunk_decay, 1, 0)))
    h_prev = jnp.moveaxis(h_prev, 0, 1)
    y_off = jnp.einsum('bclgn,bcgrpn,bclgr->bclgrp', Cc, h_prev, jnp.exp(a_cs))

    y = (y_diag + y_off).reshape(bsz, nc * BLOCK, SSD_WIDTH)[:, pad:]
    y = y + (xs * d_skip.astype(f32)[:, None]).reshape(bsz, length, SSD_WIDTH)
    y = y.astype(z.dtype) * jax.nn.silu(z)
    return rmsnorm(y, norm_g)


def stick_breaking_attention(q, k, v):
    length = k.shape[1]
    key_pos = jnp.arange(length, dtype=jnp.int32)
    scale = SB_HEAD_DIM ** -0.5

    def block_fn(qi, qpos):
        zs = jnp.einsum('bqhd,bkhd->bhqk', qi, k).astype(jnp.float32) * scale
        mask = key_pos[None, :] < qpos[:, None]
        u = jnp.where(mask, jax.nn.log_sigmoid(-zs), 0.0)
        after = lax.cumsum(u, axis=3, reverse=True) - u
        w = jnp.where(mask, jnp.exp(jax.nn.log_sigmoid(zs) + after), 0.0)
        return jnp.einsum('bhqk,bkhd->bqhd', w.astype(v.dtype), v)

    return sweep_query_blocks(q, block_fn)


def mla_attention(q_a, c_kv, k_r, q_norm_g, kv_norm_g, w_uq, w_ukv, cos, sin):
    bsz, length, _ = q_a.shape
    q = (rmsnorm(q_a, q_norm_g) @ w_uq).reshape(bsz, length, MLA_HEADS, MLA_NOPE + MLA_ROPE)
    q = jnp.concatenate([q[..., :MLA_NOPE], apply_rope(q[..., MLA_NOPE:], cos, sin)], axis=-1)
    kv = (rmsnorm(c_kv, kv_norm_g) @ w_ukv).reshape(bsz, length, MLA_HEADS, MLA_NOPE + MLA_V)
    k_rope = apply_rope(k_r[:, :, None, :], cos, sin)
    k = jnp.concatenate([kv[..., :MLA_NOPE],
                         jnp.broadcast_to(k_rope, (bsz, length, MLA_HEADS, MLA_ROPE))], axis=-1)
    v = kv[..., MLA_NOPE:]
    key_pos = jnp.arange(length, dtype=jnp.int32)
    scale = (MLA_NOPE + MLA_ROPE) ** -0.5

    def block_fn(qi, qpos):
        s = jnp.einsum('bqhd,bkhd->bhqk', qi, k).astype(jnp.float32) * scale
        mask = key_pos[None, :] <= qpos[:, None]
        p = jax.nn.softmax(jnp.where(mask, s, -1e30), axis=-1)
        return jnp.einsum('bhqk,bkhd->bqhd', p.astype(v.dtype), v)

    return sweep_query_blocks(q, block_fn)


def conv_ffn(h, w_up, conv_w, conv_b, w_down):
    u = causal_dwconv(h @ w_up, conv_w) + conv_b
    return (jax.nn.silu(u[..., :D_FF]) * u[..., D_FF:]) @ w_down


def _fwd_setup_inputs(seed: int = 0) -> dict:
    key = jax.random.key(seed)
    ks = jax.random.split(key, 24)
    f32 = jnp.float32
    nrm = lambda k, shape, s: jax.random.normal(k, shape, f32) * s
    gain = lambda k, shape: 1.0 + 0.05 * jax.random.normal(k, shape, f32)
    dt = jnp.exp(jax.random.uniform(ks[5], (DEPTH, SSD_HEADS), f32) * (math.log(0.1) - math.log(1e-3)) + math.log(1e-3))
    return {
        "x": nrm(ks[0], (BATCH, SEQ, D_MODEL), 1.0),
        "meta_tokens": nrm(ks[1], (N_META, D_MODEL), 1.0),
        "norm_mix_g": gain(ks[2], (DEPTH, D_MODEL)),
        "w_in": nrm(ks[3], (DEPTH, D_MODEL, IN_COLS), D_MODEL ** -0.5),
        "ssd_conv_w": nrm(ks[4], (DEPTH, SSD_CONV, SSD_XBC), SSD_CONV ** -0.5),
        "ssd_conv_b": nrm(ks[6], (DEPTH, SSD_XBC), 0.02),
        "ssd_dt_bias": dt + jnp.log(-jnp.expm1(-dt)),
        "ssd_a_log": jnp.log(jax.random.uniform(ks[7], (DEPTH, SSD_HEADS), f32, 1.0, 16.0)),
        "ssd_d": gain(ks[8], (DEPTH, SSD_HEADS)),
        "ssd_norm_g": gain(ks[9], (DEPTH, SSD_WIDTH)),
        "sb_norm_g": gain(ks[10], (DEPTH, SB_WIDTH)),
        "mla_q_norm_g": gain(ks[11], (DEPTH, MLA_Q_RANK)),
        "mla_kv_norm_g": gain(ks[12], (DEPTH, MLA_KV_RANK)),
        "mla_w_uq": nrm(ks[13], (DEPTH, MLA_Q_RANK, MLA_HEADS * (MLA_NOPE + MLA_ROPE)), MLA_Q_RANK ** -0.5),
        "mla_w_ukv": nrm(ks[14], (DEPTH, MLA_KV_RANK, MLA_HEADS * (MLA_NOPE + MLA_V)), MLA_KV_RANK ** -0.5),
        "mla_norm_g": gain(ks[15], (DEPTH, MLA_WIDTH)),
        "w_out": nrm(ks[16], (DEPTH, MIX_WIDTH, D_MODEL), MIX_WIDTH ** -0.5),
        "norm_ffn_g": gain(ks[17], (DEPTH, D_MODEL)),
        "ffn_w_up": nrm(ks[18], (DEPTH, D_MODEL, 2 * D_FF), D_MODEL ** -0.5),
        "ffn_conv_w": nrm(ks[19], (DEPTH, FFN_CONV, 2 * D_FF), FFN_CONV ** -0.5),
        "ffn_conv_b": nrm(ks[20], (DEPTH, 2 * D_FF), 0.02),
        "ffn_w_down": nrm(ks[21], (DEPTH, D_FF, D_MODEL), D_FF ** -0.5),
        "final_norm_g": gain(ks[22], (D_MODEL,)),
    }


def _fwd_reference(x, meta_tokens, norm_mix_g, w_in, ssd_conv_w, ssd_conv_b, ssd_dt_bias, ssd_a_log,
              ssd_d, ssd_norm_g, sb_norm_g, mla_q_norm_g, mla_kv_norm_g, mla_w_uq, mla_w_ukv,
              mla_norm_g, w_out, norm_ffn_g, ffn_w_up, ffn_conv_w, ffn_conv_b, ffn_w_down,
              final_norm_g):
    bsz = x.shape[0]
    meta = jnp.broadcast_to(meta_tokens[None].astype(x.dtype), (bsz, N_META, x.shape[2]))
    h = jnp.concatenate([meta, x], axis=1)
    length = h.shape[1]
    cos, sin = rope_tables(length, x.dtype)
    cuts = [int(c) for c in np.cumsum(IN_SIZES)[:-1]]

    for l in range(DEPTH):
        u = rmsnorm(h, norm_mix_g[l]) @ w_in[l]
        z, xbc, dt_raw, q_sb, k_sb, v_sb, q_a, c_kv, k_r = jnp.split(u, cuts, axis=-1)

        y_ssd = ssd_mixer(z, xbc, dt_raw, ssd_conv_w[l], ssd_conv_b[l], ssd_dt_bias[l],
                          ssd_a_log[l], ssd_d[l], ssd_norm_g[l])
        heads = lambda t: t.reshape(bsz, length, SB_HEADS, SB_HEAD_DIM)
        y_sb = rmsnorm(stick_breaking_attention(heads(q_sb), heads(k_sb), heads(v_sb)), sb_norm_g[l])
        y_mla = rmsnorm(mla_attention(q_a, c_kv, k_r, mla_q_norm_g[l], mla_kv_norm_g[l],
                                      mla_w_uq[l], mla_w_ukv[l], cos, sin), mla_norm_g[l])

        h = h + jnp.concatenate([y_ssd, y_sb, y_mla], axis=-1) @ w_out[l]
        h = h + conv_ffn(rmsnorm(h, norm_ffn_g[l]), ffn_w_up[l], ffn_conv_w[l], ffn_conv_b[l],
                         ffn_w_down[l])

    return rmsnorm(h, final_norm_g)[:, N_META:]


import jax as _jax
import jax.numpy as _jnp

TWIN_FORMAT = 'train_step'
FWD_PARAMS = ['x', 'meta_tokens', 'norm_mix_g', 'w_in', 'ssd_conv_w', 'ssd_conv_b', 'ssd_dt_bias', 'ssd_a_log', 'ssd_d', 'ssd_norm_g', 'sb_norm_g', 'mla_q_norm_g', 'mla_kv_norm_g', 'mla_w_uq', 'mla_w_ukv', 'mla_norm_g', 'w_out', 'norm_ffn_g', 'ffn_w_up', 'ffn_conv_w', 'ffn_conv_b', 'ffn_w_down', 'final_norm_g']
TWIN_WEIGHTS = ['meta_tokens', 'norm_mix_g', 'w_in', 'ssd_conv_w', 'ssd_conv_b', 'ssd_dt_bias', 'ssd_a_log', 'ssd_d', 'ssd_norm_g', 'sb_norm_g', 'mla_q_norm_g', 'mla_kv_norm_g', 'mla_w_uq', 'mla_w_ukv', 'mla_norm_g', 'w_out', 'norm_ffn_g', 'ffn_w_up', 'ffn_conv_w', 'ffn_conv_b', 'ffn_w_down', 'final_norm_g']
TWIN_DIFF_INPUT = 'x'
TWIN_INPUTS = ['x', 'meta_tokens', 'norm_mix_g', 'w_in', 'ssd_conv_w', 'ssd_conv_b', 'ssd_dt_bias', 'ssd_a_log', 'ssd_d', 'ssd_norm_g', 'sb_norm_g', 'mla_q_norm_g', 'mla_kv_norm_g', 'mla_w_uq', 'mla_w_ukv', 'mla_norm_g', 'w_out', 'norm_ffn_g', 'ffn_w_up', 'ffn_conv_w', 'ffn_conv_b', 'ffn_w_down', 'final_norm_g', 'loss_target', 'm_meta_tokens', 'm_norm_mix_g', 'm_w_in', 'm_ssd_conv_w', 'm_ssd_conv_b', 'm_ssd_dt_bias', 'm_ssd_a_log', 'm_ssd_d', 'm_ssd_norm_g', 'm_sb_norm_g', 'm_mla_q_norm_g', 'm_mla_kv_norm_g', 'm_mla_w_uq', 'm_mla_w_ukv', 'm_mla_norm_g', 'm_w_out', 'm_norm_ffn_g', 'm_ffn_w_up', 'm_ffn_conv_w', 'm_ffn_conv_b', 'm_ffn_w_down', 'm_final_norm_g', 'v_meta_tokens', 'v_norm_mix_g', 'v_w_in', 'v_ssd_conv_w', 'v_ssd_conv_b', 'v_ssd_dt_bias', 'v_ssd_a_log', 'v_ssd_d', 'v_ssd_norm_g', 'v_sb_norm_g', 'v_mla_q_norm_g', 'v_mla_kv_norm_g', 'v_mla_w_uq', 'v_mla_w_ukv', 'v_mla_norm_g', 'v_w_out', 'v_norm_ffn_g', 'v_ffn_w_up', 'v_ffn_conv_w', 'v_ffn_conv_b', 'v_ffn_w_down', 'v_final_norm_g']
TWIN_OUTPUTS = ['loss', 'grad_x', 'grad_meta_tokens', 'grad_norm_mix_g', 'grad_w_in', 'grad_ssd_conv_w', 'grad_ssd_conv_b', 'grad_ssd_dt_bias', 'grad_ssd_a_log', 'grad_ssd_d', 'grad_ssd_norm_g', 'grad_sb_norm_g', 'grad_mla_q_norm_g', 'grad_mla_kv_norm_g', 'grad_mla_w_uq', 'grad_mla_w_ukv', 'grad_mla_norm_g', 'grad_w_out', 'grad_norm_ffn_g', 'grad_ffn_w_up', 'grad_ffn_conv_w', 'grad_ffn_conv_b', 'grad_ffn_w_down', 'grad_final_norm_g', 'delta_meta_tokens', 'delta_norm_mix_g', 'delta_w_in', 'delta_ssd_conv_w', 'delta_ssd_conv_b', 'delta_ssd_dt_bias', 'delta_ssd_a_log', 'delta_ssd_d', 'delta_ssd_norm_g', 'delta_sb_norm_g', 'delta_mla_q_norm_g', 'delta_mla_kv_norm_g', 'delta_mla_w_uq', 'delta_mla_w_ukv', 'delta_mla_norm_g', 'delta_w_out', 'delta_norm_ffn_g', 'delta_ffn_w_up', 'delta_ffn_conv_w', 'delta_ffn_conv_b', 'delta_ffn_w_down', 'delta_final_norm_g', 'new_m_meta_tokens', 'new_m_norm_mix_g', 'new_m_w_in', 'new_m_ssd_conv_w', 'new_m_ssd_conv_b', 'new_m_ssd_dt_bias', 'new_m_ssd_a_log', 'new_m_ssd_d', 'new_m_ssd_norm_g', 'new_m_sb_norm_g', 'new_m_mla_q_norm_g', 'new_m_mla_kv_norm_g', 'new_m_mla_w_uq', 'new_m_mla_w_ukv', 'new_m_mla_norm_g', 'new_m_w_out', 'new_m_norm_ffn_g', 'new_m_ffn_w_up', 'new_m_ffn_conv_w', 'new_m_ffn_conv_b', 'new_m_ffn_w_down', 'new_m_final_norm_g', 'new_v_meta_tokens', 'new_v_norm_mix_g', 'new_v_w_in', 'new_v_ssd_conv_w', 'new_v_ssd_conv_b', 'new_v_ssd_dt_bias', 'new_v_ssd_a_log', 'new_v_ssd_d', 'new_v_ssd_norm_g', 'new_v_sb_norm_g', 'new_v_mla_q_norm_g', 'new_v_mla_kv_norm_g', 'new_v_mla_w_uq', 'new_v_mla_w_ukv', 'new_v_mla_norm_g', 'new_v_w_out', 'new_v_norm_ffn_g', 'new_v_ffn_w_up', 'new_v_ffn_conv_w', 'new_v_ffn_conv_b', 'new_v_ffn_w_down', 'new_v_final_norm_g']
TWIN_LEAF_KINDS = {'loss': 'loss', 'grad_x': 'grad_x', 'grad_meta_tokens': 'grad_w', 'grad_norm_mix_g': 'grad_w', 'grad_w_in': 'grad_w', 'grad_ssd_conv_w': 'grad_w', 'grad_ssd_conv_b': 'grad_w', 'grad_ssd_dt_bias': 'grad_w', 'grad_ssd_a_log': 'grad_w', 'grad_ssd_d': 'grad_w', 'grad_ssd_norm_g': 'grad_w', 'grad_sb_norm_g': 'grad_w', 'grad_mla_q_norm_g': 'grad_w', 'grad_mla_kv_norm_g': 'grad_w', 'grad_mla_w_uq': 'grad_w', 'grad_mla_w_ukv': 'grad_w', 'grad_mla_norm_g': 'grad_w', 'grad_w_out': 'grad_w', 'grad_norm_ffn_g': 'grad_w', 'grad_ffn_w_up': 'grad_w', 'grad_ffn_conv_w': 'grad_w', 'grad_ffn_conv_b': 'grad_w', 'grad_ffn_w_down': 'grad_w', 'grad_final_norm_g': 'grad_w', 'delta_meta_tokens': 'delta_w', 'delta_norm_mix_g': 'delta_w', 'delta_w_in': 'delta_w', 'delta_ssd_conv_w': 'delta_w', 'delta_ssd_conv_b': 'delta_w', 'delta_ssd_dt_bias': 'delta_w', 'delta_ssd_a_log': 'delta_w', 'delta_ssd_d': 'delta_w', 'delta_ssd_norm_g': 'delta_w', 'delta_sb_norm_g': 'delta_w', 'delta_mla_q_norm_g': 'delta_w', 'delta_mla_kv_norm_g': 'delta_w', 'delta_mla_w_uq': 'delta_w', 'delta_mla_w_ukv': 'delta_w', 'delta_mla_norm_g': 'delta_w', 'delta_w_out': 'delta_w', 'delta_norm_ffn_g': 'delta_w', 'delta_ffn_w_up': 'delta_w', 'delta_ffn_conv_w': 'delta_w', 'delta_ffn_conv_b': 'delta_w', 'delta_ffn_w_down': 'delta_w', 'delta_final_norm_g': 'delta_w', 'new_m_meta_tokens': 'new_m', 'new_m_norm_mix_g': 'new_m', 'new_m_w_in': 'new_m', 'new_m_ssd_conv_w': 'new_m', 'new_m_ssd_conv_b': 'new_m', 'new_m_ssd_dt_bias': 'new_m', 'new_m_ssd_a_log': 'new_m', 'new_m_ssd_d': 'new_m', 'new_m_ssd_norm_g': 'new_m', 'new_m_sb_norm_g': 'new_m', 'new_m_mla_q_norm_g': 'new_m', 'new_m_mla_kv_norm_g': 'new_m', 'new_m_mla_w_uq': 'new_m', 'new_m_mla_w_ukv': 'new_m', 'new_m_mla_norm_g': 'new_m', 'new_m_w_out': 'new_m', 'new_m_norm_ffn_g': 'new_m', 'new_m_ffn_w_up': 'new_m', 'new_m_ffn_conv_w': 'new_m', 'new_m_ffn_conv_b': 'new_m', 'new_m_ffn_w_down': 'new_m', 'new_m_final_norm_g': 'new_m', 'new_v_meta_tokens': 'new_v', 'new_v_norm_mix_g': 'new_v', 'new_v_w_in': 'new_v', 'new_v_ssd_conv_w': 'new_v', 'new_v_ssd_conv_b': 'new_v', 'new_v_ssd_dt_bias': 'new_v', 'new_v_ssd_a_log': 'new_v', 'new_v_ssd_d': 'new_v', 'new_v_ssd_norm_g': 'new_v', 'new_v_sb_norm_g': 'new_v', 'new_v_mla_q_norm_g': 'new_v', 'new_v_mla_kv_norm_g': 'new_v', 'new_v_mla_w_uq': 'new_v', 'new_v_mla_w_ukv': 'new_v', 'new_v_mla_norm_g': 'new_v', 'new_v_w_out': 'new_v', 'new_v_norm_ffn_g': 'new_v', 'new_v_ffn_w_up': 'new_v', 'new_v_ffn_conv_w': 'new_v', 'new_v_ffn_conv_b': 'new_v', 'new_v_ffn_w_down': 'new_v', 'new_v_final_norm_g': 'new_v'}


def _forward(args):
    return _fwd_reference(*[args[k] for k in FWD_PARAMS])


def _output_shape():
    out = _jax.eval_shape(lambda: _forward(_fwd_setup_inputs(0)))
    return out.shape, out.dtype

N_MICROBATCH = 1
ADAM_LR = 0.001
ADAM_B1 = 0.9
ADAM_B2 = 0.999
ADAM_EPS = 1e-08
ADAM_WD = 0.01
ADAM_STEP = 10
PER_EXAMPLE_BATCH_AXIS = {'x': 0, 'loss_target': 0}
SHARED_INPUTS = []
_WEIGHT_DTYPES = {'meta_tokens': _jnp.float32, 'norm_mix_g': _jnp.float32, 'w_in': _jnp.float32, 'ssd_conv_w': _jnp.float32, 'ssd_conv_b': _jnp.float32, 'ssd_dt_bias': _jnp.float32, 'ssd_a_log': _jnp.float32, 'ssd_d': _jnp.float32, 'ssd_norm_g': _jnp.float32, 'sb_norm_g': _jnp.float32, 'mla_q_norm_g': _jnp.float32, 'mla_kv_norm_g': _jnp.float32, 'mla_w_uq': _jnp.float32, 'mla_w_ukv': _jnp.float32, 'mla_norm_g': _jnp.float32, 'w_out': _jnp.float32, 'norm_ffn_g': _jnp.float32, 'ffn_w_up': _jnp.float32, 'ffn_conv_w': _jnp.float32, 'ffn_conv_b': _jnp.float32, 'ffn_w_down': _jnp.float32, 'final_norm_g': _jnp.float32}
MOMENT_SCALE = {'meta_tokens': 2.264779e-02, 'norm_mix_g': 2.832435e-01, 'w_in': 1.704639e-01, 'ssd_conv_w': 1.489797e-01, 'ssd_conv_b': 2.575381e-01, 'ssd_dt_bias': 5.488095e-01, 'ssd_a_log': 2.145433e+00, 'ssd_d': 1.442439e+00, 'ssd_norm_g': 2.190159e-01, 'sb_norm_g': 2.091258e-01, 'mla_q_norm_g': 2.553606e-01, 'mla_kv_norm_g': 6.946630e-01, 'mla_w_uq': 1.431729e-01, 'mla_w_ukv': 2.353301e-01, 'mla_norm_g': 3.203113e-01, 'w_out': 2.310739e-01, 'norm_ffn_g': 1.439823e-01, 'ffn_w_up': 6.174419e-02, 'ffn_conv_w': 6.159726e-02, 'ffn_conv_b': 8.962701e-02, 'ffn_w_down': 1.021917e-01, 'final_norm_g': 6.411722e+01}


def _to_microbatches(a, axis):
    t = _jnp.moveaxis(a, axis, 0)
    t = t.reshape((N_MICROBATCH, t.shape[0] // N_MICROBATCH) + t.shape[1:])
    return _jnp.moveaxis(t, 1, axis + 1)


def setup_inputs(seed: int = 0) -> dict:
    inp = _fwd_setup_inputs(seed)
    key = _jax.random.fold_in(_jax.random.key(seed), 7919)
    shape, _ = _output_shape()
    out = dict(inp)
    out["loss_target"] = _jax.random.normal(_jax.random.fold_in(key, 0), shape, _jnp.float32)
    for i, name in enumerate(TWIN_WEIGHTS):
        w = inp[name].astype(_jnp.float32)
        if MOMENT_SCALE is None:
            s = _jnp.sqrt(_jnp.mean(_jnp.square(w)) + 1e-30)
        else:
            s = MOMENT_SCALE[name]
        km, kv = _jax.random.split(_jax.random.fold_in(key, i + 1))
        out[name] = w
        out["m_" + name] = s * _jax.random.normal(km, w.shape, _jnp.float32)
        out["v_" + name] = (s * s) * _jax.random.uniform(kv, w.shape, _jnp.float32, 0.5, 1.5)
    if N_MICROBATCH > 1:
        for name, axis in PER_EXAMPLE_BATCH_AXIS.items():
            out[name] = _to_microbatches(out[name], axis)
    return {'x': out['x'], 'meta_tokens': out['meta_tokens'], 'norm_mix_g': out['norm_mix_g'], 'w_in': out['w_in'], 'ssd_conv_w': out['ssd_conv_w'], 'ssd_conv_b': out['ssd_conv_b'], 'ssd_dt_bias': out['ssd_dt_bias'], 'ssd_a_log': out['ssd_a_log'], 'ssd_d': out['ssd_d'], 'ssd_norm_g': out['ssd_norm_g'], 'sb_norm_g': out['sb_norm_g'], 'mla_q_norm_g': out['mla_q_norm_g'], 'mla_kv_norm_g': out['mla_kv_norm_g'], 'mla_w_uq': out['mla_w_uq'], 'mla_w_ukv': out['mla_w_ukv'], 'mla_norm_g': out['mla_norm_g'], 'w_out': out['w_out'], 'norm_ffn_g': out['norm_ffn_g'], 'ffn_w_up': out['ffn_w_up'], 'ffn_conv_w': out['ffn_conv_w'], 'ffn_conv_b': out['ffn_conv_b'], 'ffn_w_down': out['ffn_w_down'], 'final_norm_g': out['final_norm_g'], 'loss_target': out['loss_target'], 'm_meta_tokens': out['m_meta_tokens'], 'm_norm_mix_g': out['m_norm_mix_g'], 'm_w_in': out['m_w_in'], 'm_ssd_conv_w': out['m_ssd_conv_w'], 'm_ssd_conv_b': out['m_ssd_conv_b'], 'm_ssd_dt_bias': out['m_ssd_dt_bias'], 'm_ssd_a_log': out['m_ssd_a_log'], 'm_ssd_d': out['m_ssd_d'], 'm_ssd_norm_g': out['m_ssd_norm_g'], 'm_sb_norm_g': out['m_sb_norm_g'], 'm_mla_q_norm_g': out['m_mla_q_norm_g'], 'm_mla_kv_norm_g': out['m_mla_kv_norm_g'], 'm_mla_w_uq': out['m_mla_w_uq'], 'm_mla_w_ukv': out['m_mla_w_ukv'], 'm_mla_norm_g': out['m_mla_norm_g'], 'm_w_out': out['m_w_out'], 'm_norm_ffn_g': out['m_norm_ffn_g'], 'm_ffn_w_up': out['m_ffn_w_up'], 'm_ffn_conv_w': out['m_ffn_conv_w'], 'm_ffn_conv_b': out['m_ffn_conv_b'], 'm_ffn_w_down': out['m_ffn_w_down'], 'm_final_norm_g': out['m_final_norm_g'], 'v_meta_tokens': out['v_meta_tokens'], 'v_norm_mix_g': out['v_norm_mix_g'], 'v_w_in': out['v_w_in'], 'v_ssd_conv_w': out['v_ssd_conv_w'], 'v_ssd_conv_b': out['v_ssd_conv_b'], 'v_ssd_dt_bias': out['v_ssd_dt_bias'], 'v_ssd_a_log': out['v_ssd_a_log'], 'v_ssd_d': out['v_ssd_d'], 'v_ssd_norm_g': out['v_ssd_norm_g'], 'v_sb_norm_g': out['v_sb_norm_g'], 'v_mla_q_norm_g': out['v_mla_q_norm_g'], 'v_mla_kv_norm_g': out['v_mla_kv_norm_g'], 'v_mla_w_uq': out['v_mla_w_uq'], 'v_mla_w_ukv': out['v_mla_w_ukv'], 'v_mla_norm_g': out['v_mla_norm_g'], 'v_w_out': out['v_w_out'], 'v_norm_ffn_g': out['v_norm_ffn_g'], 'v_ffn_w_up': out['v_ffn_w_up'], 'v_ffn_conv_w': out['v_ffn_conv_w'], 'v_ffn_conv_b': out['v_ffn_conv_b'], 'v_ffn_w_down': out['v_ffn_w_down'], 'v_final_norm_g': out['v_final_norm_g']}


def _loss(weights, diff, rest, loss_target):
    with _jax.named_scope("forward"):
        args = {**rest, TWIN_DIFF_INPUT: diff, **{k: w.astype(_WEIGHT_DTYPES[k]) for k, w in weights.items()}}
        y = _forward(args)
    with _jax.named_scope("loss_head"):
        err = _jnp.square(y.astype(_jnp.float32) - loss_target)
        return 0.5 * _jnp.sum(_jnp.mean(err, axis=-1)) if err.ndim else 0.5 * err


def _adamw(w, g, m, v):
    m = ADAM_B1 * m + (1.0 - ADAM_B1) * g
    v = ADAM_B2 * v + (1.0 - ADAM_B2) * _jnp.square(g)
    m_hat = m / (1.0 - ADAM_B1 ** ADAM_STEP)
    v_hat = v / (1.0 - ADAM_B2 ** ADAM_STEP)
    delta = -ADAM_LR * (m_hat / (_jnp.sqrt(v_hat) + ADAM_EPS) + ADAM_WD * w)
    return delta, m, v


def reference(x, meta_tokens, norm_mix_g, w_in, ssd_conv_w, ssd_conv_b, ssd_dt_bias, ssd_a_log, ssd_d, ssd_norm_g, sb_norm_g, mla_q_norm_g, mla_kv_norm_g, mla_w_uq, mla_w_ukv, mla_norm_g, w_out, norm_ffn_g, ffn_w_up, ffn_conv_w, ffn_conv_b, ffn_w_down, final_norm_g, loss_target, m_meta_tokens, m_norm_mix_g, m_w_in, m_ssd_conv_w, m_ssd_conv_b, m_ssd_dt_bias, m_ssd_a_log, m_ssd_d, m_ssd_norm_g, m_sb_norm_g, m_mla_q_norm_g, m_mla_kv_norm_g, m_mla_w_uq, m_mla_w_ukv, m_mla_norm_g, m_w_out, m_norm_ffn_g, m_ffn_w_up, m_ffn_conv_w, m_ffn_conv_b, m_ffn_w_down, m_final_norm_g, v_meta_tokens, v_norm_mix_g, v_w_in, v_ssd_conv_w, v_ssd_conv_b, v_ssd_dt_bias, v_ssd_a_log, v_ssd_d, v_ssd_norm_g, v_sb_norm_g, v_mla_q_norm_g, v_mla_kv_norm_g, v_mla_w_uq, v_mla_w_ukv, v_mla_norm_g, v_w_out, v_norm_ffn_g, v_ffn_w_up, v_ffn_conv_w, v_ffn_conv_b, v_ffn_w_down, v_final_norm_g):
    given = dict(x=x, meta_tokens=meta_tokens, norm_mix_g=norm_mix_g, w_in=w_in, ssd_conv_w=ssd_conv_w, ssd_conv_b=ssd_conv_b, ssd_dt_bias=ssd_dt_bias, ssd_a_log=ssd_a_log, ssd_d=ssd_d, ssd_norm_g=ssd_norm_g, sb_norm_g=sb_norm_g, mla_q_norm_g=mla_q_norm_g, mla_kv_norm_g=mla_kv_norm_g, mla_w_uq=mla_w_uq, mla_w_ukv=mla_w_ukv, mla_norm_g=mla_norm_g, w_out=w_out, norm_ffn_g=norm_ffn_g, ffn_w_up=ffn_w_up, ffn_conv_w=ffn_conv_w, ffn_conv_b=ffn_conv_b, ffn_w_down=ffn_w_down, final_norm_g=final_norm_g, loss_target=loss_target, m_meta_tokens=m_meta_tokens, m_norm_mix_g=m_norm_mix_g, m_w_in=m_w_in, m_ssd_conv_w=m_ssd_conv_w, m_ssd_conv_b=m_ssd_conv_b, m_ssd_dt_bias=m_ssd_dt_bias, m_ssd_a_log=m_ssd_a_log, m_ssd_d=m_ssd_d, m_ssd_norm_g=m_ssd_norm_g, m_sb_norm_g=m_sb_norm_g, m_mla_q_norm_g=m_mla_q_norm_g, m_mla_kv_norm_g=m_mla_kv_norm_g, m_mla_w_uq=m_mla_w_uq, m_mla_w_ukv=m_mla_w_ukv, m_mla_norm_g=m_mla_norm_g, m_w_out=m_w_out, m_norm_ffn_g=m_norm_ffn_g, m_ffn_w_up=m_ffn_w_up, m_ffn_conv_w=m_ffn_conv_w, m_ffn_conv_b=m_ffn_conv_b, m_ffn_w_down=m_ffn_w_down, m_final_norm_g=m_final_norm_g, v_meta_tokens=v_meta_tokens, v_norm_mix_g=v_norm_mix_g, v_w_in=v_w_in, v_ssd_conv_w=v_ssd_conv_w, v_ssd_conv_b=v_ssd_conv_b, v_ssd_dt_bias=v_ssd_dt_bias, v_ssd_a_log=v_ssd_a_log, v_ssd_d=v_ssd_d, v_ssd_norm_g=v_ssd_norm_g, v_sb_norm_g=v_sb_norm_g, v_mla_q_norm_g=v_mla_q_norm_g, v_mla_kv_norm_g=v_mla_kv_norm_g, v_mla_w_uq=v_mla_w_uq, v_mla_w_ukv=v_mla_w_ukv, v_mla_norm_g=v_mla_norm_g, v_w_out=v_w_out, v_norm_ffn_g=v_norm_ffn_g, v_ffn_w_up=v_ffn_w_up, v_ffn_conv_w=v_ffn_conv_w, v_ffn_conv_b=v_ffn_conv_b, v_ffn_w_down=v_ffn_w_down, v_final_norm_g=v_final_norm_g)
    weights = {n: given[n] for n in TWIN_WEIGHTS}
    shared = {n: given[n] for n in SHARED_INPUTS}
    per_example = {n: given[n] for n in ['x']}
    grad_fn = _jax.value_and_grad(_loss, argnums=(0, 1))

    def one_microbatch(ex, loss_target):
        ex = dict(ex)
        diff = ex.pop(TWIN_DIFF_INPUT)
        return grad_fn(weights, diff, {**shared, **ex}, loss_target)

    if N_MICROBATCH == 1:
        loss, (grad_w, grad_x) = one_microbatch(per_example, given["loss_target"])
    else:
        def body(carry, xs):
            loss_sum, grad_sum = carry
            l_k, (gw_k, gx_k) = one_microbatch(xs[0], xs[1])
            with _jax.named_scope("update"):
                return (loss_sum + l_k, _jax.tree.map(_jnp.add, grad_sum, gw_k)), gx_k

        init = (_jnp.zeros((), _jnp.float32), _jax.tree.map(_jnp.zeros_like, weights))
        (loss, grad_w), grad_x = _jax.lax.scan(body, init, (per_example, given["loss_target"]))
    with _jax.named_scope("update"):
        delta_w, new_m, new_v = {}, {}, {}
        for n in TWIN_WEIGHTS:
            delta_w[n], new_m[n], new_v[n] = _adamw(weights[n], grad_w[n], given["m_" + n], given["v_" + n])
    return (loss, grad_x, *[grad_w[n] for n in TWIN_WEIGHTS], *[delta_w[n] for n in TWIN_WEIGHTS],
            *[new_m[n] for n in TWIN_WEIGHTS], *[new_v[n] for n in TWIN_WEIGHTS])
```

```python
import math

import jax
import jax.numpy as jnp
from jax import lax
from jax.experimental import pallas as pl
from jax.experimental.pallas import tpu as pltpu

F32 = jnp.float32
BF16 = jnp.bfloat16

D_MODEL = 1024
DEPTH = 2
N_META = 16
BLOCK = 128
N_PAD = BLOCK - N_META
EPS = 1e-6
SSD_WIDTH = 512
SSD_XBC = 1024
SSD_CONV = 4
SB_WIDTH = 256
SB_HEAD_DIM = 64
MLA_HEADS = 4
MLA_NOPE = 64
MLA_ROPE = 32
MLA_Q_RANK = 192
MLA_KV_RANK = 128
ROPE_BASE = 10000.0
D_FF = 2816
FFN_CONV = 3
IN_COLS = 2664
N_DEV = 8

ADAM_LR = 0.001
ADAM_B1 = 0.9
ADAM_B2 = 0.999
ADAM_EPS = 1e-08
ADAM_WD = 0.01
ADAM_STEP = 10

U_COLS = 3072
OFF_Z, OFF_XBC, OFF_QSB, OFF_KSB, OFF_VSB, OFF_QA, OFF_CKV, OFF_KR, OFF_DT = (
    0, 512, 1536, 1792, 2048, 2304, 2560, 2688, 2816)

V7X_VMEM_BYTES = 64 * 1024 * 1024
VMEM_LIMIT = (V7X_VMEM_BYTES * 3) // 4
NEG_BIG = -1e30


def _cparams(n_axes):
    return pltpu.CompilerParams(dimension_semantics=("arbitrary",) * n_axes, vmem_limit_bytes=VMEM_LIMIT)


def _tile(n, target, align):
    best = None
    for d in range(align, min(n, target) + 1, align):
        if n % d == 0:
            best = d
    return n if best is None else best


def _sigmoid(x):
    return 1.0 / (1.0 + jnp.exp(-x))


def _softplus(x):
    return jnp.maximum(x, 0.0) + jnp.log(1.0 + jnp.exp(-jnp.abs(x)))


def _dot(a, b):
    return jnp.dot(a, b, preferred_element_type=F32)


def _dot_nt(a, b):
    return lax.dot_general(a, b, (((1,), (1,)), ((), ())), preferred_element_type=F32)


def _hilo(x):
    hi = x.astype(BF16)
    lo = (x - hi.astype(F32)).astype(BF16)
    return hi, lo


def _hilo_dot_l(x, m):
    hi, lo = _hilo(x)
    return _dot(hi, m) + _dot(lo, m)


def _hilo_dot_r(m, x):
    hi, lo = _hilo(x)
    return _dot(m, hi) + _dot(m, lo)


def _hilo_dot_nt(x, m):
    hi, lo = _hilo(x)
    return _dot_nt(hi, m) + _dot_nt(lo, m)


def _ones_where(cond):
    return jnp.where(cond, 1.0, 0.0).astype(BF16)


def _matmul(a, b, *, name, ta=False, tb=False, out_dtype=F32, res=None, mask_pad=False,
            tm=640, tn=1024, tk=1024):
    m_dim = a.shape[1] if ta else a.shape[0]
    k_dim = a.shape[0] if ta else a.shape[1]
    n_dim = b.shape[0] if tb else b.shape[1]
    assert (b.shape[1] if tb else b.shape[0]) == k_dim
    tm = _tile(m_dim, tm, 128)
    tn = _tile(n_dim, tn, 128)
    tk = _tile(k_dim, tk, 128)
    nk = k_dim // tk
    dn = (((0 if ta else 1,), (1 if tb else 0,)), ((), ()))

    def body(*refs):
        if res is not None:
            a_ref, b_ref, r_ref, o_ref, acc = refs
        else:
            a_ref, b_ref, o_ref, acc = refs
        k = pl.program_id(2)

        @pl.when(k == 0)
        def _():
            acc[...] = jnp.zeros_like(acc)

        acc[...] += lax.dot_general(a_ref[...].astype(BF16), b_ref[...].astype(BF16), dn,
                                    preferred_element_type=F32)

        @pl.when(k == nk - 1)
        def _():
            r = acc[...]
            if res is not None:
                r = r + r_ref[...].astype(F32)
            if mask_pad:
                rows = pl.program_id(0) * tm + lax.broadcasted_iota(jnp.int32, (tm, 1), 0)
                r = jnp.where(rows >= N_PAD, r, 0.0)
            o_ref[...] = r.astype(out_dtype)

    a_spec = pl.BlockSpec((tk, tm), lambda i, j, k: (k, i)) if ta else pl.BlockSpec((tm, tk), lambda i, j, k: (i, k))
    b_spec = pl.BlockSpec((tn, tk), lambda i, j, k: (j, k)) if tb else pl.BlockSpec((tk, tn), lambda i, j, k: (k, j))
    o_spec = pl.BlockSpec((tm, tn), lambda i, j, k: (i, j))
    in_specs = [a_spec, b_spec]
    args = [a, b]
    if res is not None:
        in_specs.append(o_spec)
        args.append(res)
    return pl.pallas_call(
        body, name=name, grid=(m_dim // tm, n_dim // tn, nk),
        in_specs=in_specs, out_specs=o_spec,
        out_shape=jax.ShapeDtypeStruct((m_dim, n_dim), out_dtype),
        scratch_shapes=[pltpu.VMEM((tm, tn), F32)],
        compiler_params=_cparams(3),
    )(*args)


def _rmsnorm_fwd(x, g, *, width, name, z=None, out_dtype=None):
    out_dtype = BF16 if out_dtype is None else out_dtype
    rows, w = x.shape
    tr = _tile(rows, 640, 128)
    inv_w = 1.0 / width

    def body(*refs):
        if z is not None:
            x_ref, z_ref, g_ref, o_ref = refs
        else:
            x_ref, g_ref, o_ref = refs
        t = x_ref[...].astype(F32)
        if z is not None:
            zz = z_ref[...]
            t = t * (zz * _sigmoid(zz))
        ms = jnp.sum(t * t, axis=-1, keepdims=True) * inv_w
        o_ref[...] = ((t * lax.rsqrt(ms + EPS)) * g_ref[...]).astype(out_dtype)

    row_spec = pl.BlockSpec((tr, w), lambda i: (i, 0))
    g_spec = pl.BlockSpec((1, w), lambda i: (0, 0))
    in_specs = [row_spec] + ([row_spec] if z is not None else []) + [g_spec]
    args = [x] + ([z] if z is not None else []) + [g]
    return pl.pallas_call(
        body, name=name, grid=(rows // tr,), in_specs=in_specs, out_specs=row_spec,
        out_shape=jax.ShapeDtypeStruct((rows, w), out_dtype), compiler_params=_cparams(1),
    )(*args)


def _rmsnorm_bwd(x, g, dout, *, width, name, z=None, res=None, mask_pad=False):
    rows, w = x.shape
    tr = _tile(rows, 640, 128)
    inv_w = 1.0 / width

    def body(*refs):
        refs = list(refs)
        x_ref = refs.pop(0)
        z_ref = refs.pop(0) if z is not None else None
        g_ref = refs.pop(0)
        do_ref = refs.pop(0)
        r_ref = refs.pop(0) if res is not None else None
        dx_ref = refs.pop(0)
        dz_ref = refs.pop(0) if z is not None else None
        dg_ref = refs.pop(0)
        i = pl.program_id(0)

        @pl.when(i == 0)
        def _():
            dg_ref[...] = jnp.zeros_like(dg_ref)

        xv = x_ref[...].astype(F32)
        t = xv
        if z is not None:
            zz = z_ref[...]
            sig = _sigmoid(zz)
            sl = zz * sig
            t = xv * sl
        ms = jnp.sum(t * t, axis=-1, keepdims=True) * inv_w
        rstd = lax.rsqrt(ms + EPS)
        xhat = t * rstd
        do = do_ref[...].astype(F32)
        dxh = do * g_ref[...]
        c = jnp.sum(dxh * xhat, axis=-1, keepdims=True) * inv_w
        dt = rstd * (dxh - xhat * c)
        dg_ref[...] += jnp.sum(do * xhat, axis=0, keepdims=True)
        if z is not None:
            dz_ref[...] = dt * xv * (sig * (1.0 + zz * (1.0 - sig)))
            dx = dt * sl
        else:
            dx = dt
        if res is not None:
            dx = dx + r_ref[...]
        if mask_pad:
            rws = i * tr + lax.broadcasted_iota(jnp.int32, (tr, 1), 0)
            dx = jnp.where(rws >= N_PAD, dx, 0.0)
        dx_ref[...] = dx

    row_spec = pl.BlockSpec((tr, w), lambda i: (i, 0))
    g_spec = pl.BlockSpec((1, w), lambda i: (0, 0))
    in_specs = [row_spec] + ([row_spec] if z is not None else []) + [g_spec, row_spec] + (
        [row_spec] if res is not None else [])
    args = [x] + ([z] if z is not None else []) + [g, dout] + ([res] if res is not None else [])
    out_specs = [row_spec] + ([row_spec] if z is not None else []) + [g_spec]
    out_shape = [jax.ShapeDtypeStruct((rows, w), F32)] + (
        [jax.ShapeDtypeStruct((rows, w), F32)] if z is not None else []) + [jax.ShapeDtypeStruct((1, w), F32)]
    outs = pl.pallas_call(
        body, name=name, grid=(rows // tr,), in_specs=in_specs, out_specs=out_specs,
        out_shape=out_shape, compiler_params=_cparams(1),
    )(*args)
    if z is not None:
        return outs[0], outs[1], outs[2]
    return outs[0], None, outs[1]


def _final_loss(h, g, target, *, name):
    rows, w = h.shape
    nb = rows // BLOCK
    inv_w = 1.0 / w

    def body(h_ref, g_ref, t_ref, dh_ref, dg_ref, loss_ref):
        i = pl.program_id(0)

        @pl.when(i == 0)
        def _():
            dg_ref[...] = jnp.zeros_like(dg_ref)
            loss_ref[...] = jnp.zeros_like(loss_ref)

        xv = h_ref[...]
        ms = jnp.sum(xv * xv, axis=-1, keepdims=True) * inv_w
        rstd = lax.rsqrt(ms + EPS)
        xhat = xv * rstd
        gv = g_ref[...]
        err = jnp.where(i >= 1, xhat * gv - t_ref[...], 0.0)
        loss_ref[...] += (0.5 * inv_w) * jnp.sum(err * err)
        do = err * inv_w
        dxh = do * gv
        c = jnp.sum(dxh * xhat, axis=-1, keepdims=True) * inv_w
        dh_ref[...] = rstd * (dxh - xhat * c)
        dg_ref[...] += jnp.sum(do * xhat, axis=0, keepdims=True)

    row_spec = pl.BlockSpec((BLOCK, w), lambda i: (i, 0))
    g_spec = pl.BlockSpec((1, w), lambda i: (0, 0))
    return pl.pallas_call(
        body, name=name, grid=(nb,),
        in_specs=[row_spec, g_spec, pl.BlockSpec((BLOCK, w), lambda i: (jnp.maximum(i - 1, 0), 0))],
        out_specs=[row_spec, g_spec, pl.BlockSpec((1, 128), lambda i: (0, 0))],
        out_shape=[jax.ShapeDtypeStruct((rows, w), F32), jax.ShapeDtypeStruct((1, w), F32),
                   jax.ShapeDtypeStruct((1, 128), F32)],
        compiler_params=_cparams(1),
    )(h, g, target)


HALO = 8


def _dwconv_fwd(u, w8, b, *, taps, name):
    rows, ch = u.shape
    tb = _tile(rows, 640, 128)
    tc = _tile(ch, 512, 128)
    hb = tb // HALO

    def body(u_ref, h_ref, w_ref, b_ref, o_ref, buf):
        i = pl.program_id(0)
        buf[0:HALO, :] = jnp.where(i > 0, h_ref[...], 0.0)
        buf[HALO:HALO + tb, :] = u_ref[...]
        acc = jnp.broadcast_to(b_ref[...], (tb, tc))
        for k in range(taps):
            acc = acc + w_ref[k:k + 1, :] * buf[pl.ds(HALO - (taps - 1) + k, tb), :]
        o_ref[...] = acc

    return pl.pallas_call(
        body, name=name, grid=(rows // tb, ch // tc),
        in_specs=[pl.BlockSpec((tb, tc), lambda i, j: (i, j)),
                  pl.BlockSpec((HALO, tc), lambda i, j: (jnp.maximum(i * hb - 1, 0), j)),
                  pl.BlockSpec((HALO, tc), lambda i, j: (0, j)),
                  pl.BlockSpec((1, tc), lambda i, j: (0, j))],
        out_specs=pl.BlockSpec((tb, tc), lambda i, j: (i, j)),
        out_shape=jax.ShapeDtypeStruct((rows, ch), F32),
        scratch_shapes=[pltpu.VMEM((tb + HALO, tc), F32)],
        compiler_params=_cparams(2),
    )(u, u, w8, b)


def _dwconv_bwd(dpre, u, w8, *, taps, name):
    rows, ch = u.shape
    tb = _tile(rows, 640, 128)
    tc = _tile(ch, 512, 128)
    hb = tb // HALO
    nb = rows // tb
    last_halo = rows // HALO - 1

    def body(d_ref, dn_ref, u_ref, up_ref, w_ref, du_ref, dw_ref, db_ref, bufd, bufu):
        i = pl.program_id(1)

        @pl.when(i == 0)
        def _():
            dw_ref[...] = jnp.zeros_like(dw_ref)
            db_ref[...] = jnp.zeros_like(db_ref)

        d = d_ref[...]
        bufd[0:tb, :] = d
        bufd[tb:tb + HALO, :] = jnp.where(i < nb - 1, dn_ref[...], 0.0)
        bufu[0:HALO, :] = jnp.where(i > 0, up_ref[...], 0.0)
        bufu[HALO:HALO + tb, :] = u_ref[...]
        acc = jnp.zeros((tb, tc), F32)
        for k in range(taps):
            acc = acc + w_ref[k:k + 1, :] * bufd[pl.ds(taps - 1 - k, tb), :]
        du_ref[...] = acc
        for k in range(taps):
            dw_ref[k:k + 1, :] += jnp.sum(d * bufu[pl.ds(HALO - (taps - 1) + k, tb), :], axis=0, keepdims=True)
        db_ref[...] += jnp.sum(d, axis=0, keepdims=True)

    return pl.pallas_call(
        body, name=name, grid=(ch // tc, nb),
        in_specs=[pl.BlockSpec((tb, tc), lambda j, i: (i, j)),
                  pl.BlockSpec((HALO, tc), lambda j, i: (jnp.minimum((i + 1) * hb, last_halo), j)),
                  pl.BlockSpec((tb, tc), lambda j, i: (i, j)),
                  pl.BlockSpec((HALO, tc), lambda j, i: (jnp.maximum(i * hb - 1, 0), j)),
                  pl.BlockSpec((HALO, tc), lambda j, i: (0, j))],
        out_specs=[pl.BlockSpec((tb, tc), lambda j, i: (i, j)),
                   pl.BlockSpec((HALO, tc), lambda j, i: (0, j)),
                   pl.BlockSpec((1, tc), lambda j, i: (0, j))],
        out_shape=[jax.ShapeDtypeStruct((rows, ch), F32), jax.ShapeDtypeStruct((HALO, ch), F32),
                   jax.ShapeDtypeStruct((1, ch), F32)],
        scratch_shapes=[pltpu.VMEM((tb + HALO, tc), F32), pltpu.VMEM((tb + HALO, tc), F32)],
        compiler_params=_cparams(2),
    )(dpre, dpre, u, u, w8)


def _gate_fwd(pre, *, name):
    rows, c2 = pre.shape
    f = c2 // 2
    tb = _tile(rows, 640, 128)
    tc = _tile(f, 512, 128)
    nct = f // tc

    def body(p1_ref, p2_ref, o_ref):
        p1 = p1_ref[...]
        o_ref[...] = (p1 * _sigmoid(p1) * p2_ref[...]).astype(BF16)

    return pl.pallas_call(
        body, name=name, grid=(rows // tb, nct),
        in_specs=[pl.BlockSpec((tb, tc), lambda i, j: (i, j)),
                  pl.BlockSpec((tb, tc), lambda i, j: (i, j + nct))],
        out_specs=pl.BlockSpec((tb, tc), lambda i, j: (i, j)),
        out_shape=jax.ShapeDtypeStruct((rows, f), BF16), compiler_params=_cparams(2),
    )(pre, pre)


def _gate_bwd(pre, dact, *, name):
    rows, c2 = pre.shape
    f = c2 // 2
    tb = _tile(rows, 640, 128)
    tc = _tile(f, 512, 128)
    nct = f // tc

    def body(p1_ref, p2_ref, d_ref, o_ref):
        s = pl.program_id(1)
        p1 = p1_ref[...]
        d = d_ref[...].astype(F32)
        sig = _sigmoid(p1)
        dp1 = d * p2_ref[...] * (sig * (1.0 + p1 * (1.0 - sig)))
        dp2 = d * (p1 * sig)
        o_ref[...] = jnp.where(s == 0, dp1, dp2)

    return pl.pallas_call(
        body, name=name, grid=(rows // tb, 2, nct),
        in_specs=[pl.BlockSpec((tb, tc), lambda i, s, j: (i, j)),
                  pl.BlockSpec((tb, tc), lambda i, s, j: (i, j + nct)),
                  pl.BlockSpec((tb, tc), lambda i, s, j: (i, j))],
        out_specs=pl.BlockSpec((tb, tc), lambda i, s, j: (i, s * nct + j)),
        out_shape=jax.ShapeDtypeStruct((rows, c2), F32), compiler_params=_cparams(3),
    )(pre, pre, dact)


def _rope(xr, cos_t, sin_t, *, name, transpose=False):
    rows, w = xr.shape
    tr = _tile(rows, 640, 128)

    def body(x_ref, c_ref, s_ref, o_ref):
        xv = x_ref[...]
        if transpose:
            o_ref[...] = xv * c_ref[...] + pltpu.roll(xv * s_ref[...], 64, 1)
        else:
            o_ref[...] = xv * c_ref[...] + pltpu.roll(xv, 64, 1) * s_ref[...]

    spec = pl.BlockSpec((tr, w), lambda i: (i, 0))
    return pl.pallas_call(
        body, name=name, grid=(rows // tr,), in_specs=[spec, spec, spec], out_specs=spec,
        out_shape=jax.ShapeDtypeStruct((rows, w), F32), compiler_params=_cparams(1),
    )(xr, cos_t, sin_t)


def _tile_iotas():
    r_i = lax.broadcasted_iota(jnp.int32, (BLOCK, BLOCK), 0)
    c_i = lax.broadcasted_iota(jnp.int32, (BLOCK, BLOCK), 1)
    return r_i, c_i


def _sb_fwd(q, k, v, *, name):
    nh, rows, hd = q.shape
    nq = rows // BLOCK
    scale = SB_HEAD_DIM ** -0.5

    def body(q_ref, k_ref, v_ref, o_ref, u_ref):
        i = pl.program_id(1)
        r_i, c_i = _tile_iotas()
        m_after = _ones_where(r_i > c_i)
        qb = q_ref[0].astype(BF16)
        rowpos = i * BLOCK + r_i

        def step(jj, carry):
            acc, cu = carry
            j = i - jj
            off = pl.multiple_of(j * BLOCK, BLOCK)
            kb = k_ref[0, pl.ds(off, BLOCK), :].astype(BF16)
            vb = v_ref[0, pl.ds(off, BLOCK), :].astype(BF16)
            zs = _dot_nt(qb, kb) * scale
            colpos = off + c_i
            mask = (colpos < rowpos) & (colpos >= N_PAD)
            sp = _softplus(zs)
            uu = jnp.where(mask, -sp, 0.0)
            after = cu + _hilo_dot_l(uu, m_after)
            wgt = jnp.where(mask, jnp.exp(zs - sp + after), 0.0)
            acc = acc + _dot(wgt.astype(BF16), vb)
            cu = (after + uu)[:, 0:1]
            return acc, cu

        acc, cu = lax.fori_loop(0, i + 1, step, (jnp.zeros((BLOCK, hd), F32), jnp.zeros((BLOCK, 1), F32)))
        o_ref[0] = acc
        u_ref[0] = jnp.broadcast_to(cu, (BLOCK, 128))

    blk = pl.BlockSpec((1, BLOCK, hd), lambda h, i: (h, i, 0))
    full = pl.BlockSpec((1, rows, hd), lambda h, i: (h, 0, 0))
    return pl.pallas_call(
        body, name=name, grid=(nh, nq), in_specs=[blk, full, full],
        out_specs=[blk, pl.BlockSpec((1, BLOCK, 128), lambda h, i: (h, i, 0))],
        out_shape=[jax.ShapeDtypeStruct((nh, rows, hd), F32), jax.ShapeDtypeStruct((nh, rows, 128), F32)],
        compiler_params=_cparams(2),
    )(q, k, v)


def _sb_bwd(q, k, v, do, u_tot, *, name):
    nh, rows, hd = q.shape
    nq = rows // BLOCK
    scale = SB_HEAD_DIM ** -0.5

    def body(q_ref, k_ref, v_ref, do_ref, u_ref, dq_ref, dk_ref, dv_ref):
        i = pl.program_id(1)

        @pl.when(i == 0)
        def _():
            dk_ref[...] = jnp.zeros_like(dk_ref)
            dv_ref[...] = jnp.zeros_like(dv_ref)

        r_i, c_i = _tile_iotas()
        m_incl = _ones_where(r_i <= c_i)
        m_excl = _ones_where(r_i < c_i)
        qb = q_ref[0].astype(BF16)
        dob = do_ref[0].astype(BF16)
        utot = u_ref[0][:, 0:1]
        rowpos = i * BLOCK + r_i

        def step(j, carry):
            dq, cp, cg = carry
            off = pl.multiple_of(j * BLOCK, BLOCK)
            kb = k_ref[0, pl.ds(off, BLOCK), :].astype(BF16)
            vb = v_ref[0, pl.ds(off, BLOCK), :].astype(BF16)
            zs = _dot_nt(qb, kb) * scale
            colpos = off + c_i
            mask = (colpos < rowpos) & (colpos >= N_PAD)
            sp = _softplus(zs)
            sig = jnp.exp(zs - sp)
            uu = jnp.where(mask, -sp, 0.0)
            p_incl = cp + _hilo_dot_l(uu, m_incl)
            wgt = jnp.where(mask, jnp.exp(jnp.minimum(zs - sp + utot - p_incl, 0.0)), 0.0)
            dw = _dot_nt(dob, vb)
            gg = wgt * dw
            g_ex = cg + _hilo_dot_l(gg, m_excl)
            dz = jnp.where(mask, gg * (1.0 - sig) - sig * g_ex, 0.0) * scale
            dq = dq + _dot(dz.astype(BF16), kb)
            dk_ref[0, pl.ds(off, BLOCK), :] += _dot(dz.T.astype(BF16), qb)
            dv_ref[0, pl.ds(off, BLOCK), :] += _dot(wgt.T.astype(BF16), dob)
            cp = p_incl[:, BLOCK - 1:BLOCK]
            cg = (g_ex + gg)[:, BLOCK - 1:BLOCK]
            return dq, cp, cg

        z1 = jnp.zeros((BLOCK, 1), F32)
        dq, _, _ = lax.fori_loop(0, i + 1, step, (jnp.zeros((BLOCK, hd), F32), z1, z1))
        dq_ref[0] = dq

    blk = pl.BlockSpec((1, BLOCK, hd), lambda h, i: (h, i, 0))
    full = pl.BlockSpec((1, rows, hd), lambda h, i: (h, 0, 0))
    ublk = pl.BlockSpec((1, BLOCK, 128), lambda h, i: (h, i, 0))
    sds = jax.ShapeDtypeStruct((nh, rows, hd), F32)
    return pl.pallas_call(
        body, name=name, grid=(nh, nq), in_specs=[blk, full, full, blk, ublk],
        out_specs=[blk, full, full], out_shape=[sds, sds, sds], compiler_params=_cparams(2),
    )(q, k, v, do, u_tot)


def _mla_fwd(q, k, v, *, name):
    nh, rows, dk = q.shape
    dv = v.shape[2]
    nq = rows // BLOCK
    scale = (MLA_NOPE + MLA_ROPE) ** -0.5

    def body(q_ref, k_ref, v_ref, o_ref, lse_ref):
        i = pl.program_id(1)
        r_i, c_i = _tile_iotas()
        qb = q_ref[0].astype(BF16)
        rowpos = i * BLOCK + r_i

        def step(j, carry):
            m, l, acc = carry
            off = pl.multiple_of(j * BLOCK, BLOCK)
            kb = k_ref[0, pl.ds(off, BLOCK), :].astype(BF16)
            vb = v_ref[0, pl.ds(off, BLOCK), :].astype(BF16)
            colpos = off + c_i
            mask = (colpos <= rowpos) & (colpos >= N_PAD)
            s = jnp.where(mask, _dot_nt(qb, kb) * scale, NEG_BIG)
            m_new = jnp.maximum(m, jnp.max(s, axis=1, keepdims=True))
            alpha = jnp.exp(m - m_new)
            p = jnp.exp(s - m_new)
            l = alpha * l + jnp.sum(p, axis=1, keepdims=True)
            acc = alpha * acc + _dot(p.astype(BF16), vb)
            return m_new, l, acc

        init = (jnp.full((BLOCK, 1), NEG_BIG, F32), jnp.zeros((BLOCK, 1), F32), jnp.zeros((BLOCK, dv), F32))
        m, l, acc = lax.fori_loop(0, i + 1, step, init)
        o_ref[0] = acc / l
        lse_ref[0] = jnp.broadcast_to(m + jnp.log(l), (BLOCK, 128))

    qblk = pl.BlockSpec((1, BLOCK, dk), lambda h, i: (h, i, 0))
    kfull = pl.BlockSpec((1, rows, dk), lambda h, i: (h, 0, 0))
    vfull = pl.BlockSpec((1, rows, dv), lambda h, i: (h, 0, 0))
    return pl.pallas_call(
        body, name=name, grid=(nh, nq), in_specs=[qblk, kfull, vfull],
        out_specs=[pl.BlockSpec((1, BLOCK, dv), lambda h, i: (h, i, 0)),
                   pl.BlockSpec((1, BLOCK, 128), lambda h, i: (h, i, 0))],
        out_shape=[jax.ShapeDtypeStruct((nh, rows, dv), F32), jax.ShapeDtypeStruct((nh, rows, 128), F32)],
        compiler_params=_cparams(2),
    )(q, k, v)


def _mla_bwd(q, k, v, o, lse, do, *, name):
    nh, rows, dk = q.shape
    dv = v.shape[2]
    nq = rows // BLOCK
    scale = (MLA_NOPE + MLA_ROPE) ** -0.5

    def body(q_ref, k_ref, v_ref, o_ref, lse_ref, do_ref, dq_ref, dk_ref, dv_ref):
        i = pl.program_id(1)

        @pl.when(i == 0)
        def _():
            dk_ref[...] = jnp.zeros_like(dk_ref)
            dv_ref[...] = jnp.zeros_like(dv_ref)

        r_i, c_i = _tile_iotas()
        qb = q_ref[0].astype(BF16)
        dov = do_ref[0]
        dob = dov.astype(BF16)
        delta = jnp.sum(dov * o_ref[0], axis=1, keepdims=True)
        lse = lse_ref[0][:, 0:1]
        rowpos = i * BLOCK + r_i

        def step(j, dq):
            off = pl.multiple_of(j * BLOCK, BLOCK)
            kb = k_ref[0, pl.ds(off, BLOCK), :].astype(BF16)
            vb = v_ref[0, pl.ds(off, BLOCK), :].astype(BF16)
            colpos = off + c_i
            mask = (colpos <= rowpos) & (colpos >= N_PAD)
            s = _dot_nt(qb, kb) * scale
            p = jnp.where(mask, jnp.exp(jnp.minimum(s - lse, 0.0)), 0.0)
            dp = _dot_nt(dob, vb)
            ds = p * (dp - delta) * scale
            dq = dq + _dot(ds.astype(BF16), kb)
            dk_ref[0, pl.ds(off, BLOCK), :] += _dot(ds.T.astype(BF16), qb)
            dv_ref[0, pl.ds(off, BLOCK), :] += _dot(p.T.astype(BF16), dob)
            return dq

        dq_ref[0] = lax.fori_loop(0, i + 1, step, jnp.zeros((BLOCK, dk), F32))

    qblk = pl.BlockSpec((1, BLOCK, dk), lambda h, i: (h, i, 0))
    vblk = pl.BlockSpec((1, BLOCK, dv), lambda h, i: (h, i, 0))
    lblk = pl.BlockSpec((1, BLOCK, 128), lambda h, i: (h, i, 0))
    kfull = pl.BlockSpec((1, rows, dk), lambda h, i: (h, 0, 0))
    vfull = pl.BlockSpec((1, rows, dv), lambda h, i: (h, 0, 0))
    return pl.pallas_call(
        body, name=name, grid=(nh, nq), in_specs=[qblk, kfull, vfull, vblk, lblk, vblk],
        out_specs=[qblk, kfull, vfull],
        out_shape=[jax.ShapeDtypeStruct((nh, rows, dk), F32), jax.ShapeDtypeStruct((nh, rows, dk), F32),
                   jax.ShapeDtypeStruct((nh, rows, dv), F32)],
        compiler_params=_cparams(2),
    )(q, k, v, o, lse, do)


def _ssd_consts():
    r_i, c_i = _tile_iotas()
    eh = lax.broadcasted_iota(jnp.int32, (BLOCK, SSD_WIDTH), 0)
    ec = lax.broadcasted_iota(jnp.int32, (BLOCK, SSD_WIDTH), 1)
    expand = _ones_where(lax.shift_right_logical(ec, 6) == eh)
    return r_i, c_i, expand


def _ssd_common(pre_v, dtr_v, bias_v, a_v, chunk, r_i, c_i, expand):
    lower = r_i >= c_i
    sig_pre = _sigmoid(pre_v)
    xbc = pre_v * sig_pre
    xs = xbc[:, :SSD_WIDTH]
    valid = (chunk * BLOCK + lax.broadcasted_iota(jnp.int32, (BLOCK, 1), 0)) >= N_PAD
    dt_in = dtr_v + bias_v
    dtv = jnp.where(valid, _softplus(dt_in), 0.0)
    d_a = dtv * a_v
    acs = _hilo_dot_r(_ones_where(lower), d_a)
    acs_t = acs.T
    dt_exp = _hilo_dot_l(dtv, expand)
    acs_exp = _hilo_dot_l(acs, expand)
    a_last = acs_exp[BLOCK - 1:BLOCK, :]
    ea = jnp.exp(acs_exp)
    e_l = jnp.exp(a_last - acs_exp)
    ea_l = jnp.exp(a_last)
    return lower, sig_pre, xbc, xs, valid, dt_in, dtv, acs, acs_t, dt_exp, ea, e_l, ea_l


def _decay(acs, acs_t, h, lower):
    col = acs[:, h:h + 1]
    row = acs_t[h:h + 1, :]
    return jnp.where(lower, jnp.exp(jnp.minimum(col - row, 0.0)), 0.0)


def _ssd_fwd(pre, dtr, bias_row, a_row, d_exp, *, name):
    rows = pre.shape[0]
    nc = rows // BLOCK

    def body(pre_ref, dtr_ref, bias_ref, a_ref, dexp_ref, y_ref, st_ref, state):
        c = pl.program_id(0)

        @pl.when(c == 0)
        def _():
            state[...] = jnp.zeros_like(state)

        r_i, c_i, expand = _ssd_consts()
        lane_lo = c_i < 64
        (lower, _, xbc, xs, _, _, _, acs, acs_t, dt_exp, ea, e_l, ea_l) = _ssd_common(
            pre_ref[...], dtr_ref[...], bias_ref[...], a_ref[...], c, r_i, c_i, expand)
        xin = xs * dt_exp
        for g in range(2):
            bg = xbc[:, 512 + 128 * g:640 + 128 * g]
            cg = xbc[:, 768 + 128 * g:896 + 128 * g]
            bb = bg.astype(BF16)
            cbf = cg.astype(BF16)
            cb = _dot_nt(cbf, bb)
            bt = bg.T.astype(BF16)
            for pp in range(2):
                p = 2 * g + pp
                sl = slice(128 * p, 128 * p + 128)
                xp = xin[:, sl]
                xb = xp.astype(BF16)
                rs = [_dot((cb * _decay(acs, acs_t, 2 * p + hh, lower)).astype(BF16), xb) for hh in range(2)]
                ydiag = jnp.where(lane_lo, rs[0], rs[1])
                s_in = state[p]
                st_ref[0, p] = s_in
                yoff = ea[:, sl] * _dot(cbf, s_in.astype(BF16))
                y_ref[:, sl] = ydiag + yoff + xs[:, sl] * dexp_ref[:, sl]
                state[p] = ea_l[:, sl] * s_in + _dot(bt, (xp * e_l[:, sl]).astype(BF16))

    vec = pl.BlockSpec((1, 128), lambda c: (0, 0))
    return pl.pallas_call(
        body, name=name, grid=(nc,),
        in_specs=[pl.BlockSpec((BLOCK, SSD_XBC), lambda c: (c, 0)),
                  pl.BlockSpec((BLOCK, 128), lambda c: (c, 0)), vec, vec,
                  pl.BlockSpec((1, SSD_WIDTH), lambda c: (0, 0))],
        out_specs=[pl.BlockSpec((BLOCK, SSD_WIDTH), lambda c: (c, 0)),
                   pl.BlockSpec((1, 4, 128, 128), lambda c: (c, 0, 0, 0))],
        out_shape=[jax.ShapeDtypeStruct((rows, SSD_WIDTH), F32), jax.ShapeDtypeStruct((nc, 4, 128, 128), F32)],
        scratch_shapes=[pltpu.VMEM((4, 128, 128), F32)],
        compiler_params=_cparams(1),
    )(pre, dtr, bias_row, a_row, d_exp)


def _ssd_bwd(pre, dtr, bias_row, a_row, d_exp, states, dy, *, name):
    rows = pre.shape[0]
    nc = rows // BLOCK

    def body(pre_ref, dtr_ref, bias_ref, a_ref, dexp_ref, st_ref, dy_ref,
             dpre_ref, ddtr_ref, dbias_ref, da_ref, dd_ref, dstate, q_buf, dx_buf):
        step = pl.program_id(0)
        c = nc - 1 - step

        @pl.when(step == 0)
        def _():
            dstate[...] = jnp.zeros_like(dstate)
            dbias_ref[...] = jnp.zeros_like(dbias_ref)
            da_ref[...] = jnp.zeros_like(da_ref)
            dd_ref[...] = jnp.zeros_like(dd_ref)

        r_i, c_i, expand = _ssd_consts()
        lane_lo = c_i < 64
        last_row = r_i == BLOCK - 1
        pre_v = pre_ref[...]
        (lower, sig_pre, xbc, xs, valid, dt_in, dtv, acs, acs_t, dt_exp, ea, e_l, ea_l) = _ssd_common(
            pre_v, dtr_ref[...], bias_ref[...], a_ref[...], c, r_i, c_i, expand)
        dsilu = sig_pre * (1.0 + pre_v * (1.0 - sig_pre))
        xin = xs * dt_exp
        dyv = dy_ref[...]
        d_acs_diag = jnp.zeros((BLOCK, BLOCK), F32)
        for g in range(2):
            bg = xbc[:, 512 + 128 * g:640 + 128 * g]
            cg = xbc[:, 768 + 128 * g:896 + 128 * g]
            bb = bg.astype(BF16)
            cbf = cg.astype(BF16)
            cb = _dot_nt(cbf, bb)
            ct = cg.T.astype(BF16)
            dcb = jnp.zeros((BLOCK, BLOCK), F32)
            dbg = jnp.zeros((BLOCK, BLOCK), F32)
            dcg = jnp.zeros((BLOCK, BLOCK), F32)
            for pp in range(2):
                p = 2 * g + pp
                sl = slice(128 * p, 128 * p + 128)
                xp = xin[:, sl]
                xb = xp.astype(BF16)
                dyp = dyv[:, sl]
                dyb = dyp.astype(BF16)
                dxs_ = []
                for hh in range(2):
                    dec = _decay(acs, acs_t, 2 * p + hh, lower)
                    wm = cb * dec
                    dxs_.append(_dot(wm.T.astype(BF16), dyb))
                    half = lane_lo if hh == 0 else jnp.logical_not(lane_lo)
                    dwm = _dot_nt(jnp.where(half, dyp, 0.0).astype(BF16), xb)
                    dcb = dcb + dwm * dec
                    dseg = dwm * wm
                    dcol = jnp.sum(dseg, axis=1, keepdims=True) - jnp.sum(dseg.T, axis=1, keepdims=True)
                    d_acs_diag = jnp.where(c_i == 2 * p + hh, dcol, d_acs_diag)
                dxdiag =jnp.where(lane_lo, dxs_[0], dxs_[1])
                s_in = st_ref[0, p]
                sb = s_in.astype(BF16)
                ds_out = dstate[p]
                dsb = ds_out.astype(BF16)
                yoff = ea[:, sl] * _dot(cbf, sb)
                dxst = e_l[:, sl] * _dot(bb, dsb)
                dxp = dxdiag + dxst
                dye = dyp * ea[:, sl]
                dyeb = dye.astype(BF16)
                qp = dyp * yoff - xp * dxst
                lastv = (jnp.sum(xp * dxst, axis=0, keepdims=True)
                         + ea_l[:, sl] * jnp.sum(ds_out * s_in, axis=0, keepdims=True))
                q_buf[:, sl] = jnp.where(last_row, qp + lastv, qp)
                dx_buf[:, sl] = dxp
                dcg = dcg + _dot_nt(dyeb, sb)
                dbg = dbg + _dot_nt((xp * e_l[:, sl]).astype(BF16), dsb)
                dstate[p] = ea_l[:, sl] * ds_out + _dot(ct, dyeb)
            dcg = dcg + _dot(dcb.astype(BF16), bb)
            dbg = dbg + _dot(dcb.T.astype(BF16), cbf)
            bsl = slice(512 + 128 * g, 640 + 128 * g)
            csl = slice(768 + 128 * g, 896 + 128 * g)
            dpre_ref[:, bsl] = dbg * dsilu[:, bsl]
            dpre_ref[:, csl] = dcg * dsilu[:, csl]
        dxall = dx_buf[...]
        dpre_ref[:, :SSD_WIDTH] = (dyv * dexp_ref[...] + dxall * dt_exp) * dsilu[:, :SSD_WIDTH]
        dd_ref[...] += jnp.sum(dyv * xs, axis=0, keepdims=True)
        d_acs = d_acs_diag + _hilo_dot_nt(q_buf[...], expand)
        dd_a = _hilo_dot_r(_ones_where(r_i <= c_i), d_acs)
        ddt = dd_a * a_ref[...] + _hilo_dot_nt(dxall * xs, expand)
        ddt = jnp.where(valid, ddt, 0.0)
        da_ref[...] += jnp.sum(dd_a * dtv, axis=0, keepdims=True)
        ddtr = ddt * _sigmoid(dt_in)
        ddtr_ref[...] = ddtr
        dbias_ref[...] += jnp.sum(ddtr, axis=0, keepdims=True)

    vec = pl.BlockSpec((1, 128), lambda s: (0, 0))
    wide = pl.BlockSpec((1, SSD_WIDTH), lambda s: (0, 0))
    rev = lambda s: (nc - 1 - s, 0)
    return pl.pallas_call(
        body, name=name, grid=(nc,),
        in_specs=[pl.BlockSpec((BLOCK, SSD_XBC), rev), pl.BlockSpec((BLOCK, 128), rev), vec, vec, wide,
                  pl.BlockSpec((1, 4, 128, 128), lambda s: (nc - 1 - s, 0, 0, 0)),
                  pl.BlockSpec((BLOCK, SSD_WIDTH), rev)],
        out_specs=[pl.BlockSpec((BLOCK, SSD_XBC), rev), pl.BlockSpec((BLOCK, 128), rev), vec, vec, wide],
        out_shape=[jax.ShapeDtypeStruct((rows, SSD_XBC), F32), jax.ShapeDtypeStruct((rows, 128), F32),
                   jax.ShapeDtypeStruct((1, 128), F32), jax.ShapeDtypeStruct((1, 128), F32),
                   jax.ShapeDtypeStruct((1, SSD_WIDTH), F32)],
        scratch_shapes=[pltpu.VMEM((4, 128, 128), F32), pltpu.VMEM((BLOCK, SSD_WIDTH), F32),
                        pltpu.VMEM((BLOCK, SSD_WIDTH), F32)],
        compiler_params=_cparams(1),
    )(pre, dtr, bias_row, a_row, d_exp, states, dy)


def _peer(xi, yi, ci, k):
    px = (1 - xi) if (k >> 2) & 1 else xi
    py = (1 - yi) if (k >> 1) & 1 else yi
    pc = (1 - ci) if k & 1 else ci
    return (px, py, pc), 4 * px + 2 * py + pc


def _exchange(x, *, gather, name):
    out_shape = (N_DEV,) + x.shape if gather else x.shape

    def body(x_ref, o_ref, send_sems, recv_sems, local_sem):
        xi, yi, ci = lax.axis_index("x"), lax.axis_index("y"), lax.axis_index("c")
        me = 4 * xi + 2 * yi + ci

        def src(idx):
            return x_ref if gather else x_ref.at[idx]

        local = pltpu.make_async_copy(src(me), o_ref.at[me], local_sem)
        local.start()
        sends = []
        for k in range(1, N_DEV):
            peer, pidx = _peer(xi, yi, ci, k)
            cp = pltpu.make_async_remote_copy(
                src_ref=src(pidx), dst_ref=o_ref.at[me], send_sem=send_sems.at[k - 1],
                recv_sem=recv_sems.at[k - 1], device_id=peer, device_id_type=pl.DeviceIdType.MESH)
            cp.start()
            sends.append(cp)
        for k in range(1, N_DEV):
            peer, pidx = _peer(xi, yi, ci, k)
            pltpu.make_async_remote_copy(
                src_ref=src(pidx), dst_ref=o_ref.at[pidx], send_sem=send_sems.at[k - 1],
                recv_sem=recv_sems.at[k - 1], device_id=peer, device_id_type=pl.DeviceIdType.MESH).wait_recv()
        for cp in sends:
            cp.wait_send()
        local.wait()

    return pl.pallas_call(
        body, name=name, out_shape=jax.ShapeDtypeStruct(out_shape, x.dtype),
        in_specs=[pl.BlockSpec(memory_space=pltpu.HBM)], out_specs=pl.BlockSpec(memory_space=pltpu.HBM),
        scratch_shapes=[pltpu.SemaphoreType.DMA((N_DEV - 1,)), pltpu.SemaphoreType.DMA((N_DEV - 1,)),
                        pltpu.SemaphoreType.DMA],
    )(x)


def _adamw(g8, w, m, v, *, name):
    rows = w.shape[0]
    tr = _tile(rows, 1024, 8)

    def body(g_ref, w_ref, m_ref, v_ref, go_ref, d_ref, mo_ref, vo_ref):
        g = g_ref[0]
        for j in range(1, N_DEV):
            g = g + g_ref[j]
        m2 = ADAM_B1 * m_ref[...] + (1.0 - ADAM_B1) * g
        v2 = ADAM_B2 * v_ref[...] + (1.0 - ADAM_B2) * (g * g)
        m_hat = m2 / (1.0 - ADAM_B1 ** ADAM_STEP)
        v_hat = v2 / (1.0 - ADAM_B2 ** ADAM_STEP)
        go_ref[...] = g
        d_ref[...] = -ADAM_LR * (m_hat / (jnp.sqrt(v_hat) + ADAM_EPS) + ADAM_WD * w_ref[...])
        mo_ref[...] = m2
        vo_ref[...] = v2

    spec = pl.BlockSpec((tr, 128), lambda i: (i, 0))
    sds = jax.ShapeDtypeStruct((rows, 128), F32)
    return pl.pallas_call(
        body, name=name, grid=(rows // tr,),
        in_specs=[pl.BlockSpec((N_DEV, tr, 128), lambda i: (0, i, 0)), spec, spec, spec],
        out_specs=[spec, spec, spec, spec], out_shape=[sds, sds, sds, sds], compiler_params=_cparams(1),
    )(g8, w, m, v)


SHARDED = (("meta_tokens", 1), ("w_in", 2), ("ssd_conv_w", 2), ("mla_w_uq", 2), ("mla_w_ukv", 2),
           ("w_out", 1), ("ffn_w_up", 2), ("ffn_conv_w", 2), ("ffn_w_down", 1))
BIG = ("w_in", "w_out", "ffn_w_up", "ffn_w_down")
SMALL = ("meta_tokens", "ssd_conv_w", "mla_w_uq", "mla_w_ukv", "ffn_conv_w")
REPLICATED = ("norm_mix_g", "ssd_conv_b", "ssd_dt_bias", "ssd_a_log", "ssd_d", "ssd_norm_g", "sb_norm_g",
              "mla_q_norm_g", "mla_kv_norm_g", "mla_norm_g", "norm_ffn_g", "ffn_conv_b", "final_norm_g")
WEIGHTS = ("meta_tokens", "norm_mix_g", "w_in", "ssd_conv_w", "ssd_conv_b", "ssd_dt_bias", "ssd_a_log", "ssd_d",
           "ssd_norm_g", "sb_norm_g", "mla_q_norm_g", "mla_kv_norm_g", "mla_w_uq", "mla_w_ukv", "mla_norm_g",
           "w_out", "norm_ffn_g", "ffn_w_up", "ffn_conv_w", "ffn_conv_b", "ffn_w_down", "final_norm_g")
FLAT_ALIGN = 128 * 1024


def _flat_pack(arrays, dtype, align):
    flat = jnp.concatenate([a.reshape(-1).astype(dtype) for a in arrays])
    pad = (-flat.shape[0]) % align
    return jnp.pad(flat, (0, pad)).reshape(-1, 128)


def _pieces(full, axis):
    shp = full.shape
    t = full.reshape(shp[:axis] + (N_DEV, shp[axis] // N_DEV) + shp[axis + 1:])
    return jnp.moveaxis(t, axis, 0).reshape(N_DEV, -1)


def _unpieces(p8, shard_shape, axis):
    t = p8.reshape((N_DEV,) + shard_shape)
    t = jnp.moveaxis(t, 0, axis)
    return t.reshape(shard_shape[:axis] + (N_DEV * shard_shape[axis],) + shard_shape[axis + 1:])


def _gather_weights(shards):
    full = {}
    for names, dtype, tag in ((BIG, BF16, "gather_big"), (SMALL, F32, "gather_small")):
        packed = _flat_pack([shards[n] for n in names], dtype, 16 * 128)
        got = _exchange(packed, gather=True, name=tag).reshape(N_DEV, -1)
        off = 0
        for n in names:
            size = math.prod(shards[n].shape)
            axis = dict(SHARDED)[n]
            full[n] = _unpieces(got[:, off:off + size], shards[n].shape, axis)
            off += size
    return full


def _pad_cols(a, width):
    return jnp.pad(a, ((0, 0), (0, width - a.shape[1])))


def _w_in_padded(w):
    kr = w[:, 2632:2664]
    return jnp.concatenate([
        w[:, 0:512], w[:, 512:1536], w[:, 1544:2312], _pad_cols(w[:, 2312:2504], 256), w[:, 2504:2632],
        _pad_cols(kr[:, :16], 64), _pad_cols(kr[:, 16:], 64), _pad_cols(w[:, 1536:1544], 128),
        jnp.zeros((w.shape[0], 128), w.dtype)], axis=1)


def _w_in_unpadded(wp):
    return jnp.concatenate([
        wp[:, 0:512], wp[:, 512:1536], wp[:, OFF_DT:OFF_DT + 8], wp[:, 1536:2304], wp[:, OFF_QA:OFF_QA + 192],
        wp[:, OFF_CKV:OFF_CKV + 128], wp[:, OFF_KR:OFF_KR + 16], wp[:, OFF_KR + 64:OFF_KR + 80]], axis=1)


def _w_uq_perm(w):
    t = w.reshape(MLA_Q_RANK, MLA_HEADS, MLA_NOPE + MLA_ROPE)
    out = jnp.concatenate([t[:, :, :64].reshape(MLA_Q_RANK, 256), t[:, :, 64:80].reshape(MLA_Q_RANK, 64),
                           t[:, :, 80:96].reshape(MLA_Q_RANK, 64)], axis=1)
    return jnp.pad(out, ((0, 256 - MLA_Q_RANK), (0, 0)))


def _w_uq_unperm(wp):
    wp = wp[:MLA_Q_RANK]
    t = jnp.concatenate([wp[:, :256].reshape(MLA_Q_RANK, 4, 64), wp[:, 256:320].reshape(MLA_Q_RANK, 4, 16),
                         wp[:, 320:384].reshape(MLA_Q_RANK, 4, 16)], axis=2)
    return t.reshape(MLA_Q_RANK, 4 * 96)


def _w_ukv_perm(w):
    t = w.reshape(MLA_KV_RANK, MLA_HEADS, 128)
    return jnp.concatenate([t[:, :, :64].reshape(MLA_KV_RANK, 256), t[:, :, 64:].reshape(MLA_KV_RANK, 256)], axis=1)


def _w_ukv_unperm(wp):
    t = jnp.concatenate([wp[:, :256].reshape(MLA_KV_RANK, 4, 64), wp[:, 256:].reshape(MLA_KV_RANK, 4, 64)], axis=2)
    return t.reshape(MLA_KV_RANK, 512)


def _heads(a, hd):
    return jnp.moveaxis(a.reshape(a.shape[0], -1, hd), 1, 0)


def _unheads(a):
    return jnp.moveaxis(a, 0, 1).reshape(a.shape[1], -1)


def _row(v, width=None):
    v = v.reshape(1, -1)
    return v if width is None else _pad_cols(v, width)


def _rope_tables(rows):
    pos = jnp.arange(rows, dtype=F32) - float(N_PAD)
    inv = 1.0 / (ROPE_BASE ** (jnp.arange(0, MLA_ROPE, 2, dtype=F32) / MLA_ROPE))
    ang = pos[:, None] * inv[None, :]
    cos = jnp.tile(jnp.cos(ang), (1, 8))
    sin = jnp.tile(jnp.sin(ang), (1, 4))
    return cos, jnp.concatenate([-sin, sin], axis=1)


def _layer_fwd(h, p, cos_t, sin_t, tag):
    s = {"h_in": h}
    hn = _rmsnorm_fwd(h, p["norm_mix_g"], width=D_MODEL, name=tag + "norm_mix")
    u = _matmul(hn, p["w_in"], name=tag + "in_proj")
    s["hn"], s["u"] = hn, u

    xbc_in = u[:, OFF_XBC:OFF_XBC + SSD_XBC]
    pre = _dwconv_fwd(xbc_in, p["ssd_conv_w"], p["ssd_conv_b"], taps=SSD_CONV, name=tag + "ssd_conv")
    dtr = u[:, OFF_DT:OFF_DT + 128]
    y_ssd, states = _ssd_fwd(pre, dtr, p["dt_bias"], p["a_row"], p["d_exp"], name=tag + "ssd_core")
    zgate = u[:, OFF_Z:OFF_Z + SSD_WIDTH]
    yn_ssd = _rmsnorm_fwd(y_ssd, p["ssd_norm_g"], width=SSD_WIDTH, z=zgate, name=tag + "ssd_norm")
    s.update(xbc_in=xbc_in, pre=pre, dtr=dtr, y_ssd=y_ssd, states=states, zgate=zgate)

    q_sb = _heads(u[:, OFF_QSB:OFF_QSB + SB_WIDTH], SB_HEAD_DIM)
    k_sb = _heads(u[:, OFF_KSB:OFF_KSB + SB_WIDTH], SB_HEAD_DIM)
    v_sb = _heads(u[:, OFF_VSB:OFF_VSB + SB_WIDTH], SB_HEAD_DIM)
    o_sb, u_tot = _sb_fwd(q_sb, k_sb, v_sb, name=tag + "sb_attn")
    o_sb_flat = _unheads(o_sb)
    yn_sb = _rmsnorm_fwd(o_sb_flat, p["sb_norm_g"], width=SB_WIDTH, name=tag + "sb_norm")
    s.update(q_sb=q_sb, k_sb=k_sb, v_sb=v_sb, u_tot=u_tot, o_sb_flat=o_sb_flat)

    qa = u[:, OFF_QA:OFF_QA + 256]
    ckv = u[:, OFF_CKV:OFF_CKV + 128]
    qa_n = _rmsnorm_fwd(qa, p["mla_q_norm_g"], width=MLA_Q_RANK, name=tag + "mla_qnorm")
    ckv_n = _rmsnorm_fwd(ckv, p["mla_kv_norm_g"], width=MLA_KV_RANK, name=tag + "mla_kvnorm")
    qf = _matmul(qa_n, p["mla_w_uq"], name=tag + "mla_uq")
    kvf = _matmul(ckv_n, p["mla_w_ukv"], name=tag + "mla_ukv")
    q_rope = _rope(qf[:, 256:384], cos_t, sin_t, name=tag + "rope_q")
    k_rope = _rope(u[:, OFF_KR:OFF_KR + 128], cos_t, sin_t, name=tag + "rope_k")
    rows = h.shape[0]
    zpad = jnp.zeros((MLA_HEADS, rows, 32), F32)
    qh = jnp.concatenate([_heads(qf[:, :256], 64), _heads(q_rope[:, :64], 16), _heads(q_rope[:, 64:], 16), zpad], axis=2)
    kr_b = jnp.broadcast_to(jnp.concatenate([k_rope[:, 0:16], k_rope[:, 64:80]], axis=1)[None], (MLA_HEADS, rows, 32))
    kh = jnp.concatenate([_heads(kvf[:, :256], 64), kr_b, zpad], axis=2)
    vh = _heads(kvf[:, 256:], 64)
    o_mla, lse = _mla_fwd(qh, kh, vh, name=tag + "mla_attn")
    o_mla_flat = _unheads(o_mla)
    yn_mla = _rmsnorm_fwd(o_mla_flat, p["mla_norm_g"], width=256, name=tag + "mla_norm")
    s.update(qa=qa, ckv=ckv, qa_n=qa_n, ckv_n=ckv_n, qh=qh, kh=kh, vh=vh, o_mla=o_mla, lse=lse,
             o_mla_flat=o_mla_flat)

    mix = jnp.concatenate([yn_ssd, yn_sb, yn_mla], axis=1)
    h_mid = _matmul(mix, p["w_out"], res=h, mask_pad=True, name=tag + "out_proj")
    hn2 = _rmsnorm_fwd(h_mid, p["norm_ffn_g"], width=D_MODEL, name=tag + "norm_ffn")
    up = _matmul(hn2, p["ffn_w_up"], tn=1408, name=tag + "ffn_up")
    fpre = _dwconv_fwd(up, p["ffn_conv_w"], p["ffn_conv_b"], taps=FFN_CONV, name=tag + "ffn_conv")
    act = _gate_fwd(fpre, name=tag + "ffn_gate")
    h_out = _matmul(act, p["ffn_w_down"], res=h_mid, mask_pad=True, tk=1408, name=tag + "ffn_down")
    s.update(mix=mix, h_mid=h_mid, hn2=hn2, up=up, fpre=fpre, act=act)
    return h_out, s


def _layer_bwd(dh_out, p, s, cos_t, sin_t, tag):
    g = {}
    rows = dh_out.shape[0]
    dact = _matmul(dh_out, p["ffn_w_down"], tb=True, tn=1408, out_dtype=BF16, name=tag + "b_down_dx")
    g["ffn_w_down"] = _matmul(s["act"], dh_out, ta=True, tm=1408, tk=640, name=tag + "b_down_dw")
    dfpre = _gate_bwd(s["fpre"], dact, name=tag + "b_gate")
    dup, dcw, dcb_ = _dwconv_bwd(dfpre, s["up"], p["ffn_conv_w"], taps=FFN_CONV, name=tag + "b_ffn_conv")
    g["ffn_conv_w"], g["ffn_conv_b"] = dcw[:FFN_CONV], dcb_[0]
    dhn2 = _matmul(dup, p["ffn_w_up"], tb=True, tk=1408, name=tag + "b_up_dx")
    g["ffn_w_up"] = _matmul(s["hn2"], dup, ta=True, tn=1408, tk=640, name=tag + "b_up_dw")
    dh_mid, _, dg = _rmsnorm_bwd(s["h_mid"], p["norm_ffn_g"], dhn2, width=D_MODEL, res=dh_out, mask_pad=True,
                                 name=tag + "b_norm_ffn")
    g["norm_ffn_g"] = dg[0]

    dmix = _matmul(dh_mid, p["w_out"], tb=True, name=tag + "b_out_dx")
    g["w_out"] = _matmul(s["mix"], dh_mid, ta=True, tk=640, name=tag + "b_out_dw")

    dy_ssd, dz, dg = _rmsnorm_bwd(s["y_ssd"], p["ssd_norm_g"], dmix[:, :SSD_WIDTH], width=SSD_WIDTH, z=s["zgate"],
                                  name=tag + "b_ssd_norm")
    g["ssd_norm_g"] = dg[0]
    dpre, ddtr, dbias, da, dd = _ssd_bwd(s["pre"], s["dtr"], p["dt_bias"], p["a_row"], p["d_exp"], s["states"],
                                         dy_ssd, name=tag + "b_ssd_core")
    g["ssd_dt_bias"] = dbias[0, :8]
    g["ssd_a_log"] = da[0, :8] * p["a_row"][0, :8]
    g["ssd_d"] = dd.reshape(8, 64).sum(axis=1)
    dxbc_in, dcw, dcb_ = _dwconv_bwd(dpre, s["xbc_in"], p["ssd_conv_w"], taps=SSD_CONV, name=tag + "b_ssd_conv")
    g["ssd_conv_w"], g["ssd_conv_b"] = dcw[:SSD_CONV], dcb_[0]

    do_sb_flat, _, dg = _rmsnorm_bwd(s["o_sb_flat"], p["sb_norm_g"], dmix[:, 512:768], width=SB_WIDTH,
                                     name=tag + "b_sb_norm")
    g["sb_norm_g"] = dg[0]
    dq_sb, dk_sb, dv_sb = _sb_bwd(s["q_sb"], s["k_sb"], s["v_sb"], _heads(do_sb_flat, SB_HEAD_DIM), s["u_tot"],
                                  name=tag + "b_sb_attn")

    do_mla_flat, _, dg = _rmsnorm_bwd(s["o_mla_flat"], p["mla_norm_g"], dmix[:, 768:1024], width=256,
                                      name=tag + "b_mla_norm")
    g["mla_norm_g"] = dg[0]
    dqh, dkh, dvh = _mla_bwd(s["qh"], s["kh"], s["vh"], s["o_mla"], s["lse"], _heads(do_mla_flat, 64),
                             name=tag + "b_mla_attn")
    dq_rope_in = jnp.concatenate([_unheads(dqh[:, :, 64:80]), _unheads(dqh[:, :, 80:96])], axis=1)
    dq_r = _rope(dq_rope_in, cos_t, sin_t, transpose=True, name=tag + "b_rope_q")
    dqf = jnp.concatenate([_unheads(dqh[:, :, :64]), dq_r], axis=1)
    dkr_sum = jnp.sum(dkh[:, :, 64:96], axis=0)
    dk_rope_in = jnp.concatenate([_pad_cols(dkr_sum[:, :16], 64), _pad_cols(dkr_sum[:, 16:], 64)], axis=1)
    dkr = _rope(dk_rope_in, cos_t, sin_t, transpose=True, name=tag + "b_rope_k")
    dkvf = jnp.concatenate([_unheads(dkh[:, :, :64]), _unheads(dvh)], axis=1)
    dqa_n = _matmul(dqf, p["mla_w_uq"], tb=True, name=tag + "b_uq_dx")
    g["mla_w_uq"] = _matmul(s["qa_n"], dqf, ta=True, tk=640, name=tag + "b_uq_dw")
    dckv_n = _matmul(dkvf, p["mla_w_ukv"], tb=True, name=tag + "b_ukv_dx")
    g["mla_w_ukv"] = _matmul(s["ckv_n"], dkvf, ta=True, tk=640, name=tag + "b_ukv_dw")
    dqa, _, dg = _rmsnorm_bwd(s["qa"], p["mla_q_norm_g"], dqa_n, width=MLA_Q_RANK, name=tag + "b_mla_qnorm")
    g["mla_q_norm_g"] = dg[0, :MLA_Q_RANK]
    dckv, _, dg = _rmsnorm_bwd(s["ckv"], p["mla_kv_norm_g"], dckv_n, width=MLA_KV_RANK, name=tag + "b_mla_kvnorm")
    g["mla_kv_norm_g"] = dg[0]

    du = jnp.concatenate([dz, dxbc_in, _unheads(dq_sb), _unheads(dk_sb), _unheads(dv_sb), dqa, dckv, dkr, ddtr,
                          jnp.zeros((rows, 128), F32)], axis=1)
    dhn = _matmul(du, p["w_in"], tb=True, name=tag + "b_in_dx")
    g["w_in"] = _matmul(s["hn"], du, ta=True, tk=640, name=tag + "b_in_dw")
    dh_in, _, dg = _rmsnorm_bwd(s["h_in"], p["norm_mix_g"], dhn, width=D_MODEL, res=dh_mid, mask_pad=True,
                                name=tag + "b_norm_mix")
    g["norm_mix_g"] = dg[0]
    return dh_in, g


def _prepare_layer(full, rep, l):
    a_row = _row(-jnp.exp(rep["ssd_a_log"][l]), 128)
    return {
        "norm_mix_g": _row(rep["norm_mix_g"][l]),
        "w_in": _w_in_padded(full["w_in"][l]),
        "ssd_conv_w": jnp.pad(full["ssd_conv_w"][l], ((0, HALO - SSD_CONV), (0, 0))),
        "ssd_conv_b": _row(rep["ssd_conv_b"][l]),
        "dt_bias": _row(rep["ssd_dt_bias"][l], 128),
        "a_row": a_row,
        "d_exp": _row(jnp.repeat(rep["ssd_d"][l], 64)),
        "ssd_norm_g": _row(rep["ssd_norm_g"][l]),
        "sb_norm_g": _row(rep["sb_norm_g"][l]),
        "mla_q_norm_g": _row(rep["mla_q_norm_g"][l], 256),
        "mla_kv_norm_g": _row(rep["mla_kv_norm_g"][l]),
        "mla_w_uq": _w_uq_perm(full["mla_w_uq"][l]),
        "mla_w_ukv": _w_ukv_perm(full["mla_w_ukv"][l]),
        "mla_norm_g": _row(rep["mla_norm_g"][l]),
        "w_out": full["w_out"][l],
        "norm_ffn_g": _row(rep["norm_ffn_g"][l]),
        "ffn_w_up": full["ffn_w_up"][l],
        "ffn_conv_w": jnp.pad(full["ffn_conv_w"][l], ((0, HALO - FFN_CONV), (0, 0))),
        "ffn_conv_b": _row(rep["ffn_conv_b"][l]),
        "ffn_w_down": full["ffn_w_down"][l],
    }


def _layer_grads_to_full(g):
    out = dict(g)
    out["w_in"] = _w_in_unpadded(g["w_in"])
    out["mla_w_uq"] = _w_uq_unperm(g["mla_w_uq"])
    out["mla_w_ukv"] = _w_ukv_unperm(g["mla_w_ukv"])
    return out


def kernel(x, meta_tokens, norm_mix_g, w_in, ssd_conv_w, ssd_conv_b, ssd_dt_bias, ssd_a_log, ssd_d, ssd_norm_g, sb_norm_g, mla_q_norm_g, mla_kv_norm_g, mla_w_uq, mla_w_ukv, mla_norm_g, w_out, norm_ffn_g, ffn_w_up, ffn_conv_w, ffn_conv_b, ffn_w_down, final_norm_g, loss_target, m_meta_tokens, m_norm_mix_g, m_w_in, m_ssd_conv_w, m_ssd_conv_b, m_ssd_dt_bias, m_ssd_a_log, m_ssd_d, m_ssd_norm_g, m_sb_norm_g, m_mla_q_norm_g, m_mla_kv_norm_g, m_mla_w_uq, m_mla_w_ukv, m_mla_norm_g, m_w_out, m_norm_ffn_g, m_ffn_w_up, m_ffn_conv_w, m_ffn_conv_b, m_ffn_w_down, m_final_norm_g, v_meta_tokens, v_norm_mix_g, v_w_in, v_ssd_conv_w, v_ssd_conv_b, v_ssd_dt_bias, v_ssd_a_log, v_ssd_d, v_ssd_norm_g, v_sb_norm_g, v_mla_q_norm_g, v_mla_kv_norm_g, v_mla_w_uq, v_mla_w_ukv, v_mla_norm_g, v_w_out, v_norm_ffn_g, v_ffn_w_up, v_ffn_conv_w, v_ffn_conv_b, v_ffn_w_down, v_final_norm_g):
    w = dict(meta_tokens=meta_tokens, norm_mix_g=norm_mix_g, w_in=w_in, ssd_conv_w=ssd_conv_w, ssd_conv_b=ssd_conv_b,
             ssd_dt_bias=ssd_dt_bias, ssd_a_log=ssd_a_log, ssd_d=ssd_d, ssd_norm_g=ssd_norm_g, sb_norm_g=sb_norm_g,
             mla_q_norm_g=mla_q_norm_g, mla_kv_norm_g=mla_kv_norm_g, mla_w_uq=mla_w_uq, mla_w_ukv=mla_w_ukv,
             mla_norm_g=mla_norm_g, w_out=w_out, norm_ffn_g=norm_ffn_g, ffn_w_up=ffn_w_up, ffn_conv_w=ffn_conv_w,
             ffn_conv_b=ffn_conv_b, ffn_w_down=ffn_w_down, final_norm_g=final_norm_g)
    mom = dict(meta_tokens=m_meta_tokens, norm_mix_g=m_norm_mix_g, w_in=m_w_in, ssd_conv_w=m_ssd_conv_w,
               ssd_conv_b=m_ssd_conv_b, ssd_dt_bias=m_ssd_dt_bias, ssd_a_log=m_ssd_a_log, ssd_d=m_ssd_d,
               ssd_norm_g=m_ssd_norm_g, sb_norm_g=m_sb_norm_g, mla_q_norm_g=m_mla_q_norm_g,
               mla_kv_norm_g=m_mla_kv_norm_g, mla_w_uq=m_mla_w_uq, mla_w_ukv=m_mla_w_ukv, mla_norm_g=m_mla_norm_g,
               w_out=m_w_out, norm_ffn_g=m_norm_ffn_g, ffn_w_up=m_ffn_w_up, ffn_conv_w=m_ffn_conv_w,
               ffn_conv_b=m_ffn_conv_b, ffn_w_down=m_ffn_w_down, final_norm_g=m_final_norm_g)
    vel = dict(meta_tokens=v_meta_tokens, norm_mix_g=v_norm_mix_g, w_in=v_w_in, ssd_conv_w=v_ssd_conv_w,
               ssd_conv_b=v_ssd_conv_b, ssd_dt_bias=v_ssd_dt_bias, ssd_a_log=v_ssd_a_log, ssd_d=v_ssd_d,
               ssd_norm_g=v_ssd_norm_g, sb_norm_g=v_sb_norm_g, mla_q_norm_g=v_mla_q_norm_g,
               mla_kv_norm_g=v_mla_kv_norm_g, mla_w_uq=v_mla_w_uq, mla_w_ukv=v_mla_w_ukv, mla_norm_g=v_mla_norm_g,
               w_out=v_w_out, norm_ffn_g=v_norm_ffn_g, ffn_w_up=v_ffn_w_up, ffn_conv_w=v_ffn_conv_w,
               ffn_conv_b=v_ffn_conv_b, ffn_w_down=v_ffn_w_down, final_norm_g=v_final_norm_g)

    full = _gather_weights({n: w[n] for n, _ in SHARDED})
    layers = [_prepare_layer(full, w, l) for l in range(DEPTH)]

    seq = x.shape[1]
    rows = BLOCK + seq
    cos_t, sin_t = _rope_tables(rows)
    h = jnp.concatenate([jnp.zeros((N_PAD, D_MODEL), F32), full["meta_tokens"], x[0]], axis=0)

    saved = []
    for l in range(DEPTH):
        h, s = _layer_fwd(h, layers[l], cos_t, sin_t, "l%d_" % l)
        saved.append(s)
    dh, dg_final, loss_part = _final_loss(h, _row(final_norm_g), loss_target[0], name="final_loss")
    loss = lax.psum(loss_part[0, 0], ("x", "y", "c"))

    layer_grads = [None] * DEPTH
    for l in reversed(range(DEPTH)):
        dh, g = _layer_bwd(dh, layers[l], saved[l], cos_t, sin_t, "l%d_" % l)
        layer_grads[l] = _layer_grads_to_full(g)
    grad_x = dh[BLOCK:][None]

    partial = {n: jnp.stack([layer_grads[l][n] for l in range(DEPTH)]) for n in layer_grads[0]}
    partial["meta_tokens"] = dh[N_PAD:BLOCK]
    partial["final_norm_g"] = dg_final[0]

    send = jnp.concatenate([_pieces(partial[n], axis) for n, axis in SHARDED], axis=1)
    n_sh = send.shape[1]
    pad = (-n_sh) % FLAT_ALIGN
    send = jnp.pad(send, ((0, 0), (0, pad))).reshape(N_DEV, -1, 128)
    got = _exchange(send, gather=False, name="grad_all_to_all")
    pack = lambda d: _flat_pack([d[n] for n, _ in SHARDED], F32, FLAT_ALIGN)
    sh_out = _adamw(got, pack(w), pack(mom), pack(vel), name="adamw_sharded")

    rep_g = _flat_pack([partial[n] for n in REPLICATED], F32, 8 * 128)
    got_r = _exchange(rep_g, gather=True, name="grad_all_gather")
    packr = lambda d: _flat_pack([d[n] for n in REPLICATED], F32, 8 * 128)
    rep_out = _adamw(got_r, packr(w), packr(mom), packr(vel), name="adamw_replicated")

    results = [dict(), dict(), dict(), dict()]
    for names, outs in (([n for n, _ in SHARDED], sh_out), (list(REPLICATED), rep_out)):
        off = 0
        for n in names:
            size = math.prod(w[n].shape)
            for kind in range(4):
                results[kind][n] = outs[kind].reshape(-1)[off:off + size].reshape(w[n].shape)
            off += size

    return (loss, grad_x, *[results[0][n] for n in WEIGHTS], *[results[1][n] for n in WEIGHTS],
            *[results[2][n] for n in WEIGHTS], *[results[3][n] for n in WEIGHTS])
```

```python
import math

import jax
import jax.numpy as jnp
from jax import lax
from jax.experimental import pallas as pl
from jax.experimental.pallas import tpu as pltpu

F32 = jnp.float32
BF16 = jnp.bfloat16

D_MODEL = 1024
DEPTH = 2
N_META = 16
BLOCK = 128
N_PAD = BLOCK - N_META
EPS = 1e-6
SSD_WIDTH = 512
SSD_XBC = 1024
SSD_CONV = 4
SB_WIDTH = 256
SB_HEAD_DIM = 64
MLA_HEADS = 4
MLA_NOPE = 64
MLA_ROPE = 32
MLA_Q_RANK = 192
MLA_KV_RANK = 128
ROPE_BASE = 10000.0
D_FF = 2816
FFN_CONV = 3
IN_COLS = 2664
N_DEV = 8

ADAM_LR = 0.001
ADAM_B1 = 0.9
ADAM_B2 = 0.999
ADAM_EPS = 1e-08
ADAM_WD = 0.01
ADAM_STEP = 10

U_COLS = 3072
OFF_Z, OFF_XBC, OFF_QSB, OFF_KSB, OFF_VSB, OFF_QA, OFF_CKV, OFF_KR, OFF_DT = (
    0, 512, 1536, 1792, 2048, 2304, 2560, 2688, 2816)

V7X_VMEM_BYTES = 64 * 1024 * 1024
VMEM_LIMIT = (V7X_VMEM_BYTES * 3) // 4
NEG_BIG = -1e30


def _cparams(n_axes):
    return pltpu.CompilerParams(dimension_semantics=("arbitrary",) * n_axes, vmem_limit_bytes=VMEM_LIMIT)


def _tile(n, target, align):
    best = None
    for d in range(align, min(n, target) + 1, align):
        if n % d == 0:
            best = d
    return n if best is None else best


def _sigmoid(x):
    return 1.0 / (1.0 + jnp.exp(-x))


def _softplus(x):
    return jnp.maximum(x, 0.0) + jnp.log(1.0 + jnp.exp(-jnp.abs(x)))


def _dot(a, b):
    return jnp.dot(a, b, preferred_element_type=F32)


def _dot_nt(a, b):
    return lax.dot_general(a, b, (((1,), (1,)), ((), ())), preferred_element_type=F32)


def _hilo(x):
    hi = x.astype(BF16)
    lo = (x - hi.astype(F32)).astype(BF16)
    return hi, lo


def _hilo_dot_l(x, m):
    hi, lo = _hilo(x)
    return _dot(hi, m) + _dot(lo, m)


def _hilo_dot_r(m, x):
    hi, lo = _hilo(x)
    return _dot(m, hi) + _dot(m, lo)


def _hilo_dot_nt(x, m):
    hi, lo = _hilo(x)
    return _dot_nt(hi, m) + _dot_nt(lo, m)


def _ones_where(cond):
    return jnp.where(cond, 1.0, 0.0).astype(BF16)


def _matmul(a, b, *, name, ta=False, tb=False, out_dtype=F32, res=None, mask_pad=False,
            tm=640, tn=1024, tk=1024):
    m_dim = a.shape[1] if ta else a.shape[0]
    k_dim = a.shape[0] if ta else a.shape[1]
    n_dim = b.shape[0] if tb else b.shape[1]
    assert (b.shape[1] if tb else b.shape[0]) == k_dim
    tm = _tile(m_dim, tm, 128)
    tn = _tile(n_dim, tn, 128)
    tk = _tile(k_dim, tk, 128)
    nk = k_dim // tk
    dn = (((0 if ta else 1,), (1 if tb else 0,)), ((), ()))

    def body(*refs):
        if res is not None:
            a_ref, b_ref, r_ref, o_ref, acc = refs
        else:
            a_ref, b_ref, o_ref, acc = refs
        k = pl.program_id(2)

        @pl.when(k == 0)
        def _():
            acc[...] = jnp.zeros_like(acc)

        acc[...] += lax.dot_general(a_ref[...].astype(BF16), b_ref[...].astype(BF16), dn,
                                    preferred_element_type=F32)

        @pl.when(k == nk - 1)
        def _():
            r = acc[...]
            if res is not None:
                r = r + r_ref[...].astype(F32)
            if mask_pad:
                rows = pl.program_id(0) * tm + lax.broadcasted_iota(jnp.int32, (tm, 1), 0)
                r = jnp.where(rows >= N_PAD, r, 0.0)
            o_ref[...] = r.astype(out_dtype)

    a_spec = pl.BlockSpec((tk, tm), lambda i, j, k: (k, i)) if ta else pl.BlockSpec((tm, tk), lambda i, j, k: (i, k))
    b_spec = pl.BlockSpec((tn, tk), lambda i, j, k: (j, k)) if tb else pl.BlockSpec((tk, tn), lambda i, j, k: (k, j))
    o_spec = pl.BlockSpec((tm, tn), lambda i, j, k: (i, j))
    in_specs = [a_spec, b_spec]
    args = [a, b]
    if res is not None:
        in_specs.append(o_spec)
        args.append(res)
    return pl.pallas_call(
        body, name=name, grid=(m_dim // tm, n_dim // tn, nk),
        in_specs=in_specs, out_specs=o_spec,
        out_shape=jax.ShapeDtypeStruct((m_dim, n_dim), out_dtype),
        scratch_shapes=[pltpu.VMEM((tm, tn), F32)],
        compiler_params=_cparams(3),
    )(*args)


def _rmsnorm_fwd(x, g, *, width, name, z=None, out_dtype=None):
    out_dtype = BF16 if out_dtype is None else out_dtype
    rows, w = x.shape
    tr = _tile(rows, 640, 128)
    inv_w = 1.0 / width

    def body(*refs):
        if z is not None:
            x_ref, z_ref, g_ref, o_ref = refs
        else:
            x_ref, g_ref, o_ref = refs
        t = x_ref[...].astype(F32)
        if z is not None:
            zz = z_ref[...]
            t = t * (zz * _sigmoid(zz))
        ms = jnp.sum(t * t, axis=-1, keepdims=True) * inv_w
        o_ref[...] = ((t * lax.rsqrt(ms + EPS)) * g_ref[...]).astype(out_dtype)

    row_spec = pl.BlockSpec((tr, w), lambda i: (i, 0))
    g_spec = pl.BlockSpec((1, w), lambda i: (0, 0))
    in_specs = [row_spec] + ([row_spec] if z is not None else []) + [g_spec]
    args = [x] + ([z] if z is not None else []) + [g]
    return pl.pallas_call(
        body, name=name, grid=(rows // tr,), in_specs=in_specs, out_specs=row_spec,
        out_shape=jax.ShapeDtypeStruct((rows, w), out_dtype), compiler_params=_cparams(1),
    )(*args)


def _rmsnorm_bwd(x, g, dout, *, width, name, z=None, res=None, mask_pad=False):
    rows, w = x.shape
    tr = _tile(rows, 640, 128)
    inv_w = 1.0 / width

    def body(*refs):
        refs = list(refs)
        x_ref = refs.pop(0)
        z_ref = refs.pop(0) if z is not None else None
        g_ref = refs.pop(0)
        do_ref = refs.pop(0)
        r_ref = refs.pop(0) if res is not None else None
        dx_ref = refs.pop(0)
        dz_ref = refs.pop(0) if z is not None else None
        dg_ref = refs.pop(0)
        i = pl.program_id(0)

        @pl.when(i == 0)
        def _():
            dg_ref[...] = jnp.zeros_like(dg_ref)

        xv = x_ref[...].astype(F32)
        t = xv
        if z is not None:
            zz = z_ref[...]
            sig = _sigmoid(zz)
            sl = zz * sig
            t = xv * sl
        ms = jnp.sum(t * t, axis=-1, keepdims=True) * inv_w
        rstd = lax.rsqrt(ms + EPS)
        xhat = t * rstd
        do = do_ref[...].astype(F32)
        dxh = do * g_ref[...]
        c = jnp.sum(dxh * xhat, axis=-1, keepdims=True) * inv_w
        dt = rstd * (dxh - xhat * c)
        dg_ref[...] += jnp.sum(do * xhat, axis=0, keepdims=True)
        if z is not None:
            dz_ref[...] = dt * xv * (sig * (1.0 + zz * (1.0 - sig)))
            dx = dt * sl
        else:
            dx = dt
        if res is not None:
            dx = dx + r_ref[...]
        if mask_pad:
            rws = i * tr + lax.broadcasted_iota(jnp.int32, (tr, 1), 0)
            dx = jnp.where(rws >= N_PAD, dx, 0.0)
        dx_ref[...] = dx

    row_spec = pl.BlockSpec((tr, w), lambda i: (i, 0))
    g_spec = pl.BlockSpec((1, w), lambda i: (0, 0))
    in_specs = [row_spec] + ([row_spec] if z is not None else []) + [g_spec, row_spec] + (
        [row_spec] if res is not None else [])
    args = [x] + ([z] if z is not None else []) + [g, dout] + ([res] if res is not None else [])
    out_specs = [row_spec] + ([row_spec] if z is not None else []) + [g_spec]
    out_shape = [jax.ShapeDtypeStruct((rows, w), F32)] + (
        [jax.ShapeDtypeStruct((rows, w), F32)] if z is not None else []) + [jax.ShapeDtypeStruct((1, w), F32)]
    outs = pl.pallas_call(
        body, name=name, grid=(rows // tr,), in_specs=in_specs, out_specs=out_specs,
        out_shape=out_shape, compiler_params=_cparams(1),
    )(*args)
    if z is not None:
        return outs[0], outs[1], outs[2]
    return outs[0], None, outs[1]


def _final_loss(h, g, target, *, name):
    rows, w = h.shape
    nb = rows // BLOCK
    inv_w = 1.0 / w

    def body(h_ref, g_ref, t_ref, dh_ref, dg_ref, loss_ref):
        i = pl.program_id(0)

        @pl.when(i == 0)
        def _():
            dg_ref[...] = jnp.zeros_like(dg_ref)
            loss_ref[...] = jnp.zeros_like(loss_ref)

        xv = h_ref[...]
        ms = jnp.sum(xv * xv, axis=-1, keepdims=True) * inv_w
        rstd = lax.rsqrt(ms + EPS)
        xhat = xv * rstd
        gv = g_ref[...]
        err = jnp.where(i >= 1, xhat * gv - t_ref[...], 0.0)
        loss_ref[...] += (0.5 * inv_w) * jnp.sum(err * err)
        do = err * inv_w
        dxh = do * gv
        c = jnp.sum(dxh * xhat, axis=-1, keepdims=True) * inv_w
        dh_ref[...] = rstd * (dxh - xhat * c)
        dg_ref[...] += jnp.sum(do * xhat, axis=0, keepdims=True)

    row_spec = pl.BlockSpec((BLOCK, w), lambda i: (i, 0))
    g_spec = pl.BlockSpec((1, w), lambda i: (0, 0))
    return pl.pallas_call(
        body, name=name, grid=(nb,),
        in_specs=[row_spec, g_spec, pl.BlockSpec((BLOCK, w), lambda i: (jnp.maximum(i - 1, 0), 0))],
        out_specs=[row_spec, g_spec, pl.BlockSpec((1, 128), lambda i: (0, 0))],
        out_shape=[jax.ShapeDtypeStruct((rows, w), F32), jax.ShapeDtypeStruct((1, w), F32),
                   jax.ShapeDtypeStruct((1, 128), F32)],
        compiler_params=_cparams(1),
    )(h, g, target)


HALO = 8


def _dwconv_fwd(u, w8, b, *, taps, name):
    rows, ch = u.shape
    tb = _tile(rows, 640, 128)
    tc = _tile(ch, 512, 128)
    hb = tb // HALO

    def body(u_ref, h_ref, w_ref, b_ref, o_ref, buf):
        i = pl.program_id(0)
        buf[0:HALO, :] = jnp.where(i > 0, h_ref[...], 0.0)
        buf[HALO:HALO + tb, :] = u_ref[...]
        acc = jnp.broadcast_to(b_ref[...], (tb, tc))
        for k in range(taps):
            acc = acc + w_ref[k:k + 1, :] * buf[pl.ds(HALO - (taps - 1) + k, tb), :]
        o_ref[...] = acc

    return pl.pallas_call(
        body, name=name, grid=(rows // tb, ch // tc),
        in_specs=[pl.BlockSpec((tb, tc), lambda i, j: (i, j)),
                  pl.BlockSpec((HALO, tc), lambda i, j: (jnp.maximum(i * hb - 1, 0), j)),
                  pl.BlockSpec((HALO, tc), lambda i, j: (0, j)),
                  pl.BlockSpec((1, tc), lambda i, j: (0, j))],
        out_specs=pl.BlockSpec((tb, tc), lambda i, j: (i, j)),
        out_shape=jax.ShapeDtypeStruct((rows, ch), F32),
        scratch_shapes=[pltpu.VMEM((tb + HALO, tc), F32)],
        compiler_params=_cparams(2),
    )(u, u, w8, b)


def _dwconv_bwd(dpre, u, w8, *, taps, name):
    rows, ch = u.shape
    tb = _tile(rows, 640, 128)
    tc = _tile(ch, 512, 128)
    hb = tb // HALO
    nb = rows // tb
    last_halo = rows // HALO - 1

    def body(d_ref, dn_ref, u_ref, up_ref, w_ref, du_ref, dw_ref, db_ref, bufd, bufu):
        i = pl.program_id(1)

        @pl.when(i == 0)
        def _():
            dw_ref[...] = jnp.zeros_like(dw_ref)
            db_ref[...] = jnp.zeros_like(db_ref)

        d = d_ref[...]
        bufd[0:tb, :] = d
        bufd[tb:tb + HALO, :] = jnp.where(i < nb - 1, dn_ref[...], 0.0)
        bufu[0:HALO, :] = jnp.where(i > 0, up_ref[...], 0.0)
        bufu[HALO:HALO + tb, :] = u_ref[...]
        acc = jnp.zeros((tb, tc), F32)
        for k in range(taps):
            acc = acc + w_ref[k:k + 1, :] * bufd[pl.ds(taps - 1 - k, tb), :]
        du_ref[...] = acc
        for k in range(taps):
            dw_ref[k:k + 1, :] += jnp.sum(d * bufu[pl.ds(HALO - (taps - 1) + k, tb), :], axis=0, keepdims=True)
        db_ref[...] += jnp.sum(d, axis=0, keepdims=True)

    return pl.pallas_call(
        body, name=name, grid=(ch // tc, nb),
        in_specs=[pl.BlockSpec((tb, tc), lambda j, i: (i, j)),
                  pl.BlockSpec((HALO, tc), lambda j, i: (jnp.minimum((i + 1) * hb, last_halo), j)),
                  pl.BlockSpec((tb, tc), lambda j, i: (i, j)),
                  pl.BlockSpec((HALO, tc), lambda j, i: (jnp.maximum(i * hb - 1, 0), j)),
                  pl.BlockSpec((HALO, tc), lambda j, i: (0, j))],
        out_specs=[pl.BlockSpec((tb, tc), lambda j, i: (i, j)),
                   pl.BlockSpec((HALO, tc), lambda j, i: (0, j)),
                   pl.BlockSpec((1, tc), lambda j, i: (0, j))],
        out_shape=[jax.ShapeDtypeStruct((rows, ch), F32), jax.ShapeDtypeStruct((HALO, ch), F32),
                   jax.ShapeDtypeStruct((1, ch), F32)],
        scratch_shapes=[pltpu.VMEM((tb + HALO, tc), F32), pltpu.VMEM((tb + HALO, tc), F32)],
        compiler_params=_cparams(2),
    )(dpre, dpre, u, u, w8)


def _gate_fwd(pre, *, name):
    rows, c2 = pre.shape
    f = c2 // 2
    tb = _tile(rows, 640, 128)
    tc = _tile(f, 512, 128)
    nct = f // tc

    def body(p1_ref, p2_ref, o_ref):
        p1 = p1_ref[...]
        o_ref[...] = (p1 * _sigmoid(p1) * p2_ref[...]).astype(BF16)

    return pl.pallas_call(
        body, name=name, grid=(rows // tb, nct),
        in_specs=[pl.BlockSpec((tb, tc), lambda i, j: (i, j)),
                  pl.BlockSpec((tb, tc), lambda i, j: (i, j + nct))],
        out_specs=pl.BlockSpec((tb, tc), lambda i, j: (i, j)),
        out_shape=jax.ShapeDtypeStruct((rows, f), BF16), compiler_params=_cparams(2),
    )(pre, pre)


def _gate_bwd(pre, dact, *, name):
    rows, c2 = pre.shape
    f = c2 // 2
    tb = _tile(rows, 640, 128)
    tc = _tile(f, 512, 128)
    nct = f // tc

    def body(p1_ref, p2_ref, d_ref, o_ref):
        s = pl.program_id(1)
        p1 = p1_ref[...]
        d = d_ref[...].astype(F32)
        sig = _sigmoid(p1)
        dp1 = d * p2_ref[...] * (sig * (1.0 + p1 * (1.0 - sig)))
        dp2 = d * (p1 * sig)
        o_ref[...] = jnp.where(s == 0, dp1, dp2)

    return pl.pallas_call(
        body, name=name, grid=(rows // tb, 2, nct),
        in_specs=[pl.BlockSpec((tb, tc), lambda i, s, j: (i, j)),
                  pl.BlockSpec((tb, tc), lambda i, s, j: (i, j + nct)),
                  pl.BlockSpec((tb, tc), lambda i, s, j: (i, j))],
        out_specs=pl.BlockSpec((tb, tc), lambda i, s, j: (i, s * nct + j)),
        out_shape=jax.ShapeDtypeStruct((rows, c2), F32), compiler_params=_cparams(3),
    )(pre, pre, dact)


def _rope(xr, cos_t, sin_t, *, name, transpose=False):
    rows, w = xr.shape
    tr = _tile(rows, 640, 128)

    def body(x_ref, c_ref, s_ref, o_ref):
        xv = x_ref[...]
        if transpose:
            o_ref[...] = xv * c_ref[...] + pltpu.roll(xv * s_ref[...], 64, 1)
        else:
            o_ref[...] = xv * c_ref[...] + pltpu.roll(xv, 64, 1) * s_ref[...]

    spec = pl.BlockSpec((tr, w), lambda i: (i, 0))
    return pl.pallas_call(
        body, name=name, grid=(rows // tr,), in_specs=[spec, spec, spec], out_specs=spec,
        out_shape=jax.ShapeDtypeStruct((rows, w), F32), compiler_params=_cparams(1),
    )(xr, cos_t, sin_t)


ATT_TQ = 640


def _tile_iotas(rows=BLOCK):
    r_i = lax.broadcasted_iota(jnp.int32, (rows, BLOCK), 0)
    c_i = lax.broadcasted_iota(jnp.int32, (rows, BLOCK), 1)
    return r_i, c_i


def _key_ranges(i, tq):
    n_blocks = ((i + 1) * tq + (BLOCK - 1)) >> 7
    first_diag = jnp.maximum((i * tq) >> 7, 1)
    return first_diag, n_blocks


def _dot_tn(a, b):
    return lax.dot_general(a, b, (((0,), (0,)), ((), ())), preferred_element_type=F32)


def _sb_fwd(q, k, v, *, name):
    nh, rows, hd = q.shape
    tq = _tile(rows, ATT_TQ, 8)
    scale = SB_HEAD_DIM ** -0.5

    def body(q_ref, k_ref, v_ref, o_ref, u_ref):
        i = pl.program_id(1)
        r_i, c_i = _tile_iotas(tq)
        r_b, c_b = _tile_iotas()
        m_after = _ones_where(r_b > c_b)
        qb = q_ref[0].astype(BF16)
        rowpos = i * tq + r_i
        first_diag, n_blocks = _key_ranges(i, tq)

        def step(j, carry, masked):
            acc, cu = carry
            off = pl.multiple_of(j * BLOCK, BLOCK)
            kb = k_ref[0, pl.ds(off, BLOCK), :].astype(BF16)
            vb = v_ref[0, pl.ds(off, BLOCK), :].astype(BF16)
            zs = _dot_nt(qb, kb) * scale
            sp = _softplus(zs)
            if masked:
                colpos = off + c_i
                mask = (colpos < rowpos) & (colpos >= N_PAD)
                uu = jnp.where(mask, -sp, 0.0)
            else:
                uu = -sp
            after = cu + _hilo_dot_l(uu, m_after)
            wgt = jnp.exp(zs - sp + after)
            if masked:
                wgt = jnp.where(mask, wgt, 0.0)
            acc = acc + _dot(wgt.astype(BF16), vb)
            cu = (after + uu)[:, 0:1]
            return acc, cu

        carry = (jnp.zeros((tq, hd), F32), jnp.zeros((tq, 1), F32))
        carry = lax.fori_loop(0, n_blocks - first_diag, lambda t, c: step(n_blocks - 1 - t, c, True), carry)
        carry = lax.fori_loop(0, first_diag - 1, lambda t, c: step(first_diag - 1 - t, c, False), carry)
        acc, cu = step(0, carry, True)
        o_ref[0] = acc
        u_ref[0] = jnp.broadcast_to(cu, (tq, 128))

    blk = pl.BlockSpec((1, tq, hd), lambda h, i: (h, i, 0))
    full = pl.BlockSpec((1, rows, hd), lambda h, i: (h, 0, 0))
    return pl.pallas_call(
        body, name=name, grid=(nh, rows // tq), in_specs=[blk, full, full],
        out_specs=[blk, pl.BlockSpec((1, tq, 128), lambda h, i: (h, i, 0))],
        out_shape=[jax.ShapeDtypeStruct((nh, rows, hd), F32), jax.ShapeDtypeStruct((nh, rows, 128), F32)],
        compiler_params=_cparams(2),
    )(q, k, v)


def _sb_bwd(q, k, v, do, u_tot, *, name):
    nh, rows, hd = q.shape
    tq = _tile(rows, ATT_TQ, 8)
    scale = SB_HEAD_DIM ** -0.5

    def body(q_ref, k_ref, v_ref, do_ref, u_ref, dq_ref, dk_ref, dv_ref):
        i = pl.program_id(1)

        @pl.when(i == 0)
        def _():
            dk_ref[...] = jnp.zeros_like(dk_ref)
            dv_ref[...] = jnp.zeros_like(dv_ref)

        r_i, c_i = _tile_iotas(tq)
        r_b, c_b = _tile_iotas()
        m_incl = _ones_where(r_b <= c_b)
        m_excl = _ones_where(r_b < c_b)
        qb = q_ref[0].astype(BF16)
        dob = do_ref[0].astype(BF16)
        utot = u_ref[0][:, 0:1]
        rowpos = i * tq + r_i
        first_diag, n_blocks = _key_ranges(i, tq)

        def step(j, carry, masked):
            dq, cp, cg = carry
            off = pl.multiple_of(j * BLOCK, BLOCK)
            kb = k_ref[0, pl.ds(off, BLOCK), :].astype(BF16)
            vb = v_ref[0, pl.ds(off, BLOCK), :].astype(BF16)
            zs = _dot_nt(qb, kb) * scale
            sp = _softplus(zs)
            sig = jnp.exp(zs - sp)
            if masked:
                colpos = off + c_i
                mask = (colpos < rowpos) & (colpos >= N_PAD)
                uu = jnp.where(mask, -sp, 0.0)
            else:
                uu = -sp
            p_incl = cp + _hilo_dot_l(uu, m_incl)
            wgt = jnp.exp(jnp.minimum(zs - sp + utot - p_incl, 0.0))
            if masked:
                wgt = jnp.where(mask, wgt, 0.0)
            gg = wgt * _dot_nt(dob, vb)
            g_ex = cg + _hilo_dot_l(gg, m_excl)
            dz = (gg * (1.0 - sig) - sig * g_ex) * scale
            if masked:
                dz = jnp.where(mask, dz, 0.0)
            dzb = dz.astype(BF16)
            dq = dq + _dot(dzb, kb)
            dk_ref[0, pl.ds(off, BLOCK), :] += _dot_tn(dzb, qb)
            dv_ref[0, pl.ds(off, BLOCK), :] += _dot_tn(wgt.astype(BF16), dob)
            cp = p_incl[:, BLOCK - 1:BLOCK]
            cg = (g_ex + gg)[:, BLOCK - 1:BLOCK]
            return dq, cp, cg

        z1 = jnp.zeros((tq, 1), F32)
        carry = step(0, (jnp.zeros((tq, hd), F32), z1, z1), True)
        carry = lax.fori_loop(1, first_diag, lambda j, c: step(j, c, False), carry)
        dq, _, _ = lax.fori_loop(first_diag, n_blocks, lambda j, c: step(j, c, True), carry)
        dq_ref[0] = dq

    blk = pl.BlockSpec((1, tq, hd), lambda h, i: (h, i, 0))
    full = pl.BlockSpec((1, rows, hd), lambda h, i: (h, 0, 0))
    ublk = pl.BlockSpec((1, tq, 128), lambda h, i: (h, i, 0))
    sds = jax.ShapeDtypeStruct((nh, rows, hd), F32)
    return pl.pallas_call(
        body, name=name, grid=(nh, rows // tq), in_specs=[blk, full, full, blk, ublk],
        out_specs=[blk, full, full], out_shape=[sds, sds, sds], compiler_params=_cparams(2),
    )(q, k, v, do, u_tot)


def _mla_fwd(q, k, v, *, name):
    nh, rows, dk = q.shape
    dv = v.shape[2]
    tq = _tile(rows, ATT_TQ, 8)
    scale = (MLA_NOPE + MLA_ROPE) ** -0.5

    def body(q_ref, k_ref, v_ref, o_ref, lse_ref):
        i = pl.program_id(1)
        r_i, c_i = _tile_iotas(tq)
        qb = q_ref[0].astype(BF16)
        rowpos = i * tq + r_i
        first_diag, n_blocks = _key_ranges(i, tq)

        def step(j, carry, masked):
            m, l, acc = carry
            off = pl.multiple_of(j * BLOCK, BLOCK)
            kb = k_ref[0, pl.ds(off, BLOCK), :].astype(BF16)
            vb = v_ref[0, pl.ds(off, BLOCK), :].astype(BF16)
            s = _dot_nt(qb, kb) * scale
            if masked:
                colpos = off + c_i
                s = jnp.where((colpos <= rowpos) & (colpos >= N_PAD), s, NEG_BIG)
            m_new = jnp.maximum(m, jnp.max(s, axis=1, keepdims=True))
            alpha = jnp.exp(m - m_new)
            p = jnp.exp(s - m_new)
            l = alpha * l + jnp.sum(p, axis=1, keepdims=True)
            acc = alpha * acc + _dot(p.astype(BF16), vb)
            return m_new, l, acc

        carry = (jnp.full((tq, 1), NEG_BIG, F32), jnp.zeros((tq, 1), F32), jnp.zeros((tq, dv), F32))
        carry = step(0, carry, True)
        carry = lax.fori_loop(1, first_diag, lambda j, c: step(j, c, False), carry)
        m, l, acc = lax.fori_loop(first_diag, n_blocks, lambda j, c: step(j, c, True), carry)
        o_ref[0] = acc / l
        lse_ref[0] = jnp.broadcast_to(m + jnp.log(l), (tq, 128))

    qblk = pl.BlockSpec((1, tq, dk), lambda h, i: (h, i, 0))
    kfull = pl.BlockSpec((1, rows, dk), lambda h, i: (h, 0, 0))
    vfull = pl.BlockSpec((1, rows, dv), lambda h, i: (h, 0, 0))
    return pl.pallas_call(
        body, name=name, grid=(nh, rows // tq), in_specs=[qblk, kfull, vfull],
        out_specs=[pl.BlockSpec((1, tq, dv), lambda h, i: (h, i, 0)),
                   pl.BlockSpec((1, tq, 128), lambda h, i: (h, i, 0))],
        out_shape=[jax.ShapeDtypeStruct((nh, rows, dv), F32), jax.ShapeDtypeStruct((nh, rows, 128), F32)],
        compiler_params=_cparams(2),
    )(q, k, v)


def _mla_bwd(q, k, v, o, lse, do, *, name):
    nh, rows, dk = q.shape
    dv = v.shape[2]
    tq = _tile(rows, ATT_TQ, 8)
    scale = (MLA_NOPE + MLA_ROPE) ** -0.5

    def body(q_ref, k_ref, v_ref, o_ref, lse_ref, do_ref, dq_ref, dk_ref, dv_ref):
        i = pl.program_id(1)

        @pl.when(i == 0)
        def _():
            dk_ref[...] = jnp.zeros_like(dk_ref)
            dv_ref[...] = jnp.zeros_like(dv_ref)

        r_i, c_i = _tile_iotas(tq)
        qb = q_ref[0].astype(BF16)
        dov = do_ref[0]
        dob = dov.astype(BF16)
        delta = jnp.sum(dov * o_ref[0], axis=1, keepdims=True)
        lse = lse_ref[0][:, 0:1]
        rowpos = i * tq + r_i
        first_diag, n_blocks = _key_ranges(i, tq)

        def step(j, dq, masked):
            off = pl.multiple_of(j * BLOCK, BLOCK)
            kb = k_ref[0, pl.ds(off, BLOCK), :].astype(BF16)
            vb = v_ref[0, pl.ds(off, BLOCK), :].astype(BF16)
            p = jnp.exp(jnp.minimum(_dot_nt(qb, kb) * scale - lse, 0.0))
            if masked:
                colpos = off + c_i
                p = jnp.where((colpos <= rowpos) & (colpos >= N_PAD), p, 0.0)
            ds = (p * (_dot_nt(dob, vb) - delta) * scale).astype(BF16)
            dk_ref[0, pl.ds(off, BLOCK), :] += _dot_tn(ds, qb)
            dv_ref[0, pl.ds(off, BLOCK), :] += _dot_tn(p.astype(BF16), dob)
            return dq + _dot(ds, kb)

        dq = step(0, jnp.zeros((tq, dk), F32), True)
        dq = lax.fori_loop(1, first_diag, lambda j, c: step(j, c, False), dq)
        dq_ref[0] = lax.fori_loop(first_diag, n_blocks, lambda j, c: step(j, c, True), dq)

    qblk = pl.BlockSpec((1, tq, dk), lambda h, i: (h, i, 0))
    vblk = pl.BlockSpec((1, tq, dv), lambda h, i: (h, i, 0))
    lblk = pl.BlockSpec((1, tq, 128), lambda h, i: (h, i, 0))
    kfull = pl.BlockSpec((1, rows, dk), lambda h, i: (h, 0, 0))
    vfull = pl.BlockSpec((1, rows, dv), lambda h, i: (h, 0, 0))
    return pl.pallas_call(
        body, name=name, grid=(nh, rows // tq), in_specs=[qblk, kfull, vfull, vblk, lblk, vblk],
        out_specs=[qblk, kfull, vfull],
        out_shape=[jax.ShapeDtypeStruct((nh, rows, dk), F32), jax.ShapeDtypeStruct((nh, rows, dk), F32),
                   jax.ShapeDtypeStruct((nh, rows, dv), F32)],
        compiler_params=_cparams(2),
    )(q, k, v, o, lse, do)


def _ssd_consts():
    r_i, c_i = _tile_iotas()
    eh = lax.broadcasted_iota(jnp.int32, (BLOCK, SSD_WIDTH), 0)
    ec = lax.broadcasted_iota(jnp.int32, (BLOCK, SSD_WIDTH), 1)
    expand = _ones_where(lax.shift_right_logical(ec, 6) == eh)
    return r_i, c_i, expand


def _ssd_common(pre_v, dtr_v, bias_v, a_v, chunk, r_i, c_i, expand):
    lower = r_i >= c_i
    sig_pre = _sigmoid(pre_v)
    xbc = pre_v * sig_pre
    xs = xbc[:, :SSD_WIDTH]
    valid = (chunk * BLOCK + lax.broadcasted_iota(jnp.int32, (BLOCK, 1), 0)) >= N_PAD
    dt_in = dtr_v + bias_v
    dtv = jnp.where(valid, _softplus(dt_in), 0.0)
    d_a = dtv * a_v
    acs = _hilo_dot_r(_ones_where(lower), d_a)
    acs_t = acs.T
    dt_exp = _hilo_dot_l(dtv, expand)
    acs_exp = _hilo_dot_l(acs, expand)
    a_last = acs_exp[BLOCK - 1:BLOCK, :]
    ea = jnp.exp(acs_exp)
    e_l = jnp.exp(a_last - acs_exp)
    ea_l = jnp.exp(a_last)
    return lower, sig_pre, xbc, xs, valid, dt_in, dtv, acs, acs_t, dt_exp, ea, e_l, ea_l


def _decay(acs, acs_t, h, lower):
    col = acs[:, h:h + 1]
    row = acs_t[h:h + 1, :]
    return jnp.where(lower, jnp.exp(jnp.minimum(col - row, 0.0)), 0.0)


def _ssd_fwd(pre, dtr, bias_row, a_row, d_exp, *, name):
    rows = pre.shape[0]
    nc = rows // BLOCK

    def body(pre_ref, dtr_ref, bias_ref, a_ref, dexp_ref, y_ref, st_ref, state):
        c = pl.program_id(0)

        @pl.when(c == 0)
        def _():
            state[...] = jnp.zeros_like(state)

        r_i, c_i, expand = _ssd_consts()
        lane_lo = c_i < 64
        (lower, _, xbc, xs, _, _, _, acs, acs_t, dt_exp, ea, e_l, ea_l) = _ssd_common(
            pre_ref[...], dtr_ref[...], bias_ref[...], a_ref[...], c, r_i, c_i, expand)
        xin = xs * dt_exp
        for g in range(2):
            bg = xbc[:, 512 + 128 * g:640 + 128 * g]
            cg = xbc[:, 768 + 128 * g:896 + 128 * g]
            bb = bg.astype(BF16)
            cbf = cg.astype(BF16)
            cb = _dot_nt(cbf, bb)
            bt = bg.T.astype(BF16)
            for pp in range(2):
                p = 2 * g + pp
                sl = slice(128 * p, 128 * p + 128)
                xp = xin[:, sl]
                xb = xp.astype(BF16)
                rs = [_dot((cb * _decay(acs, acs_t, 2 * p + hh, lower)).astype(BF16), xb) for hh in range(2)]
                ydiag = jnp.where(lane_lo, rs[0], rs[1])
                s_in = state[p]
                st_ref[0, p] = s_in
                yoff = ea[:, sl] * _dot(cbf, s_in.astype(BF16))
                y_ref[:, sl] = ydiag + yoff + xs[:, sl] * dexp_ref[:, sl]
                state[p] = ea_l[:, sl] * s_in + _dot(bt, (xp * e_l[:, sl]).astype(BF16))

    vec = pl.BlockSpec((1, 128), lambda c: (0, 0))
    return pl.pallas_call(
        body, name=name, grid=(nc,),
        in_specs=[pl.BlockSpec((BLOCK, SSD_XBC), lambda c: (c, 0)),
                  pl.BlockSpec((BLOCK, 128), lambda c: (c, 0)), vec, vec,
                  pl.BlockSpec((1, SSD_WIDTH), lambda c: (0, 0))],
        out_specs=[pl.BlockSpec((BLOCK, SSD_WIDTH), lambda c: (c, 0)),
                   pl.BlockSpec((1, 4, 128, 128), lambda c: (c, 0, 0, 0))],
        out_shape=[jax.ShapeDtypeStruct((rows, SSD_WIDTH), F32), jax.ShapeDtypeStruct((nc, 4, 128, 128), F32)],
        scratch_shapes=[pltpu.VMEM((4, 128, 128), F32)],
        compiler_params=_cparams(1),
    )(pre, dtr, bias_row, a_row, d_exp)


def _ssd_bwd(pre, dtr, bias_row, a_row, d_exp, states, dy, *, name):
    rows = pre.shape[0]
    nc = rows // BLOCK

    def body(pre_ref, dtr_ref, bias_ref, a_ref, dexp_ref, st_ref, dy_ref,
             dpre_ref, ddtr_ref, dbias_ref, da_ref, dd_ref, dstate, q_buf, dx_buf):
        step = pl.program_id(0)
        c = nc - 1 - step

        @pl.when(step == 0)
        def _():
            dstate[...] = jnp.zeros_like(dstate)
            dbias_ref[...] = jnp.zeros_like(dbias_ref)
            da_ref[...] = jnp.zeros_like(da_ref)
            dd_ref[...] = jnp.zeros_like(dd_ref)

        r_i, c_i, expand = _ssd_consts()
        lane_lo = c_i < 64
        last_row = r_i == BLOCK - 1
        pre_v = pre_ref[...]
        (lower, sig_pre, xbc, xs, valid, dt_in, dtv, acs, acs_t, dt_exp, ea, e_l, ea_l) = _ssd_common(
            pre_v, dtr_ref[...], bias_ref[...], a_ref[...], c, r_i, c_i, expand)
        dsilu = sig_pre * (1.0 + pre_v * (1.0 - sig_pre))
        xin = xs * dt_exp
        dyv = dy_ref[...]
        d_acs_diag = jnp.zeros((BLOCK, BLOCK), F32)
        for g in range(2):
            bg = xbc[:, 512 + 128 * g:640 + 128 * g]
            cg = xbc[:, 768 + 128 * g:896 + 128 * g]
            bb = bg.astype(BF16)
            cbf = cg.astype(BF16)
            cb = _dot_nt(cbf, bb)
            ct = cg.T.astype(BF16)
            dcb = jnp.zeros((BLOCK, BLOCK), F32)
            dbg = jnp.zeros((BLOCK, BLOCK), F32)
            dcg = jnp.zeros((BLOCK, BLOCK), F32)
            for pp in range(2):
                p = 2 * g + pp
                sl = slice(128 * p, 128 * p + 128)
                xp = xin[:, sl]
                xb = xp.astype(BF16)
                dyp = dyv[:, sl]
                dyb = dyp.astype(BF16)
                dxs_ = []
                for hh in range(2):
                    dec = _decay(acs, acs_t, 2 * p + hh, lower)
                    wm = cb * dec
                    dxs_.append(_dot(wm.T.astype(BF16), dyb))
                    half = lane_lo if hh == 0 else jnp.logical_not(lane_lo)
                    dwm = _dot_nt(jnp.where(half, dyp, 0.0).astype(BF16), xb)
                    dcb = dcb + dwm * dec
                    dseg = dwm * wm
                    dcol = jnp.sum(dseg, axis=1, keepdims=True) - jnp.sum(dseg.T, axis=1, keepdims=True)
                    d_acs_diag = jnp.where(c_i == 2 * p + hh, dcol, d_acs_diag)
                dxdiag =jnp.where(lane_lo, dxs_[0], dxs_[1])
                s_in = st_ref[0, p]
                sb = s_in.astype(BF16)
                ds_out = dstate[p]
                dsb = ds_out.astype(BF16)
                yoff = ea[:, sl] * _dot(cbf, sb)
                dxst = e_l[:, sl] * _dot(bb, dsb)
                dxp = dxdiag + dxst
                dye = dyp * ea[:, sl]
                dyeb = dye.astype(BF16)
                qp = dyp * yoff - xp * dxst
                lastv = (jnp.sum(xp * dxst, axis=0, keepdims=True)
                         + ea_l[:, sl] * jnp.sum(ds_out * s_in, axis=0, keepdims=True))
                q_buf[:, sl] = jnp.where(last_row, qp + lastv, qp)
                dx_buf[:, sl] = dxp
                dcg = dcg + _dot_nt(dyeb, sb)
                dbg = dbg + _dot_nt((xp * e_l[:, sl]).astype(BF16), dsb)
                dstate[p] = ea_l[:, sl] * ds_out + _dot(ct, dyeb)
            dcg = dcg + _dot(dcb.astype(BF16), bb)
            dbg = dbg + _dot(dcb.T.astype(BF16), cbf)
            bsl = slice(512 + 128 * g, 640 + 128 * g)
            csl = slice(768 + 128 * g, 896 + 128 * g)
            dpre_ref[:, bsl] = dbg * dsilu[:, bsl]
            dpre_ref[:, csl] = dcg * dsilu[:, csl]
        dxall = dx_buf[...]
        dpre_ref[:, :SSD_WIDTH] = (dyv * dexp_ref[...] + dxall * dt_exp) * dsilu[:, :SSD_WIDTH]
        dd_ref[...] += jnp.sum(dyv * xs, axis=0, keepdims=True)
        d_acs = d_acs_diag + _hilo_dot_nt(q_buf[...], expand)
        dd_a = _hilo_dot_r(_ones_where(r_i <= c_i), d_acs)
        ddt = dd_a * a_ref[...] + _hilo_dot_nt(dxall * xs, expand)
        ddt = jnp.where(valid, ddt, 0.0)
        da_ref[...] += jnp.sum(dd_a * dtv, axis=0, keepdims=True)
        ddtr = ddt * _sigmoid(dt_in)
        ddtr_ref[...] = ddtr
        dbias_ref[...] += jnp.sum(ddtr, axis=0, keepdims=True)

    vec = pl.BlockSpec((1, 128), lambda s: (0, 0))
    wide = pl.BlockSpec((1, SSD_WIDTH), lambda s: (0, 0))
    rev = lambda s: (nc - 1 - s, 0)
    return pl.pallas_call(
        body, name=name, grid=(nc,),
        in_specs=[pl.BlockSpec((BLOCK, SSD_XBC), rev), pl.BlockSpec((BLOCK, 128), rev), vec, vec, wide,
                  pl.BlockSpec((1, 4, 128, 128), lambda s: (nc - 1 - s, 0, 0, 0)),
                  pl.BlockSpec((BLOCK, SSD_WIDTH), rev)],
        out_specs=[pl.BlockSpec((BLOCK, SSD_XBC), rev), pl.BlockSpec((BLOCK, 128), rev), vec, vec, wide],
        out_shape=[jax.ShapeDtypeStruct((rows, SSD_XBC), F32), jax.ShapeDtypeStruct((rows, 128), F32),
                   jax.ShapeDtypeStruct((1, 128), F32), jax.ShapeDtypeStruct((1, 128), F32),
                   jax.ShapeDtypeStruct((1, SSD_WIDTH), F32)],
        scratch_shapes=[pltpu.VMEM((4, 128, 128), F32), pltpu.VMEM((BLOCK, SSD_WIDTH), F32),
                        pltpu.VMEM((BLOCK, SSD_WIDTH), F32)],
        compiler_params=_cparams(1),
    )(pre, dtr, bias_row, a_row, d_exp, states, dy)


def _peer(xi, yi, ci, k):
    px = (1 - xi) if (k >> 2) & 1 else xi
    py = (1 - yi) if (k >> 1) & 1 else yi
    pc = (1 - ci) if k & 1 else ci
    return (px, py, pc), 4 * px + 2 * py + pc


def _exchange(xs, *, gather, name):
    n = len(xs)
    n_peers = N_DEV - 1
    out_shape = [jax.ShapeDtypeStruct((N_DEV,) + x.shape if gather else x.shape, x.dtype) for x in xs]

    def body(*refs):
        x_refs, o_refs = refs[:n], refs[n:2 * n]
        send_sems, recv_sems, local_sems = refs[2 * n:]
        xi, yi, ci = lax.axis_index("x"), lax.axis_index("y"), lax.axis_index("c")
        me = 4 * xi + 2 * yi + ci

        def copy(a, k, src_idx, dst_idx, peer):
            src = x_refs[a] if gather else x_refs[a].at[src_idx]
            return pltpu.make_async_remote_copy(
                src_ref=src, dst_ref=o_refs[a].at[dst_idx], send_sem=send_sems.at[a * n_peers + k - 1],
                recv_sem=recv_sems.at[a * n_peers + k - 1], device_id=peer, device_id_type=pl.DeviceIdType.MESH)

        local = [pltpu.make_async_copy(x_refs[a] if gather else x_refs[a].at[me], o_refs[a].at[me], local_sems.at[a])
                 for a in range(n)]
        for cp in local:
            cp.start()
        sends = []
        for k in range(1, N_DEV):
            peer, pidx = _peer(xi, yi, ci, k)
            for a in range(n):
                sends.append(copy(a, k, pidx, me, peer))
                sends[-1].start()
        for k in range(1, N_DEV):
            peer, pidx = _peer(xi, yi, ci, k)
            for a in range(n):
                copy(a, k, pidx, pidx, peer).wait_recv()
        for cp in sends:
            cp.wait_send()
        for cp in local:
            cp.wait()

    hbm = pl.BlockSpec(memory_space=pltpu.HBM)
    return pl.pallas_call(
        body, name=name, out_shape=out_shape, in_specs=[hbm] * n, out_specs=[hbm] * n,
        scratch_shapes=[pltpu.SemaphoreType.DMA((n * n_peers,)), pltpu.SemaphoreType.DMA((n * n_peers,)),
                        pltpu.SemaphoreType.DMA((n,))],
    )(*xs)


def _adamw(g8, w, m, v, *, name):
    rows, cols = w.shape
    lanes = -(-cols // 128) * 128
    tr = _tile(rows, max(8, (128 * 1024) // lanes), 8)

    def body(g_ref, w_ref, m_ref, v_ref, go_ref, d_ref, mo_ref, vo_ref):
        g = g_ref[0]
        for j in range(1, N_DEV):
            g = g + g_ref[j]
        m2 = ADAM_B1 * m_ref[...] + (1.0 - ADAM_B1) * g
        v2 = ADAM_B2 * v_ref[...] + (1.0 - ADAM_B2) * (g * g)
        m_hat = m2 / (1.0 - ADAM_B1 ** ADAM_STEP)
        v_hat = v2 / (1.0 - ADAM_B2 ** ADAM_STEP)
        go_ref[...] = g
        d_ref[...] = -ADAM_LR * (m_hat / (jnp.sqrt(v_hat) + ADAM_EPS) + ADAM_WD * w_ref[...])
        mo_ref[...] = m2
        vo_ref[...] = v2

    spec = pl.BlockSpec((tr, cols), lambda i: (i, 0))
    sds = jax.ShapeDtypeStruct((rows, cols), F32)
    return pl.pallas_call(
        body, name=name, grid=(rows // tr,),
        in_specs=[pl.BlockSpec((N_DEV, tr, cols), lambda i: (0, i, 0)), spec, spec, spec],
        out_specs=[spec, spec, spec, spec], out_shape=[sds, sds, sds, sds], compiler_params=_cparams(1),
    )(g8, w, m, v)


SHARDED = (("meta_tokens", 1), ("w_in", 2), ("ssd_conv_w", 2), ("mla_w_uq", 2), ("mla_w_ukv", 2),
           ("w_out", 1), ("ffn_w_up", 2), ("ffn_conv_w", 2), ("ffn_w_down", 1))
BIG = ("w_in", "w_out", "ffn_w_up", "ffn_w_down")
SMALL = ("meta_tokens", "ssd_conv_w", "mla_w_uq", "mla_w_ukv", "ffn_conv_w")
REPLICATED = ("norm_mix_g", "ssd_conv_b", "ssd_dt_bias", "ssd_a_log", "ssd_d", "ssd_norm_g", "sb_norm_g",
              "mla_q_norm_g", "mla_kv_norm_g", "mla_norm_g", "norm_ffn_g", "ffn_conv_b", "final_norm_g")
WEIGHTS = ("meta_tokens", "norm_mix_g", "w_in", "ssd_conv_w", "ssd_conv_b", "ssd_dt_bias", "ssd_a_log", "ssd_d",
           "ssd_norm_g", "sb_norm_g", "mla_q_norm_g", "mla_kv_norm_g", "mla_w_uq", "mla_w_ukv", "mla_norm_g",
           "w_out", "norm_ffn_g", "ffn_w_up", "ffn_conv_w", "ffn_conv_b", "ffn_w_down", "final_norm_g")


def _flat_pack(arrays, dtype, align):
    flat = jnp.concatenate([a.reshape(-1).astype(dtype) for a in arrays])
    pad = (-flat.shape[0]) % align
    return jnp.pad(flat, (0, pad)).reshape(-1, 128)


def _pieces(full, axis):
    shp = full.shape
    t = full.reshape(shp[:axis] + (N_DEV, shp[axis] // N_DEV) + shp[axis + 1:])
    return jnp.moveaxis(t, axis, 0).reshape(N_DEV, -1)


def _unpieces(p8, shard_shape, axis):
    t = p8.reshape((N_DEV,) + shard_shape)
    t = jnp.moveaxis(t, 0, axis)
    return t.reshape(shard_shape[:axis] + (N_DEV * shard_shape[axis],) + shard_shape[axis + 1:])


def _split_blocks(full, axis):
    shp = full.shape
    t = full.reshape(shp[:axis] + (N_DEV, shp[axis] // N_DEV) + shp[axis + 1:])
    return jnp.moveaxis(t, axis, 0)


def _merge_blocks(b8, axis):
    shard = b8.shape[1:]
    t = jnp.moveaxis(b8, 0, axis)
    return t.reshape(shard[:axis] + (N_DEV * shard[axis],) + shard[axis + 1:])


def _gather_weights(shards):
    axes = dict(SHARDED)
    got = _exchange([shards[n].astype(BF16) for n in BIG], gather=True, name="gather_big")
    full = {n: _merge_blocks(b8, axes[n]) for n, b8 in zip(BIG, got)}
    packed = _flat_pack([shards[n] for n in SMALL], F32, 8 * 128)
    got = _exchange([packed], gather=True, name="gather_small")[0].reshape(N_DEV, -1)
    off = 0
    for n in SMALL:
        size = math.prod(shards[n].shape)
        full[n] = _unpieces(got[:, off:off + size], shards[n].shape, axes[n])
        off += size
    return full


def _pad_cols(a, width):
    return jnp.pad(a, ((0, 0), (0, width - a.shape[1])))


def _w_in_padded(w):
    kr = w[:, 2632:2664]
    return jnp.concatenate([
        w[:, 0:512], w[:, 512:1536], w[:, 1544:2312], _pad_cols(w[:, 2312:2504], 256), w[:, 2504:2632],
        _pad_cols(kr[:, :16], 64), _pad_cols(kr[:, 16:], 64), _pad_cols(w[:, 1536:1544], 128),
        jnp.zeros((w.shape[0], 128), w.dtype)], axis=1)


def _w_in_unpadded(wp):
    return jnp.concatenate([
        wp[:, 0:512], wp[:, 512:1536], wp[:, OFF_DT:OFF_DT + 8], wp[:, 1536:2304], wp[:, OFF_QA:OFF_QA + 192],
        wp[:, OFF_CKV:OFF_CKV + 128], wp[:, OFF_KR:OFF_KR + 16], wp[:, OFF_KR + 64:OFF_KR + 80]], axis=1)


def _w_uq_perm(w):
    t = w.reshape(MLA_Q_RANK, MLA_HEADS, MLA_NOPE + MLA_ROPE)
    out = jnp.concatenate([t[:, :, :64].reshape(MLA_Q_RANK, 256), t[:, :, 64:80].reshape(MLA_Q_RANK, 64),
                           t[:, :, 80:96].reshape(MLA_Q_RANK, 64)], axis=1)
    return jnp.pad(out, ((0, 256 - MLA_Q_RANK), (0, 0)))


def _w_uq_unperm(wp):
    wp = wp[:MLA_Q_RANK]
    t = jnp.concatenate([wp[:, :256].reshape(MLA_Q_RANK, 4, 64), wp[:, 256:320].reshape(MLA_Q_RANK, 4, 16),
                         wp[:, 320:384].reshape(MLA_Q_RANK, 4, 16)], axis=2)
    return t.reshape(MLA_Q_RANK, 4 * 96)


def _w_ukv_perm(w):
    t = w.reshape(MLA_KV_RANK, MLA_HEADS, 128)
    return jnp.concatenate([t[:, :, :64].reshape(MLA_KV_RANK, 256), t[:, :, 64:].reshape(MLA_KV_RANK, 256)], axis=1)


def _w_ukv_unperm(wp):
    t = jnp.concatenate([wp[:, :256].reshape(MLA_KV_RANK, 4, 64), wp[:, 256:].reshape(MLA_KV_RANK, 4, 64)], axis=2)
    return t.reshape(MLA_KV_RANK, 512)


def _heads(a, hd):
    return jnp.moveaxis(a.reshape(a.shape[0], -1, hd), 1, 0)


def _unheads(a):
    return jnp.moveaxis(a, 0, 1).reshape(a.shape[1], -1)


def _row(v, width=None):
    v = v.reshape(1, -1)
    return v if width is None else _pad_cols(v, width)


def _rope_tables(rows):
    pos = jnp.arange(rows, dtype=F32) - float(N_PAD)
    inv = 1.0 / (ROPE_BASE ** (jnp.arange(0, MLA_ROPE, 2, dtype=F32) / MLA_ROPE))
    ang = pos[:, None] * inv[None, :]
    cos = jnp.tile(jnp.cos(ang), (1, 8))
    sin = jnp.tile(jnp.sin(ang), (1, 4))
    return cos, jnp.concatenate([-sin, sin], axis=1)


def _layer_fwd(h, p, cos_t, sin_t, tag):
    s = {"h_in": h}
    hn = _rmsnorm_fwd(h, p["norm_mix_g"], width=D_MODEL, name=tag + "norm_mix")
    u = _matmul(hn, p["w_in"], name=tag + "in_proj")
    s["hn"], s["u"] = hn, u

    xbc_in = u[:, OFF_XBC:OFF_XBC + SSD_XBC]
    pre = _dwconv_fwd(xbc_in, p["ssd_conv_w"], p["ssd_conv_b"], taps=SSD_CONV, name=tag + "ssd_conv")
    dtr = u[:, OFF_DT:OFF_DT + 128]
    y_ssd, states = _ssd_fwd(pre, dtr, p["dt_bias"], p["a_row"], p["d_exp"], name=tag + "ssd_core")
    zgate = u[:, OFF_Z:OFF_Z + SSD_WIDTH]
    yn_ssd = _rmsnorm_fwd(y_ssd, p["ssd_norm_g"], width=SSD_WIDTH, z=zgate, name=tag + "ssd_norm")
    s.update(xbc_in=xbc_in, pre=pre, dtr=dtr, y_ssd=y_ssd, states=states, zgate=zgate)

    q_sb = _heads(u[:, OFF_QSB:OFF_QSB + SB_WIDTH], SB_HEAD_DIM)
    k_sb = _heads(u[:, OFF_KSB:OFF_KSB + SB_WIDTH], SB_HEAD_DIM)
    v_sb = _heads(u[:, OFF_VSB:OFF_VSB + SB_WIDTH], SB_HEAD_DIM)
    o_sb, u_tot = _sb_fwd(q_sb, k_sb, v_sb, name=tag + "sb_attn")
    o_sb_flat = _unheads(o_sb)
    yn_sb = _rmsnorm_fwd(o_sb_flat, p["sb_norm_g"], width=SB_WIDTH, name=tag + "sb_norm")
    s.update(q_sb=q_sb, k_sb=k_sb, v_sb=v_sb, u_tot=u_tot, o_sb_flat=o_sb_flat)

    qa = u[:, OFF_QA:OFF_QA + 256]
    ckv = u[:, OFF_CKV:OFF_CKV + 128]
    qa_n = _rmsnorm_fwd(qa, p["mla_q_norm_g"], width=MLA_Q_RANK, name=tag + "mla_qnorm")
    ckv_n = _rmsnorm_fwd(ckv, p["mla_kv_norm_g"], width=MLA_KV_RANK, name=tag + "mla_kvnorm")
    qf = _matmul(qa_n, p["mla_w_uq"], name=tag + "mla_uq")
    kvf = _matmul(ckv_n, p["mla_w_ukv"], name=tag + "mla_ukv")
    q_rope = _rope(qf[:, 256:384], cos_t, sin_t, name=tag + "rope_q")
    k_rope = _rope(u[:, OFF_KR:OFF_KR + 128], cos_t, sin_t, name=tag + "rope_k")
    rows = h.shape[0]
    zpad = jnp.zeros((MLA_HEADS, rows, 32), F32)
    qh = jnp.concatenate([_heads(qf[:, :256], 64), _heads(q_rope[:, :64], 16), _heads(q_rope[:, 64:], 16), zpad], axis=2)
    kr_b = jnp.broadcast_to(jnp.concatenate([k_rope[:, 0:16], k_rope[:, 64:80]], axis=1)[None], (MLA_HEADS, rows, 32))
    kh = jnp.concatenate([_heads(kvf[:, :256], 64), kr_b, zpad], axis=2)
    vh = _heads(kvf[:, 256:], 64)
    o_mla, lse = _mla_fwd(qh, kh, vh, name=tag + "mla_attn")
    o_mla_flat = _unheads(o_mla)
    yn_mla = _rmsnorm_fwd(o_mla_flat, p["mla_norm_g"], width=256, name=tag + "mla_norm")
    s.update(qa=qa, ckv=ckv, qa_n=qa_n, ckv_n=ckv_n, qh=qh, kh=kh, vh=vh, o_mla=o_mla, lse=lse,
             o_mla_flat=o_mla_flat)

    mix = jnp.concatenate([yn_ssd, yn_sb, yn_mla], axis=1)
    h_mid = _matmul(mix, p["w_out"], res=h, mask_pad=True, name=tag + "out_proj")
    hn2 = _rmsnorm_fwd(h_mid, p["norm_ffn_g"], width=D_MODEL, name=tag + "norm_ffn")
    up = _matmul(hn2, p["ffn_w_up"], tn=1408, name=tag + "ffn_up")
    fpre = _dwconv_fwd(up, p["ffn_conv_w"], p["ffn_conv_b"], taps=FFN_CONV, name=tag + "ffn_conv")
    act = _gate_fwd(fpre, name=tag + "ffn_gate")
    h_out = _matmul(act, p["ffn_w_down"], res=h_mid, mask_pad=True, tk=1408, name=tag + "ffn_down")
    s.update(mix=mix, h_mid=h_mid, hn2=hn2, up=up, fpre=fpre, act=act)
    return h_out, s


def _layer_bwd(dh_out, p, s, cos_t, sin_t, tag):
    g = {}
    rows = dh_out.shape[0]
    dact = _matmul(dh_out, p["ffn_w_down"], tb=True, tn=1408, out_dtype=BF16, name=tag + "b_down_dx")
    g["ffn_w_down"] = _matmul(s["act"], dh_out, ta=True, tm=1408, tk=640, name=tag + "b_down_dw")
    dfpre = _gate_bwd(s["fpre"], dact, name=tag + "b_gate")
    dup, dcw, dcb_ = _dwconv_bwd(dfpre, s["up"], p["ffn_conv_w"], taps=FFN_CONV, name=tag + "b_ffn_conv")
    g["ffn_conv_w"], g["ffn_conv_b"] = dcw[:FFN_CONV], dcb_[0]
    dhn2 = _matmul(dup, p["ffn_w_up"], tb=True, tk=1408, name=tag + "b_up_dx")
    g["ffn_w_up"] = _matmul(s["hn2"], dup, ta=True, tn=1408, tk=640, name=tag + "b_up_dw")
    dh_mid, _, dg = _rmsnorm_bwd(s["h_mid"], p["norm_ffn_g"], dhn2, width=D_MODEL, res=dh_out, mask_pad=True,
                                 name=tag + "b_norm_ffn")
    g["norm_ffn_g"] = dg[0]

    dmix = _matmul(dh_mid, p["w_out"], tb=True, name=tag + "b_out_dx")
    g["w_out"] = _matmul(s["mix"], dh_mid, ta=True, tk=640, name=tag + "b_out_dw")

    dy_ssd, dz, dg = _rmsnorm_bwd(s["y_ssd"], p["ssd_norm_g"], dmix[:, :SSD_WIDTH], width=SSD_WIDTH, z=s["zgate"],
                                  name=tag + "b_ssd_norm")
    g["ssd_norm_g"] = dg[0]
    dpre, ddtr, dbias, da, dd = _ssd_bwd(s["pre"], s["dtr"], p["dt_bias"], p["a_row"], p["d_exp"], s["states"],
                                         dy_ssd, name=tag + "b_ssd_core")
    g["ssd_dt_bias"] = dbias[0, :8]
    g["ssd_a_log"] = da[0, :8] * p["a_row"][0, :8]
    g["ssd_d"] = dd.reshape(8, 64).sum(axis=1)
    dxbc_in, dcw, dcb_ = _dwconv_bwd(dpre, s["xbc_in"], p["ssd_conv_w"], taps=SSD_CONV, name=tag + "b_ssd_conv")
    g["ssd_conv_w"], g["ssd_conv_b"] = dcw[:SSD_CONV], dcb_[0]

    do_sb_flat, _, dg = _rmsnorm_bwd(s["o_sb_flat"], p["sb_norm_g"], dmix[:, 512:768], width=SB_WIDTH,
                                     name=tag + "b_sb_norm")
    g["sb_norm_g"] = dg[0]
    dq_sb, dk_sb, dv_sb = _sb_bwd(s["q_sb"], s["k_sb"], s["v_sb"], _heads(do_sb_flat, SB_HEAD_DIM), s["u_tot"],
                                  name=tag + "b_sb_attn")

    do_mla_flat, _, dg = _rmsnorm_bwd(s["o_mla_flat"], p["mla_norm_g"], dmix[:, 768:1024], width=256,
                                      name=tag + "b_mla_norm")
    g["mla_norm_g"] = dg[0]
    dqh, dkh, dvh = _mla_bwd(s["qh"], s["kh"], s["vh"], s["o_mla"], s["lse"], _heads(do_mla_flat, 64),
                             name=tag + "b_mla_attn")
    dq_rope_in = jnp.concatenate([_unheads(dqh[:, :, 64:80]), _unheads(dqh[:, :, 80:96])], axis=1)
    dq_r = _rope(dq_rope_in, cos_t, sin_t, transpose=True, name=tag + "b_rope_q")
    dqf = jnp.concatenate([_unheads(dqh[:, :, :64]), dq_r], axis=1)
    dkr_sum = jnp.sum(dkh[:, :, 64:96], axis=0)
    dk_rope_in = jnp.concatenate([_pad_cols(dkr_sum[:, :16], 64), _pad_cols(dkr_sum[:, 16:], 64)], axis=1)
    dkr = _rope(dk_rope_in, cos_t, sin_t, transpose=True, name=tag + "b_rope_k")
    dkvf = jnp.concatenate([_unheads(dkh[:, :, :64]), _unheads(dvh)], axis=1)
    dqa_n = _matmul(dqf, p["mla_w_uq"], tb=True, name=tag + "b_uq_dx")
    g["mla_w_uq"] = _matmul(s["qa_n"], dqf, ta=True, tk=640, name=tag + "b_uq_dw")
    dckv_n = _matmul(dkvf, p["mla_w_ukv"], tb=True, name=tag + "b_ukv_dx")
    g["mla_w_ukv"] = _matmul(s["ckv_n"], dkvf, ta=True, tk=640, name=tag + "b_ukv_dw")
    dqa, _, dg = _rmsnorm_bwd(s["qa"], p["mla_q_norm_g"], dqa_n, width=MLA_Q_RANK, name=tag + "b_mla_qnorm")
    g["mla_q_norm_g"] = dg[0, :MLA_Q_RANK]
    dckv, _, dg = _rmsnorm_bwd(s["ckv"], p["mla_kv_norm_g"], dckv_n, width=MLA_KV_RANK, name=tag + "b_mla_kvnorm")
    g["mla_kv_norm_g"] = dg[0]

    du = jnp.concatenate([dz, dxbc_in, _unheads(dq_sb), _unheads(dk_sb), _unheads(dv_sb), dqa, dckv, dkr, ddtr,
                          jnp.zeros((rows, 128), F32)], axis=1)
    dhn = _matmul(du, p["w_in"], tb=True, name=tag + "b_in_dx")
    g["w_in"] = _matmul(s["hn"], du, ta=True, tk=640, name=tag + "b_in_dw")
    dh_in, _, dg = _rmsnorm_bwd(s["h_in"], p["norm_mix_g"], dhn, width=D_MODEL, res=dh_mid, mask_pad=True,
                                name=tag + "b_norm_mix")
    g["norm_mix_g"] = dg[0]
    return dh_in, g


def _prepare_layer(full, rep, l):
    a_row = _row(-jnp.exp(rep["ssd_a_log"][l]), 128)
    return {
        "norm_mix_g": _row(rep["norm_mix_g"][l]),
        "w_in": _w_in_padded(full["w_in"][l]),
        "ssd_conv_w": jnp.pad(full["ssd_conv_w"][l], ((0, HALO - SSD_CONV), (0, 0))),
        "ssd_conv_b": _row(rep["ssd_conv_b"][l]),
        "dt_bias": _row(rep["ssd_dt_bias"][l], 128),
        "a_row": a_row,
        "d_exp": _row(jnp.repeat(rep["ssd_d"][l], 64)),
        "ssd_norm_g": _row(rep["ssd_norm_g"][l]),
        "sb_norm_g": _row(rep["sb_norm_g"][l]),
        "mla_q_norm_g": _row(rep["mla_q_norm_g"][l], 256),
        "mla_kv_norm_g": _row(rep["mla_kv_norm_g"][l]),
        "mla_w_uq": _w_uq_perm(full["mla_w_uq"][l]),
        "mla_w_ukv": _w_ukv_perm(full["mla_w_ukv"][l]),
        "mla_norm_g": _row(rep["mla_norm_g"][l]),
        "w_out": full["w_out"][l],
        "norm_ffn_g": _row(rep["norm_ffn_g"][l]),
        "ffn_w_up": full["ffn_w_up"][l],
        "ffn_conv_w": jnp.pad(full["ffn_conv_w"][l], ((0, HALO - FFN_CONV), (0, 0))),
        "ffn_conv_b": _row(rep["ffn_conv_b"][l]),
        "ffn_w_down": full["ffn_w_down"][l],
    }


def _layer_grads_to_full(g):
    out = dict(g)
    out["w_in"] = _w_in_unpadded(g["w_in"])
    out["mla_w_uq"] = _w_uq_unperm(g["mla_w_uq"])
    out["mla_w_ukv"] = _w_ukv_unperm(g["mla_w_ukv"])
    return out


def kernel(x, meta_tokens, norm_mix_g, w_in, ssd_conv_w, ssd_conv_b, ssd_dt_bias, ssd_a_log, ssd_d, ssd_norm_g, sb_norm_g, mla_q_norm_g, mla_kv_norm_g, mla_w_uq, mla_w_ukv, mla_norm_g, w_out, norm_ffn_g, ffn_w_up, ffn_conv_w, ffn_conv_b, ffn_w_down, final_norm_g, loss_target, m_meta_tokens, m_norm_mix_g, m_w_in, m_ssd_conv_w, m_ssd_conv_b, m_ssd_dt_bias, m_ssd_a_log, m_ssd_d, m_ssd_norm_g, m_sb_norm_g, m_mla_q_norm_g, m_mla_kv_norm_g, m_mla_w_uq, m_mla_w_ukv, m_mla_norm_g, m_w_out, m_norm_ffn_g, m_ffn_w_up, m_ffn_conv_w, m_ffn_conv_b, m_ffn_w_down, m_final_norm_g, v_meta_tokens, v_norm_mix_g, v_w_in, v_ssd_conv_w, v_ssd_conv_b, v_ssd_dt_bias, v_ssd_a_log, v_ssd_d, v_ssd_norm_g, v_sb_norm_g, v_mla_q_norm_g, v_mla_kv_norm_g, v_mla_w_uq, v_mla_w_ukv, v_mla_norm_g, v_w_out, v_norm_ffn_g, v_ffn_w_up, v_ffn_conv_w, v_ffn_conv_b, v_ffn_w_down, v_final_norm_g):
    w = dict(meta_tokens=meta_tokens, norm_mix_g=norm_mix_g, w_in=w_in, ssd_conv_w=ssd_conv_w, ssd_conv_b=ssd_conv_b,
             ssd_dt_bias=ssd_dt_bias, ssd_a_log=ssd_a_log, ssd_d=ssd_d, ssd_norm_g=ssd_norm_g, sb_norm_g=sb_norm_g,
             mla_q_norm_g=mla_q_norm_g, mla_kv_norm_g=mla_kv_norm_g, mla_w_uq=mla_w_uq, mla_w_ukv=mla_w_ukv,
             mla_norm_g=mla_norm_g, w_out=w_out, norm_ffn_g=norm_ffn_g, ffn_w_up=ffn_w_up, ffn_conv_w=ffn_conv_w,
             ffn_conv_b=ffn_conv_b, ffn_w_down=ffn_w_down, final_norm_g=final_norm_g)
    mom = dict(meta_tokens=m_meta_tokens, norm_mix_g=m_norm_mix_g, w_in=m_w_in, ssd_conv_w=m_ssd_conv_w,
               ssd_conv_b=m_ssd_conv_b, ssd_dt_bias=m_ssd_dt_bias, ssd_a_log=m_ssd_a_log, ssd_d=m_ssd_d,
               ssd_norm_g=m_ssd_norm_g, sb_norm_g=m_sb_norm_g, mla_q_norm_g=m_mla_q_norm_g,
               mla_kv_norm_g=m_mla_kv_norm_g, mla_w_uq=m_mla_w_uq, mla_w_ukv=m_mla_w_ukv, mla_norm_g=m_mla_norm_g,
               w_out=m_w_out, norm_ffn_g=m_norm_ffn_g, ffn_w_up=m_ffn_w_up, ffn_conv_w=m_ffn_conv_w,
               ffn_conv_b=m_ffn_conv_b, ffn_w_down=m_ffn_w_down, final_norm_g=m_final_norm_g)
    vel = dict(meta_tokens=v_meta_tokens, norm_mix_g=v_norm_mix_g, w_in=v_w_in, ssd_conv_w=v_ssd_conv_w,
               ssd_conv_b=v_ssd_conv_b, ssd_dt_bias=v_ssd_dt_bias, ssd_a_log=v_ssd_a_log, ssd_d=v_ssd_d,
               ssd_norm_g=v_ssd_norm_g, sb_norm_g=v_sb_norm_g, mla_q_norm_g=v_mla_q_norm_g,
               mla_kv_norm_g=v_mla_kv_norm_g, mla_w_uq=v_mla_w_uq, mla_w_ukv=v_mla_w_ukv, mla_norm_g=v_mla_norm_g,
               w_out=v_w_out, norm_ffn_g=v_norm_ffn_g, ffn_w_up=v_ffn_w_up, ffn_conv_w=v_ffn_conv_w,
               ffn_conv_b=v_ffn_conv_b, ffn_w_down=v_ffn_w_down, final_norm_g=v_final_norm_g)

    full = _gather_weights({n: w[n] for n, _ in SHARDED})
    layers = [_prepare_layer(full, w, l) for l in range(DEPTH)]

    seq = x.shape[1]
    rows = BLOCK + seq
    cos_t, sin_t = _rope_tables(rows)
    h = jnp.concatenate([jnp.zeros((N_PAD, D_MODEL), F32), full["meta_tokens"], x[0]], axis=0)

    saved = []
    for l in range(DEPTH):
        h, s = _layer_fwd(h, layers[l], cos_t, sin_t, "l%d_" % l)
        saved.append(s)
    dh, dg_final, loss_part = _final_loss(h, _row(final_norm_g), loss_target[0], name="final_loss")
    loss = lax.psum(loss_part[0, 0], ("x", "y", "c"))

    layer_grads = [None] * DEPTH
    for l in reversed(range(DEPTH)):
        dh, g = _layer_bwd(dh, layers[l], saved[l], cos_t, sin_t, "l%d_" % l)
        layer_grads[l] = _layer_grads_to_full(g)
    grad_x = dh[BLOCK:][None]

    partial = {n: jnp.stack([layer_grads[l][n] for l in range(DEPTH)]) for n in layer_grads[0]}
    partial["meta_tokens"] = dh[N_PAD:BLOCK]
    partial["final_norm_g"] = dg_final[0]

    results = [dict(), dict(), dict(), dict()]
    axes = dict(SHARDED)

    got_big = _exchange([_split_blocks(partial[n], axes[n]) for n in BIG], gather=False, name="grad_all_to_all_big")
    for n, g8 in zip(BIG, got_big):
        shp = w[n].shape
        view = (math.prod(shp[:-1]), shp[-1])
        outs = _adamw(g8.reshape((N_DEV,) + view), w[n].reshape(view), mom[n].reshape(view), vel[n].reshape(view),
                      name="adamw_" + n)
        for kind in range(4):
            results[kind][n] = outs[kind].reshape(shp)

    send = jnp.concatenate([_pieces(partial[n], axes[n]) for n in SMALL], axis=1)
    pad = (-send.shape[1]) % (8 * 128)
    send = jnp.pad(send, ((0, 0), (0, pad))).reshape(N_DEV, -1, 128)
    got = _exchange([send], gather=False, name="grad_all_to_all_small")[0]
    pack = lambda d: _flat_pack([d[n] for n in SMALL], F32, 8 * 128)
    sh_out = _adamw(got, pack(w), pack(mom), pack(vel), name="adamw_small")

    rep_g = _flat_pack([partial[n] for n in REPLICATED], F32, 8 * 128)
    got_r = _exchange([rep_g], gather=True, name="grad_all_gather")[0]
    packr = lambda d: _flat_pack([d[n] for n in REPLICATED], F32, 8 * 128)
    rep_out = _adamw(got_r, packr(w), packr(mom), packr(vel), name="adamw_replicated")

    for names, outs in ((list(SMALL), sh_out), (list(REPLICATED), rep_out)):
        off = 0
        for n in names:
            size = math.prod(w[n].shape)
            for kind in range(4):
                results[kind][n] = outs[kind].reshape(-1)[off:off + size].reshape(w[n].shape)
            off += size

    return (loss, grad_x, *[results[0][n] for n in WEIGHTS], *[results[1][n] for n in WEIGHTS],
            *[results[2][n] for n in WEIGHTS], *[results[3][n] for n in WEIGHTS])
```

```python
import math

import jax
import jax.numpy as jnp
from jax import lax
from jax.experimental import pallas as pl
from jax.experimental.pallas import tpu as pltpu

F32 = jnp.float32
BF16 = jnp.bfloat16

D_MODEL = 1024
DEPTH = 2
N_META = 16
BLOCK = 128
N_PAD = BLOCK - N_META
EPS = 1e-6
SSD_WIDTH = 512
SSD_XBC = 1024
SSD_CONV = 4
SB_WIDTH = 256
SB_HEAD_DIM = 64
MLA_HEADS = 4
MLA_NOPE = 64
MLA_ROPE = 32
MLA_Q_RANK = 192
MLA_KV_RANK = 128
ROPE_BASE = 10000.0
D_FF = 2816
FFN_CONV = 3
IN_COLS = 2664
N_DEV = 8

ADAM_LR = 0.001
ADAM_B1 = 0.9
ADAM_B2 = 0.999
ADAM_EPS = 1e-08
ADAM_WD = 0.01
ADAM_STEP = 10

U_COLS = 3072
OFF_Z, OFF_XBC, OFF_QSB, OFF_KSB, OFF_VSB, OFF_QA, OFF_CKV, OFF_KR, OFF_DT = (
    0, 512, 1536, 1792, 2048, 2304, 2560, 2688, 2816)

V7X_VMEM_BYTES = 64 * 1024 * 1024
VMEM_LIMIT = (V7X_VMEM_BYTES * 7) // 8
NEG_BIG = -1e30


def _cparams(n_axes):
    return pltpu.CompilerParams(dimension_semantics=("arbitrary",) * n_axes, vmem_limit_bytes=VMEM_LIMIT)


def _tile(n, target, align):
    best = None
    for d in range(align, min(n, target) + 1, align):
        if n % d == 0:
            best = d
    return n if best is None else best


def _sigmoid(x):
    return 1.0 / (1.0 + jnp.exp(-x))


def _softplus(x):
    return jnp.maximum(x, 0.0) + jnp.log(1.0 + jnp.exp(-jnp.abs(x)))


def _dot(a, b):
    return jnp.dot(a, b, preferred_element_type=F32)


def _dot_nt(a, b):
    return lax.dot_general(a, b, (((1,), (1,)), ((), ())), preferred_element_type=F32)


def _hilo(x):
    hi = x.astype(BF16)
    lo = (x - hi.astype(F32)).astype(BF16)
    return hi, lo


def _hilo_dot_l(x, m):
    hi, lo = _hilo(x)
    return _dot(hi, m) + _dot(lo, m)


def _hilo_dot_r(m, x):
    hi, lo = _hilo(x)
    return _dot(m, hi) + _dot(m, lo)


def _hilo_dot_nt(x, m):
    hi, lo = _hilo(x)
    return _dot_nt(hi, m) + _dot_nt(lo, m)


def _ones_where(cond):
    return jnp.where(cond, 1.0, 0.0).astype(BF16)


def _matmul(a, b, *, name, ta=False, tb=False, out_dtype=F32, res=None, mask_pad=False,
            tm=640, tn=1024, tk=1024):
    m_dim = a.shape[1] if ta else a.shape[0]
    k_dim = a.shape[0] if ta else a.shape[1]
    n_dim = b.shape[0] if tb else b.shape[1]
    assert (b.shape[1] if tb else b.shape[0]) == k_dim
    tm = _tile(m_dim, tm, 128)
    tn = _tile(n_dim, tn, 128)
    tk = _tile(k_dim, tk, 128)
    nk = k_dim // tk
    dn = (((0 if ta else 1,), (1 if tb else 0,)), ((), ()))

    def body(*refs):
        if res is not None:
            a_ref, b_ref, r_ref, o_ref, acc = refs
        else:
            a_ref, b_ref, o_ref, acc = refs
        k = pl.program_id(2)

        @pl.when(k == 0)
        def _():
            acc[...] = jnp.zeros_like(acc)

        acc[...] += lax.dot_general(a_ref[...].astype(BF16), b_ref[...].astype(BF16), dn,
                                    preferred_element_type=F32)

        @pl.when(k == nk - 1)
        def _():
            r = acc[...]
            if res is not None:
                r = r + r_ref[...].astype(F32)
            if mask_pad:
                rows = pl.program_id(0) * tm + lax.broadcasted_iota(jnp.int32, (tm, 1), 0)
                r = jnp.where(rows >= N_PAD, r, 0.0)
            o_ref[...] = r.astype(out_dtype)

    a_spec = pl.BlockSpec((tk, tm), lambda i, j, k: (k, i)) if ta else pl.BlockSpec((tm, tk), lambda i, j, k: (i, k))
    b_spec = pl.BlockSpec((tn, tk), lambda i, j, k: (j, k)) if tb else pl.BlockSpec((tk, tn), lambda i, j, k: (k, j))
    o_spec = pl.BlockSpec((tm, tn), lambda i, j, k: (i, j))
    in_specs = [a_spec, b_spec]
    args = [a, b]
    if res is not None:
        in_specs.append(o_spec)
        args.append(res)
    return pl.pallas_call(
        body, name=name, grid=(m_dim // tm, n_dim // tn, nk),
        in_specs=in_specs, out_specs=o_spec,
        out_shape=jax.ShapeDtypeStruct((m_dim, n_dim), out_dtype),
        scratch_shapes=[pltpu.VMEM((tm, tn), F32)],
        compiler_params=_cparams(3),
    )(*args)


def _rmsnorm_fwd(x, g, *, width, name, z=None, out_dtype=None):
    out_dtype = BF16 if out_dtype is None else out_dtype
    rows, w = x.shape
    tr = _tile(rows, 640, 128)
    inv_w = 1.0 / width

    def body(*refs):
        if z is not None:
            x_ref, z_ref, g_ref, o_ref = refs
        else:
            x_ref, g_ref, o_ref = refs
        t = x_ref[...].astype(F32)
        if z is not None:
            zz = z_ref[...]
            t = t * (zz * _sigmoid(zz))
        ms = jnp.sum(t * t, axis=-1, keepdims=True) * inv_w
        o_ref[...] = ((t * lax.rsqrt(ms + EPS)) * g_ref[...]).astype(out_dtype)

    row_spec = pl.BlockSpec((tr, w), lambda i: (i, 0))
    g_spec = pl.BlockSpec((1, w), lambda i: (0, 0))
    in_specs = [row_spec] + ([row_spec] if z is not None else []) + [g_spec]
    args = [x] + ([z] if z is not None else []) + [g]
    return pl.pallas_call(
        body, name=name, grid=(rows // tr,), in_specs=in_specs, out_specs=row_spec,
        out_shape=jax.ShapeDtypeStruct((rows, w), out_dtype), compiler_params=_cparams(1),
    )(*args)


def _rmsnorm_bwd(x, g, dout, *, width, name, z=None, res=None, mask_pad=False):
    rows, w = x.shape
    tr = _tile(rows, 640, 128)
    inv_w = 1.0 / width

    def body(*refs):
        refs = list(refs)
        x_ref = refs.pop(0)
        z_ref = refs.pop(0) if z is not None else None
        g_ref = refs.pop(0)
        do_ref = refs.pop(0)
        r_ref = refs.pop(0) if res is not None else None
        dx_ref = refs.pop(0)
        dz_ref = refs.pop(0) if z is not None else None
        dg_ref = refs.pop(0)
        i = pl.program_id(0)

        @pl.when(i == 0)
        def _():
            dg_ref[...] = jnp.zeros_like(dg_ref)

        xv = x_ref[...].astype(F32)
        t = xv
        if z is not None:
            zz = z_ref[...]
            sig = _sigmoid(zz)
            sl = zz * sig
            t = xv * sl
        ms = jnp.sum(t * t, axis=-1, keepdims=True) * inv_w
        rstd = lax.rsqrt(ms + EPS)
        xhat = t * rstd
        do = do_ref[...].astype(F32)
        dxh = do * g_ref[...]
        c = jnp.sum(dxh * xhat, axis=-1, keepdims=True) * inv_w
        dt = rstd * (dxh - xhat * c)
        dg_ref[...] += jnp.sum(do * xhat, axis=0, keepdims=True)
        if z is not None:
            dz_ref[...] = dt * xv * (sig * (1.0 + zz * (1.0 - sig)))
            dx = dt * sl
        else:
            dx = dt
        if res is not None:
            dx = dx + r_ref[...]
        if mask_pad:
            rws = i * tr + lax.broadcasted_iota(jnp.int32, (tr, 1), 0)
            dx = jnp.where(rws >= N_PAD, dx, 0.0)
        dx_ref[...] = dx

    row_spec = pl.BlockSpec((tr, w), lambda i: (i, 0))
    g_spec = pl.BlockSpec((1, w), lambda i: (0, 0))
    in_specs = [row_spec] + ([row_spec] if z is not None else []) + [g_spec, row_spec] + (
        [row_spec] if res is not None else [])
    args = [x] + ([z] if z is not None else []) + [g, dout] + ([res] if res is not None else [])
    out_specs = [row_spec] + ([row_spec] if z is not None else []) + [g_spec]
    out_shape = [jax.ShapeDtypeStruct((rows, w), F32)] + (
        [jax.ShapeDtypeStruct((rows, w), F32)] if z is not None else []) + [jax.ShapeDtypeStruct((1, w), F32)]
    outs = pl.pallas_call(
        body, name=name, grid=(rows // tr,), in_specs=in_specs, out_specs=out_specs,
        out_shape=out_shape, compiler_params=_cparams(1),
    )(*args)
    if z is not None:
        return outs[0], outs[1], outs[2]
    return outs[0], None, outs[1]


def _final_loss(h, g, target, *, name):
    rows, w = h.shape
    nb = rows // BLOCK
    inv_w = 1.0 / w

    def body(h_ref, g_ref, t_ref, dh_ref, dg_ref, loss_ref):
        i = pl.program_id(0)

        @pl.when(i == 0)
        def _():
            dg_ref[...] = jnp.zeros_like(dg_ref)
            loss_ref[...] = jnp.zeros_like(loss_ref)

        xv = h_ref[...]
        ms = jnp.sum(xv * xv, axis=-1, keepdims=True) * inv_w
        rstd = lax.rsqrt(ms + EPS)
        xhat = xv * rstd
        gv = g_ref[...]
        err = jnp.where(i >= 1, xhat * gv - t_ref[...], 0.0)
        loss_ref[...] += (0.5 * inv_w) * jnp.sum(err * err)
        do = err * inv_w
        dxh = do * gv
        c = jnp.sum(dxh * xhat, axis=-1, keepdims=True) * inv_w
        dh_ref[...] = rstd * (dxh - xhat * c)
        dg_ref[...] += jnp.sum(do * xhat, axis=0, keepdims=True)

    row_spec = pl.BlockSpec((BLOCK, w), lambda i: (i, 0))
    g_spec = pl.BlockSpec((1, w), lambda i: (0, 0))
    return pl.pallas_call(
        body, name=name, grid=(nb,),
        in_specs=[row_spec, g_spec, pl.BlockSpec((BLOCK, w), lambda i: (jnp.maximum(i - 1, 0), 0))],
        out_specs=[row_spec, g_spec, pl.BlockSpec((1, 128), lambda i: (0, 0))],
        out_shape=[jax.ShapeDtypeStruct((rows, w), F32), jax.ShapeDtypeStruct((1, w), F32),
                   jax.ShapeDtypeStruct((1, 128), F32)],
        compiler_params=_cparams(1),
    )(h, g, target)


HALO = 8


def _dwconv_fwd(u, w8, b, *, taps, name):
    rows, ch = u.shape
    tb = _tile(rows, 640, 128)
    tc = _tile(ch, 512, 128)
    hb = tb // HALO

    def body(u_ref, h_ref, w_ref, b_ref, o_ref, buf):
        i = pl.program_id(0)
        buf[0:HALO, :] = jnp.where(i > 0, h_ref[...], 0.0)
        buf[HALO:HALO + tb, :] = u_ref[...]
        acc = jnp.broadcast_to(b_ref[...], (tb, tc))
        for k in range(taps):
            acc = acc + w_ref[k:k + 1, :] * buf[pl.ds(HALO - (taps - 1) + k, tb), :]
        o_ref[...] = acc

    return pl.pallas_call(
        body, name=name, grid=(rows // tb, ch // tc),
        in_specs=[pl.BlockSpec((tb, tc), lambda i, j: (i, j)),
                  pl.BlockSpec((HALO, tc), lambda i, j: (jnp.maximum(i * hb - 1, 0), j)),
                  pl.BlockSpec((HALO, tc), lambda i, j: (0, j)),
                  pl.BlockSpec((1, tc), lambda i, j: (0, j))],
        out_specs=pl.BlockSpec((tb, tc), lambda i, j: (i, j)),
        out_shape=jax.ShapeDtypeStruct((rows, ch), F32),
        scratch_shapes=[pltpu.VMEM((tb + HALO, tc), F32)],
        compiler_params=_cparams(2),
    )(u, u, w8, b)


def _dwconv_bwd(dpre, u, w8, *, taps, name):
    rows, ch = u.shape
    tb = _tile(rows, 640, 128)
    tc = _tile(ch, 512, 128)
    hb = tb // HALO
    nb = rows // tb
    last_halo = rows // HALO - 1

    def body(d_ref, dn_ref, u_ref, up_ref, w_ref, du_ref, dw_ref, db_ref, bufd, bufu):
        i = pl.program_id(1)

        @pl.when(i == 0)
        def _():
            dw_ref[...] = jnp.zeros_like(dw_ref)
            db_ref[...] = jnp.zeros_like(db_ref)

        d = d_ref[...]
        bufd[0:tb, :] = d
        bufd[tb:tb + HALO, :] = jnp.where(i < nb - 1, dn_ref[...], 0.0)
        bufu[0:HALO, :] = jnp.where(i > 0, up_ref[...], 0.0)
        bufu[HALO:HALO + tb, :] = u_ref[...]
        acc = jnp.zeros((tb, tc), F32)
        for k in range(taps):
            acc = acc + w_ref[k:k + 1, :] * bufd[pl.ds(taps - 1 - k, tb), :]
        du_ref[...] = acc
        for k in range(taps):
            dw_ref[k:k + 1, :] += jnp.sum(d * bufu[pl.ds(HALO - (taps - 1) + k, tb), :], axis=0, keepdims=True)
        db_ref[...] += jnp.sum(d, axis=0, keepdims=True)

    return pl.pallas_call(
        body, name=name, grid=(ch // tc, nb),
        in_specs=[pl.BlockSpec((tb, tc), lambda j, i: (i, j)),
                  pl.BlockSpec((HALO, tc), lambda j, i: (jnp.minimum((i + 1) * hb, last_halo), j)),
                  pl.BlockSpec((tb, tc), lambda j, i: (i, j)),
                  pl.BlockSpec((HALO, tc), lambda j, i: (jnp.maximum(i * hb - 1, 0), j)),
                  pl.BlockSpec((HALO, tc), lambda j, i: (0, j))],
        out_specs=[pl.BlockSpec((tb, tc), lambda j, i: (i, j)),
                   pl.BlockSpec((HALO, tc), lambda j, i: (0, j)),
                   pl.BlockSpec((1, tc), lambda j, i: (0, j))],
        out_shape=[jax.ShapeDtypeStruct((rows, ch), F32), jax.ShapeDtypeStruct((HALO, ch), F32),
                   jax.ShapeDtypeStruct((1, ch), F32)],
        scratch_shapes=[pltpu.VMEM((tb + HALO, tc), F32), pltpu.VMEM((tb + HALO, tc), F32)],
        compiler_params=_cparams(2),
    )(dpre, dpre, u, u, w8)


def _gate_fwd(pre, *, name):
    rows, c2 = pre.shape
    f = c2 // 2
    tb = _tile(rows, 640, 128)
    tc = _tile(f, 512, 128)
    nct = f // tc

    def body(p1_ref, p2_ref, o_ref):
        p1 = p1_ref[...]
        o_ref[...] = (p1 * _sigmoid(p1) * p2_ref[...]).astype(BF16)

    return pl.pallas_call(
        body, name=name, grid=(rows // tb, nct),
        in_specs=[pl.BlockSpec((tb, tc), lambda i, j: (i, j)),
                  pl.BlockSpec((tb, tc), lambda i, j: (i, j + nct))],
        out_specs=pl.BlockSpec((tb, tc), lambda i, j: (i, j)),
        out_shape=jax.ShapeDtypeStruct((rows, f), BF16), compiler_params=_cparams(2),
    )(pre, pre)


def _gate_bwd(pre, dact, *, name):
    rows, c2 = pre.shape
    f = c2 // 2
    tb = _tile(rows, 640, 128)
    tc = _tile(f, 512, 128)
    nct = f // tc

    def body(p1_ref, p2_ref, d_ref, o_ref):
        s = pl.program_id(1)
        p1 = p1_ref[...]
        d = d_ref[...].astype(F32)
        sig = _sigmoid(p1)
        dp1 = d * p2_ref[...] * (sig * (1.0 + p1 * (1.0 - sig)))
        dp2 = d * (p1 * sig)
        o_ref[...] = jnp.where(s == 0, dp1, dp2)

    return pl.pallas_call(
        body, name=name, grid=(rows // tb, 2, nct),
        in_specs=[pl.BlockSpec((tb, tc), lambda i, s, j: (i, j)),
                  pl.BlockSpec((tb, tc), lambda i, s, j: (i, j + nct)),
                  pl.BlockSpec((tb, tc), lambda i, s, j: (i, j))],
        out_specs=pl.BlockSpec((tb, tc), lambda i, s, j: (i, s * nct + j)),
        out_shape=jax.ShapeDtypeStruct((rows, c2), F32), compiler_params=_cparams(3),
    )(pre, pre, dact)


def _rope(xr, cos_t, sin_t, *, name, transpose=False):
    rows, w = xr.shape
    tr = _tile(rows, 640, 128)

    def body(x_ref, c_ref, s_ref, o_ref):
        xv = x_ref[...]
        if transpose:
            o_ref[...] = xv * c_ref[...] + pltpu.roll(xv * s_ref[...], 64, 1)
        else:
            o_ref[...] = xv * c_ref[...] + pltpu.roll(xv, 64, 1) * s_ref[...]

    spec = pl.BlockSpec((tr, w), lambda i: (i, 0))
    return pl.pallas_call(
        body, name=name, grid=(rows // tr,), in_specs=[spec, spec, spec], out_specs=spec,
        out_shape=jax.ShapeDtypeStruct((rows, w), F32), compiler_params=_cparams(1),
    )(xr, cos_t, sin_t)


ATT_TQ = 640
ATT_GROUP = 4


def _grouped_loop(lo, hi, step, carry, *, descending=False):
    n = hi - lo
    n_groups = lax.div(n, jnp.int32(ATT_GROUP))
    rest = n - n_groups * ATT_GROUP
    if descending:
        carry = lax.fori_loop(
            0, n_groups, lambda t, c: step([hi - 1 - ATT_GROUP * t - b for b in range(ATT_GROUP)], c), carry)
        return lax.fori_loop(0, rest, lambda t, c: step([lo + rest - 1 - t], c), carry)
    carry = lax.fori_loop(0, n_groups, lambda t, c: step([lo + ATT_GROUP * t + b for b in range(ATT_GROUP)], c), carry)
    return lax.fori_loop(0, rest, lambda t, c: step([hi - rest + t], c), carry)


def _tile_iotas(rows=BLOCK):
    r_i = lax.broadcasted_iota(jnp.int32, (rows, BLOCK), 0)
    c_i = lax.broadcasted_iota(jnp.int32, (rows, BLOCK), 1)
    return r_i, c_i


def _key_ranges(i, tq):
    n_blocks = ((i + 1) * tq + (BLOCK - 1)) >> 7
    first_diag = jnp.maximum((i * tq) >> 7, 1)
    return first_diag, n_blocks


def _dot_tn(a, b):
    return lax.dot_general(a, b, (((0,), (0,)), ((), ())), preferred_element_type=F32)


def _cumsum_rhs(pred):
    r = lax.broadcasted_iota(jnp.int32, (2 * BLOCK, 2 * BLOCK), 0) & (BLOCK - 1)
    c = lax.broadcasted_iota(jnp.int32, (2 * BLOCK, 2 * BLOCK), 1)
    return _ones_where((c >= BLOCK) | pred(r, c))


def _cumsum_dot(x, rhs):
    hi, lo = _hilo(x)
    r = _dot(jnp.concatenate([hi, lo], axis=1), rhs)
    return r[:, :BLOCK], r[:, BLOCK:]


def _sb_fwd(q, k, v, *, name):
    nh, rows, hd = q.shape
    tq = _tile(rows, ATT_TQ, 8)
    scale = SB_HEAD_DIM ** -0.5

    def body(q_ref, k_ref, v_ref, o_ref, u_ref):
        i = pl.program_id(1)
        r_i, c_i = _tile_iotas(tq)
        m_after = _cumsum_rhs(lambda j, s: j > s)
        qb = (q_ref[0] * scale).astype(BF16)
        rowpos = i * tq + r_i
        first_diag, n_blocks = _key_ranges(i, tq)

        def step(js, carry, masked):
            acc, cu = carry
            offs = [pl.multiple_of(j * BLOCK, BLOCK) for j in js]
            zs = [_dot_nt(qb, k_ref[0, pl.ds(off, BLOCK), :].astype(BF16)) for off in offs]
            sp = [_softplus(z) for z in zs]
            if masked:
                masks = [((off + c_i) < rowpos) & ((off + c_i) >= N_PAD) for off in offs]
                cs = [_cumsum_dot(jnp.where(m_, s, 0.0), m_after) for m_, s in zip(masks, sp)]
            else:
                cs = [_cumsum_dot(s, m_after) for s in sp]
            wgt = []
            for b in range(len(js)):
                w_ = jnp.exp(zs[b] - sp[b] - (cu + cs[b][0]))
                wgt.append(jnp.where(masks[b], w_, 0.0) if masked else w_)
                cu = cu + cs[b][1]
            for b, off in enumerate(offs):
                acc = acc + _dot(wgt[b].astype(BF16), v_ref[0, pl.ds(off, BLOCK), :].astype(BF16))
            return acc, cu

        carry = (jnp.zeros((tq, hd), F32), jnp.zeros((tq, BLOCK), F32))
        carry = _grouped_loop(first_diag, n_blocks, lambda js, c: step(js, c, True), carry, descending=True)
        carry = _grouped_loop(1, first_diag, lambda js, c: step(js, c, False), carry, descending=True)
        acc, cu = step([0], carry, True)
        o_ref[0] = acc
        u_ref[0] = -cu

    blk = pl.BlockSpec((1, tq, hd), lambda h, i: (h, i, 0))
    full = pl.BlockSpec((1, rows, hd), lambda h, i: (h, 0, 0))
    return pl.pallas_call(
        body, name=name, grid=(nh, rows // tq), in_specs=[blk, full, full],
        out_specs=[blk, pl.BlockSpec((1, tq, 128), lambda h, i: (h, i, 0))],
        out_shape=[jax.ShapeDtypeStruct((nh, rows, hd), F32), jax.ShapeDtypeStruct((nh, rows, 128), F32)],
        compiler_params=_cparams(2),
    )(q, k, v)


def _sb_bwd(q, k, v, do, u_tot, *, name):
    nh, rows, hd = q.shape
    tq = _tile(rows, ATT_TQ, 8)
    scale = SB_HEAD_DIM ** -0.5

    def body(q_ref, k_ref, v_ref, do_ref, u_ref, dq_ref, dk_ref, dv_ref):
        i = pl.program_id(1)

        @pl.when(i == 0)
        def _():
            dk_ref[...] = jnp.zeros_like(dk_ref)
            dv_ref[...] = jnp.zeros_like(dv_ref)

        r_i, c_i = _tile_iotas(tq)
        m_incl = _cumsum_rhs(lambda j, s: j <= s)
        m_excl = _cumsum_rhs(lambda j, s: j < s)
        qb = (q_ref[0] * scale).astype(BF16)
        dob = do_ref[0].astype(BF16)
        rowpos = i * tq + r_i
        first_diag, n_blocks = _key_ranges(i, tq)

        def step(js, carry, masked):
            dq, rem, cg = carry
            nb = range(len(js))
            offs = [pl.multiple_of(j * BLOCK, BLOCK) for j in js]
            kbs = [k_ref[0, pl.ds(off, BLOCK), :].astype(BF16) for off in offs]
            vbs = [v_ref[0, pl.ds(off, BLOCK), :].astype(BF16) for off in offs]
            zs = [_dot_nt(qb, kb) for kb in kbs]
            dws = [_dot_nt(dob, vb) for vb in vbs]
            sp = [_softplus(z) for z in zs]
            sig = [jnp.exp(zs[b] - sp[b]) for b in nb]
            if masked:
                masks = [((off + c_i) < rowpos) & ((off + c_i) >= N_PAD) for off in offs]
                cs = [_cumsum_dot(jnp.where(masks[b], sp[b], 0.0), m_incl) for b in nb]
            else:
                cs = [_cumsum_dot(sp[b], m_incl) for b in nb]
            wgt, gg = [], []
            for b in nb:
                w_ = jnp.exp(jnp.minimum(zs[b] - sp[b] - (rem - cs[b][0]), 0.0))
                wgt.append(jnp.where(masks[b], w_, 0.0) if masked else w_)
                gg.append(wgt[b] * dws[b])
                rem = rem - cs[b][1]
            gs = [_cumsum_dot(g_, m_excl) for g_ in gg]
            dzb = []
            for b in nb:
                dz = gg[b] * (1.0 - sig[b]) - sig[b] * (cg + gs[b][0])
                dzb.append((jnp.where(masks[b], dz, 0.0) if masked else dz).astype(BF16))
                cg = cg + gs[b][1]
            for b, off in enumerate(offs):
                dq = dq + _dot(dzb[b], kbs[b])
                dk_ref[0, pl.ds(off, BLOCK), :] += _dot_tn(dzb[b], qb)
                dv_ref[0, pl.ds(off, BLOCK), :] += _dot_tn(wgt[b].astype(BF16), dob)
            return dq, rem, cg

        carry = step([0], (jnp.zeros((tq, hd), F32), -u_ref[0], jnp.zeros((tq, BLOCK), F32)), True)
        carry = _grouped_loop(1, first_diag, lambda js, c: step(js, c, False), carry)
        dq, _, _ = _grouped_loop(first_diag, n_blocks, lambda js, c: step(js, c, True), carry)
        dq_ref[0] = dq * scale

    blk = pl.BlockSpec((1, tq, hd), lambda h, i: (h, i, 0))
    full = pl.BlockSpec((1, rows, hd), lambda h, i: (h, 0, 0))
    ublk = pl.BlockSpec((1, tq, 128), lambda h, i: (h, i, 0))
    sds = jax.ShapeDtypeStruct((nh, rows, hd), F32)
    return pl.pallas_call(
        body, name=name, grid=(nh, rows // tq), in_specs=[blk, full, full, blk, ublk],
        out_specs=[blk, full, full], out_shape=[sds, sds, sds], compiler_params=_cparams(2),
    )(q, k, v, do, u_tot)


def _mla_fwd(q, k, v, *, name):
    nh, rows, dk = q.shape
    dv = v.shape[2]
    tq = _tile(rows, ATT_TQ, 8)
    scale = (MLA_NOPE + MLA_ROPE) ** -0.5

    def body(q_ref, k_ref, v_ref, o_ref, lse_ref):
        i = pl.program_id(1)
        r_i, c_i = _tile_iotas(tq)
        qb = q_ref[0].astype(BF16)
        rowpos = i * tq + r_i
        first_diag, n_blocks = _key_ranges(i, tq)

        def step(js, carry, masked):
            m, l, acc = carry
            offs = [pl.multiple_of(j * BLOCK, BLOCK) for j in js]
            ss = [_dot_nt(qb, k_ref[0, pl.ds(off, BLOCK), :].astype(BF16)) * scale for off in offs]
            if masked:
                ss = [jnp.where(((off + c_i) <= rowpos) & ((off + c_i) >= N_PAD), s, NEG_BIG)
                      for off, s in zip(offs, ss)]
            m_new = m
            for s in ss:
                m_new = jnp.maximum(m_new, jnp.max(s, axis=1, keepdims=True))
            alpha = jnp.exp(m - m_new)
            ps = [jnp.exp(s - m_new) for s in ss]
            l = alpha * l
            acc = alpha * acc
            for off, p in zip(offs, ps):
                l = l + jnp.sum(p, axis=1, keepdims=True)
                acc = acc + _dot(p.astype(BF16), v_ref[0, pl.ds(off, BLOCK), :].astype(BF16))
            return m_new, l, acc

        carry = (jnp.full((tq, 1), NEG_BIG, F32), jnp.zeros((tq, 1), F32), jnp.zeros((tq, dv), F32))
        carry = step([0], carry, True)
        carry = _grouped_loop(1, first_diag, lambda js, c: step(js, c, False), carry)
        m, l, acc = _grouped_loop(first_diag, n_blocks, lambda js, c: step(js, c, True), carry)
        o_ref[0] = acc / l
        lse_ref[0] = jnp.broadcast_to(m + jnp.log(l), (tq, 128))

    qblk = pl.BlockSpec((1, tq, dk), lambda h, i: (h, i, 0))
    kfull = pl.BlockSpec((1, rows, dk), lambda h, i: (h, 0, 0))
    vfull = pl.BlockSpec((1, rows, dv), lambda h, i: (h, 0, 0))
    return pl.pallas_call(
        body, name=name, grid=(nh, rows // tq), in_specs=[qblk, kfull, vfull],
        out_specs=[pl.BlockSpec((1, tq, dv), lambda h, i: (h, i, 0)),
                   pl.BlockSpec((1, tq, 128), lambda h, i: (h, i, 0))],
        out_shape=[jax.ShapeDtypeStruct((nh, rows, dv), F32), jax.ShapeDtypeStruct((nh, rows, 128), F32)],
        compiler_params=_cparams(2),
    )(q, k, v)


def _mla_bwd(q, k, v, o, lse, do, *, name):
    nh, rows, dk = q.shape
    dv = v.shape[2]
    tq = _tile(rows, ATT_TQ, 8)
    scale = (MLA_NOPE + MLA_ROPE) ** -0.5

    def body(q_ref, k_ref, v_ref, o_ref, lse_ref, do_ref, dq_ref, dk_ref, dv_ref):
        i = pl.program_id(1)

        @pl.when(i == 0)
        def _():
            dk_ref[...] = jnp.zeros_like(dk_ref)
            dv_ref[...] = jnp.zeros_like(dv_ref)

        r_i, c_i = _tile_iotas(tq)
        qb = q_ref[0].astype(BF16)
        dov = do_ref[0]
        dob = dov.astype(BF16)
        delta = jnp.sum(dov * o_ref[0], axis=1, keepdims=True)
        lse = lse_ref[0][:, 0:1]
        rowpos = i * tq + r_i
        first_diag, n_blocks = _key_ranges(i, tq)

        def step(js, dq, masked):
            nb = range(len(js))
            offs = [pl.multiple_of(j * BLOCK, BLOCK) for j in js]
            kbs = [k_ref[0, pl.ds(off, BLOCK), :].astype(BF16) for off in offs]
            ss = [_dot_nt(qb, kb) for kb in kbs]
            dps = [_dot_nt(dob, v_ref[0, pl.ds(off, BLOCK), :].astype(BF16)) for off in offs]
            ps = [jnp.exp(jnp.minimum(s * scale - lse, 0.0)) for s in ss]
            if masked:
                ps = [jnp.where(((off + c_i) <= rowpos) & ((off + c_i) >= N_PAD), p, 0.0) for off, p in zip(offs, ps)]
            dss = [(ps[b] * (dps[b] - delta) * scale).astype(BF16) for b in nb]
            for b, off in enumerate(offs):
                dq = dq + _dot(dss[b], kbs[b])
                dk_ref[0, pl.ds(off, BLOCK), :] += _dot_tn(dss[b], qb)
                dv_ref[0, pl.ds(off, BLOCK), :] += _dot_tn(ps[b].astype(BF16), dob)
            return dq

        dq = step([0], jnp.zeros((tq, dk), F32), True)
        dq = _grouped_loop(1, first_diag, lambda js, c: step(js, c, False), dq)
        dq_ref[0] = _grouped_loop(first_diag, n_blocks, lambda js, c: step(js, c, True), dq)

    qblk = pl.BlockSpec((1, tq, dk), lambda h, i: (h, i, 0))
    vblk = pl.BlockSpec((1, tq, dv), lambda h, i: (h, i, 0))
    lblk = pl.BlockSpec((1, tq, 128), lambda h, i: (h, i, 0))
    kfull = pl.BlockSpec((1, rows, dk), lambda h, i: (h, 0, 0))
    vfull = pl.BlockSpec((1, rows, dv), lambda h, i: (h, 0, 0))
    return pl.pallas_call(
        body, name=name, grid=(nh, rows // tq), in_specs=[qblk, kfull, vfull, vblk, lblk, vblk],
        out_specs=[qblk, kfull, vfull],
        out_shape=[jax.ShapeDtypeStruct((nh, rows, dk), F32), jax.ShapeDtypeStruct((nh, rows, dk), F32),
                   jax.ShapeDtypeStruct((nh, rows, dv), F32)],
        compiler_params=_cparams(2),
    )(q, k, v, o, lse, do)


def _ssd_consts():
    r_i, c_i = _tile_iotas()
    eh = lax.broadcasted_iota(jnp.int32, (BLOCK, SSD_WIDTH), 0)
    ec = lax.broadcasted_iota(jnp.int32, (BLOCK, SSD_WIDTH), 1)
    expand = _ones_where(lax.shift_right_logical(ec, 6) == eh)
    return r_i, c_i, expand


def _ssd_common(pre_v, dtr_v, bias_v, a_v, chunk, r_i, c_i, expand):
    lower = r_i >= c_i
    sig_pre = _sigmoid(pre_v)
    xbc = pre_v * sig_pre
    xs = xbc[:, :SSD_WIDTH]
    valid = (chunk * BLOCK + lax.broadcasted_iota(jnp.int32, (BLOCK, 1), 0)) >= N_PAD
    dt_in = dtr_v + bias_v
    dtv = jnp.where(valid, _softplus(dt_in), 0.0)
    d_a = dtv * a_v
    acs = _hilo_dot_r(_ones_where(lower), d_a)
    acs_t = acs.T
    dt_exp = _hilo_dot_l(dtv, expand)
    acs_exp = _hilo_dot_l(acs, expand)
    a_last = acs_exp[BLOCK - 1:BLOCK, :]
    ea = jnp.exp(acs_exp)
    e_l = jnp.exp(a_last - acs_exp)
    ea_l = jnp.exp(a_last)
    return lower, sig_pre, xbc, xs, valid, dt_in, dtv, acs, acs_t, dt_exp, ea, e_l, ea_l


def _decay(acs, acs_t, h, lower):
    col = acs[:, h:h + 1]
    row = acs_t[h:h + 1, :]
    return jnp.where(lower, jnp.exp(jnp.minimum(col - row, 0.0)), 0.0)


def _ssd_fwd(pre, dtr, bias_row, a_row, d_exp, *, name):
    rows = pre.shape[0]
    nc = rows // BLOCK

    def body(pre_ref, dtr_ref, bias_ref, a_ref, dexp_ref, y_ref, st_ref, state):
        c = pl.program_id(0)

        @pl.when(c == 0)
        def _():
            state[...] = jnp.zeros_like(state)

        r_i, c_i, expand = _ssd_consts()
        lane_lo = c_i < 64
        (lower, _, xbc, xs, _, _, _, acs, acs_t, dt_exp, ea, e_l, ea_l) = _ssd_common(
            pre_ref[...], dtr_ref[...], bias_ref[...], a_ref[...], c, r_i, c_i, expand)
        xin = xs * dt_exp
        for g in range(2):
            bg = xbc[:, 512 + 128 * g:640 + 128 * g]
            cg = xbc[:, 768 + 128 * g:896 + 128 * g]
            bb = bg.astype(BF16)
            cbf = cg.astype(BF16)
            cb = _dot_nt(cbf, bb)
            bt = bg.T.astype(BF16)
            for pp in range(2):
                p = 2 * g + pp
                sl = slice(128 * p, 128 * p + 128)
                xp = xin[:, sl]
                xb = xp.astype(BF16)
                rs = [_dot((cb * _decay(acs, acs_t, 2 * p + hh, lower)).astype(BF16), xb) for hh in range(2)]
                ydiag = jnp.where(lane_lo, rs[0], rs[1])
                s_in = state[p]
                st_ref[0, p] = s_in
                yoff = ea[:, sl] * _dot(cbf, s_in.astype(BF16))
                y_ref[:, sl] = ydiag + yoff + xs[:, sl] * dexp_ref[:, sl]
                state[p] = ea_l[:, sl] * s_in + _dot(bt, (xp * e_l[:, sl]).astype(BF16))

    vec = pl.BlockSpec((1, 128), lambda c: (0, 0))
    return pl.pallas_call(
        body, name=name, grid=(nc,),
        in_specs=[pl.BlockSpec((BLOCK, SSD_XBC), lambda c: (c, 0)),
                  pl.BlockSpec((BLOCK, 128), lambda c: (c, 0)), vec, vec,
                  pl.BlockSpec((1, SSD_WIDTH), lambda c: (0, 0))],
        out_specs=[pl.BlockSpec((BLOCK, SSD_WIDTH), lambda c: (c, 0)),
                   pl.BlockSpec((1, 4, 128, 128), lambda c: (c, 0, 0, 0))],
        out_shape=[jax.ShapeDtypeStruct((rows, SSD_WIDTH), F32), jax.ShapeDtypeStruct((nc, 4, 128, 128), F32)],
        scratch_shapes=[pltpu.VMEM((4, 128, 128), F32)],
        compiler_params=_cparams(1),
    )(pre, dtr, bias_row, a_row, d_exp)


def _ssd_bwd(pre, dtr, bias_row, a_row, d_exp, states, dy, *, name):
    rows = pre.shape[0]
    nc = rows // BLOCK

    def body(pre_ref, dtr_ref, bias_ref, a_ref, dexp_ref, st_ref, dy_ref,
             dpre_ref, ddtr_ref, dbias_ref, da_ref, dd_ref, dstate, q_buf, dx_buf):
        step = pl.program_id(0)
        c = nc - 1 - step

        @pl.when(step == 0)
        def _():
            dstate[...] = jnp.zeros_like(dstate)
            dbias_ref[...] = jnp.zeros_like(dbias_ref)
            da_ref[...] = jnp.zeros_like(da_ref)
            dd_ref[...] = jnp.zeros_like(dd_ref)

        r_i, c_i, expand = _ssd_consts()
        lane_lo = c_i < 64
        last_row = r_i == BLOCK - 1
        pre_v = pre_ref[...]
        (lower, sig_pre, xbc, xs, valid, dt_in, dtv, acs, acs_t, dt_exp, ea, e_l, ea_l) = _ssd_common(
            pre_v, dtr_ref[...], bias_ref[...], a_ref[...], c, r_i, c_i, expand)
        dsilu = sig_pre * (1.0 + pre_v * (1.0 - sig_pre))
        xin = xs * dt_exp
        dyv = dy_ref[...]
        d_acs_diag = jnp.zeros((BLOCK, BLOCK), F32)
        for g in range(2):
            bg = xbc[:, 512 + 128 * g:640 + 128 * g]
            cg = xbc[:, 768 + 128 * g:896 + 128 * g]
            bb = bg.astype(BF16)
            cbf = cg.astype(BF16)
            cb = _dot_nt(cbf, bb)
            ct = cg.T.astype(BF16)
            dcb = jnp.zeros((BLOCK, BLOCK), F32)
            dbg = jnp.zeros((BLOCK, BLOCK), F32)
            dcg = jnp.zeros((BLOCK, BLOCK), F32)
            for pp in range(2):
                p = 2 * g + pp
                sl = slice(128 * p, 128 * p + 128)
                xp = xin[:, sl]
                xb = xp.astype(BF16)
                dyp = dyv[:, sl]
                dyb = dyp.astype(BF16)
                dxs_ = []
                for hh in range(2):
                    dec = _decay(acs, acs_t, 2 * p + hh, lower)
                    wm = cb * dec
                    dxs_.append(_dot(wm.T.astype(BF16), dyb))
                    half = lane_lo if hh == 0 else jnp.logical_not(lane_lo)
                    dwm = _dot_nt(jnp.where(half, dyp, 0.0).astype(BF16), xb)
                    dcb = dcb + dwm * dec
                    dseg = dwm * wm
                    dcol = jnp.sum(dseg, axis=1, keepdims=True) - jnp.sum(dseg.T, axis=1, keepdims=True)
                    d_acs_diag = jnp.where(c_i == 2 * p + hh, dcol, d_acs_diag)
                dxdiag =jnp.where(lane_lo, dxs_[0], dxs_[1])
                s_in = st_ref[0, p]
                sb = s_in.astype(BF16)
                ds_out = dstate[p]
                dsb = ds_out.astype(BF16)
                yoff = ea[:, sl] * _dot(cbf, sb)
                dxst = e_l[:, sl] * _dot(bb, dsb)
                dxp = dxdiag + dxst
                dye = dyp * ea[:, sl]
                dyeb = dye.astype(BF16)
                qp = dyp * yoff - xp * dxst
                lastv = (jnp.sum(xp * dxst, axis=0, keepdims=True)
                         + ea_l[:, sl] * jnp.sum(ds_out * s_in, axis=0, keepdims=True))
                q_buf[:, sl] = jnp.where(last_row, qp + lastv, qp)
                dx_buf[:, sl] = dxp
                dcg = dcg + _dot_nt(dyeb, sb)
                dbg = dbg + _dot_nt((xp * e_l[:, sl]).astype(BF16), dsb)
                dstate[p] = ea_l[:, sl] * ds_out + _dot(ct, dyeb)
            dcg = dcg + _dot(dcb.astype(BF16), bb)
            dbg = dbg + _dot(dcb.T.astype(BF16), cbf)
            bsl = slice(512 + 128 * g, 640 + 128 * g)
            csl = slice(768 + 128 * g, 896 + 128 * g)
            dpre_ref[:, bsl] = dbg * dsilu[:, bsl]
            dpre_ref[:, csl] = dcg * dsilu[:, csl]
        dxall = dx_buf[...]
        dpre_ref[:, :SSD_WIDTH] = (dyv * dexp_ref[...] + dxall * dt_exp) * dsilu[:, :SSD_WIDTH]
        dd_ref[...] += jnp.sum(dyv * xs, axis=0, keepdims=True)
        d_acs = d_acs_diag + _hilo_dot_nt(q_buf[...], expand)
        dd_a = _hilo_dot_r(_ones_where(r_i <= c_i), d_acs)
        ddt = dd_a * a_ref[...] + _hilo_dot_nt(dxall * xs, expand)
        ddt = jnp.where(valid, ddt, 0.0)
        da_ref[...] += jnp.sum(dd_a * dtv, axis=0, keepdims=True)
        ddtr = ddt * _sigmoid(dt_in)
        ddtr_ref[...] = ddtr
        dbias_ref[...] += jnp.sum(ddtr, axis=0, keepdims=True)

    vec = pl.BlockSpec((1, 128), lambda s: (0, 0))
    wide = pl.BlockSpec((1, SSD_WIDTH), lambda s: (0, 0))
    rev = lambda s: (nc - 1 - s, 0)
    return pl.pallas_call(
        body, name=name, grid=(nc,),
        in_specs=[pl.BlockSpec((BLOCK, SSD_XBC), rev), pl.BlockSpec((BLOCK, 128), rev), vec, vec, wide,
                  pl.BlockSpec((1, 4, 128, 128), lambda s: (nc - 1 - s, 0, 0, 0)),
                  pl.BlockSpec((BLOCK, SSD_WIDTH), rev)],
        out_specs=[pl.BlockSpec((BLOCK, SSD_XBC), rev), pl.BlockSpec((BLOCK, 128), rev), vec, vec, wide],
        out_shape=[jax.ShapeDtypeStruct((rows, SSD_XBC), F32), jax.ShapeDtypeStruct((rows, 128), F32),
                   jax.ShapeDtypeStruct((1, 128), F32), jax.ShapeDtypeStruct((1, 128), F32),
                   jax.ShapeDtypeStruct((1, SSD_WIDTH), F32)],
        scratch_shapes=[pltpu.VMEM((4, 128, 128), F32), pltpu.VMEM((BLOCK, SSD_WIDTH), F32),
                        pltpu.VMEM((BLOCK, SSD_WIDTH), F32)],
        compiler_params=_cparams(1),
    )(pre, dtr, bias_row, a_row, d_exp, states, dy)


def _peer(xi, yi, ci, k):
    px = (1 - xi) if (k >> 2) & 1 else xi
    py = (1 - yi) if (k >> 1) & 1 else yi
    pc = (1 - ci) if k & 1 else ci
    return (px, py, pc), 4 * px + 2 * py + pc


def _exchange(xs, *, gather, name):
    n = len(xs)
    n_peers = N_DEV - 1
    out_shape = [jax.ShapeDtypeStruct((N_DEV,) + x.shape if gather else x.shape, x.dtype) for x in xs]

    def body(*refs):
        x_refs, o_refs = refs[:n], refs[n:2 * n]
        send_sems, recv_sems, local_sems = refs[2 * n:]
        xi, yi, ci = lax.axis_index("x"), lax.axis_index("y"), lax.axis_index("c")
        me = 4 * xi + 2 * yi + ci

        def copy(a, k, src_idx, dst_idx, peer):
            src = x_refs[a] if gather else x_refs[a].at[src_idx]
            return pltpu.make_async_remote_copy(
                src_ref=src, dst_ref=o_refs[a].at[dst_idx], send_sem=send_sems.at[a * n_peers + k - 1],
                recv_sem=recv_sems.at[a * n_peers + k - 1], device_id=peer, device_id_type=pl.DeviceIdType.MESH)

        local = [pltpu.make_async_copy(x_refs[a] if gather else x_refs[a].at[me], o_refs[a].at[me], local_sems.at[a])
                 for a in range(n)]
        for cp in local:
            cp.start()
        sends = []
        for k in range(1, N_DEV):
            peer, pidx = _peer(xi, yi, ci, k)
            for a in range(n):
                sends.append(copy(a, k, pidx, me, peer))
                sends[-1].start()
        for k in range(1, N_DEV):
            peer, pidx = _peer(xi, yi, ci, k)
            for a in range(n):
                copy(a, k, pidx, pidx, peer).wait_recv()
        for cp in sends:
            cp.wait_send()
        for cp in local:
            cp.wait()

    hbm = pl.BlockSpec(memory_space=pltpu.HBM)
    return pl.pallas_call(
        body, name=name, out_shape=out_shape, in_specs=[hbm] * n, out_specs=[hbm] * n,
        scratch_shapes=[pltpu.SemaphoreType.DMA((n * n_peers,)), pltpu.SemaphoreType.DMA((n * n_peers,)),
                        pltpu.SemaphoreType.DMA((n,))],
    )(*xs)


def _adamw(g8, w, m, v, *, name):
    rows, cols = w.shape
    lanes = -(-cols // 128) * 128
    tr = _tile(rows, max(8, (128 * 1024) // lanes), 8)

    def body(g_ref, w_ref, m_ref, v_ref, go_ref, d_ref, mo_ref, vo_ref):
        g = g_ref[0]
        for j in range(1, N_DEV):
            g = g + g_ref[j]
        m2 = ADAM_B1 * m_ref[...] + (1.0 - ADAM_B1) * g
        v2 = ADAM_B2 * v_ref[...] + (1.0 - ADAM_B2) * (g * g)
        m_hat = m2 / (1.0 - ADAM_B1 ** ADAM_STEP)
        v_hat = v2 / (1.0 - ADAM_B2 ** ADAM_STEP)
        go_ref[...] = g
        d_ref[...] = -ADAM_LR * (m_hat / (jnp.sqrt(v_hat) + ADAM_EPS) + ADAM_WD * w_ref[...])
        mo_ref[...] = m2
        vo_ref[...] = v2

    spec = pl.BlockSpec((tr, cols), lambda i: (i, 0))
    sds = jax.ShapeDtypeStruct((rows, cols), F32)
    return pl.pallas_call(
        body, name=name, grid=(rows // tr,),
        in_specs=[pl.BlockSpec((N_DEV, tr, cols), lambda i: (0, i, 0)), spec, spec, spec],
        out_specs=[spec, spec, spec, spec], out_shape=[sds, sds, sds, sds], compiler_params=_cparams(1),
    )(g8, w, m, v)


SHARDED = (("meta_tokens", 1), ("w_in", 2), ("ssd_conv_w", 2), ("mla_w_uq", 2), ("mla_w_ukv", 2),
           ("w_out", 1), ("ffn_w_up", 2), ("ffn_conv_w", 2), ("ffn_w_down", 1))
BIG = ("w_in", "w_out", "ffn_w_up", "ffn_w_down")
SMALL = ("meta_tokens", "ssd_conv_w", "mla_w_uq", "mla_w_ukv", "ffn_conv_w")
REPLICATED = ("norm_mix_g", "ssd_conv_b", "ssd_dt_bias", "ssd_a_log", "ssd_d", "ssd_norm_g", "sb_norm_g",
              "mla_q_norm_g", "mla_kv_norm_g", "mla_norm_g", "norm_ffn_g", "ffn_conv_b", "final_norm_g")
WEIGHTS = ("meta_tokens", "norm_mix_g", "w_in", "ssd_conv_w", "ssd_conv_b", "ssd_dt_bias", "ssd_a_log", "ssd_d",
           "ssd_norm_g", "sb_norm_g", "mla_q_norm_g", "mla_kv_norm_g", "mla_w_uq", "mla_w_ukv", "mla_norm_g",
           "w_out", "norm_ffn_g", "ffn_w_up", "ffn_conv_w", "ffn_conv_b", "ffn_w_down", "final_norm_g")


def _flat_pack(arrays, dtype, align):
    flat = jnp.concatenate([a.reshape(-1).astype(dtype) for a in arrays])
    pad = (-flat.shape[0]) % align
    return jnp.pad(flat, (0, pad)).reshape(-1, 128)


def _pieces(full, axis):
    shp = full.shape
    t = full.reshape(shp[:axis] + (N_DEV, shp[axis] // N_DEV) + shp[axis + 1:])
    return jnp.moveaxis(t, axis, 0).reshape(N_DEV, -1)


def _unpieces(p8, shard_shape, axis):
    t = p8.reshape((N_DEV,) + shard_shape)
    t = jnp.moveaxis(t, 0, axis)
    return t.reshape(shard_shape[:axis] + (N_DEV * shard_shape[axis],) + shard_shape[axis + 1:])


def _split_blocks(full, axis):
    shp = full.shape
    t = full.reshape(shp[:axis] + (N_DEV, shp[axis] // N_DEV) + shp[axis + 1:])
    return jnp.moveaxis(t, axis, 0)


def _merge_blocks(b8, axis):
    shard = b8.shape[1:]
    t = jnp.moveaxis(b8, 0, axis)
    return t.reshape(shard[:axis] + (N_DEV * shard[axis],) + shard[axis + 1:])


def _gather_weights(shards):
    axes = dict(SHARDED)
    got = _exchange([shards[n].astype(BF16) for n in BIG], gather=True, name="gather_big")
    full = {n: _merge_blocks(b8, axes[n]) for n, b8 in zip(BIG, got)}
    packed = _flat_pack([shards[n] for n in SMALL], F32, 8 * 128)
    got = _exchange([packed], gather=True, name="gather_small")[0].reshape(N_DEV, -1)
    off = 0
    for n in SMALL:
        size = math.prod(shards[n].shape)
        full[n] = _unpieces(got[:, off:off + size], shards[n].shape, axes[n])
        off += size
    return full


def _pad_cols(a, width):
    return jnp.pad(a, ((0, 0), (0, width - a.shape[1])))


def _w_in_padded(w):
    kr = w[:, 2632:2664]
    return jnp.concatenate([
        w[:, 0:512], w[:, 512:1536], w[:, 1544:2312], _pad_cols(w[:, 2312:2504], 256), w[:, 2504:2632],
        _pad_cols(kr[:, :16], 64), _pad_cols(kr[:, 16:], 64), _pad_cols(w[:, 1536:1544], 128),
        jnp.zeros((w.shape[0], 128), w.dtype)], axis=1)


def _w_in_unpadded(wp):
    return jnp.concatenate([
        wp[:, 0:512], wp[:, 512:1536], wp[:, OFF_DT:OFF_DT + 8], wp[:, 1536:2304], wp[:, OFF_QA:OFF_QA + 192],
        wp[:, OFF_CKV:OFF_CKV + 128], wp[:, OFF_KR:OFF_KR + 16], wp[:, OFF_KR + 64:OFF_KR + 80]], axis=1)


def _w_uq_perm(w):
    t = w.reshape(MLA_Q_RANK, MLA_HEADS, MLA_NOPE + MLA_ROPE)
    out = jnp.concatenate([t[:, :, :64].reshape(MLA_Q_RANK, 256), t[:, :, 64:80].reshape(MLA_Q_RANK, 64),
                           t[:, :, 80:96].reshape(MLA_Q_RANK, 64)], axis=1)
    return jnp.pad(out, ((0, 256 - MLA_Q_RANK), (0, 0)))


def _w_uq_unperm(wp):
    wp = wp[:MLA_Q_RANK]
    t = jnp.concatenate([wp[:, :256].reshape(MLA_Q_RANK, 4, 64), wp[:, 256:320].reshape(MLA_Q_RANK, 4, 16),
                         wp[:, 320:384].reshape(MLA_Q_RANK, 4, 16)], axis=2)
    return t.reshape(MLA_Q_RANK, 4 * 96)


def _w_ukv_perm(w):
    t = w.reshape(MLA_KV_RANK, MLA_HEADS, 128)
    return jnp.concatenate([t[:, :, :64].reshape(MLA_KV_RANK, 256), t[:, :, 64:].reshape(MLA_KV_RANK, 256)], axis=1)


def _w_ukv_unperm(wp):
    t = jnp.concatenate([wp[:, :256].reshape(MLA_KV_RANK, 4, 64), wp[:, 256:].reshape(MLA_KV_RANK, 4, 64)], axis=2)
    return t.reshape(MLA_KV_RANK, 512)


def _heads(a, hd):
    return jnp.moveaxis(a.reshape(a.shape[0], -1, hd), 1, 0)


def _unheads(a):
    return jnp.moveaxis(a, 0, 1).reshape(a.shape[1], -1)


def _row(v, width=None):
    v = v.reshape(1, -1)
    return v if width is None else _pad_cols(v, width)


def _rope_tables(rows):
    pos = jnp.arange(rows, dtype=F32) - float(N_PAD)
    inv = 1.0 / (ROPE_BASE ** (jnp.arange(0, MLA_ROPE, 2, dtype=F32) / MLA_ROPE))
    ang = pos[:, None] * inv[None, :]
    cos = jnp.tile(jnp.cos(ang), (1, 8))
    sin = jnp.tile(jnp.sin(ang), (1, 4))
    return cos, jnp.concatenate([-sin, sin], axis=1)


def _layer_fwd(h, p, cos_t, sin_t, tag):
    s = {"h_in": h}
    hn = _rmsnorm_fwd(h, p["norm_mix_g"], width=D_MODEL, name=tag + "norm_mix")
    u = _matmul(hn, p["w_in"], name=tag + "in_proj")
    s["hn"], s["u"] = hn, u

    xbc_in = u[:, OFF_XBC:OFF_XBC + SSD_XBC]
    pre = _dwconv_fwd(xbc_in, p["ssd_conv_w"], p["ssd_conv_b"], taps=SSD_CONV, name=tag + "ssd_conv")
    dtr = u[:, OFF_DT:OFF_DT + 128]
    y_ssd, states = _ssd_fwd(pre, dtr, p["dt_bias"], p["a_row"], p["d_exp"], name=tag + "ssd_core")
    zgate = u[:, OFF_Z:OFF_Z + SSD_WIDTH]
    yn_ssd = _rmsnorm_fwd(y_ssd, p["ssd_norm_g"], width=SSD_WIDTH, z=zgate, name=tag + "ssd_norm")
    s.update(xbc_in=xbc_in, pre=pre, dtr=dtr, y_ssd=y_ssd, states=states, zgate=zgate)

    q_sb = _heads(u[:, OFF_QSB:OFF_QSB + SB_WIDTH], SB_HEAD_DIM)
    k_sb = _heads(u[:, OFF_KSB:OFF_KSB + SB_WIDTH], SB_HEAD_DIM).astype(BF16)
    v_sb = _heads(u[:, OFF_VSB:OFF_VSB + SB_WIDTH], SB_HEAD_DIM).astype(BF16)
    o_sb, u_tot = _sb_fwd(q_sb, k_sb, v_sb, name=tag + "sb_attn")
    o_sb_flat = _unheads(o_sb)
    yn_sb = _rmsnorm_fwd(o_sb_flat, p["sb_norm_g"], width=SB_WIDTH, name=tag + "sb_norm")
    s.update(q_sb=q_sb, k_sb=k_sb, v_sb=v_sb, u_tot=u_tot, o_sb_flat=o_sb_flat)

    qa = u[:, OFF_QA:OFF_QA + 256]
    ckv = u[:, OFF_CKV:OFF_CKV + 128]
    qa_n = _rmsnorm_fwd(qa, p["mla_q_norm_g"], width=MLA_Q_RANK, name=tag + "mla_qnorm")
    ckv_n = _rmsnorm_fwd(ckv, p["mla_kv_norm_g"], width=MLA_KV_RANK, name=tag + "mla_kvnorm")
    qf = _matmul(qa_n, p["mla_w_uq"], name=tag + "mla_uq")
    kvf = _matmul(ckv_n, p["mla_w_ukv"], name=tag + "mla_ukv")
    q_rope = _rope(qf[:, 256:384], cos_t, sin_t, name=tag + "rope_q")
    k_rope = _rope(u[:, OFF_KR:OFF_KR + 128], cos_t, sin_t, name=tag + "rope_k")
    rows = h.shape[0]
    zpad = jnp.zeros((MLA_HEADS, rows, 32), F32)
    qh = jnp.concatenate([_heads(qf[:, :256], 64), _heads(q_rope[:, :64], 16), _heads(q_rope[:, 64:], 16), zpad], axis=2)
    kr_b = jnp.broadcast_to(jnp.concatenate([k_rope[:, 0:16], k_rope[:, 64:80]], axis=1)[None], (MLA_HEADS, rows, 32))
    kh = jnp.concatenate([_heads(kvf[:, :256], 64), kr_b, zpad], axis=2).astype(BF16)
    vh = _heads(kvf[:, 256:], 64).astype(BF16)
    o_mla, lse = _mla_fwd(qh, kh, vh, name=tag + "mla_attn")
    o_mla_flat = _unheads(o_mla)
    yn_mla = _rmsnorm_fwd(o_mla_flat, p["mla_norm_g"], width=256, name=tag + "mla_norm")
    s.update(qa=qa, ckv=ckv, qa_n=qa_n, ckv_n=ckv_n, qh=qh, kh=kh, vh=vh, o_mla=o_mla, lse=lse,
             o_mla_flat=o_mla_flat)

    mix = jnp.concatenate([yn_ssd, yn_sb, yn_mla], axis=1)
    h_mid = _matmul(mix, p["w_out"], res=h, mask_pad=True, name=tag + "out_proj")
    hn2 = _rmsnorm_fwd(h_mid, p["norm_ffn_g"], width=D_MODEL, name=tag + "norm_ffn")
    up = _matmul(hn2, p["ffn_w_up"], tn=1408, name=tag + "ffn_up")
    fpre = _dwconv_fwd(up, p["ffn_conv_w"], p["ffn_conv_b"], taps=FFN_CONV, name=tag + "ffn_conv")
    act = _gate_fwd(fpre, name=tag + "ffn_gate")
    h_out = _matmul(act, p["ffn_w_down"], res=h_mid, mask_pad=True, tk=1408, name=tag + "ffn_down")
    s.update(mix=mix, h_mid=h_mid, hn2=hn2, up=up, fpre=fpre, act=act)
    return h_out, s


def _layer_bwd(dh_out, p, s, cos_t, sin_t, tag):
    g = {}
    rows = dh_out.shape[0]
    dact = _matmul(dh_out, p["ffn_w_down"], tb=True, tn=1408, out_dtype=BF16, name=tag + "b_down_dx")
    g["ffn_w_down"] = _matmul(s["act"], dh_out, ta=True, tm=1408, tk=640, name=tag + "b_down_dw")
    dfpre = _gate_bwd(s["fpre"], dact, name=tag + "b_gate")
    dup, dcw, dcb_ = _dwconv_bwd(dfpre, s["up"], p["ffn_conv_w"], taps=FFN_CONV, name=tag + "b_ffn_conv")
    g["ffn_conv_w"], g["ffn_conv_b"] = dcw[:FFN_CONV], dcb_[0]
    dhn2 = _matmul(dup, p["ffn_w_up"], tb=True, tk=1408, name=tag + "b_up_dx")
    g["ffn_w_up"] = _matmul(s["hn2"], dup, ta=True, tn=1408, tk=640, name=tag + "b_up_dw")
    dh_mid, _, dg = _rmsnorm_bwd(s["h_mid"], p["norm_ffn_g"], dhn2, width=D_MODEL, res=dh_out, mask_pad=True,
                                 name=tag + "b_norm_ffn")
    g["norm_ffn_g"] = dg[0]

    dmix = _matmul(dh_mid, p["w_out"], tb=True, name=tag + "b_out_dx")
    g["w_out"] = _matmul(s["mix"], dh_mid, ta=True, tk=640, name=tag + "b_out_dw")

    dy_ssd, dz, dg = _rmsnorm_bwd(s["y_ssd"], p["ssd_norm_g"], dmix[:, :SSD_WIDTH], width=SSD_WIDTH, z=s["zgate"],
                                  name=tag + "b_ssd_norm")
    g["ssd_norm_g"] = dg[0]
    dpre, ddtr, dbias, da, dd = _ssd_bwd(s["pre"], s["dtr"], p["dt_bias"], p["a_row"], p["d_exp"], s["states"],
                                         dy_ssd, name=tag + "b_ssd_core")
    g["ssd_dt_bias"] = dbias[0, :8]
    g["ssd_a_log"] = da[0, :8] * p["a_row"][0, :8]
    g["ssd_d"] = dd.reshape(8, 64).sum(axis=1)
    dxbc_in, dcw, dcb_ = _dwconv_bwd(dpre, s["xbc_in"], p["ssd_conv_w"], taps=SSD_CONV, name=tag + "b_ssd_conv")
    g["ssd_conv_w"], g["ssd_conv_b"] = dcw[:SSD_CONV], dcb_[0]

    do_sb_flat, _, dg = _rmsnorm_bwd(s["o_sb_flat"], p["sb_norm_g"], dmix[:, 512:768], width=SB_WIDTH,
                                     name=tag + "b_sb_norm")
    g["sb_norm_g"] = dg[0]
    dq_sb, dk_sb, dv_sb = _sb_bwd(s["q_sb"], s["k_sb"], s["v_sb"], _heads(do_sb_flat, SB_HEAD_DIM), s["u_tot"],
                                  name=tag + "b_sb_attn")

    do_mla_flat, _, dg = _rmsnorm_bwd(s["o_mla_flat"], p["mla_norm_g"], dmix[:, 768:1024], width=256,
                                      name=tag + "b_mla_norm")
    g["mla_norm_g"] = dg[0]
    dqh, dkh, dvh = _mla_bwd(s["qh"], s["kh"], s["vh"], s["o_mla"], s["lse"], _heads(do_mla_flat, 64),
                             name=tag + "b_mla_attn")
    dq_rope_in = jnp.concatenate([_unheads(dqh[:, :, 64:80]), _unheads(dqh[:, :, 80:96])], axis=1)
    dq_r = _rope(dq_rope_in, cos_t, sin_t, transpose=True, name=tag + "b_rope_q")
    dqf = jnp.concatenate([_unheads(dqh[:, :, :64]), dq_r], axis=1)
    dkr_sum = jnp.sum(dkh[:, :, 64:96], axis=0)
    dk_rope_in = jnp.concatenate([_pad_cols(dkr_sum[:, :16], 64), _pad_cols(dkr_sum[:, 16:], 64)], axis=1)
    dkr = _rope(dk_rope_in, cos_t, sin_t, transpose=True, name=tag + "b_rope_k")
    dkvf = jnp.concatenate([_unheads(dkh[:, :, :64]), _unheads(dvh)], axis=1)
    dqa_n = _matmul(dqf, p["mla_w_uq"], tb=True, name=tag + "b_uq_dx")
    g["mla_w_uq"] = _matmul(s["qa_n"], dqf, ta=True, tk=640, name=tag + "b_uq_dw")
    dckv_n = _matmul(dkvf, p["mla_w_ukv"], tb=True, name=tag + "b_ukv_dx")
    g["mla_w_ukv"] = _matmul(s["ckv_n"], dkvf, ta=True, tk=640, name=tag + "b_ukv_dw")
    dqa, _, dg = _rmsnorm_bwd(s["qa"], p["mla_q_norm_g"], dqa_n, width=MLA_Q_RANK, name=tag + "b_mla_qnorm")
    g["mla_q_norm_g"] = dg[0, :MLA_Q_RANK]
    dckv, _, dg = _rmsnorm_bwd(s["ckv"], p["mla_kv_norm_g"], dckv_n, width=MLA_KV_RANK, name=tag + "b_mla_kvnorm")
    g["mla_kv_norm_g"] = dg[0]

    du = jnp.concatenate([dz, dxbc_in, _unheads(dq_sb), _unheads(dk_sb), _unheads(dv_sb), dqa, dckv, dkr, ddtr,
                          jnp.zeros((rows, 128), F32)], axis=1)
    dhn = _matmul(du, p["w_in"], tb=True, name=tag + "b_in_dx")
    g["w_in"] = _matmul(s["hn"], du, ta=True, tk=640, name=tag + "b_in_dw")
    dh_in, _, dg = _rmsnorm_bwd(s["h_in"], p["norm_mix_g"], dhn, width=D_MODEL, res=dh_mid, mask_pad=True,
                                name=tag + "b_norm_mix")
    g["norm_mix_g"] = dg[0]
    return dh_in, g


def _prepare_layer(full, rep, l):
    a_row = _row(-jnp.exp(rep["ssd_a_log"][l]), 128)
    return {
        "norm_mix_g": _row(rep["norm_mix_g"][l]),
        "w_in": _w_in_padded(full["w_in"][l]),
        "ssd_conv_w": jnp.pad(full["ssd_conv_w"][l], ((0, HALO - SSD_CONV), (0, 0))),
        "ssd_conv_b": _row(rep["ssd_conv_b"][l]),
        "dt_bias": _row(rep["ssd_dt_bias"][l], 128),
        "a_row": a_row,
        "d_exp": _row(jnp.repeat(rep["ssd_d"][l], 64)),
        "ssd_norm_g": _row(rep["ssd_norm_g"][l]),
        "sb_norm_g": _row(rep["sb_norm_g"][l]),
        "mla_q_norm_g": _row(rep["mla_q_norm_g"][l], 256),
        "mla_kv_norm_g": _row(rep["mla_kv_norm_g"][l]),
        "mla_w_uq": _w_uq_perm(full["mla_w_uq"][l]),
        "mla_w_ukv": _w_ukv_perm(full["mla_w_ukv"][l]),
        "mla_norm_g": _row(rep["mla_norm_g"][l]),
        "w_out": full["w_out"][l],
        "norm_ffn_g": _row(rep["norm_ffn_g"][l]),
        "ffn_w_up": full["ffn_w_up"][l],
        "ffn_conv_w": jnp.pad(full["ffn_conv_w"][l], ((0, HALO - FFN_CONV), (0, 0))),
        "ffn_conv_b": _row(rep["ffn_conv_b"][l]),
        "ffn_w_down": full["ffn_w_down"][l],
    }


def _layer_grads_to_full(g):
    out = dict(g)
    out["w_in"] = _w_in_unpadded(g["w_in"])
    out["mla_w_uq"] = _w_uq_unperm(g["mla_w_uq"])
    out["mla_w_ukv"] = _w_ukv_unperm(g["mla_w_ukv"])
    return out


def kernel(x, meta_tokens, norm_mix_g, w_in, ssd_conv_w, ssd_conv_b, ssd_dt_bias, ssd_a_log, ssd_d, ssd_norm_g, sb_norm_g, mla_q_norm_g, mla_kv_norm_g, mla_w_uq, mla_w_ukv, mla_norm_g, w_out, norm_ffn_g, ffn_w_up, ffn_conv_w, ffn_conv_b, ffn_w_down, final_norm_g, loss_target, m_meta_tokens, m_norm_mix_g, m_w_in, m_ssd_conv_w, m_ssd_conv_b, m_ssd_dt_bias, m_ssd_a_log, m_ssd_d, m_ssd_norm_g, m_sb_norm_g, m_mla_q_norm_g, m_mla_kv_norm_g, m_mla_w_uq, m_mla_w_ukv, m_mla_norm_g, m_w_out, m_norm_ffn_g, m_ffn_w_up, m_ffn_conv_w, m_ffn_conv_b, m_ffn_w_down, m_final_norm_g, v_meta_tokens, v_norm_mix_g, v_w_in, v_ssd_conv_w, v_ssd_conv_b, v_ssd_dt_bias, v_ssd_a_log, v_ssd_d, v_ssd_norm_g, v_sb_norm_g, v_mla_q_norm_g, v_mla_kv_norm_g, v_mla_w_uq, v_mla_w_ukv, v_mla_norm_g, v_w_out, v_norm_ffn_g, v_ffn_w_up, v_ffn_conv_w, v_ffn_conv_b, v_ffn_w_down, v_final_norm_g):
    w = dict(meta_tokens=meta_tokens, norm_mix_g=norm_mix_g, w_in=w_in, ssd_conv_w=ssd_conv_w, ssd_conv_b=ssd_conv_b,
             ssd_dt_bias=ssd_dt_bias, ssd_a_log=ssd_a_log, ssd_d=ssd_d, ssd_norm_g=ssd_norm_g, sb_norm_g=sb_norm_g,
             mla_q_norm_g=mla_q_norm_g, mla_kv_norm_g=mla_kv_norm_g, mla_w_uq=mla_w_uq, mla_w_ukv=mla_w_ukv,
             mla_norm_g=mla_norm_g, w_out=w_out, norm_ffn_g=norm_ffn_g, ffn_w_up=ffn_w_up, ffn_conv_w=ffn_conv_w,
             ffn_conv_b=ffn_conv_b, ffn_w_down=ffn_w_down, final_norm_g=final_norm_g)
    mom = dict(meta_tokens=m_meta_tokens, norm_mix_g=m_norm_mix_g, w_in=m_w_in, ssd_conv_w=m_ssd_conv_w,
               ssd_conv_b=m_ssd_conv_b, ssd_dt_bias=m_ssd_dt_bias, ssd_a_log=m_ssd_a_log, ssd_d=m_ssd_d,
               ssd_norm_g=m_ssd_norm_g, sb_norm_g=m_sb_norm_g, mla_q_norm_g=m_mla_q_norm_g,
               mla_kv_norm_g=m_mla_kv_norm_g, mla_w_uq=m_mla_w_uq, mla_w_ukv=m_mla_w_ukv, mla_norm_g=m_mla_norm_g,
               w_out=m_w_out, norm_ffn_g=m_norm_ffn_g, ffn_w_up=m_ffn_w_up, ffn_conv_w=m_ffn_conv_w,
               ffn_conv_b=m_ffn_conv_b, ffn_w_down=m_ffn_w_down, final_norm_g=m_final_norm_g)
    vel = dict(meta_tokens=v_meta_tokens, norm_mix_g=v_norm_mix_g, w_in=v_w_in, ssd_conv_w=v_ssd_conv_w,
               ssd_conv_b=v_ssd_conv_b, ssd_dt_bias=v_ssd_dt_bias, ssd_a_log=v_ssd_a_log, ssd_d=v_ssd_d,
               ssd_norm_g=v_ssd_norm_g, sb_norm_g=v_sb_norm_g, mla_q_norm_g=v_mla_q_norm_g,
               mla_kv_norm_g=v_mla_kv_norm_g, mla_w_uq=v_mla_w_uq, mla_w_ukv=v_mla_w_ukv, mla_norm_g=v_mla_norm_g,
               w_out=v_w_out, norm_ffn_g=v_norm_ffn_g, ffn_w_up=v_ffn_w_up, ffn_conv_w=v_ffn_conv_w,
               ffn_conv_b=v_ffn_conv_b, ffn_w_down=v_ffn_w_down, final_norm_g=v_final_norm_g)

    full = _gather_weights({n: w[n] for n, _ in SHARDED})
    layers = [_prepare_layer(full, w, l) for l in range(DEPTH)]

    seq = x.shape[1]
    rows = BLOCK + seq
    cos_t, sin_t = _rope_tables(rows)
    h = jnp.concatenate([jnp.zeros((N_PAD, D_MODEL), F32), full["meta_tokens"], x[0]], axis=0)

    saved = []
    for l in range(DEPTH):
        h, s = _layer_fwd(h, layers[l], cos_t, sin_t, "l%d_" % l)
        saved.append(s)
    dh, dg_final, loss_part = _final_loss(h, _row(final_norm_g), loss_target[0], name="final_loss")
    loss = lax.psum(loss_part[0, 0], ("x", "y", "c"))

    layer_grads = [None] * DEPTH
    for l in reversed(range(DEPTH)):
        dh, g = _layer_bwd(dh, layers[l], saved[l], cos_t, sin_t, "l%d_" % l)
        layer_grads[l] = _layer_grads_to_full(g)
    grad_x = dh[BLOCK:][None]

    partial = {n: jnp.stack([layer_grads[l][n] for l in range(DEPTH)]) for n in layer_grads[0]}
    partial["meta_tokens"] = dh[N_PAD:BLOCK]
    partial["final_norm_g"] = dg_final[0]

    results = [dict(), dict(), dict(), dict()]
    axes = dict(SHARDED)

    got_big = _exchange([_split_blocks(partial[n], axes[n]) for n in BIG], gather=False, name="grad_all_to_all_big")
    for n, g8 in zip(BIG, got_big):
        shp = w[n].shape
        view = (math.prod(shp[:-1]), shp[-1])
        outs = _adamw(g8.reshape((N_DEV,) + view), w[n].reshape(view), mom[n].reshape(view), vel[n].reshape(view),
                      name="adamw_" + n)
        for kind in range(4):
            results[kind][n] = outs[kind].reshape(shp)

    send = jnp.concatenate([_pieces(partial[n], axes[n]) for n in SMALL], axis=1)
    pad = (-send.shape[1]) % (8 * 128)
    send = jnp.pad(send, ((0, 0), (0, pad))).reshape(N_DEV, -1, 128)
    got = _exchange([send], gather=False, name="grad_all_to_all_small")[0]
    pack = lambda d: _flat_pack([d[n] for n in SMALL], F32, 8 * 128)
    sh_out = _adamw(got, pack(w), pack(mom), pack(vel), name="adamw_small")

    rep_g = _flat_pack([partial[n] for n in REPLICATED], F32, 8 * 128)
    got_r = _exchange([rep_g], gather=True, name="grad_all_gather")[0]
    packr = lambda d: _flat_pack([d[n] for n in REPLICATED], F32, 8 * 128)
    rep_out = _adamw(got_r, packr(w), packr(mom), packr(vel), name="adamw_replicated")

    for names, outs in ((list(SMALL), sh_out), (list(REPLICATED), rep_out)):
        off = 0
        for n in names:
            size = math.prod(w[n].shape)
            for kind in range(4):
                results[kind][n] = outs[kind].reshape(-1)[off:off + size].reshape(w[n].shape)
            off += size

    return (loss, grad_x, *[results[0][n] for n in WEIGHTS], *[results[1][n] for n in WEIGHTS],
            *[results[2][n] for n in WEIGHTS], *[results[3][n] for n in WEIGHTS])
```

```python
import math

import jax
import jax.numpy as jnp
from jax import lax
from jax.experimental import pallas as pl
from jax.experimental.pallas import tpu as pltpu

F32 = jnp.float32
BF16 = jnp.bfloat16

D_MODEL = 1024
DEPTH = 2
N_META = 16
BLOCK = 128
N_PAD = BLOCK - N_META
EPS = 1e-6
SSD_WIDTH = 512
SSD_XBC = 1024
SSD_CONV = 4
SB_WIDTH = 256
SB_HEAD_DIM = 64
MLA_HEADS = 4
MLA_NOPE = 64
MLA_ROPE = 32
MLA_Q_RANK = 192
MLA_KV_RANK = 128
ROPE_BASE = 10000.0
D_FF = 2816
FFN_CONV = 3
IN_COLS = 2664
N_DEV = 8

ADAM_LR = 0.001
ADAM_B1 = 0.9
ADAM_B2 = 0.999
ADAM_EPS = 1e-08
ADAM_WD = 0.01
ADAM_STEP = 10

U_COLS = 3072
OFF_Z, OFF_XBC, OFF_QSB, OFF_KSB, OFF_VSB, OFF_QA, OFF_CKV, OFF_KR, OFF_DT = (
    0, 512, 1536, 1792, 2048, 2304, 2560, 2688, 2816)

V7X_VMEM_BYTES = 64 * 1024 * 1024
VMEM_LIMIT = (V7X_VMEM_BYTES * 7) // 8
NEG_BIG = -1e30


def _cparams(n_axes):
    return pltpu.CompilerParams(dimension_semantics=("arbitrary",) * n_axes, vmem_limit_bytes=VMEM_LIMIT)


def _tile(n, target, align):
    best = None
    for d in range(align, min(n, target) + 1, align):
        if n % d == 0:
            best = d
    return n if best is None else best


def _sigmoid(x):
    return 1.0 / (1.0 + jnp.exp(-x))


def _softplus(x):
    return jnp.maximum(x, 0.0) + jnp.log(1.0 + jnp.exp(-jnp.abs(x)))


def _dot(a, b):
    return jnp.dot(a, b, preferred_element_type=F32)


def _dot_nt(a, b):
    return lax.dot_general(a, b, (((1,), (1,)), ((), ())), preferred_element_type=F32)


def _hilo(x):
    hi = x.astype(BF16)
    lo = (x - hi.astype(F32)).astype(BF16)
    return hi, lo


def _hilo_dot_l(x, m):
    hi, lo = _hilo(x)
    return _dot(hi, m) + _dot(lo, m)


def _hilo_dot_r(m, x):
    hi, lo = _hilo(x)
    return _dot(m, hi) + _dot(m, lo)


def _hilo_dot_nt(x, m):
    hi, lo = _hilo(x)
    return _dot_nt(hi, m) + _dot_nt(lo, m)


def _ones_where(cond):
    return jnp.where(cond, 1.0, 0.0).astype(BF16)


def _matmul(a, b, *, name, ta=False, tb=False, out_dtype=F32, res=None, mask_pad=False,
            tm=640, tn=1024, tk=1024):
    m_dim = a.shape[1] if ta else a.shape[0]
    k_dim = a.shape[0] if ta else a.shape[1]
    n_dim = b.shape[0] if tb else b.shape[1]
    assert (b.shape[1] if tb else b.shape[0]) == k_dim
    tm = _tile(m_dim, tm, 128)
    tn = _tile(n_dim, tn, 128)
    tk = _tile(k_dim, tk, 128)
    nk = k_dim // tk
    dn = (((0 if ta else 1,), (1 if tb else 0,)), ((), ()))

    def body(*refs):
        if res is not None:
            a_ref, b_ref, r_ref, o_ref, acc = refs
        else:
            a_ref, b_ref, o_ref, acc = refs
        k = pl.program_id(2)

        @pl.when(k == 0)
        def _():
            acc[...] = jnp.zeros_like(acc)

        acc[...] += lax.dot_general(a_ref[...].astype(BF16), b_ref[...].astype(BF16), dn,
                                    preferred_element_type=F32)

        @pl.when(k == nk - 1)
        def _():
            r = acc[...]
            if res is not None:
                r = r + r_ref[...].astype(F32)
            if mask_pad:
                rows = pl.program_id(0) * tm + lax.broadcasted_iota(jnp.int32, (tm, 1), 0)
                r = jnp.where(rows >= N_PAD, r, 0.0)
            o_ref[...] = r.astype(out_dtype)

    a_spec = pl.BlockSpec((tk, tm), lambda i, j, k: (k, i)) if ta else pl.BlockSpec((tm, tk), lambda i, j, k: (i, k))
    b_spec = pl.BlockSpec((tn, tk), lambda i, j, k: (j, k)) if tb else pl.BlockSpec((tk, tn), lambda i, j, k: (k, j))
    o_spec = pl.BlockSpec((tm, tn), lambda i, j, k: (i, j))
    in_specs = [a_spec, b_spec]
    args = [a, b]
    if res is not None:
        in_specs.append(o_spec)
        args.append(res)
    return pl.pallas_call(
        body, name=name, grid=(m_dim // tm, n_dim // tn, nk),
        in_specs=in_specs, out_specs=o_spec,
        out_shape=jax.ShapeDtypeStruct((m_dim, n_dim), out_dtype),
        scratch_shapes=[pltpu.VMEM((tm, tn), F32)],
        compiler_params=_cparams(3),
    )(*args)


def _rmsnorm_fwd(x, g, *, width, name, z=None, out_dtype=None):
    out_dtype = BF16 if out_dtype is None else out_dtype
    rows, w = x.shape
    tr = _tile(rows, 640, 128)
    inv_w = 1.0 / width

    def body(*refs):
        if z is not None:
            x_ref, z_ref, g_ref, o_ref = refs
        else:
            x_ref, g_ref, o_ref = refs
        t = x_ref[...].astype(F32)
        if z is not None:
            zz = z_ref[...]
            t = t * (zz * _sigmoid(zz))
        ms = jnp.sum(t * t, axis=-1, keepdims=True) * inv_w
        o_ref[...] = ((t * lax.rsqrt(ms + EPS)) * g_ref[...]).astype(out_dtype)

    row_spec = pl.BlockSpec((tr, w), lambda i: (i, 0))
    g_spec = pl.BlockSpec((1, w), lambda i: (0, 0))
    in_specs = [row_spec] + ([row_spec] if z is not None else []) + [g_spec]
    args = [x] + ([z] if z is not None else []) + [g]
    return pl.pallas_call(
        body, name=name, grid=(rows // tr,), in_specs=in_specs, out_specs=row_spec,
        out_shape=jax.ShapeDtypeStruct((rows, w), out_dtype), compiler_params=_cparams(1),
    )(*args)


def _rmsnorm_bwd(x, g, dout, *, width, name, z=None, res=None, mask_pad=False):
    rows, w = x.shape
    tr = _tile(rows, 640, 128)
    inv_w = 1.0 / width

    def body(*refs):
        refs = list(refs)
        x_ref = refs.pop(0)
        z_ref = refs.pop(0) if z is not None else None
        g_ref = refs.pop(0)
        do_ref = refs.pop(0)
        r_ref = refs.pop(0) if res is not None else None
        dx_ref = refs.pop(0)
        dz_ref = refs.pop(0) if z is not None else None
        dg_ref = refs.pop(0)
        i = pl.program_id(0)

        @pl.when(i == 0)
        def _():
            dg_ref[...] = jnp.zeros_like(dg_ref)

        xv = x_ref[...].astype(F32)
        t = xv
        if z is not None:
            zz = z_ref[...]
            sig = _sigmoid(zz)
            sl = zz * sig
            t = xv * sl
        ms = jnp.sum(t * t, axis=-1, keepdims=True) * inv_w
        rstd = lax.rsqrt(ms + EPS)
        xhat = t * rstd
        do = do_ref[...].astype(F32)
        dxh = do * g_ref[...]
        c = jnp.sum(dxh * xhat, axis=-1, keepdims=True) * inv_w
        dt = rstd * (dxh - xhat * c)
        dg_ref[...] += jnp.sum(do * xhat, axis=0, keepdims=True)
        if z is not None:
            dz_ref[...] = dt * xv * (sig * (1.0 + zz * (1.0 - sig)))
            dx = dt * sl
        else:
            dx = dt
        if res is not None:
            dx = dx + r_ref[...]
        if mask_pad:
            rws = i * tr + lax.broadcasted_iota(jnp.int32, (tr, 1), 0)
            dx = jnp.where(rws >= N_PAD, dx, 0.0)
        dx_ref[...] = dx

    row_spec = pl.BlockSpec((tr, w), lambda i: (i, 0))
    g_spec = pl.BlockSpec((1, w), lambda i: (0, 0))
    in_specs = [row_spec] + ([row_spec] if z is not None else []) + [g_spec, row_spec] + (
        [row_spec] if res is not None else [])
    args = [x] + ([z] if z is not None else []) + [g, dout] + ([res] if res is not None else [])
    out_specs = [row_spec] + ([row_spec] if z is not None else []) + [g_spec]
    out_shape = [jax.ShapeDtypeStruct((rows, w), F32)] + (
        [jax.ShapeDtypeStruct((rows, w), F32)] if z is not None else []) + [jax.ShapeDtypeStruct((1, w), F32)]
    outs = pl.pallas_call(
        body, name=name, grid=(rows // tr,), in_specs=in_specs, out_specs=out_specs,
        out_shape=out_shape, compiler_params=_cparams(1),
    )(*args)
    if z is not None:
        return outs[0], outs[1], outs[2]
    return outs[0], None, outs[1]


def _final_loss(h, g, target, *, name):
    rows, w = h.shape
    nb = rows // BLOCK
    inv_w = 1.0 / w

    def body(h_ref, g_ref, t_ref, dh_ref, dg_ref, loss_ref):
        i = pl.program_id(0)

        @pl.when(i == 0)
        def _():
            dg_ref[...] = jnp.zeros_like(dg_ref)
            loss_ref[...] = jnp.zeros_like(loss_ref)

        xv = h_ref[...]
        ms = jnp.sum(xv * xv, axis=-1, keepdims=True) * inv_w
        rstd = lax.rsqrt(ms + EPS)
        xhat = xv * rstd
        gv = g_ref[...]
        err = jnp.where(i >= 1, xhat * gv - t_ref[...], 0.0)
        loss_ref[...] += (0.5 * inv_w) * jnp.sum(err * err)
        do = err * inv_w
        dxh = do * gv
        c = jnp.sum(dxh * xhat, axis=-1, keepdims=True) * inv_w
        dh_ref[...] = rstd * (dxh - xhat * c)
        dg_ref[...] += jnp.sum(do * xhat, axis=0, keepdims=True)

    row_spec = pl.BlockSpec((BLOCK, w), lambda i: (i, 0))
    g_spec = pl.BlockSpec((1, w), lambda i: (0, 0))
    return pl.pallas_call(
        body, name=name, grid=(nb,),
        in_specs=[row_spec, g_spec, pl.BlockSpec((BLOCK, w), lambda i: (jnp.maximum(i - 1, 0), 0))],
        out_specs=[row_spec, g_spec, pl.BlockSpec((1, 128), lambda i: (0, 0))],
        out_shape=[jax.ShapeDtypeStruct((rows, w), F32), jax.ShapeDtypeStruct((1, w), F32),
                   jax.ShapeDtypeStruct((1, 128), F32)],
        compiler_params=_cparams(1),
    )(h, g, target)


HALO = 8


def _dwconv_fwd(u, w8, b, *, taps, name):
    rows, ch = u.shape
    tb = _tile(rows, 640, 128)
    tc = _tile(ch, 512, 128)
    hb = tb // HALO

    def body(u_ref, h_ref, w_ref, b_ref, o_ref, buf):
        i = pl.program_id(0)
        buf[0:HALO, :] = jnp.where(i > 0, h_ref[...], 0.0)
        buf[HALO:HALO + tb, :] = u_ref[...]
        acc = jnp.broadcast_to(b_ref[...], (tb, tc))
        for k in range(taps):
            acc = acc + w_ref[k:k + 1, :] * buf[pl.ds(HALO - (taps - 1) + k, tb), :]
        o_ref[...] = acc

    return pl.pallas_call(
        body, name=name, grid=(rows // tb, ch // tc),
        in_specs=[pl.BlockSpec((tb, tc), lambda i, j: (i, j)),
                  pl.BlockSpec((HALO, tc), lambda i, j: (jnp.maximum(i * hb - 1, 0), j)),
                  pl.BlockSpec((HALO, tc), lambda i, j: (0, j)),
                  pl.BlockSpec((1, tc), lambda i, j: (0, j))],
        out_specs=pl.BlockSpec((tb, tc), lambda i, j: (i, j)),
        out_shape=jax.ShapeDtypeStruct((rows, ch), F32),
        scratch_shapes=[pltpu.VMEM((tb + HALO, tc), F32)],
        compiler_params=_cparams(2),
    )(u, u, w8, b)


def _dwconv_bwd(dpre, u, w8, *, taps, name):
    rows, ch = u.shape
    tb = _tile(rows, 640, 128)
    tc = _tile(ch, 512, 128)
    hb = tb // HALO
    nb = rows // tb
    last_halo = rows // HALO - 1

    def body(d_ref, dn_ref, u_ref, up_ref, w_ref, du_ref, dw_ref, db_ref, bufd, bufu):
        i = pl.program_id(1)

        @pl.when(i == 0)
        def _():
            dw_ref[...] = jnp.zeros_like(dw_ref)
            db_ref[...] = jnp.zeros_like(db_ref)

        d = d_ref[...]
        bufd[0:tb, :] = d
        bufd[tb:tb + HALO, :] = jnp.where(i < nb - 1, dn_ref[...], 0.0)
        bufu[0:HALO, :] = jnp.where(i > 0, up_ref[...], 0.0)
        bufu[HALO:HALO + tb, :] = u_ref[...]
        acc = jnp.zeros((tb, tc), F32)
        for k in range(taps):
            acc = acc + w_ref[k:k + 1, :] * bufd[pl.ds(taps - 1 - k, tb), :]
        du_ref[...] = acc
        for k in range(taps):
            dw_ref[k:k + 1, :] += jnp.sum(d * bufu[pl.ds(HALO - (taps - 1) + k, tb), :], axis=0, keepdims=True)
        db_ref[...] += jnp.sum(d, axis=0, keepdims=True)

    return pl.pallas_call(
        body, name=name, grid=(ch // tc, nb),
        in_specs=[pl.BlockSpec((tb, tc), lambda j, i: (i, j)),
                  pl.BlockSpec((HALO, tc), lambda j, i: (jnp.minimum((i + 1) * hb, last_halo), j)),
                  pl.BlockSpec((tb, tc), lambda j, i: (i, j)),
                  pl.BlockSpec((HALO, tc), lambda j, i: (jnp.maximum(i * hb - 1, 0), j)),
                  pl.BlockSpec((HALO, tc), lambda j, i: (0, j))],
        out_specs=[pl.BlockSpec((tb, tc), lambda j, i: (i, j)),
                   pl.BlockSpec((HALO, tc), lambda j, i: (0, j)),
                   pl.BlockSpec((1, tc), lambda j, i: (0, j))],
        out_shape=[jax.ShapeDtypeStruct((rows, ch), F32), jax.ShapeDtypeStruct((HALO, ch), F32),
                   jax.ShapeDtypeStruct((1, ch), F32)],
        scratch_shapes=[pltpu.VMEM((tb + HALO, tc), F32), pltpu.VMEM((tb + HALO, tc), F32)],
        compiler_params=_cparams(2),
    )(dpre, dpre, u, u, w8)


def _ffn_conv_gate_fwd(up, w8, b, *, taps, name):
    rows, c2 = up.shape
    f = c2 // 2
    tb = _tile(rows, 640, 128)
    tc = _tile(f, 512, 128)
    nct = f // tc
    hb = tb // HALO

    def body(u1_ref, u2_ref, h1_ref, h2_ref, w1_ref, w2_ref, b1_ref, b2_ref, o_ref, buf1, buf2):
        i = pl.program_id(0)
        pre = []
        for u_ref, h_ref, w_ref, b_ref, buf in ((u1_ref, h1_ref, w1_ref, b1_ref, buf1),
                                                 (u2_ref, h2_ref, w2_ref, b2_ref, buf2)):
            buf[0:HALO, :] = jnp.where(i > 0, h_ref[...], 0.0)
            buf[HALO:HALO + tb, :] = u_ref[...]
            acc = jnp.broadcast_to(b_ref[...], (tb, tc))
            for k in range(taps):
                acc = acc + w_ref[k:k + 1, :] * buf[pl.ds(HALO - (taps - 1) + k, tb), :]
            pre.append(acc)
        o_ref[...] = (pre[0] * _sigmoid(pre[0]) * pre[1]).astype(BF16)

    main = lambda off: pl.BlockSpec((tb, tc), lambda i, j: (i, j + off))
    halo = lambda off: pl.BlockSpec((HALO, tc), lambda i, j: (jnp.maximum(i * hb - 1, 0), j + off))
    wrow = lambda off: pl.BlockSpec((HALO, tc), lambda i, j: (0, j + off))
    brow = lambda off: pl.BlockSpec((1, tc), lambda i, j: (0, j + off))
    return pl.pallas_call(
        body, name=name, grid=(rows // tb, nct),
        in_specs=[main(0), main(nct), halo(0), halo(nct), wrow(0), wrow(nct), brow(0), brow(nct)],
        out_specs=pl.BlockSpec((tb, tc), lambda i, j: (i, j)),
        out_shape=jax.ShapeDtypeStruct((rows, f), BF16),
        scratch_shapes=[pltpu.VMEM((tb + HALO, tc), F32), pltpu.VMEM((tb + HALO, tc), F32)],
        compiler_params=_cparams(2),
    )(up, up, up, up, w8, w8, b, b)


def _ffn_conv_gate_bwd(dact, up, w8, b, *, taps, name):
    rows, c2 = up.shape
    f = c2 // 2
    tb = _tile(rows, 640, 128)
    tc = _tile(f, 512, 128)
    nct = f // tc
    hb = tb // HALO
    nb = rows // tb
    last_halo = rows // HALO - 1
    ext = tb + HALO

    def body(d_ref, dn_ref, u1_ref, u2_ref, p1_ref, p2_ref, n1_ref, n2_ref, w1_ref, w2_ref, b1_ref, b2_ref,
             du1_ref, du2_ref, dw1_ref, dw2_ref, db1_ref, db2_ref, bufu1, bufu2, bufd1, bufd2):
        i = pl.program_id(1)

        @pl.when(i == 0)
        def _():
            for r in (dw1_ref, dw2_ref, db1_ref, db2_ref):
                r[...] = jnp.zeros_like(r)

        has_next = i < nb - 1
        pre = []
        for u_ref, p_ref, n_ref, w_ref, b_ref, buf in ((u1_ref, p1_ref, n1_ref, w1_ref, b1_ref, bufu1),
                                                       (u2_ref, p2_ref, n2_ref, w2_ref, b2_ref, bufu2)):
            buf[0:HALO, :] = jnp.where(i > 0, p_ref[...], 0.0)
            buf[HALO:HALO + tb, :] = u_ref[...]
            buf[HALO + tb:HALO + ext, :] = jnp.where(has_next, n_ref[...], 0.0)
            acc = jnp.broadcast_to(b_ref[...], (ext, tc))
            for k in range(taps):
                acc = acc + w_ref[k:k + 1, :] * buf[pl.ds(HALO - (taps - 1) + k, ext), :]
            pre.append(acc)
        d_ext = jnp.concatenate([d_ref[...], jnp.where(has_next, dn_ref[...], 0.0)], axis=0)
        sig = _sigmoid(pre[0])
        bufd1[...] = d_ext * pre[1] * (sig * (1.0 + pre[0] * (1.0 - sig)))
        bufd2[...] = d_ext * (pre[0] * sig)
        for w_ref, bufd, bufu, du_ref, dw_ref, db_ref in ((w1_ref, bufd1, bufu1, du1_ref, dw1_ref, db1_ref),
                                                          (w2_ref, bufd2, bufu2, du2_ref, dw2_ref, db2_ref)):
            acc = jnp.zeros((tb, tc), F32)
            for k in range(taps):
                acc = acc + w_ref[k:k + 1, :] * bufd[pl.ds(taps - 1 - k, tb), :]
            du_ref[...] = acc
            dmain = bufd[0:tb, :]
            for k in range(taps):
                dw_ref[k:k + 1, :] += jnp.sum(dmain * bufu[pl.ds(HALO - (taps - 1) + k, tb), :], axis=0, keepdims=True)
            db_ref[...] += jnp.sum(dmain, axis=0, keepdims=True)

    main = lambda off: pl.BlockSpec((tb, tc), lambda j, i: (i, j + off))
    prev = lambda off: pl.BlockSpec((HALO, tc), lambda j, i: (jnp.maximum(i * hb - 1, 0), j + off))
    nxt = lambda off: pl.BlockSpec((HALO, tc), lambda j, i: (jnp.minimum((i + 1) * hb, last_halo), j + off))
    wrow = lambda off: pl.BlockSpec((HALO, tc), lambda j, i: (0, j + off))
    brow = lambda off: pl.BlockSpec((1, tc), lambda j, i: (0, j + off))
    half = jax.ShapeDtypeStruct((rows, f), F32)
    return pl.pallas_call(
        body, name=name, grid=(nct, nb),
        in_specs=[main(0), nxt(0), main(0), main(nct), prev(0), prev(nct), nxt(0), nxt(nct),
                  wrow(0), wrow(nct), brow(0), brow(nct)],
        out_specs=[main(0), main(0), wrow(0), wrow(0), brow(0), brow(0)],
        out_shape=[half, half, jax.ShapeDtypeStruct((HALO, f), F32), jax.ShapeDtypeStruct((HALO, f), F32),
                   jax.ShapeDtypeStruct((1, f), F32), jax.ShapeDtypeStruct((1, f), F32)],
        scratch_shapes=[pltpu.VMEM((ext + HALO, tc), F32), pltpu.VMEM((ext + HALO, tc), F32),
                        pltpu.VMEM((ext, tc), F32), pltpu.VMEM((ext, tc), F32)],
        compiler_params=_cparams(2),
    )(dact, dact, up, up, up, up, up, up, w8, w8, b, b)


def _rope(xr, cos_t, sin_t, *, name, transpose=False):
    rows, w = xr.shape
    tr = _tile(rows, 640, 128)

    def body(x_ref, c_ref, s_ref, o_ref):
        xv = x_ref[...]
        if transpose:
            o_ref[...] = xv * c_ref[...] + pltpu.roll(xv * s_ref[...], 64, 1)
        else:
            o_ref[...] = xv * c_ref[...] + pltpu.roll(xv, 64, 1) * s_ref[...]

    spec = pl.BlockSpec((tr, w), lambda i: (i, 0))
    return pl.pallas_call(
        body, name=name, grid=(rows // tr,), in_specs=[spec, spec, spec], out_specs=spec,
        out_shape=jax.ShapeDtypeStruct((rows, w), F32), compiler_params=_cparams(1),
    )(xr, cos_t, sin_t)


ATT_TQ = 640
ATT_GROUP = 4


def _grouped_loop(lo, hi, step, carry, *, descending=False):
    n = hi - lo
    n_groups = lax.div(n, jnp.int32(ATT_GROUP))
    rest = n - n_groups * ATT_GROUP
    if descending:
        carry = lax.fori_loop(
            0, n_groups, lambda t, c: step([hi - 1 - ATT_GROUP * t - b for b in range(ATT_GROUP)], c), carry)
        return lax.fori_loop(0, rest, lambda t, c: step([lo + rest - 1 - t], c), carry)
    carry = lax.fori_loop(0, n_groups, lambda t, c: step([lo + ATT_GROUP * t + b for b in range(ATT_GROUP)], c), carry)
    return lax.fori_loop(0, rest, lambda t, c: step([hi - rest + t], c), carry)


def _tile_iotas(rows=BLOCK):
    r_i = lax.broadcasted_iota(jnp.int32, (rows, BLOCK), 0)
    c_i = lax.broadcasted_iota(jnp.int32, (rows, BLOCK), 1)
    return r_i, c_i


def _key_ranges(i, tq):
    n_blocks = ((i + 1) * tq + (BLOCK - 1)) >> 7
    first_diag = jnp.maximum((i * tq) >> 7, 1)
    return first_diag, n_blocks


def _dot_tn(a, b):
    return lax.dot_general(a, b, (((0,), (0,)), ((), ())), preferred_element_type=F32)


def _cumsum_rhs(pred):
    r = lax.broadcasted_iota(jnp.int32, (2 * BLOCK, 2 * BLOCK), 0) & (BLOCK - 1)
    c = lax.broadcasted_iota(jnp.int32, (2 * BLOCK, 2 * BLOCK), 1)
    return _ones_where((c >= BLOCK) | pred(r, c))


def _cumsum_dot(x, rhs):
    hi, lo = _hilo(x)
    r = _dot(jnp.concatenate([hi, lo], axis=1), rhs)
    return r[:, :BLOCK], r[:, BLOCK:]


def _sb_fwd(q, k, v, *, name):
    nh, rows, hd = q.shape
    tq = _tile(rows, ATT_TQ, 8)
    scale = SB_HEAD_DIM ** -0.5

    def body(q_ref, k_ref, v_ref, o_ref, u_ref):
        i = pl.program_id(1)
        r_i, c_i = _tile_iotas(tq)
        m_after = _cumsum_rhs(lambda j, s: j > s)
        qb = (q_ref[0] * scale).astype(BF16)
        rowpos = i * tq + r_i
        first_diag, n_blocks = _key_ranges(i, tq)

        def step(js, carry, masked):
            acc, cu = carry
            offs = [pl.multiple_of(j * BLOCK, BLOCK) for j in js]
            zs = [_dot_nt(qb, k_ref[0, pl.ds(off, BLOCK), :].astype(BF16)) for off in offs]
            sp = [_softplus(z) for z in zs]
            if masked:
                masks = [((off + c_i) < rowpos) & ((off + c_i) >= N_PAD) for off in offs]
                cs = [_cumsum_dot(jnp.where(m_, s, 0.0), m_after) for m_, s in zip(masks, sp)]
            else:
                cs = [_cumsum_dot(s, m_after) for s in sp]
            wgt = []
            for b in range(len(js)):
                w_ = jnp.exp(zs[b] - sp[b] - (cu + cs[b][0]))
                wgt.append(jnp.where(masks[b], w_, 0.0) if masked else w_)
                cu = cu + cs[b][1]
            for b, off in enumerate(offs):
                acc = acc + _dot(wgt[b].astype(BF16), v_ref[0, pl.ds(off, BLOCK), :].astype(BF16))
            return acc, cu

        carry = (jnp.zeros((tq, hd), F32), jnp.zeros((tq, BLOCK), F32))
        carry = _grouped_loop(first_diag, n_blocks, lambda js, c: step(js, c, True), carry, descending=True)
        carry = _grouped_loop(1, first_diag, lambda js, c: step(js, c, False), carry, descending=True)
        acc, cu = step([0], carry, True)
        o_ref[0] = acc
        u_ref[0] = -cu

    blk = pl.BlockSpec((1, tq, hd), lambda h, i: (h, i, 0))
    full = pl.BlockSpec((1, rows, hd), lambda h, i: (h, 0, 0))
    return pl.pallas_call(
        body, name=name, grid=(nh, rows // tq), in_specs=[blk, full, full],
        out_specs=[blk, pl.BlockSpec((1, tq, 128), lambda h, i: (h, i, 0))],
        out_shape=[jax.ShapeDtypeStruct((nh, rows, hd), F32), jax.ShapeDtypeStruct((nh, rows, 128), F32)],
        compiler_params=_cparams(2),
    )(q, k, v)


def _sb_bwd(q, k, v, do, u_tot, *, name):
    nh, rows, hd = q.shape
    tq = _tile(rows, ATT_TQ, 8)
    scale = SB_HEAD_DIM ** -0.5

    def body(q_ref, k_ref, v_ref, do_ref, u_ref, dq_ref, dk_ref, dv_ref):
        i = pl.program_id(1)

        @pl.when(i == 0)
        def _():
            dk_ref[...] = jnp.zeros_like(dk_ref)
            dv_ref[...] = jnp.zeros_like(dv_ref)

        r_i, c_i = _tile_iotas(tq)
        m_incl = _cumsum_rhs(lambda j, s: j <= s)
        m_excl = _cumsum_rhs(lambda j, s: j < s)
        qb = (q_ref[0] * scale).astype(BF16)
        dob = do_ref[0].astype(BF16)
        rowpos = i * tq + r_i
        first_diag, n_blocks = _key_ranges(i, tq)

        def step(js, carry, masked):
            dq, rem, cg = carry
            nb = range(len(js))
            offs = [pl.multiple_of(j * BLOCK, BLOCK) for j in js]
            kbs = [k_ref[0, pl.ds(off, BLOCK), :].astype(BF16) for off in offs]
            vbs = [v_ref[0, pl.ds(off, BLOCK), :].astype(BF16) for off in offs]
            zs = [_dot_nt(qb, kb) for kb in kbs]
            dws = [_dot_nt(dob, vb) for vb in vbs]
            sp = [_softplus(z) for z in zs]
            sig = [jnp.exp(zs[b] - sp[b]) for b in nb]
            if masked:
                masks = [((off + c_i) < rowpos) & ((off + c_i) >= N_PAD) for off in offs]
                cs = [_cumsum_dot(jnp.where(masks[b], sp[b], 0.0), m_incl) for b in nb]
            else:
                cs = [_cumsum_dot(sp[b], m_incl) for b in nb]
            wgt, gg = [], []
            for b in nb:
                w_ = jnp.exp(jnp.minimum(zs[b] - sp[b] - (rem - cs[b][0]), 0.0))
                wgt.append(jnp.where(masks[b], w_, 0.0) if masked else w_)
                gg.append(wgt[b] * dws[b])
                rem = rem - cs[b][1]
            gs = [_cumsum_dot(g_, m_excl) for g_ in gg]
            dzb = []
            for b in nb:
                dz = gg[b] * (1.0 - sig[b]) - sig[b] * (cg + gs[b][0])
                dzb.append((jnp.where(masks[b], dz, 0.0) if masked else dz).astype(BF16))
                cg = cg + gs[b][1]
            for b, off in enumerate(offs):
                dq = dq + _dot(dzb[b], kbs[b])
                dk_ref[0, pl.ds(off, BLOCK), :] += _dot_tn(dzb[b], qb)
                dv_ref[0, pl.ds(off, BLOCK), :] += _dot_tn(wgt[b].astype(BF16), dob)
            return dq, rem, cg

        carry = step([0], (jnp.zeros((tq, hd), F32), -u_ref[0], jnp.zeros((tq, BLOCK), F32)), True)
        carry = _grouped_loop(1, first_diag, lambda js, c: step(js, c, False), carry)
        dq, _, _ = _grouped_loop(first_diag, n_blocks, lambda js, c: step(js, c, True), carry)
        dq_ref[0] = dq * scale

    blk = pl.BlockSpec((1, tq, hd), lambda h, i: (h, i, 0))
    full = pl.BlockSpec((1, rows, hd), lambda h, i: (h, 0, 0))
    ublk = pl.BlockSpec((1, tq, 128), lambda h, i: (h, i, 0))
    sds = jax.ShapeDtypeStruct((nh, rows, hd), F32)
    return pl.pallas_call(
        body, name=name, grid=(nh, rows // tq), in_specs=[blk, full, full, blk, ublk],
        out_specs=[blk, full, full], out_shape=[sds, sds, sds], compiler_params=_cparams(2),
    )(q, k, v, do, u_tot)


def _mla_fwd(q, k, v, *, name):
    nh, rows, dk = q.shape
    dv = v.shape[2]
    tq = _tile(rows, ATT_TQ, 8)
    scale = (MLA_NOPE + MLA_ROPE) ** -0.5

    def body(q_ref, k_ref, v_ref, o_ref, lse_ref):
        i = pl.program_id(1)
        r_i, c_i = _tile_iotas(tq)
        qb = q_ref[0].astype(BF16)
        rowpos = i * tq + r_i
        first_diag, n_blocks = _key_ranges(i, tq)

        def step(js, carry, masked):
            m, l, acc = carry
            offs = [pl.multiple_of(j * BLOCK, BLOCK) for j in js]
            ss = [_dot_nt(qb, k_ref[0, pl.ds(off, BLOCK), :].astype(BF16)) * scale for off in offs]
            if masked:
                ss = [jnp.where(((off + c_i) <= rowpos) & ((off + c_i) >= N_PAD), s, NEG_BIG)
                      for off, s in zip(offs, ss)]
            m_new = m
            for s in ss:
                m_new = jnp.maximum(m_new, jnp.max(s, axis=1, keepdims=True))
            alpha = jnp.exp(m - m_new)
            ps = [jnp.exp(s - m_new) for s in ss]
            l = alpha * l
            acc = alpha * acc
            for off, p in zip(offs, ps):
                l = l + jnp.sum(p, axis=1, keepdims=True)
                acc = acc + _dot(p.astype(BF16), v_ref[0, pl.ds(off, BLOCK), :].astype(BF16))
            return m_new, l, acc

        carry = (jnp.full((tq, 1), NEG_BIG, F32), jnp.zeros((tq, 1), F32), jnp.zeros((tq, dv), F32))
        carry = step([0], carry, True)
        carry = _grouped_loop(1, first_diag, lambda js, c: step(js, c, False), carry)
        m, l, acc = _grouped_loop(first_diag, n_blocks, lambda js, c: step(js, c, True), carry)
        o_ref[0] = acc / l
        lse_ref[0] = jnp.broadcast_to(m + jnp.log(l), (tq, 128))

    qblk = pl.BlockSpec((1, tq, dk), lambda h, i: (h, i, 0))
    kfull = pl.BlockSpec((1, rows, dk), lambda h, i: (h, 0, 0))
    vfull = pl.BlockSpec((1, rows, dv), lambda h, i: (h, 0, 0))
    return pl.pallas_call(
        body, name=name, grid=(nh, rows // tq), in_specs=[qblk, kfull, vfull],
        out_specs=[pl.BlockSpec((1, tq, dv), lambda h, i: (h, i, 0)),
                   pl.BlockSpec((1, tq, 128), lambda h, i: (h, i, 0))],
        out_shape=[jax.ShapeDtypeStruct((nh, rows, dv), F32), jax.ShapeDtypeStruct((nh, rows, 128), F32)],
        compiler_params=_cparams(2),
    )(q, k, v)


def _mla_bwd(q, k, v, o, lse, do, *, name):
    nh, rows, dk = q.shape
    dv = v.shape[2]
    tq = _tile(rows, ATT_TQ, 8)
    scale = (MLA_NOPE + MLA_ROPE) ** -0.5

    def body(q_ref, k_ref, v_ref, o_ref, lse_ref, do_ref, dq_ref, dk_ref, dv_ref):
        i = pl.program_id(1)

        @pl.when(i == 0)
        def _():
            dk_ref[...] = jnp.zeros_like(dk_ref)
            dv_ref[...] = jnp.zeros_like(dv_ref)

        r_i, c_i = _tile_iotas(tq)
        qb = q_ref[0].astype(BF16)
        dov = do_ref[0]
        dob = dov.astype(BF16)
        delta = jnp.sum(dov * o_ref[0], axis=1, keepdims=True)
        lse = lse_ref[0][:, 0:1]
        rowpos = i * tq + r_i
        first_diag, n_blocks = _key_ranges(i, tq)

        def step(js, dq, masked):
            nb = range(len(js))
            offs = [pl.multiple_of(j * BLOCK, BLOCK) for j in js]
            kbs = [k_ref[0, pl.ds(off, BLOCK), :].astype(BF16) for off in offs]
            ss = [_dot_nt(qb, kb) for kb in kbs]
            dps = [_dot_nt(dob, v_ref[0, pl.ds(off, BLOCK), :].astype(BF16)) for off in offs]
            ps = [jnp.exp(jnp.minimum(s * scale - lse, 0.0)) for s in ss]
            if masked:
                ps = [jnp.where(((off + c_i) <= rowpos) & ((off + c_i) >= N_PAD), p, 0.0) for off, p in zip(offs, ps)]
            dss = [(ps[b] * (dps[b] - delta) * scale).astype(BF16) for b in nb]
            for b, off in enumerate(offs):
                dq = dq + _dot(dss[b], kbs[b])
                dk_ref[0, pl.ds(off, BLOCK), :] += _dot_tn(dss[b], qb)
                dv_ref[0, pl.ds(off, BLOCK), :] += _dot_tn(ps[b].astype(BF16), dob)
            return dq

        dq = step([0], jnp.zeros((tq, dk), F32), True)
        dq = _grouped_loop(1, first_diag, lambda js, c: step(js, c, False), dq)
        dq_ref[0] = _grouped_loop(first_diag, n_blocks, lambda js, c: step(js, c, True), dq)

    qblk = pl.BlockSpec((1, tq, dk), lambda h, i: (h, i, 0))
    vblk = pl.BlockSpec((1, tq, dv), lambda h, i: (h, i, 0))
    lblk = pl.BlockSpec((1, tq, 128), lambda h, i: (h, i, 0))
    kfull = pl.BlockSpec((1, rows, dk), lambda h, i: (h, 0, 0))
    vfull = pl.BlockSpec((1, rows, dv), lambda h, i: (h, 0, 0))
    return pl.pallas_call(
        body, name=name, grid=(nh, rows // tq), in_specs=[qblk, kfull, vfull, vblk, lblk, vblk],
        out_specs=[qblk, kfull, vfull],
        out_shape=[jax.ShapeDtypeStruct((nh, rows, dk), F32), jax.ShapeDtypeStruct((nh, rows, dk), F32),
                   jax.ShapeDtypeStruct((nh, rows, dv), F32)],
        compiler_params=_cparams(2),
    )(q, k, v, o, lse, do)


def _ssd_consts():
    r_i, c_i = _tile_iotas()
    eh = lax.broadcasted_iota(jnp.int32, (BLOCK, SSD_WIDTH), 0)
    ec = lax.broadcasted_iota(jnp.int32, (BLOCK, SSD_WIDTH), 1)
    expand = _ones_where(lax.shift_right_logical(ec, 6) == eh)
    return r_i, c_i, expand


def _ssd_common(pre_v, dtr_v, bias_v, a_v, chunk, r_i, c_i, expand):
    lower = r_i >= c_i
    sig_pre = _sigmoid(pre_v)
    xbc = pre_v * sig_pre
    xs = xbc[:, :SSD_WIDTH]
    valid = (chunk * BLOCK + lax.broadcasted_iota(jnp.int32, (BLOCK, 1), 0)) >= N_PAD
    dt_in = dtr_v + bias_v
    dtv = jnp.where(valid, _softplus(dt_in), 0.0)
    d_a = dtv * a_v
    acs = _hilo_dot_r(_ones_where(lower), d_a)
    acs_t = acs.T
    dt_exp = _hilo_dot_l(dtv, expand)
    acs_exp = _hilo_dot_l(acs, expand)
    a_last = acs_exp[BLOCK - 1:BLOCK, :]
    ea = jnp.exp(acs_exp)
    e_l = jnp.exp(a_last - acs_exp)
    ea_l = jnp.exp(a_last)
    return lower, sig_pre, xbc, xs, valid, dt_in, dtv, acs, acs_t, dt_exp, ea, e_l, ea_l


def _decay(acs, acs_t, h, lower):
    col = acs[:, h:h + 1]
    row = acs_t[h:h + 1, :]
    return jnp.where(lower, jnp.exp(jnp.minimum(col - row, 0.0)), 0.0)


def _ssd_fwd(pre, dtr, bias_row, a_row, d_exp, *, name):
    rows = pre.shape[0]
    nc = rows // BLOCK

    def body(pre_ref, dtr_ref, bias_ref, a_ref, dexp_ref, y_ref, st_ref, state):
        c = pl.program_id(0)

        @pl.when(c == 0)
        def _():
            state[...] = jnp.zeros_like(state)

        r_i, c_i, expand = _ssd_consts()
        lane_lo = c_i < 64
        (lower, _, xbc, xs, _, _, _, acs, acs_t, dt_exp, ea, e_l, ea_l) = _ssd_common(
            pre_ref[...], dtr_ref[...], bias_ref[...], a_ref[...], c, r_i, c_i, expand)
        xin = xs * dt_exp
        for g in range(2):
            bg = xbc[:, 512 + 128 * g:640 + 128 * g]
            cg = xbc[:, 768 + 128 * g:896 + 128 * g]
            bb = bg.astype(BF16)
            cbf = cg.astype(BF16)
            cb = _dot_nt(cbf, bb)
            bt = bg.T.astype(BF16)
            for pp in range(2):
                p = 2 * g + pp
                sl = slice(128 * p, 128 * p + 128)
                xp = xin[:, sl]
                xb = xp.astype(BF16)
                rs = [_dot((cb * _decay(acs, acs_t, 2 * p + hh, lower)).astype(BF16), xb) for hh in range(2)]
                ydiag = jnp.where(lane_lo, rs[0], rs[1])
                s_in = state[p]
                st_ref[0, p] = s_in
                yoff = ea[:, sl] * _dot(cbf, s_in.astype(BF16))
                y_ref[:, sl] = ydiag + yoff + xs[:, sl] * dexp_ref[:, sl]
                state[p] = ea_l[:, sl] * s_in + _dot(bt, (xp * e_l[:, sl]).astype(BF16))

    vec = pl.BlockSpec((1, 128), lambda c: (0, 0))
    return pl.pallas_call(
        body, name=name, grid=(nc,),
        in_specs=[pl.BlockSpec((BLOCK, SSD_XBC), lambda c: (c, 0)),
                  pl.BlockSpec((BLOCK, 128), lambda c: (c, 0)), vec, vec,
                  pl.BlockSpec((1, SSD_WIDTH), lambda c: (0, 0))],
        out_specs=[pl.BlockSpec((BLOCK, SSD_WIDTH), lambda c: (c, 0)),
                   pl.BlockSpec((1, 4, 128, 128), lambda c: (c, 0, 0, 0))],
        out_shape=[jax.ShapeDtypeStruct((rows, SSD_WIDTH), F32), jax.ShapeDtypeStruct((nc, 4, 128, 128), F32)],
        scratch_shapes=[pltpu.VMEM((4, 128, 128), F32)],
        compiler_params=_cparams(1),
    )(pre, dtr, bias_row, a_row, d_exp)


def _ssd_bwd(pre, dtr, bias_row, a_row, d_exp, states, dy, *, name):
    rows = pre.shape[0]
    nc = rows // BLOCK

    def body(pre_ref, dtr_ref, bias_ref, a_ref, dexp_ref, st_ref, dy_ref,
             dpre_ref, ddtr_ref, dbias_ref, da_ref, dd_ref, dstate, q_buf, dx_buf):
        step = pl.program_id(0)
        c = nc - 1 - step

        @pl.when(step == 0)
        def _():
            dstate[...] = jnp.zeros_like(dstate)
            dbias_ref[...] = jnp.zeros_like(dbias_ref)
            da_ref[...] = jnp.zeros_like(da_ref)
            dd_ref[...] = jnp.zeros_like(dd_ref)

        r_i, c_i, expand = _ssd_consts()
        lane_lo = c_i < 64
        last_row = r_i == BLOCK - 1
        pre_v = pre_ref[...]
        (lower, sig_pre, xbc, xs, valid, dt_in, dtv, acs, acs_t, dt_exp, ea, e_l, ea_l) = _ssd_common(
            pre_v, dtr_ref[...], bias_ref[...], a_ref[...], c, r_i, c_i, expand)
        dsilu = sig_pre * (1.0 + pre_v * (1.0 - sig_pre))
        xin = xs * dt_exp
        dyv = dy_ref[...]
        d_acs_diag = jnp.zeros((BLOCK, BLOCK), F32)
        for g in range(2):
            bg = xbc[:, 512 + 128 * g:640 + 128 * g]
            cg = xbc[:, 768 + 128 * g:896 + 128 * g]
            bb = bg.astype(BF16)
            cbf = cg.astype(BF16)
            cb = _dot_nt(cbf, bb)
            ct = cg.T.astype(BF16)
            dcb = jnp.zeros((BLOCK, BLOCK), F32)
            dbg = jnp.zeros((BLOCK, BLOCK), F32)
            dcg = jnp.zeros((BLOCK, BLOCK), F32)
            for pp in range(2):
                p = 2 * g + pp
                sl = slice(128 * p, 128 * p + 128)
                xp = xin[:, sl]
                xb = xp.astype(BF16)
                dyp = dyv[:, sl]
                dyb = dyp.astype(BF16)
                dxs_ = []
                for hh in range(2):
                    dec = _decay(acs, acs_t, 2 * p + hh, lower)
                    wm = cb * dec
                    dxs_.append(_dot(wm.T.astype(BF16), dyb))
                    half = lane_lo if hh == 0 else jnp.logical_not(lane_lo)
                    dwm = _dot_nt(jnp.where(half, dyp, 0.0).astype(BF16), xb)
                    dcb = dcb + dwm * dec
                    dseg = dwm * wm
                    dcol = jnp.sum(dseg, axis=1, keepdims=True) - jnp.sum(dseg.T, axis=1, keepdims=True)
                    d_acs_diag = jnp.where(c_i == 2 * p + hh, dcol, d_acs_diag)
                dxdiag =jnp.where(lane_lo, dxs_[0], dxs_[1])
                s_in = st_ref[0, p]
                sb = s_in.astype(BF16)
                ds_out = dstate[p]
                dsb = ds_out.astype(BF16)
                yoff = ea[:, sl] * _dot(cbf, sb)
                dxst = e_l[:, sl] * _dot(bb, dsb)
                dxp = dxdiag + dxst
                dye = dyp * ea[:, sl]
                dyeb = dye.astype(BF16)
                qp = dyp * yoff - xp * dxst
                lastv = (jnp.sum(xp * dxst, axis=0, keepdims=True)
                         + ea_l[:, sl] * jnp.sum(ds_out * s_in, axis=0, keepdims=True))
                q_buf[:, sl] = jnp.where(last_row, qp + lastv, qp)
                dx_buf[:, sl] = dxp
                dcg = dcg + _dot_nt(dyeb, sb)
                dbg = dbg + _dot_nt((xp * e_l[:, sl]).astype(BF16), dsb)
                dstate[p] = ea_l[:, sl] * ds_out + _dot(ct, dyeb)
            dcg = dcg + _dot(dcb.astype(BF16), bb)
            dbg = dbg + _dot(dcb.T.astype(BF16), cbf)
            bsl = slice(512 + 128 * g, 640 + 128 * g)
            csl = slice(768 + 128 * g, 896 + 128 * g)
            dpre_ref[:, bsl] = dbg * dsilu[:, bsl]
            dpre_ref[:, csl] = dcg * dsilu[:, csl]
        dxall = dx_buf[...]
        dpre_ref[:, :SSD_WIDTH] = (dyv * dexp_ref[...] + dxall * dt_exp) * dsilu[:, :SSD_WIDTH]
        dd_ref[...] += jnp.sum(dyv * xs, axis=0, keepdims=True)
        d_acs = d_acs_diag + _hilo_dot_nt(q_buf[...], expand)
        dd_a = _hilo_dot_r(_ones_where(r_i <= c_i), d_acs)
        ddt = dd_a * a_ref[...] + _hilo_dot_nt(dxall * xs, expand)
        ddt = jnp.where(valid, ddt, 0.0)
        da_ref[...] += jnp.sum(dd_a * dtv, axis=0, keepdims=True)
        ddtr = ddt * _sigmoid(dt_in)
        ddtr_ref[...] = ddtr
        dbias_ref[...] += jnp.sum(ddtr, axis=0, keepdims=True)

    vec = pl.BlockSpec((1, 128), lambda s: (0, 0))
    wide = pl.BlockSpec((1, SSD_WIDTH), lambda s: (0, 0))
    rev = lambda s: (nc - 1 - s, 0)
    return pl.pallas_call(
        body, name=name, grid=(nc,),
        in_specs=[pl.BlockSpec((BLOCK, SSD_XBC), rev), pl.BlockSpec((BLOCK, 128), rev), vec, vec, wide,
                  pl.BlockSpec((1, 4, 128, 128), lambda s: (nc - 1 - s, 0, 0, 0)),
                  pl.BlockSpec((BLOCK, SSD_WIDTH), rev)],
        out_specs=[pl.BlockSpec((BLOCK, SSD_XBC), rev), pl.BlockSpec((BLOCK, 128), rev), vec, vec, wide],
        out_shape=[jax.ShapeDtypeStruct((rows, SSD_XBC), F32), jax.ShapeDtypeStruct((rows, 128), F32),
                   jax.ShapeDtypeStruct((1, 128), F32), jax.ShapeDtypeStruct((1, 128), F32),
                   jax.ShapeDtypeStruct((1, SSD_WIDTH), F32)],
        scratch_shapes=[pltpu.VMEM((4, 128, 128), F32), pltpu.VMEM((BLOCK, SSD_WIDTH), F32),
                        pltpu.VMEM((BLOCK, SSD_WIDTH), F32)],
        compiler_params=_cparams(1),
    )(pre, dtr, bias_row, a_row, d_exp, states, dy)


def _peer(xi, yi, ci, k):
    px = (1 - xi) if (k >> 2) & 1 else xi
    py = (1 - yi) if (k >> 1) & 1 else yi
    pc = (1 - ci) if k & 1 else ci
    return (px, py, pc), 4 * px + 2 * py + pc


def _exchange(xs, *, gather, name):
    n = len(xs)
    n_peers = N_DEV - 1
    out_shape = [jax.ShapeDtypeStruct((N_DEV,) + x.shape if gather else x.shape, x.dtype) for x in xs]

    def body(*refs):
        x_refs, o_refs = refs[:n], refs[n:2 * n]
        send_sems, recv_sems, local_sems = refs[2 * n:]
        xi, yi, ci = lax.axis_index("x"), lax.axis_index("y"), lax.axis_index("c")
        me = 4 * xi + 2 * yi + ci

        def copy(a, k, src_idx, dst_idx, peer):
            src = x_refs[a] if gather else x_refs[a].at[src_idx]
            return pltpu.make_async_remote_copy(
                src_ref=src, dst_ref=o_refs[a].at[dst_idx], send_sem=send_sems.at[a * n_peers + k - 1],
                recv_sem=recv_sems.at[a * n_peers + k - 1], device_id=peer, device_id_type=pl.DeviceIdType.MESH)

        local = [pltpu.make_async_copy(x_refs[a] if gather else x_refs[a].at[me], o_refs[a].at[me], local_sems.at[a])
                 for a in range(n)]
        for cp in local:
            cp.start()
        sends = []
        for k in range(1, N_DEV):
            peer, pidx = _peer(xi, yi, ci, k)
            for a in range(n):
                sends.append(copy(a, k, pidx, me, peer))
                sends[-1].start()
        for k in range(1, N_DEV):
            peer, pidx = _peer(xi, yi, ci, k)
            for a in range(n):
                copy(a, k, pidx, pidx, peer).wait_recv()
        for cp in sends:
            cp.wait_send()
        for cp in local:
            cp.wait()

    hbm = pl.BlockSpec(memory_space=pltpu.HBM)
    return pl.pallas_call(
        body, name=name, out_shape=out_shape, in_specs=[hbm] * n, out_specs=[hbm] * n,
        scratch_shapes=[pltpu.SemaphoreType.DMA((n * n_peers,)), pltpu.SemaphoreType.DMA((n * n_peers,)),
                        pltpu.SemaphoreType.DMA((n,))],
    )(*xs)


def _other_chips(xi, yi):
    return [(1 - xi, yi), (xi, 1 - yi), (1 - xi, 1 - yi)]


def _gather_two_level(xs, *, name):
    n = len(xs)

    def body(*refs):
        x_refs, o_refs = refs[:n], refs[n:2 * n]
        send_sems, recv_sems, local_sems = refs[2 * n:]
        xi, yi, ci = lax.axis_index("x"), lax.axis_index("y"), lax.axis_index("c")
        me, sibling = (xi, yi, ci), (xi, yi, 1 - ci)
        chips = _other_chips(xi, yi)

        def slot(px, py, pc):
            return 4 * px + 2 * py + pc

        def copy(a, k, block, to, from_input=False):
            return pltpu.make_async_remote_copy(
                src_ref=x_refs[a] if from_input else o_refs[a].at[slot(*block)], dst_ref=o_refs[a].at[slot(*block)],
                send_sem=send_sems.at[7 * a + k], recv_sem=recv_sems.at[7 * a + k],
                device_id=to, device_id_type=pl.DeviceIdType.MESH)

        local = [pltpu.make_async_copy(x_refs[a], o_refs[a].at[slot(*me)], local_sems.at[a]) for a in range(n)]
        for cp in local:
            cp.start()
        sends = []
        for a in range(n):
            sends.append(copy(a, 0, me, sibling, from_input=True))
            sends += [copy(a, 1 + j, me, (*chip, ci), from_input=True) for j, chip in enumerate(chips)]
        for cp in sends:
            cp.start()
        for j, chip in enumerate(chips):
            for a in range(n):
                copy(a, 1 + j, (*chip, ci), me).wait_recv()
                sends.append(copy(a, 4 + j, (*chip, ci), sibling))
                sends[-1].start()
        for a in range(n):
            copy(a, 0, sibling, me).wait_recv()
            for j, chip in enumerate(chips):
                copy(a, 4 + j, (*chip, 1 - ci), me).wait_recv()
        for cp in sends:
            cp.wait_send()
        for cp in local:
            cp.wait()

    hbm = pl.BlockSpec(memory_space=pltpu.HBM)
    return pl.pallas_call(
        body, name=name, out_shape=[jax.ShapeDtypeStruct((N_DEV,) + x.shape, x.dtype) for x in xs],
        in_specs=[hbm] * n, out_specs=[hbm] * n,
        scratch_shapes=[pltpu.SemaphoreType.DMA((7 * n,)), pltpu.SemaphoreType.DMA((7 * n,)),
                        pltpu.SemaphoreType.DMA((n,))],
    )(*xs)


def _sibling_swap(xs, *, name):
    n = len(xs)

    def body(*refs):
        x_refs, o_refs = refs[:n], refs[n:2 * n]
        send_sems, recv_sems = refs[2 * n:]
        xi, yi, ci = lax.axis_index("x"), lax.axis_index("y"), lax.axis_index("c")
        copies = [pltpu.make_async_remote_copy(
            src_ref=x_refs[a].at[1 - ci], dst_ref=o_refs[a], send_sem=send_sems.at[a], recv_sem=recv_sems.at[a],
            device_id=(xi, yi, 1 - ci), device_id_type=pl.DeviceIdType.MESH) for a in range(n)]
        for cp in copies:
            cp.start()
        for cp in copies:
            cp.wait()

    hbm = pl.BlockSpec(memory_space=pltpu.HBM)
    return pl.pallas_call(
        body, name=name, out_shape=[jax.ShapeDtypeStruct(x.shape[1:], x.dtype) for x in xs],
        in_specs=[hbm] * n, out_specs=[hbm] * n,
        scratch_shapes=[pltpu.SemaphoreType.DMA((n,)), pltpu.SemaphoreType.DMA((n,))],
    )(*xs)


def _chip_all_to_all(xs, *, name):
    n = len(xs)

    def body(*refs):
        x_refs, o_refs = refs[:n], refs[n:2 * n]
        send_sems, recv_sems, local_sems = refs[2 * n:]
        xi, yi, ci = lax.axis_index("x"), lax.axis_index("y"), lax.axis_index("c")
        mine = 2 * xi + yi
        chips = _other_chips(xi, yi)

        def copy(a, j, src_slot, dst_slot, chip):
            return pltpu.make_async_remote_copy(
                src_ref=x_refs[a].at[src_slot], dst_ref=o_refs[a].at[dst_slot], send_sem=send_sems.at[3 * a + j],
                recv_sem=recv_sems.at[3 * a + j], device_id=(*chip, ci), device_id_type=pl.DeviceIdType.MESH)

        local = [pltpu.make_async_copy(x_refs[a].at[mine], o_refs[a].at[mine], local_sems.at[a]) for a in range(n)]
        for cp in local:
            cp.start()
        sends = [copy(a, j, 2 * chip[0] + chip[1], mine, chip) for j, chip in enumerate(chips) for a in range(n)]
        for cp in sends:
            cp.start()
        for j, chip in enumerate(chips):
            for a in range(n):
                copy(a, j, mine, 2 * chip[0] + chip[1], chip).wait_recv()
        for cp in sends:
            cp.wait_send()
        for cp in local:
            cp.wait()

    hbm = pl.BlockSpec(memory_space=pltpu.HBM)
    return pl.pallas_call(
        body, name=name, out_shape=[jax.ShapeDtypeStruct(x.shape, x.dtype) for x in xs],
        in_specs=[hbm] * n, out_specs=[hbm] * n,
        scratch_shapes=[pltpu.SemaphoreType.DMA((3 * n,)), pltpu.SemaphoreType.DMA((3 * n,)),
                        pltpu.SemaphoreType.DMA((n,))],
    )(*xs)


def _pair_add(a, b, *, name):
    rows, cols = a.shape
    lanes = -(-cols // 128) * 128
    tr = _tile(rows, max(16, (512 * 1024) // lanes), 16)

    def body(a_ref, b_ref, o_ref):
        o_ref[...] = (a_ref[...].astype(F32) + b_ref[...].astype(F32)).astype(BF16)

    spec = pl.BlockSpec((tr, cols), lambda i: (i, 0))
    return pl.pallas_call(
        body, name=name, grid=(rows // tr,), in_specs=[spec, spec], out_specs=spec,
        out_shape=jax.ShapeDtypeStruct((rows, cols), BF16), compiler_params=_cparams(1),
    )(a, b)


def _adamw(gs, w, m, v, *, name):
    n_slots = gs.shape[0]
    rows, cols = w.shape
    lanes = -(-cols // 128) * 128
    tr = _tile(rows, max(16, (128 * 1024) // lanes), 16 if gs.dtype == BF16 else 8)

    def body(g_ref, w_ref, m_ref, v_ref, go_ref, d_ref, mo_ref, vo_ref):
        g = g_ref[0].astype(F32)
        for j in range(1, n_slots):
            g = g + g_ref[j].astype(F32)
        m2 = ADAM_B1 * m_ref[...] + (1.0 - ADAM_B1) * g
        v2 = ADAM_B2 * v_ref[...] + (1.0 - ADAM_B2) * (g * g)
        m_hat = m2 / (1.0 - ADAM_B1 ** ADAM_STEP)
        v_hat = v2 / (1.0 - ADAM_B2 ** ADAM_STEP)
        go_ref[...] = g
        d_ref[...] = -ADAM_LR * (m_hat / (jnp.sqrt(v_hat) + ADAM_EPS) + ADAM_WD * w_ref[...])
        mo_ref[...] = m2
        vo_ref[...] = v2

    spec = pl.BlockSpec((tr, cols), lambda i: (i, 0))
    sds = jax.ShapeDtypeStruct((rows, cols), F32)
    return pl.pallas_call(
        body, name=name, grid=(rows // tr,),
        in_specs=[pl.BlockSpec((n_slots, tr, cols), lambda i: (0, i, 0)), spec, spec, spec],
        out_specs=[spec, spec, spec, spec], out_shape=[sds, sds, sds, sds], compiler_params=_cparams(1),
    )(gs, w, m, v)


SHARDED = (("meta_tokens", 1), ("w_in", 2), ("ssd_conv_w", 2), ("mla_w_uq", 2), ("mla_w_ukv", 2),
           ("w_out", 1), ("ffn_w_up", 2), ("ffn_conv_w", 2), ("ffn_w_down", 1))
BIG = ("w_in", "w_out", "ffn_w_up", "ffn_w_down")
SMALL = ("meta_tokens", "ssd_conv_w", "mla_w_uq", "mla_w_ukv", "ffn_conv_w")
REPLICATED = ("norm_mix_g", "ssd_conv_b", "ssd_dt_bias", "ssd_a_log", "ssd_d", "ssd_norm_g", "sb_norm_g",
              "mla_q_norm_g", "mla_kv_norm_g", "mla_norm_g", "norm_ffn_g", "ffn_conv_b", "final_norm_g")
WEIGHTS = ("meta_tokens", "norm_mix_g", "w_in", "ssd_conv_w", "ssd_conv_b", "ssd_dt_bias", "ssd_a_log", "ssd_d",
           "ssd_norm_g", "sb_norm_g", "mla_q_norm_g", "mla_kv_norm_g", "mla_w_uq", "mla_w_ukv", "mla_norm_g",
           "w_out", "norm_ffn_g", "ffn_w_up", "ffn_conv_w", "ffn_conv_b", "ffn_w_down", "final_norm_g")


def _flat_pack(arrays, dtype, align):
    flat = jnp.concatenate([a.reshape(-1).astype(dtype) for a in arrays])
    pad = (-flat.shape[0]) % align
    return jnp.pad(flat, (0, pad)).reshape(-1, 128)


def _pieces(full, axis):
    shp = full.shape
    t = full.reshape(shp[:axis] + (N_DEV, shp[axis] // N_DEV) + shp[axis + 1:])
    return jnp.moveaxis(t, axis, 0).reshape(N_DEV, -1)


def _unpieces(p8, shard_shape, axis):
    t = p8.reshape((N_DEV,) + shard_shape)
    t = jnp.moveaxis(t, 0, axis)
    return t.reshape(shard_shape[:axis] + (N_DEV * shard_shape[axis],) + shard_shape[axis + 1:])


def _split_core_chip(full, axis):
    shp = full.shape
    t = full.reshape(shp[:axis] + (4, 2, shp[axis] // N_DEV) + shp[axis + 1:])
    return jnp.moveaxis(t, (axis + 1, axis), (0, 1))


def _merge_blocks(b8, axis):
    shard = b8.shape[1:]
    t = jnp.moveaxis(b8, 0, axis)
    return t.reshape(shard[:axis] + (N_DEV * shard[axis],) + shard[axis + 1:])


def _gather_weights(shards):
    axes = dict(SHARDED)
    got = _gather_two_level([shards[n].astype(BF16) for n in BIG], name="gather_big")
    full = {n: _merge_blocks(b8, axes[n]) for n, b8 in zip(BIG, got)}
    packed = _flat_pack([shards[n] for n in SMALL], F32, 8 * 128)
    got = _exchange([packed], gather=True, name="gather_small")[0].reshape(N_DEV, -1)
    off = 0
    for n in SMALL:
        size = math.prod(shards[n].shape)
        full[n] = _unpieces(got[:, off:off + size], shards[n].shape, axes[n])
        off += size
    return full


def _pad_cols(a, width):
    return jnp.pad(a, ((0, 0), (0, width - a.shape[1])))


def _w_in_padded(w):
    kr = w[:, 2632:2664]
    return jnp.concatenate([
        w[:, 0:512], w[:, 512:1536], w[:, 1544:2312], _pad_cols(w[:, 2312:2504], 256), w[:, 2504:2632],
        _pad_cols(kr[:, :16], 64), _pad_cols(kr[:, 16:], 64), _pad_cols(w[:, 1536:1544], 128),
        jnp.zeros((w.shape[0], 128), w.dtype)], axis=1)


def _w_in_unpadded(wp):
    return jnp.concatenate([
        wp[:, 0:512], wp[:, 512:1536], wp[:, OFF_DT:OFF_DT + 8], wp[:, 1536:2304], wp[:, OFF_QA:OFF_QA + 192],
        wp[:, OFF_CKV:OFF_CKV + 128], wp[:, OFF_KR:OFF_KR + 16], wp[:, OFF_KR + 64:OFF_KR + 80]], axis=1)


def _w_uq_perm(w):
    t = w.reshape(MLA_Q_RANK, MLA_HEADS, MLA_NOPE + MLA_ROPE)
    out = jnp.concatenate([t[:, :, :64].reshape(MLA_Q_RANK, 256), t[:, :, 64:80].reshape(MLA_Q_RANK, 64),
                           t[:, :, 80:96].reshape(MLA_Q_RANK, 64)], axis=1)
    return jnp.pad(out, ((0, 256 - MLA_Q_RANK), (0, 0)))


def _w_uq_unperm(wp):
    wp = wp[:MLA_Q_RANK]
    t = jnp.concatenate([wp[:, :256].reshape(MLA_Q_RANK, 4, 64), wp[:, 256:320].reshape(MLA_Q_RANK, 4, 16),
                         wp[:, 320:384].reshape(MLA_Q_RANK, 4, 16)], axis=2)
    return t.reshape(MLA_Q_RANK, 4 * 96)


def _w_ukv_perm(w):
    t = w.reshape(MLA_KV_RANK, MLA_HEADS, 128)
    return jnp.concatenate([t[:, :, :64].reshape(MLA_KV_RANK, 256), t[:, :, 64:].reshape(MLA_KV_RANK, 256)], axis=1)


def _w_ukv_unperm(wp):
    t = jnp.concatenate([wp[:, :256].reshape(MLA_KV_RANK, 4, 64), wp[:, 256:].reshape(MLA_KV_RANK, 4, 64)], axis=2)
    return t.reshape(MLA_KV_RANK, 512)


def _heads(a, hd):
    return jnp.moveaxis(a.reshape(a.shape[0], -1, hd), 1, 0)


def _unheads(a):
    return jnp.moveaxis(a, 0, 1).reshape(a.shape[1], -1)


def _row(v, width=None):
    v = v.reshape(1, -1)
    return v if width is None else _pad_cols(v, width)


def _rope_tables(rows):
    pos = jnp.arange(rows, dtype=F32) - float(N_PAD)
    inv = 1.0 / (ROPE_BASE ** (jnp.arange(0, MLA_ROPE, 2, dtype=F32) / MLA_ROPE))
    ang = pos[:, None] * inv[None, :]
    cos = jnp.tile(jnp.cos(ang), (1, 8))
    sin = jnp.tile(jnp.sin(ang), (1, 4))
    return cos, jnp.concatenate([-sin, sin], axis=1)


def _layer_fwd(h, p, cos_t, sin_t, tag):
    s = {"h_in": h}
    hn = _rmsnorm_fwd(h, p["norm_mix_g"], width=D_MODEL, name=tag + "norm_mix")
    u = _matmul(hn, p["w_in"], name=tag + "in_proj")
    s["hn"], s["u"] = hn, u

    xbc_in = u[:, OFF_XBC:OFF_XBC + SSD_XBC]
    pre = _dwconv_fwd(xbc_in, p["ssd_conv_w"], p["ssd_conv_b"], taps=SSD_CONV, name=tag + "ssd_conv")
    dtr = u[:, OFF_DT:OFF_DT + 128]
    y_ssd, states = _ssd_fwd(pre, dtr, p["dt_bias"], p["a_row"], p["d_exp"], name=tag + "ssd_core")
    zgate = u[:, OFF_Z:OFF_Z + SSD_WIDTH]
    yn_ssd = _rmsnorm_fwd(y_ssd, p["ssd_norm_g"], width=SSD_WIDTH, z=zgate, name=tag + "ssd_norm")
    s.update(xbc_in=xbc_in, pre=pre, dtr=dtr, y_ssd=y_ssd, states=states, zgate=zgate)

    q_sb = _heads(u[:, OFF_QSB:OFF_QSB + SB_WIDTH], SB_HEAD_DIM)
    k_sb = _heads(u[:, OFF_KSB:OFF_KSB + SB_WIDTH], SB_HEAD_DIM).astype(BF16)
    v_sb = _heads(u[:, OFF_VSB:OFF_VSB + SB_WIDTH], SB_HEAD_DIM).astype(BF16)
    o_sb, u_tot = _sb_fwd(q_sb, k_sb, v_sb, name=tag + "sb_attn")
    o_sb_flat = _unheads(o_sb)
    yn_sb = _rmsnorm_fwd(o_sb_flat, p["sb_norm_g"], width=SB_WIDTH, name=tag + "sb_norm")
    s.update(q_sb=q_sb, k_sb=k_sb, v_sb=v_sb, u_tot=u_tot, o_sb_flat=o_sb_flat)

    qa = u[:, OFF_QA:OFF_QA + 256]
    ckv = u[:, OFF_CKV:OFF_CKV + 128]
    qa_n = _rmsnorm_fwd(qa, p["mla_q_norm_g"], width=MLA_Q_RANK, name=tag + "mla_qnorm")
    ckv_n = _rmsnorm_fwd(ckv, p["mla_kv_norm_g"], width=MLA_KV_RANK, name=tag + "mla_kvnorm")
    qf = _matmul(qa_n, p["mla_w_uq"], name=tag + "mla_uq")
    kvf = _matmul(ckv_n, p["mla_w_ukv"], name=tag + "mla_ukv")
    q_rope = _rope(qf[:, 256:384], cos_t, sin_t, name=tag + "rope_q")
    k_rope = _rope(u[:, OFF_KR:OFF_KR + 128], cos_t, sin_t, name=tag + "rope_k")
    rows = h.shape[0]
    zpad = jnp.zeros((MLA_HEADS, rows, 32), F32)
    qh = jnp.concatenate([_heads(qf[:, :256], 64), _heads(q_rope[:, :64], 16), _heads(q_rope[:, 64:], 16), zpad], axis=2)
    kr_b = jnp.broadcast_to(jnp.concatenate([k_rope[:, 0:16], k_rope[:, 64:80]], axis=1)[None], (MLA_HEADS, rows, 32))
    kh = jnp.concatenate([_heads(kvf[:, :256], 64), kr_b, zpad], axis=2).astype(BF16)
    vh = _heads(kvf[:, 256:], 64).astype(BF16)
    o_mla, lse = _mla_fwd(qh, kh, vh, name=tag + "mla_attn")
    o_mla_flat = _unheads(o_mla)
    yn_mla = _rmsnorm_fwd(o_mla_flat, p["mla_norm_g"], width=256, name=tag + "mla_norm")
    s.update(qa=qa, ckv=ckv, qa_n=qa_n, ckv_n=ckv_n, qh=qh, kh=kh, vh=vh, o_mla=o_mla, lse=lse,
             o_mla_flat=o_mla_flat)

    mix = jnp.concatenate([yn_ssd, yn_sb, yn_mla], axis=1)
    h_mid = _matmul(mix, p["w_out"], res=h, mask_pad=True, name=tag + "out_proj")
    hn2 = _rmsnorm_fwd(h_mid, p["norm_ffn_g"], width=D_MODEL, name=tag + "norm_ffn")
    up = _matmul(hn2, p["ffn_w_up"], tn=1408, name=tag + "ffn_up")
    act = _ffn_conv_gate_fwd(up, p["ffn_conv_w"], p["ffn_conv_b"], taps=FFN_CONV, name=tag + "ffn_conv_gate")
    h_out = _matmul(act, p["ffn_w_down"], res=h_mid, mask_pad=True, tk=1408, name=tag + "ffn_down")
    s.update(mix=mix, h_mid=h_mid, hn2=hn2, up=up, act=act)
    return h_out, s


def _layer_bwd(dh_out, p, s, cos_t, sin_t, tag):
    g = {}
    rows = dh_out.shape[0]
    dact = _matmul(dh_out, p["ffn_w_down"], tb=True, tn=1408, name=tag + "b_down_dx")
    g["ffn_w_down"] = _matmul(s["act"], dh_out, ta=True, tm=1408, tk=640, name=tag + "b_down_dw")
    dup1, dup2, dcw1, dcw2, dcb1, dcb2 = _ffn_conv_gate_bwd(
        dact, s["up"], p["ffn_conv_w"], p["ffn_conv_b"], taps=FFN_CONV, name=tag + "b_ffn_conv_gate")
    g["ffn_conv_w"] = jnp.concatenate([dcw1[:FFN_CONV], dcw2[:FFN_CONV]], axis=1)
    g["ffn_conv_b"] = jnp.concatenate([dcb1[0], dcb2[0]])
    w_up1, w_up2 = p["ffn_w_up"][:, :D_FF], p["ffn_w_up"][:, D_FF:]
    dhn2 = _matmul(dup1, w_up1, tb=True, tk=1408, name=tag + "b_up_dx1")
    dhn2 = _matmul(dup2, w_up2, tb=True, tk=1408, res=dhn2, name=tag + "b_up_dx2")
    g["ffn_w_up"] = jnp.concatenate([_matmul(s["hn2"], dup1, ta=True, tn=1408, tk=640, name=tag + "b_up_dw1"),
                                     _matmul(s["hn2"], dup2, ta=True, tn=1408, tk=640, name=tag + "b_up_dw2")], axis=1)
    dh_mid, _, dg = _rmsnorm_bwd(s["h_mid"], p["norm_ffn_g"], dhn2, width=D_MODEL, res=dh_out, mask_pad=True,
                                 name=tag + "b_norm_ffn")
    g["norm_ffn_g"] = dg[0]

    dmix = _matmul(dh_mid, p["w_out"], tb=True, name=tag + "b_out_dx")
    g["w_out"] = _matmul(s["mix"], dh_mid, ta=True, tk=640, name=tag + "b_out_dw")

    dy_ssd, dz, dg = _rmsnorm_bwd(s["y_ssd"], p["ssd_norm_g"], dmix[:, :SSD_WIDTH], width=SSD_WIDTH, z=s["zgate"],
                                  name=tag + "b_ssd_norm")
    g["ssd_norm_g"] = dg[0]
    dpre, ddtr, dbias, da, dd = _ssd_bwd(s["pre"], s["dtr"], p["dt_bias"], p["a_row"], p["d_exp"], s["states"],
                                         dy_ssd, name=tag + "b_ssd_core")
    g["ssd_dt_bias"] = dbias[0, :8]
    g["ssd_a_log"] = da[0, :8] * p["a_row"][0, :8]
    g["ssd_d"] = dd.reshape(8, 64).sum(axis=1)
    dxbc_in, dcw, dcb_ = _dwconv_bwd(dpre, s["xbc_in"], p["ssd_conv_w"], taps=SSD_CONV, name=tag + "b_ssd_conv")
    g["ssd_conv_w"], g["ssd_conv_b"] = dcw[:SSD_CONV], dcb_[0]

    do_sb_flat, _, dg = _rmsnorm_bwd(s["o_sb_flat"], p["sb_norm_g"], dmix[:, 512:768], width=SB_WIDTH,
                                     name=tag + "b_sb_norm")
    g["sb_norm_g"] = dg[0]
    dq_sb, dk_sb, dv_sb = _sb_bwd(s["q_sb"], s["k_sb"], s["v_sb"], _heads(do_sb_flat, SB_HEAD_DIM), s["u_tot"],
                                  name=tag + "b_sb_attn")

    do_mla_flat, _, dg = _rmsnorm_bwd(s["o_mla_flat"], p["mla_norm_g"], dmix[:, 768:1024], width=256,
                                      name=tag + "b_mla_norm")
    g["mla_norm_g"] = dg[0]
    dqh, dkh, dvh = _mla_bwd(s["qh"], s["kh"], s["vh"], s["o_mla"], s["lse"], _heads(do_mla_flat, 64),
                             name=tag + "b_mla_attn")
    dq_rope_in = jnp.concatenate([_unheads(dqh[:, :, 64:80]), _unheads(dqh[:, :, 80:96])], axis=1)
    dq_r = _rope(dq_rope_in, cos_t, sin_t, transpose=True, name=tag + "b_rope_q")
    dqf = jnp.concatenate([_unheads(dqh[:, :, :64]), dq_r], axis=1)
    dkr_sum = jnp.sum(dkh[:, :, 64:96], axis=0)
    dk_rope_in = jnp.concatenate([_pad_cols(dkr_sum[:, :16], 64), _pad_cols(dkr_sum[:, 16:], 64)], axis=1)
    dkr = _rope(dk_rope_in, cos_t, sin_t, transpose=True, name=tag + "b_rope_k")
    dkvf = jnp.concatenate([_unheads(dkh[:, :, :64]), _unheads(dvh)], axis=1)
    dqa_n = _matmul(dqf, p["mla_w_uq"], tb=True, name=tag + "b_uq_dx")
    g["mla_w_uq"] = _matmul(s["qa_n"], dqf, ta=True, tk=640, name=tag + "b_uq_dw")
    dckv_n = _matmul(dkvf, p["mla_w_ukv"], tb=True, name=tag + "b_ukv_dx")
    g["mla_w_ukv"] = _matmul(s["ckv_n"], dkvf, ta=True, tk=640, name=tag + "b_ukv_dw")
    dqa, _, dg = _rmsnorm_bwd(s["qa"], p["mla_q_norm_g"], dqa_n, width=MLA_Q_RANK, name=tag + "b_mla_qnorm")
    g["mla_q_norm_g"] = dg[0, :MLA_Q_RANK]
    dckv, _, dg = _rmsnorm_bwd(s["ckv"], p["mla_kv_norm_g"], dckv_n, width=MLA_KV_RANK, name=tag + "b_mla_kvnorm")
    g["mla_kv_norm_g"] = dg[0]

    du = jnp.concatenate([dz, dxbc_in, _unheads(dq_sb), _unheads(dk_sb), _unheads(dv_sb), dqa, dckv, dkr, ddtr,
                          jnp.zeros((rows, 128), F32)], axis=1)
    dhn = _matmul(du, p["w_in"], tb=True, name=tag + "b_in_dx")
    g["w_in"] = _matmul(s["hn"], du, ta=True, tk=640, name=tag + "b_in_dw")
    dh_in, _, dg = _rmsnorm_bwd(s["h_in"], p["norm_mix_g"], dhn, width=D_MODEL, res=dh_mid, mask_pad=True,
                                name=tag + "b_norm_mix")
    g["norm_mix_g"] = dg[0]
    return dh_in, g


def _prepare_layer(full, rep, l):
    a_row = _row(-jnp.exp(rep["ssd_a_log"][l]), 128)
    return {
        "norm_mix_g": _row(rep["norm_mix_g"][l]),
        "w_in": _w_in_padded(full["w_in"][l]),
        "ssd_conv_w": jnp.pad(full["ssd_conv_w"][l], ((0, HALO - SSD_CONV), (0, 0))),
        "ssd_conv_b": _row(rep["ssd_conv_b"][l]),
        "dt_bias": _row(rep["ssd_dt_bias"][l], 128),
        "a_row": a_row,
        "d_exp": _row(jnp.repeat(rep["ssd_d"][l], 64)),
        "ssd_norm_g": _row(rep["ssd_norm_g"][l]),
        "sb_norm_g": _row(rep["sb_norm_g"][l]),
        "mla_q_norm_g": _row(rep["mla_q_norm_g"][l], 256),
        "mla_kv_norm_g": _row(rep["mla_kv_norm_g"][l]),
        "mla_w_uq": _w_uq_perm(full["mla_w_uq"][l]),
        "mla_w_ukv": _w_ukv_perm(full["mla_w_ukv"][l]),
        "mla_norm_g": _row(rep["mla_norm_g"][l]),
        "w_out": full["w_out"][l],
        "norm_ffn_g": _row(rep["norm_ffn_g"][l]),
        "ffn_w_up": full["ffn_w_up"][l],
        "ffn_conv_w": jnp.pad(full["ffn_conv_w"][l], ((0, HALO - FFN_CONV), (0, 0))),
        "ffn_conv_b": _row(rep["ffn_conv_b"][l]),
        "ffn_w_down": full["ffn_w_down"][l],
    }


def _layer_grads_to_full(g):
    out = dict(g)
    out["w_in"] = _w_in_unpadded(g["w_in"])
    out["mla_w_uq"] = _w_uq_unperm(g["mla_w_uq"])
    out["mla_w_ukv"] = _w_ukv_unperm(g["mla_w_ukv"])
    return out


def kernel(x, meta_tokens, norm_mix_g, w_in, ssd_conv_w, ssd_conv_b, ssd_dt_bias, ssd_a_log, ssd_d, ssd_norm_g, sb_norm_g, mla_q_norm_g, mla_kv_norm_g, mla_w_uq, mla_w_ukv, mla_norm_g, w_out, norm_ffn_g, ffn_w_up, ffn_conv_w, ffn_conv_b, ffn_w_down, final_norm_g, loss_target, m_meta_tokens, m_norm_mix_g, m_w_in, m_ssd_conv_w, m_ssd_conv_b, m_ssd_dt_bias, m_ssd_a_log, m_ssd_d, m_ssd_norm_g, m_sb_norm_g, m_mla_q_norm_g, m_mla_kv_norm_g, m_mla_w_uq, m_mla_w_ukv, m_mla_norm_g, m_w_out, m_norm_ffn_g, m_ffn_w_up, m_ffn_conv_w, m_ffn_conv_b, m_ffn_w_down, m_final_norm_g, v_meta_tokens, v_norm_mix_g, v_w_in, v_ssd_conv_w, v_ssd_conv_b, v_ssd_dt_bias, v_ssd_a_log, v_ssd_d, v_ssd_norm_g, v_sb_norm_g, v_mla_q_norm_g, v_mla_kv_norm_g, v_mla_w_uq, v_mla_w_ukv, v_mla_norm_g, v_w_out, v_norm_ffn_g, v_ffn_w_up, v_ffn_conv_w, v_ffn_conv_b, v_ffn_w_down, v_final_norm_g):
    w = dict(meta_tokens=meta_tokens, norm_mix_g=norm_mix_g, w_in=w_in, ssd_conv_w=ssd_conv_w, ssd_conv_b=ssd_conv_b,
             ssd_dt_bias=ssd_dt_bias, ssd_a_log=ssd_a_log, ssd_d=ssd_d, ssd_norm_g=ssd_norm_g, sb_norm_g=sb_norm_g,
             mla_q_norm_g=mla_q_norm_g, mla_kv_norm_g=mla_kv_norm_g, mla_w_uq=mla_w_uq, mla_w_ukv=mla_w_ukv,
             mla_norm_g=mla_norm_g, w_out=w_out, norm_ffn_g=norm_ffn_g, ffn_w_up=ffn_w_up, ffn_conv_w=ffn_conv_w,
             ffn_conv_b=ffn_conv_b, ffn_w_down=ffn_w_down, final_norm_g=final_norm_g)
    mom = dict(meta_tokens=m_meta_tokens, norm_mix_g=m_norm_mix_g, w_in=m_w_in, ssd_conv_w=m_ssd_conv_w,
               ssd_conv_b=m_ssd_conv_b, ssd_dt_bias=m_ssd_dt_bias, ssd_a_log=m_ssd_a_log, ssd_d=m_ssd_d,
               ssd_norm_g=m_ssd_norm_g, sb_norm_g=m_sb_norm_g, mla_q_norm_g=m_mla_q_norm_g,
               mla_kv_norm_g=m_mla_kv_norm_g, mla_w_uq=m_mla_w_uq, mla_w_ukv=m_mla_w_ukv, mla_norm_g=m_mla_norm_g,
               w_out=m_w_out, norm_ffn_g=m_norm_ffn_g, ffn_w_up=m_ffn_w_up, ffn_conv_w=m_ffn_conv_w,
               ffn_conv_b=m_ffn_conv_b, ffn_w_down=m_ffn_w_down, final_norm_g=m_final_norm_g)
    vel = dict(meta_tokens=v_meta_tokens, norm_mix_g=v_norm_mix_g, w_in=v_w_in, ssd_conv_w=v_ssd_conv_w,
               ssd_conv_b=v_ssd_conv_b, ssd_dt_bias=v_ssd_dt_bias, ssd_a_log=v_ssd_a_log, ssd_d=v_ssd_d,
               ssd_norm_g=v_ssd_norm_g, sb_norm_g=v_sb_norm_g, mla_q_norm_g=v_mla_q_norm_g,
               mla_kv_norm_g=v_mla_kv_norm_g, mla_w_uq=v_mla_w_uq, mla_w_ukv=v_mla_w_ukv, mla_norm_g=v_mla_norm_g,
               w_out=v_w_out, norm_ffn_g=v_norm_ffn_g, ffn_w_up=v_ffn_w_up, ffn_conv_w=v_ffn_conv_w,
               ffn_conv_b=v_ffn_conv_b, ffn_w_down=v_ffn_w_down, final_norm_g=v_final_norm_g)

    full = _gather_weights({n: w[n] for n, _ in SHARDED})
    layers = [_prepare_layer(full, w, l) for l in range(DEPTH)]

    seq = x.shape[1]
    rows = BLOCK + seq
    cos_t, sin_t = _rope_tables(rows)
    h = jnp.concatenate([jnp.zeros((N_PAD, D_MODEL), F32), full["meta_tokens"], x[0]], axis=0)

    saved = []
    for l in range(DEPTH):
        h, s = _layer_fwd(h, layers[l], cos_t, sin_t, "l%d_" % l)
        saved.append(s)
    dh, dg_final, loss_part = _final_loss(h, _row(final_norm_g), loss_target[0], name="final_loss")
    loss = lax.psum(loss_part[0, 0], ("x", "y", "c"))

    layer_grads = [None] * DEPTH
    for l in reversed(range(DEPTH)):
        dh, g = _layer_bwd(dh, layers[l], saved[l], cos_t, sin_t, "l%d_" % l)
        layer_grads[l] = _layer_grads_to_full(g)
    grad_x = dh[BLOCK:][None]

    partial = {n: jnp.stack([layer_grads[l][n] for l in range(DEPTH)]) for n in layer_grads[0]}
    partial["meta_tokens"] = dh[N_PAD:BLOCK]
    partial["final_norm_g"] = dg_final[0]

    results = [dict(), dict(), dict(), dict()]
    axes = dict(SHARDED)

    core = lax.axis_index("c")
    halves = [_split_core_chip(partial[n], axes[n]).astype(BF16) for n in BIG]
    theirs = _sibling_swap(halves, name="grad_sibling_swap")
    chip_sums = []
    for n, h2, t4 in zip(BIG, halves, theirs):
        view = (4 * math.prod(w[n].shape[:-1]), w[n].shape[-1])
        mine = lax.dynamic_index_in_dim(h2, core, 0, keepdims=False)
        chip_sums.append(_pair_add(mine.reshape(view), t4.reshape(view), name="grad_pair_add_" + n).reshape(t4.shape))
    got_big = _chip_all_to_all(chip_sums, name="grad_chip_all_to_all")
    for n, g4 in zip(BIG, got_big):
        shp = w[n].shape
        view = (math.prod(shp[:-1]), shp[-1])
        outs = _adamw(g4.reshape((4,) + view), w[n].reshape(view), mom[n].reshape(view), vel[n].reshape(view),
                      name="adamw_" + n)
        for kind in range(4):
            results[kind][n] = outs[kind].reshape(shp)

    send = jnp.concatenate([_pieces(partial[n], axes[n]) for n in SMALL], axis=1)
    pad = (-send.shape[1]) % (8 * 128)
    send = jnp.pad(send, ((0, 0), (0, pad))).reshape(N_DEV, -1, 128)
    got = _exchange([send], gather=False, name="grad_all_to_all_small")[0]
    pack = lambda d: _flat_pack([d[n] for n in SMALL], F32, 8 * 128)
    sh_out = _adamw(got, pack(w), pack(mom), pack(vel), name="adamw_small")

    rep_g = _flat_pack([partial[n] for n in REPLICATED], F32, 8 * 128)
    got_r = _exchange([rep_g], gather=True, name="grad_all_gather")[0]
    packr = lambda d: _flat_pack([d[n] for n in REPLICATED], F32, 8 * 128)
    rep_out = _adamw(got_r, packr(w), packr(mom), packr(vel), name="adamw_replicated")

    for names, outs in ((list(SMALL), sh_out), (list(REPLICATED), rep_out)):
        off = 0
        for n in names:
            size = math.prod(w[n].shape)
            for kind in range(4):
                results[kind][n] = outs[kind].reshape(-1)[off:off + size].reshape(w[n].shape)
            off += size

    return (loss, grad_x, *[results[0][n] for n in WEIGHTS], *[results[1][n] for n in WEIGHTS],
            *[results[2][n] for n in WEIGHTS], *[results[3][n] for n in WEIGHTS])
```

```python
import math

import jax
import jax.numpy as jnp
from jax import lax
from jax.experimental import pallas as pl
from jax.experimental.pallas import tpu as pltpu

F32 = jnp.float32
BF16 = jnp.bfloat16

D_MODEL = 1024
DEPTH = 2
N_META = 16
BLOCK = 128
N_PAD = BLOCK - N_META
EPS = 1e-6
SSD_WIDTH = 512
SSD_XBC = 1024
SSD_CONV = 4
SB_WIDTH = 256
SB_HEAD_DIM = 64
MLA_HEADS = 4
MLA_NOPE = 64
MLA_ROPE = 32
MLA_Q_RANK = 192
MLA_KV_RANK = 128
ROPE_BASE = 10000.0
D_FF = 2816
FFN_CONV = 3
IN_COLS = 2664
N_DEV = 8

ADAM_LR = 0.001
ADAM_B1 = 0.9
ADAM_B2 = 0.999
ADAM_EPS = 1e-08
ADAM_WD = 0.01
ADAM_STEP = 10

U_COLS = 3072
OFF_Z, OFF_XBC, OFF_QSB, OFF_KSB, OFF_VSB, OFF_QA, OFF_CKV, OFF_KR, OFF_DT = (
    0, 512, 1536, 1792, 2048, 2304, 2560, 2688, 2816)

V7X_VMEM_BYTES = 64 * 1024 * 1024
VMEM_LIMIT = (V7X_VMEM_BYTES * 7) // 8
NEG_BIG = -1e30


def _cparams(n_axes):
    return pltpu.CompilerParams(dimension_semantics=("arbitrary",) * n_axes, vmem_limit_bytes=VMEM_LIMIT)


def _tile(n, target, align):
    best = None
    for d in range(align, min(n, target) + 1, align):
        if n % d == 0:
            best = d
    return n if best is None else best


def _sigmoid(x):
    return 1.0 / (1.0 + jnp.exp(-x))


def _softplus(x):
    return jnp.maximum(x, 0.0) + jnp.log(1.0 + jnp.exp(-jnp.abs(x)))


def _dot(a, b):
    return jnp.dot(a, b, preferred_element_type=F32)


def _dot_nt(a, b):
    return lax.dot_general(a, b, (((1,), (1,)), ((), ())), preferred_element_type=F32)


def _hilo(x):
    hi = x.astype(BF16)
    lo = (x - hi.astype(F32)).astype(BF16)
    return hi, lo


def _hilo_dot_l(x, m):
    hi, lo = _hilo(x)
    return _dot(hi, m) + _dot(lo, m)


def _hilo_dot_r(m, x):
    hi, lo = _hilo(x)
    return _dot(m, hi) + _dot(m, lo)


def _hilo_dot_nt(x, m):
    hi, lo = _hilo(x)
    return _dot_nt(hi, m) + _dot_nt(lo, m)


def _ones_where(cond):
    return jnp.where(cond, 1.0, 0.0).astype(BF16)


def _matmul(a, b, *, name, ta=False, tb=False, out_dtype=F32, res=None, mask_pad=False,
            tm=640, tn=1024, tk=1024):
    m_dim = a.shape[1] if ta else a.shape[0]
    k_dim = a.shape[0] if ta else a.shape[1]
    n_dim = b.shape[0] if tb else b.shape[1]
    assert (b.shape[1] if tb else b.shape[0]) == k_dim
    tm = _tile(m_dim, tm, 128)
    tn = _tile(n_dim, tn, 128)
    tk = _tile(k_dim, tk, 128)
    nk = k_dim // tk
    dn = (((0 if ta else 1,), (1 if tb else 0,)), ((), ()))

    def body(*refs):
        if res is not None:
            a_ref, b_ref, r_ref, o_ref, acc = refs
        else:
            a_ref, b_ref, o_ref, acc = refs
        k = pl.program_id(2)

        @pl.when(k == 0)
        def _():
            acc[...] = jnp.zeros_like(acc)

        acc[...] += lax.dot_general(a_ref[...].astype(BF16), b_ref[...].astype(BF16), dn,
                                    preferred_element_type=F32)

        @pl.when(k == nk - 1)
        def _():
            r = acc[...]
            if res is not None:
                r = r + r_ref[...].astype(F32)
            if mask_pad:
                rows = pl.program_id(0) * tm + lax.broadcasted_iota(jnp.int32, (tm, 1), 0)
                r = jnp.where(rows >= N_PAD, r, 0.0)
            o_ref[...] = r.astype(out_dtype)

    a_spec = pl.BlockSpec((tk, tm), lambda i, j, k: (k, i)) if ta else pl.BlockSpec((tm, tk), lambda i, j, k: (i, k))
    b_spec = pl.BlockSpec((tn, tk), lambda i, j, k: (j, k)) if tb else pl.BlockSpec((tk, tn), lambda i, j, k: (k, j))
    o_spec = pl.BlockSpec((tm, tn), lambda i, j, k: (i, j))
    in_specs = [a_spec, b_spec]
    args = [a, b]
    if res is not None:
        in_specs.append(o_spec)
        args.append(res)
    return pl.pallas_call(
        body, name=name, grid=(m_dim // tm, n_dim // tn, nk),
        in_specs=in_specs, out_specs=o_spec,
        out_shape=jax.ShapeDtypeStruct((m_dim, n_dim), out_dtype),
        scratch_shapes=[pltpu.VMEM((tm, tn), F32)],
        compiler_params=_cparams(3),
    )(*args)


def _rmsnorm_fwd(x, g, *, width, name, z=None, out_dtype=None):
    out_dtype = BF16 if out_dtype is None else out_dtype
    rows, w = x.shape
    tr = _tile(rows, 640, 128)
    inv_w = 1.0 / width

    def body(*refs):
        if z is not None:
            x_ref, z_ref, g_ref, o_ref = refs
        else:
            x_ref, g_ref, o_ref = refs
        t = x_ref[...].astype(F32)
        if z is not None:
            zz = z_ref[...]
            t = t * (zz * _sigmoid(zz))
        ms = jnp.sum(t * t, axis=-1, keepdims=True) * inv_w
        o_ref[...] = ((t * lax.rsqrt(ms + EPS)) * g_ref[...]).astype(out_dtype)

    row_spec = pl.BlockSpec((tr, w), lambda i: (i, 0))
    g_spec = pl.BlockSpec((1, w), lambda i: (0, 0))
    in_specs = [row_spec] + ([row_spec] if z is not None else []) + [g_spec]
    args = [x] + ([z] if z is not None else []) + [g]
    return pl.pallas_call(
        body, name=name, grid=(rows // tr,), in_specs=in_specs, out_specs=row_spec,
        out_shape=jax.ShapeDtypeStruct((rows, w), out_dtype), compiler_params=_cparams(1),
    )(*args)


def _rmsnorm_bwd(x, g, dout, *, width, name, z=None, res=None, mask_pad=False):
    rows, w = x.shape
    tr = _tile(rows, 640, 128)
    inv_w = 1.0 / width

    def body(*refs):
        refs = list(refs)
        x_ref = refs.pop(0)
        z_ref = refs.pop(0) if z is not None else None
        g_ref = refs.pop(0)
        do_ref = refs.pop(0)
        r_ref = refs.pop(0) if res is not None else None
        dx_ref = refs.pop(0)
        dz_ref = refs.pop(0) if z is not None else None
        dg_ref = refs.pop(0)
        i = pl.program_id(0)

        @pl.when(i == 0)
        def _():
            dg_ref[...] = jnp.zeros_like(dg_ref)

        xv = x_ref[...].astype(F32)
        t = xv
        if z is not None:
            zz = z_ref[...]
            sig = _sigmoid(zz)
            sl = zz * sig
            t = xv * sl
        ms = jnp.sum(t * t, axis=-1, keepdims=True) * inv_w
        rstd = lax.rsqrt(ms + EPS)
        xhat = t * rstd
        do = do_ref[...].astype(F32)
        dxh = do * g_ref[...]
        c = jnp.sum(dxh * xhat, axis=-1, keepdims=True) * inv_w
        dt = rstd * (dxh - xhat * c)
        dg_ref[...] += jnp.sum(do * xhat, axis=0, keepdims=True)
        if z is not None:
            dz_ref[...] = dt * xv * (sig * (1.0 + zz * (1.0 - sig)))
            dx = dt * sl
        else:
            dx = dt
        if res is not None:
            dx = dx + r_ref[...]
        if mask_pad:
            rws = i * tr + lax.broadcasted_iota(jnp.int32, (tr, 1), 0)
            dx = jnp.where(rws >= N_PAD, dx, 0.0)
        dx_ref[...] = dx

    row_spec = pl.BlockSpec((tr, w), lambda i: (i, 0))
    g_spec = pl.BlockSpec((1, w), lambda i: (0, 0))
    in_specs = [row_spec] + ([row_spec] if z is not None else []) + [g_spec, row_spec] + (
        [row_spec] if res is not None else [])
    args = [x] + ([z] if z is not None else []) + [g, dout] + ([res] if res is not None else [])
    out_specs = [row_spec] + ([row_spec] if z is not None else []) + [g_spec]
    out_shape = [jax.ShapeDtypeStruct((rows, w), F32)] + (
        [jax.ShapeDtypeStruct((rows, w), F32)] if z is not None else []) + [jax.ShapeDtypeStruct((1, w), F32)]
    outs = pl.pallas_call(
        body, name=name, grid=(rows // tr,), in_specs=in_specs, out_specs=out_specs,
        out_shape=out_shape, compiler_params=_cparams(1),
    )(*args)
    if z is not None:
        return outs[0], outs[1], outs[2]
    return outs[0], None, outs[1]


def _final_loss(h, g, target, *, name):
    rows, w = h.shape
    nb = rows // BLOCK
    inv_w = 1.0 / w

    def body(h_ref, g_ref, t_ref, dh_ref, dg_ref, loss_ref):
        i = pl.program_id(0)

        @pl.when(i == 0)
        def _():
            dg_ref[...] = jnp.zeros_like(dg_ref)
            loss_ref[...] = jnp.zeros_like(loss_ref)

        xv = h_ref[...]
        ms = jnp.sum(xv * xv, axis=-1, keepdims=True) * inv_w
        rstd = lax.rsqrt(ms + EPS)
        xhat = xv * rstd
        gv = g_ref[...]
        err = jnp.where(i >= 1, xhat * gv - t_ref[...], 0.0)
        loss_ref[...] += (0.5 * inv_w) * jnp.sum(err * err)
        do = err * inv_w
        dxh = do * gv
        c = jnp.sum(dxh * xhat, axis=-1, keepdims=True) * inv_w
        dh_ref[...] = rstd * (dxh - xhat * c)
        dg_ref[...] += jnp.sum(do * xhat, axis=0, keepdims=True)

    row_spec = pl.BlockSpec((BLOCK, w), lambda i: (i, 0))
    g_spec = pl.BlockSpec((1, w), lambda i: (0, 0))
    return pl.pallas_call(
        body, name=name, grid=(nb,),
        in_specs=[row_spec, g_spec, pl.BlockSpec((BLOCK, w), lambda i: (jnp.maximum(i - 1, 0), 0))],
        out_specs=[row_spec, g_spec, pl.BlockSpec((1, 128), lambda i: (0, 0))],
        out_shape=[jax.ShapeDtypeStruct((rows, w), F32), jax.ShapeDtypeStruct((1, w), F32),
                   jax.ShapeDtypeStruct((1, 128), F32)],
        compiler_params=_cparams(1),
    )(h, g, target)


HALO = 8


def _dwconv_fwd(u, w8, b, *, taps, name):
    rows, ch = u.shape
    tb = _tile(rows, 640, 128)
    tc = _tile(ch, 512, 128)
    hb = tb // HALO

    def body(u_ref, h_ref, w_ref, b_ref, o_ref, buf):
        i = pl.program_id(0)
        buf[0:HALO, :] = jnp.where(i > 0, h_ref[...], 0.0)
        buf[HALO:HALO + tb, :] = u_ref[...]
        acc = jnp.broadcast_to(b_ref[...], (tb, tc))
        for k in range(taps):
            acc = acc + w_ref[k:k + 1, :] * buf[pl.ds(HALO - (taps - 1) + k, tb), :]
        o_ref[...] = acc

    return pl.pallas_call(
        body, name=name, grid=(rows // tb, ch // tc),
        in_specs=[pl.BlockSpec((tb, tc), lambda i, j: (i, j)),
                  pl.BlockSpec((HALO, tc), lambda i, j: (jnp.maximum(i * hb - 1, 0), j)),
                  pl.BlockSpec((HALO, tc), lambda i, j: (0, j)),
                  pl.BlockSpec((1, tc), lambda i, j: (0, j))],
        out_specs=pl.BlockSpec((tb, tc), lambda i, j: (i, j)),
        out_shape=jax.ShapeDtypeStruct((rows, ch), F32),
        scratch_shapes=[pltpu.VMEM((tb + HALO, tc), F32)],
        compiler_params=_cparams(2),
    )(u, u, w8, b)


def _dwconv_bwd(dpre, u, w8, *, taps, name):
    rows, ch = u.shape
    tb = _tile(rows, 640, 128)
    tc = _tile(ch, 512, 128)
    hb = tb // HALO
    nb = rows // tb
    last_halo = rows // HALO - 1

    def body(d_ref, dn_ref, u_ref, up_ref, w_ref, du_ref, dw_ref, db_ref, bufd, bufu):
        i = pl.program_id(1)

        @pl.when(i == 0)
        def _():
            dw_ref[...] = jnp.zeros_like(dw_ref)
            db_ref[...] = jnp.zeros_like(db_ref)

        d = d_ref[...]
        bufd[0:tb, :] = d
        bufd[tb:tb + HALO, :] = jnp.where(i < nb - 1, dn_ref[...], 0.0)
        bufu[0:HALO, :] = jnp.where(i > 0, up_ref[...], 0.0)
        bufu[HALO:HALO + tb, :] = u_ref[...]
        acc = jnp.zeros((tb, tc), F32)
        for k in range(taps):
            acc = acc + w_ref[k:k + 1, :] * bufd[pl.ds(taps - 1 - k, tb), :]
        du_ref[...] = acc
        for k in range(taps):
            dw_ref[k:k + 1, :] += jnp.sum(d * bufu[pl.ds(HALO - (taps - 1) + k, tb), :], axis=0, keepdims=True)
        db_ref[...] += jnp.sum(d, axis=0, keepdims=True)

    return pl.pallas_call(
        body, name=name, grid=(ch // tc, nb),
        in_specs=[pl.BlockSpec((tb, tc), lambda j, i: (i, j)),
                  pl.BlockSpec((HALO, tc), lambda j, i: (jnp.minimum((i + 1) * hb, last_halo), j)),
                  pl.BlockSpec((tb, tc), lambda j, i: (i, j)),
                  pl.BlockSpec((HALO, tc), lambda j, i: (jnp.maximum(i * hb - 1, 0), j)),
                  pl.BlockSpec((HALO, tc), lambda j, i: (0, j))],
        out_specs=[pl.BlockSpec((tb, tc), lambda j, i: (i, j)),
                   pl.BlockSpec((HALO, tc), lambda j, i: (0, j)),
                   pl.BlockSpec((1, tc), lambda j, i: (0, j))],
        out_shape=[jax.ShapeDtypeStruct((rows, ch), F32), jax.ShapeDtypeStruct((HALO, ch), F32),
                   jax.ShapeDtypeStruct((1, ch), F32)],
        scratch_shapes=[pltpu.VMEM((tb + HALO, tc), F32), pltpu.VMEM((tb + HALO, tc), F32)],
        compiler_params=_cparams(2),
    )(dpre, dpre, u, u, w8)


def _ffn_conv_gate_fwd(up, w8, b, *, taps, name):
    rows, c2 = up.shape
    f = c2 // 2
    tb = _tile(rows, 640, 128)
    tc = _tile(f, 512, 128)
    nct = f // tc
    hb = tb // HALO

    def body(u1_ref, u2_ref, h1_ref, h2_ref, w1_ref, w2_ref, b1_ref, b2_ref, o_ref, buf1, buf2):
        i = pl.program_id(0)
        pre = []
        for u_ref, h_ref, w_ref, b_ref, buf in ((u1_ref, h1_ref, w1_ref, b1_ref, buf1),
                                                 (u2_ref, h2_ref, w2_ref, b2_ref, buf2)):
            buf[0:HALO, :] = jnp.where(i > 0, h_ref[...], 0.0)
            buf[HALO:HALO + tb, :] = u_ref[...]
            acc = jnp.broadcast_to(b_ref[...], (tb, tc))
            for k in range(taps):
                acc = acc + w_ref[k:k + 1, :] * buf[pl.ds(HALO - (taps - 1) + k, tb), :]
            pre.append(acc)
        o_ref[...] = (pre[0] * _sigmoid(pre[0]) * pre[1]).astype(BF16)

    main = lambda off: pl.BlockSpec((tb, tc), lambda i, j: (i, j + off))
    halo = lambda off: pl.BlockSpec((HALO, tc), lambda i, j: (jnp.maximum(i * hb - 1, 0), j + off))
    wrow = lambda off: pl.BlockSpec((HALO, tc), lambda i, j: (0, j + off))
    brow = lambda off: pl.BlockSpec((1, tc), lambda i, j: (0, j + off))
    return pl.pallas_call(
        body, name=name, grid=(rows // tb, nct),
        in_specs=[main(0), main(nct), halo(0), halo(nct), wrow(0), wrow(nct), brow(0), brow(nct)],
        out_specs=pl.BlockSpec((tb, tc), lambda i, j: (i, j)),
        out_shape=jax.ShapeDtypeStruct((rows, f), BF16),
        scratch_shapes=[pltpu.VMEM((tb + HALO, tc), F32), pltpu.VMEM((tb + HALO, tc), F32)],
        compiler_params=_cparams(2),
    )(up, up, up, up, w8, w8, b, b)


def _ffn_conv_gate_bwd(dact, up, w8, b, *, taps, name):
    rows, c2 = up.shape
    f = c2 // 2
    tb = _tile(rows, 640, 128)
    tc = _tile(f, 512, 128)
    nct = f // tc
    hb = tb // HALO
    nb = rows // tb
    last_halo = rows // HALO - 1
    ext = tb + HALO

    def body(d_ref, dn_ref, u1_ref, u2_ref, p1_ref, p2_ref, n1_ref, n2_ref, w1_ref, w2_ref, b1_ref, b2_ref,
             du1_ref, du2_ref, dw1_ref, dw2_ref, db1_ref, db2_ref, bufu1, bufu2, bufd1, bufd2):
        i = pl.program_id(1)

        @pl.when(i == 0)
        def _():
            for r in (dw1_ref, dw2_ref, db1_ref, db2_ref):
                r[...] = jnp.zeros_like(r)

        has_next = i < nb - 1
        pre = []
        for u_ref, p_ref, n_ref, w_ref, b_ref, buf in ((u1_ref, p1_ref, n1_ref, w1_ref, b1_ref, bufu1),
                                                       (u2_ref, p2_ref, n2_ref, w2_ref, b2_ref, bufu2)):
            buf[0:HALO, :] = jnp.where(i > 0, p_ref[...], 0.0)
            buf[HALO:HALO + tb, :] = u_ref[...]
            buf[HALO + tb:HALO + ext, :] = jnp.where(has_next, n_ref[...], 0.0)
            acc = jnp.broadcast_to(b_ref[...], (ext, tc))
            for k in range(taps):
                acc = acc + w_ref[k:k + 1, :] * buf[pl.ds(HALO - (taps - 1) + k, ext), :]
            pre.append(acc)
        d_ext = jnp.concatenate([d_ref[...], jnp.where(has_next, dn_ref[...], 0.0)], axis=0)
        sig = _sigmoid(pre[0])
        bufd1[...] = d_ext * pre[1] * (sig * (1.0 + pre[0] * (1.0 - sig)))
        bufd2[...] = d_ext * (pre[0] * sig)
        for w_ref, bufd, bufu, du_ref, dw_ref, db_ref in ((w1_ref, bufd1, bufu1, du1_ref, dw1_ref, db1_ref),
                                                          (w2_ref, bufd2, bufu2, du2_ref, dw2_ref, db2_ref)):
            acc = jnp.zeros((tb, tc), F32)
            for k in range(taps):
                acc = acc + w_ref[k:k + 1, :] * bufd[pl.ds(taps - 1 - k, tb), :]
            du_ref[...] = acc
            dmain = bufd[0:tb, :]
            for k in range(taps):
                dw_ref[k:k + 1, :] += jnp.sum(dmain * bufu[pl.ds(HALO - (taps - 1) + k, tb), :], axis=0, keepdims=True)
            db_ref[...] += jnp.sum(dmain, axis=0, keepdims=True)

    main = lambda off: pl.BlockSpec((tb, tc), lambda j, i: (i, j + off))
    prev = lambda off: pl.BlockSpec((HALO, tc), lambda j, i: (jnp.maximum(i * hb - 1, 0), j + off))
    nxt = lambda off: pl.BlockSpec((HALO, tc), lambda j, i: (jnp.minimum((i + 1) * hb, last_halo), j + off))
    wrow = lambda off: pl.BlockSpec((HALO, tc), lambda j, i: (0, j + off))
    brow = lambda off: pl.BlockSpec((1, tc), lambda j, i: (0, j + off))
    half = jax.ShapeDtypeStruct((rows, f), F32)
    return pl.pallas_call(
        body, name=name, grid=(nct, nb),
        in_specs=[main(0), nxt(0), main(0), main(nct), prev(0), prev(nct), nxt(0), nxt(nct),
                  wrow(0), wrow(nct), brow(0), brow(nct)],
        out_specs=[main(0), main(0), wrow(0), wrow(0), brow(0), brow(0)],
        out_shape=[half, half, jax.ShapeDtypeStruct((HALO, f), F32), jax.ShapeDtypeStruct((HALO, f), F32),
                   jax.ShapeDtypeStruct((1, f), F32), jax.ShapeDtypeStruct((1, f), F32)],
        scratch_shapes=[pltpu.VMEM((ext + HALO, tc), F32), pltpu.VMEM((ext + HALO, tc), F32),
                        pltpu.VMEM((ext, tc), F32), pltpu.VMEM((ext, tc), F32)],
        compiler_params=_cparams(2),
    )(dact, dact, up, up, up, up, up, up, w8, w8, b, b)


def _rope(xr, cos_t, sin_t, *, name, transpose=False):
    rows, w = xr.shape
    tr = _tile(rows, 640, 128)

    def body(x_ref, c_ref, s_ref, o_ref):
        xv = x_ref[...]
        if transpose:
            o_ref[...] = xv * c_ref[...] + pltpu.roll(xv * s_ref[...], 64, 1)
        else:
            o_ref[...] = xv * c_ref[...] + pltpu.roll(xv, 64, 1) * s_ref[...]

    spec = pl.BlockSpec((tr, w), lambda i: (i, 0))
    return pl.pallas_call(
        body, name=name, grid=(rows // tr,), in_specs=[spec, spec, spec], out_specs=spec,
        out_shape=jax.ShapeDtypeStruct((rows, w), F32), compiler_params=_cparams(1),
    )(xr, cos_t, sin_t)


ATT_TQ = 640
ATT_GROUP = 4


def _grouped_loop(lo, hi, step, carry, *, descending=False):
    n = jnp.maximum(hi - lo, 0)
    n_groups = lax.div(n, jnp.int32(ATT_GROUP))
    rest = n - n_groups * ATT_GROUP
    n_pairs = rest >> 1
    n_single = rest & 1
    if descending:
        top = hi - 1 - ATT_GROUP * n_groups
        carry = lax.fori_loop(
            0, n_groups, lambda t, c: step([hi - 1 - ATT_GROUP * t - b for b in range(ATT_GROUP)], c), carry)
        carry = lax.fori_loop(0, n_pairs, lambda t, c: step([top - 2 * t, top - 2 * t - 1], c), carry)
        return lax.fori_loop(0, n_single, lambda t, c: step([lo], c), carry)
    base = lo + ATT_GROUP * n_groups
    carry = lax.fori_loop(0, n_groups, lambda t, c: step([lo + ATT_GROUP * t + b for b in range(ATT_GROUP)], c), carry)
    carry = lax.fori_loop(0, n_pairs, lambda t, c: step([base + 2 * t, base + 2 * t + 1], c), carry)
    return lax.fori_loop(0, n_single, lambda t, c: step([hi - 1], c), carry)


def _tile_iotas(rows=BLOCK):
    r_i = lax.broadcasted_iota(jnp.int32, (rows, BLOCK), 0)
    c_i = lax.broadcasted_iota(jnp.int32, (rows, BLOCK), 1)
    return r_i, c_i


def _key_ranges(i, tq, first=0):
    n_blocks = ((i + 1) * tq + (BLOCK - 1)) >> 7
    first_diag = jnp.maximum((i * tq) >> 7, first)
    return first_diag, n_blocks


def _dot_tn(a, b):
    return lax.dot_general(a, b, (((0,), (0,)), ((), ())), preferred_element_type=F32)


def _cumsum_rhs(pred):
    r = lax.broadcasted_iota(jnp.int32, (BLOCK, 2 * BLOCK), 0)
    c = lax.broadcasted_iota(jnp.int32, (BLOCK, 2 * BLOCK), 1)
    return _ones_where((c >= BLOCK) | pred(r, c))


def _cumsum_dot(x, rhs):
    r = _dot(x.astype(BF16), rhs)
    return r[:, :BLOCK], r[:, BLOCK:]


def _sb_fwd(q, k, v, *, name):
    nh, rows, hd = q.shape
    tq = _tile(rows, ATT_TQ, 8)
    scale = SB_HEAD_DIM ** -0.5

    def body(q_ref, k_ref, v_ref, o_ref, u_ref):
        i = pl.program_id(1)
        r_i, c_i = _tile_iotas(tq)
        m_after = _cumsum_rhs(lambda j, s: j > s)
        qb = (q_ref[0] * scale).astype(BF16)
        rowpos = i * tq + r_i
        first_diag, n_blocks = _key_ranges(i, tq)

        def step(js, carry, masked):
            acc, cu = carry
            offs = [pl.multiple_of(j * BLOCK, BLOCK) for j in js]
            zs = [_dot_nt(qb, k_ref[0, pl.ds(off, BLOCK), :].astype(BF16)) for off in offs]
            sp = [_softplus(z) for z in zs]
            if masked:
                masks = [((off + c_i) < rowpos) & ((off + c_i) >= N_PAD) for off in offs]
                cs = [_cumsum_dot(jnp.where(m_, s, 0.0), m_after) for m_, s in zip(masks, sp)]
            else:
                cs = [_cumsum_dot(s, m_after) for s in sp]
            wgt = []
            for b in range(len(js)):
                w_ = jnp.exp(zs[b] - sp[b] - (cu + cs[b][0]))
                wgt.append(jnp.where(masks[b], w_, 0.0) if masked else w_)
                cu = cu + cs[b][1]
            for b, off in enumerate(offs):
                acc = acc + _dot(wgt[b].astype(BF16), v_ref[0, pl.ds(off, BLOCK), :].astype(BF16))
            return acc, cu

        carry = (jnp.zeros((tq, hd), F32), jnp.zeros((tq, BLOCK), F32))
        carry = _grouped_loop(first_diag, n_blocks, lambda js, c: step(js, c, True), carry, descending=True)
        acc, cu = _grouped_loop(0, first_diag, lambda js, c: step(js, c, False), carry, descending=True)
        o_ref[0] = acc
        u_ref[0] = -cu

    blk = pl.BlockSpec((1, tq, hd), lambda h, i: (h, i, 0))
    full = pl.BlockSpec((1, rows, hd), lambda h, i: (h, 0, 0))
    return pl.pallas_call(
        body, name=name, grid=(nh, rows // tq), in_specs=[blk, full, full],
        out_specs=[blk, pl.BlockSpec((1, tq, 128), lambda h, i: (h, i, 0))],
        out_shape=[jax.ShapeDtypeStruct((nh, rows, hd), F32), jax.ShapeDtypeStruct((nh, rows, 128), F32)],
        compiler_params=_cparams(2),
    )(q, k, v)


def _sb_bwd(q, k, v, do, u_tot, *, name):
    nh, rows, hd = q.shape
    tq = _tile(rows, ATT_TQ, 8)
    scale = SB_HEAD_DIM ** -0.5

    def body(q_ref, k_ref, v_ref, do_ref, u_ref, dq_ref, dk_ref, dv_ref):
        i = pl.program_id(1)

        @pl.when(i == 0)
        def _():
            dk_ref[...] = jnp.zeros_like(dk_ref)
            dv_ref[...] = jnp.zeros_like(dv_ref)

        r_i, c_i = _tile_iotas(tq)
        m_incl = _cumsum_rhs(lambda j, s: j <= s)
        m_excl = _cumsum_rhs(lambda j, s: j < s)
        qb = (q_ref[0] * scale).astype(BF16)
        dob = do_ref[0].astype(BF16)
        rowpos = i * tq + r_i
        first_diag, n_blocks = _key_ranges(i, tq)

        def step(js, carry, masked):
            dq, rem, cg = carry
            nb = range(len(js))
            offs = [pl.multiple_of(j * BLOCK, BLOCK) for j in js]
            kbs = [k_ref[0, pl.ds(off, BLOCK), :].astype(BF16) for off in offs]
            vbs = [v_ref[0, pl.ds(off, BLOCK), :].astype(BF16) for off in offs]
            zs = [_dot_nt(qb, kb) for kb in kbs]
            dws = [_dot_nt(dob, vb) for vb in vbs]
            sp = [_softplus(z) for z in zs]
            sig = [jnp.exp(zs[b] - sp[b]) for b in nb]
            if masked:
                masks = [((off + c_i) < rowpos) & ((off + c_i) >= N_PAD) for off in offs]
                cs = [_cumsum_dot(jnp.where(masks[b], sp[b], 0.0), m_incl) for b in nb]
            else:
                cs = [_cumsum_dot(sp[b], m_incl) for b in nb]
            wgt, gg = [], []
            for b in nb:
                w_ = jnp.exp(jnp.minimum(zs[b] - sp[b] - (rem - cs[b][0]), 0.0))
                wgt.append(jnp.where(masks[b], w_, 0.0) if masked else w_)
                gg.append(wgt[b] * dws[b])
                rem = rem - cs[b][1]
            gs = [_cumsum_dot(g_, m_excl) for g_ in gg]
            dzb = []
            for b in nb:
                dz = gg[b] * (1.0 - sig[b]) - sig[b] * (cg + gs[b][0])
                dzb.append((jnp.where(masks[b], dz, 0.0) if masked else dz).astype(BF16))
                cg = cg + gs[b][1]
            for b, off in enumerate(offs):
                dq = dq + _dot(dzb[b], kbs[b])
                dk_ref[0, pl.ds(off, BLOCK), :] += _dot_tn(dzb[b], qb)
                dv_ref[0, pl.ds(off, BLOCK), :] += _dot_tn(wgt[b].astype(BF16), dob)
            return dq, rem, cg

        carry = (jnp.zeros((tq, hd), F32), -u_ref[0], jnp.zeros((tq, BLOCK), F32))
        carry = _grouped_loop(0, first_diag, lambda js, c: step(js, c, False), carry)
        dq, _, _ = _grouped_loop(first_diag, n_blocks, lambda js, c: step(js, c, True), carry)
        dq_ref[0] = dq * scale

    blk = pl.BlockSpec((1, tq, hd), lambda h, i: (h, i, 0))
    full = pl.BlockSpec((1, rows, hd), lambda h, i: (h, 0, 0))
    ublk = pl.BlockSpec((1, tq, 128), lambda h, i: (h, i, 0))
    sds = jax.ShapeDtypeStruct((nh, rows, hd), F32)
    return pl.pallas_call(
        body, name=name, grid=(nh, rows // tq), in_specs=[blk, full, full, blk, ublk],
        out_specs=[blk, full, full], out_shape=[sds, sds, sds], compiler_params=_cparams(2),
    )(q, k, v, do, u_tot)


def _mla_fwd(q, k, v_ones, *, name):
    nh, rows, dk = q.shape
    dv = v_ones.shape[2] // 2
    tq = _tile(rows, ATT_TQ, 8)
    scale = (MLA_NOPE + MLA_ROPE) ** -0.5

    def body(q_ref, k_ref, v_ref, o_ref, lse_ref):
        i = pl.program_id(1)
        r_i, c_i = _tile_iotas(tq)
        qb = q_ref[0].astype(BF16)
        rowpos = i * tq + r_i
        first_diag, n_blocks = _key_ranges(i, tq, 1)

        def step(js, carry, masked):
            m, acc = carry
            offs = [pl.multiple_of(j * BLOCK, BLOCK) for j in js]
            ss = [_dot_nt(qb, k_ref[0, pl.ds(off, BLOCK), :].astype(BF16)) * scale for off in offs]
            if masked:
                ss = [jnp.where(((off + c_i) <= rowpos) & ((off + c_i) >= N_PAD), s, NEG_BIG)
                      for off, s in zip(offs, ss)]
            m_new = m
            for s in ss:
                m_new = jnp.maximum(m_new, jnp.max(s, axis=1, keepdims=True))
            ps = [jnp.exp(s - m_new).astype(BF16) for s in ss]
            acc = jnp.exp(m - m_new) * acc
            for off, p in zip(offs, ps):
                acc = acc + _dot(p, v_ref[0, pl.ds(off, BLOCK), :].astype(BF16))
            return m_new, acc

        carry = (jnp.full((tq, 1), NEG_BIG, F32), jnp.zeros((tq, 2 * dv), F32))
        carry = step([0], carry, True)
        carry = _grouped_loop(1, first_diag, lambda js, c: step(js, c, False), carry)
        m, acc = _grouped_loop(first_diag, n_blocks, lambda js, c: step(js, c, True), carry)
        o_ref[0] = (acc / pltpu.roll(acc, dv, 1))[:, :dv]
        lse_ref[0] = m + jnp.log(jnp.where(c_i >= dv, acc, 1.0))

    qblk = pl.BlockSpec((1, tq, dk), lambda h, i: (h, i, 0))
    kfull = pl.BlockSpec((1, rows, dk), lambda h, i: (h, 0, 0))
    return pl.pallas_call(
        body, name=name, grid=(nh, rows // tq), in_specs=[qblk, kfull, kfull],
        out_specs=[pl.BlockSpec((1, tq, dv), lambda h, i: (h, i, 0)),
                   pl.BlockSpec((1, tq, 128), lambda h, i: (h, i, 0))],
        out_shape=[jax.ShapeDtypeStruct((nh, rows, dv), F32), jax.ShapeDtypeStruct((nh, rows, 128), F32)],
        compiler_params=_cparams(2),
    )(q, k, v_ones)


def _mla_bwd(q, k, v, o, lse, do, *, name):
    nh, rows, dk = q.shape
    dv = v.shape[2]
    tq = _tile(rows, ATT_TQ, 8)
    scale = (MLA_NOPE + MLA_ROPE) ** -0.5

    def body(q_ref, k_ref, v_ref, o_ref, lse_ref, do_ref, dq_ref, dk_ref, dv_ref):
        i = pl.program_id(1)

        @pl.when(i == 0)
        def _():
            dk_ref[...] = jnp.zeros_like(dk_ref)
            dv_ref[...] = jnp.zeros_like(dv_ref)

        r_i, c_i = _tile_iotas(tq)
        qb = q_ref[0].astype(BF16)
        dov = do_ref[0]
        dob = dov.astype(BF16)
        delta = jnp.sum(dov * o_ref[0], axis=1, keepdims=True)
        lse = lse_ref[0][:, BLOCK - 1:BLOCK]
        rowpos = i * tq + r_i
        first_diag, n_blocks = _key_ranges(i, tq, 1)

        def step(js, dq, masked):
            nb = range(len(js))
            offs = [pl.multiple_of(j * BLOCK, BLOCK) for j in js]
            kbs = [k_ref[0, pl.ds(off, BLOCK), :].astype(BF16) for off in offs]
            ss = [_dot_nt(qb, kb) for kb in kbs]
            dps = [_dot_nt(dob, v_ref[0, pl.ds(off, BLOCK), :].astype(BF16)) for off in offs]
            ps = [jnp.exp(jnp.minimum(s * scale - lse, 0.0)) for s in ss]
            if masked:
                ps = [jnp.where(((off + c_i) <= rowpos) & ((off + c_i) >= N_PAD), p, 0.0) for off, p in zip(offs, ps)]
            dss = [(ps[b] * (dps[b] - delta) * scale).astype(BF16) for b in nb]
            for b, off in enumerate(offs):
                dq = dq + _dot(dss[b], kbs[b])
                dk_ref[0, pl.ds(off, BLOCK), :] += _dot_tn(dss[b], qb)
                dv_ref[0, pl.ds(off, BLOCK), :] += _dot_tn(ps[b].astype(BF16), dob)
            return dq

        dq = step([0], jnp.zeros((tq, dk), F32), True)
        dq = _grouped_loop(1, first_diag, lambda js, c: step(js, c, False), dq)
        dq_ref[0] = _grouped_loop(first_diag, n_blocks, lambda js, c: step(js, c, True), dq)

    qblk = pl.BlockSpec((1, tq, dk), lambda h, i: (h, i, 0))
    vblk = pl.BlockSpec((1, tq, dv), lambda h, i: (h, i, 0))
    lblk = pl.BlockSpec((1, tq, 128), lambda h, i: (h, i, 0))
    kfull = pl.BlockSpec((1, rows, dk), lambda h, i: (h, 0, 0))
    vfull = pl.BlockSpec((1, rows, dv), lambda h, i: (h, 0, 0))
    return pl.pallas_call(
        body, name=name, grid=(nh, rows // tq), in_specs=[qblk, kfull, vfull, vblk, lblk, vblk],
        out_specs=[qblk, kfull, vfull],
        out_shape=[jax.ShapeDtypeStruct((nh, rows, dk), F32), jax.ShapeDtypeStruct((nh, rows, dk), F32),
                   jax.ShapeDtypeStruct((nh, rows, dv), F32)],
        compiler_params=_cparams(2),
    )(q, k, v, o, lse, do)


def _ssd_consts():
    r_i, c_i = _tile_iotas()
    eh = lax.broadcasted_iota(jnp.int32, (BLOCK, SSD_WIDTH), 0)
    ec = lax.broadcasted_iota(jnp.int32, (BLOCK, SSD_WIDTH), 1)
    expand = _ones_where(lax.shift_right_logical(ec, 6) == eh)
    return r_i, c_i, expand


def _ssd_common(pre_v, dtr_v, bias_v, a_v, chunk, r_i, c_i, expand):
    lower = r_i >= c_i
    sig_pre = _sigmoid(pre_v)
    xbc = pre_v * sig_pre
    xs = xbc[:, :SSD_WIDTH]
    valid = (chunk * BLOCK + lax.broadcasted_iota(jnp.int32, (BLOCK, 1), 0)) >= N_PAD
    dt_in = dtr_v + bias_v
    dtv = jnp.where(valid, _softplus(dt_in), 0.0)
    d_a = dtv * a_v
    acs = _hilo_dot_r(_ones_where(lower), d_a)
    acs_t = acs.T
    dt_exp = _hilo_dot_l(dtv, expand)
    acs_exp = _hilo_dot_l(acs, expand)
    a_last = acs_exp[BLOCK - 1:BLOCK, :]
    ea = jnp.exp(acs_exp)
    e_l = jnp.exp(a_last - acs_exp)
    ea_l = jnp.exp(a_last)
    return lower, sig_pre, xbc, xs, valid, dt_in, dtv, acs, acs_t, dt_exp, ea, e_l, ea_l


def _decay(acs, acs_t, h, lower):
    col = acs[:, h:h + 1]
    row = acs_t[h:h + 1, :]
    return jnp.where(lower, jnp.exp(jnp.minimum(col - row, 0.0)), 0.0)


def _ssd_fwd(pre, dtr, bias_row, a_row, d_exp, *, name):
    rows = pre.shape[0]
    nc = rows // BLOCK

    def body(pre_ref, dtr_ref, bias_ref, a_ref, dexp_ref, y_ref, st_ref, state):
        c = pl.program_id(0)

        @pl.when(c == 0)
        def _():
            state[...] = jnp.zeros_like(state)

        r_i, c_i, expand = _ssd_consts()
        lane_lo = c_i < 64
        (lower, _, xbc, xs, _, _, _, acs, acs_t, dt_exp, ea, e_l, ea_l) = _ssd_common(
            pre_ref[...], dtr_ref[...], bias_ref[...], a_ref[...], c, r_i, c_i, expand)
        xin = xs * dt_exp
        for g in range(2):
            bg = xbc[:, 512 + 128 * g:640 + 128 * g]
            cg = xbc[:, 768 + 128 * g:896 + 128 * g]
            bb = bg.astype(BF16)
            cbf = cg.astype(BF16)
            cb = _dot_nt(cbf, bb)
            bt = bg.T.astype(BF16)
            for pp in range(2):
                p = 2 * g + pp
                sl = slice(128 * p, 128 * p + 128)
                xp = xin[:, sl]
                xb = xp.astype(BF16)
                rs = [_dot((cb * _decay(acs, acs_t, 2 * p + hh, lower)).astype(BF16), xb) for hh in range(2)]
                ydiag = jnp.where(lane_lo, rs[0], rs[1])
                s_in = state[p]
                st_ref[0, p] = s_in
                yoff = ea[:, sl] * _dot(cbf, s_in.astype(BF16))
                y_ref[:, sl] = ydiag + yoff + xs[:, sl] * dexp_ref[:, sl]
                state[p] = ea_l[:, sl] * s_in + _dot(bt, (xp * e_l[:, sl]).astype(BF16))

    vec = pl.BlockSpec((1, 128), lambda c: (0, 0))
    return pl.pallas_call(
        body, name=name, grid=(nc,),
        in_specs=[pl.BlockSpec((BLOCK, SSD_XBC), lambda c: (c, 0)),
                  pl.BlockSpec((BLOCK, 128), lambda c: (c, 0)), vec, vec,
                  pl.BlockSpec((1, SSD_WIDTH), lambda c: (0, 0))],
        out_specs=[pl.BlockSpec((BLOCK, SSD_WIDTH), lambda c: (c, 0)),
                   pl.BlockSpec((1, 4, 128, 128), lambda c: (c, 0, 0, 0))],
        out_shape=[jax.ShapeDtypeStruct((rows, SSD_WIDTH), F32), jax.ShapeDtypeStruct((nc, 4, 128, 128), F32)],
        scratch_shapes=[pltpu.VMEM((4, 128, 128), F32)],
        compiler_params=_cparams(1),
    )(pre, dtr, bias_row, a_row, d_exp)


def _ssd_bwd(pre, dtr, bias_row, a_row, d_exp, states, dy, *, name):
    rows = pre.shape[0]
    nc = rows // BLOCK

    def body(pre_ref, dtr_ref, bias_ref, a_ref, dexp_ref, st_ref, dy_ref,
             dpre_ref, ddtr_ref, dbias_ref, da_ref, dd_ref, dstate, q_buf, dx_buf):
        step = pl.program_id(0)
        c = nc - 1 - step

        @pl.when(step == 0)
        def _():
            dstate[...] = jnp.zeros_like(dstate)
            dbias_ref[...] = jnp.zeros_like(dbias_ref)
            da_ref[...] = jnp.zeros_like(da_ref)
            dd_ref[...] = jnp.zeros_like(dd_ref)

        r_i, c_i, expand = _ssd_consts()
        lane_lo = c_i < 64
        last_row = r_i == BLOCK - 1
        pre_v = pre_ref[...]
        (lower, sig_pre, xbc, xs, valid, dt_in, dtv, acs, acs_t, dt_exp, ea, e_l, ea_l) = _ssd_common(
            pre_v, dtr_ref[...], bias_ref[...], a_ref[...], c, r_i, c_i, expand)
        dsilu = sig_pre * (1.0 + pre_v * (1.0 - sig_pre))
        xin = xs * dt_exp
        dyv = dy_ref[...]
        d_acs_diag = jnp.zeros((BLOCK, BLOCK), F32)
        for g in range(2):
            bg = xbc[:, 512 + 128 * g:640 + 128 * g]
            cg = xbc[:, 768 + 128 * g:896 + 128 * g]
            bb = bg.astype(BF16)
            cbf = cg.astype(BF16)
            cb = _dot_nt(cbf, bb)
            ct = cg.T.astype(BF16)
            dcb = jnp.zeros((BLOCK, BLOCK), F32)
            dbg = jnp.zeros((BLOCK, BLOCK), F32)
            dcg = jnp.zeros((BLOCK, BLOCK), F32)
            for pp in range(2):
                p = 2 * g + pp
                sl = slice(128 * p, 128 * p + 128)
                xp = xin[:, sl]
                xb = xp.astype(BF16)
                dyp = dyv[:, sl]
                dyb = dyp.astype(BF16)
                dxs_ = []
                for hh in range(2):
                    dec = _decay(acs, acs_t, 2 * p + hh, lower)
                    wm = cb * dec
                    dxs_.append(_dot(wm.T.astype(BF16), dyb))
                    half = lane_lo if hh == 0 else jnp.logical_not(lane_lo)
                    dwm = _dot_nt(jnp.where(half, dyp, 0.0).astype(BF16), xb)
                    dcb = dcb + dwm * dec
                    dseg = dwm * wm
                    dcol = jnp.sum(dseg, axis=1, keepdims=True) - jnp.sum(dseg.T, axis=1, keepdims=True)
                    d_acs_diag = jnp.where(c_i == 2 * p + hh, dcol, d_acs_diag)
                dxdiag =jnp.where(lane_lo, dxs_[0], dxs_[1])
                s_in = st_ref[0, p]
                sb = s_in.astype(BF16)
                ds_out = dstate[p]
                dsb = ds_out.astype(BF16)
                yoff = ea[:, sl] * _dot(cbf, sb)
                dxst = e_l[:, sl] * _dot(bb, dsb)
                dxp = dxdiag + dxst
                dye = dyp * ea[:, sl]
                dyeb = dye.astype(BF16)
                qp = dyp * yoff - xp * dxst
                lastv = (jnp.sum(xp * dxst, axis=0, keepdims=True)
                         + ea_l[:, sl] * jnp.sum(ds_out * s_in, axis=0, keepdims=True))
                q_buf[:, sl] = jnp.where(last_row, qp + lastv, qp)
                dx_buf[:, sl] = dxp
                dcg = dcg + _dot_nt(dyeb, sb)
                dbg = dbg + _dot_nt((xp * e_l[:, sl]).astype(BF16), dsb)
                dstate[p] = ea_l[:, sl] * ds_out + _dot(ct, dyeb)
            dcg = dcg + _dot(dcb.astype(BF16), bb)
            dbg = dbg + _dot(dcb.T.astype(BF16), cbf)
            bsl = slice(512 + 128 * g, 640 + 128 * g)
            csl = slice(768 + 128 * g, 896 + 128 * g)
            dpre_ref[:, bsl] = dbg * dsilu[:, bsl]
            dpre_ref[:, csl] = dcg * dsilu[:, csl]
        dxall = dx_buf[...]
        dpre_ref[:, :SSD_WIDTH] = (dyv * dexp_ref[...] + dxall * dt_exp) * dsilu[:, :SSD_WIDTH]
        dd_ref[...] += jnp.sum(dyv * xs, axis=0, keepdims=True)
        d_acs = d_acs_diag + _hilo_dot_nt(q_buf[...], expand)
        dd_a = _hilo_dot_r(_ones_where(r_i <= c_i), d_acs)
        ddt = dd_a * a_ref[...] + _hilo_dot_nt(dxall * xs, expand)
        ddt = jnp.where(valid, ddt, 0.0)
        da_ref[...] += jnp.sum(dd_a * dtv, axis=0, keepdims=True)
        ddtr = ddt * _sigmoid(dt_in)
        ddtr_ref[...] = ddtr
        dbias_ref[...] += jnp.sum(ddtr, axis=0, keepdims=True)

    vec = pl.BlockSpec((1, 128), lambda s: (0, 0))
    wide = pl.BlockSpec((1, SSD_WIDTH), lambda s: (0, 0))
    rev = lambda s: (nc - 1 - s, 0)
    return pl.pallas_call(
        body, name=name, grid=(nc,),
        in_specs=[pl.BlockSpec((BLOCK, SSD_XBC), rev), pl.BlockSpec((BLOCK, 128), rev), vec, vec, wide,
                  pl.BlockSpec((1, 4, 128, 128), lambda s: (nc - 1 - s, 0, 0, 0)),
                  pl.BlockSpec((BLOCK, SSD_WIDTH), rev)],
        out_specs=[pl.BlockSpec((BLOCK, SSD_XBC), rev), pl.BlockSpec((BLOCK, 128), rev), vec, vec, wide],
        out_shape=[jax.ShapeDtypeStruct((rows, SSD_XBC), F32), jax.ShapeDtypeStruct((rows, 128), F32),
                   jax.ShapeDtypeStruct((1, 128), F32), jax.ShapeDtypeStruct((1, 128), F32),
                   jax.ShapeDtypeStruct((1, SSD_WIDTH), F32)],
        scratch_shapes=[pltpu.VMEM((4, 128, 128), F32), pltpu.VMEM((BLOCK, SSD_WIDTH), F32),
                        pltpu.VMEM((BLOCK, SSD_WIDTH), F32)],
        compiler_params=_cparams(1),
    )(pre, dtr, bias_row, a_row, d_exp, states, dy)


def _peer(xi, yi, ci, k):
    px = (1 - xi) if (k >> 2) & 1 else xi
    py = (1 - yi) if (k >> 1) & 1 else yi
    pc = (1 - ci) if k & 1 else ci
    return (px, py, pc), 4 * px + 2 * py + pc


def _exchange(xs, *, gather, name):
    n = len(xs)
    n_peers = N_DEV - 1
    out_shape = [jax.ShapeDtypeStruct((N_DEV,) + x.shape if gather else x.shape, x.dtype) for x in xs]

    def body(*refs):
        x_refs, o_refs = refs[:n], refs[n:2 * n]
        send_sems, recv_sems, local_sems = refs[2 * n:]
        xi, yi, ci = lax.axis_index("x"), lax.axis_index("y"), lax.axis_index("c")
        me = 4 * xi + 2 * yi + ci

        def copy(a, k, src_idx, dst_idx, peer):
            src = x_refs[a] if gather else x_refs[a].at[src_idx]
            return pltpu.make_async_remote_copy(
                src_ref=src, dst_ref=o_refs[a].at[dst_idx], send_sem=send_sems.at[a * n_peers + k - 1],
                recv_sem=recv_sems.at[a * n_peers + k - 1], device_id=peer, device_id_type=pl.DeviceIdType.MESH)

        local = [pltpu.make_async_copy(x_refs[a] if gather else x_refs[a].at[me], o_refs[a].at[me], local_sems.at[a])
                 for a in range(n)]
        for cp in local:
            cp.start()
        sends = []
        for k in range(1, N_DEV):
            peer, pidx = _peer(xi, yi, ci, k)
            for a in range(n):
                sends.append(copy(a, k, pidx, me, peer))
                sends[-1].start()
        for k in range(1, N_DEV):
            peer, pidx = _peer(xi, yi, ci, k)
            for a in range(n):
                copy(a, k, pidx, pidx, peer).wait_recv()
        for cp in sends:
            cp.wait_send()
        for cp in local:
            cp.wait()

    hbm = pl.BlockSpec(memory_space=pltpu.HBM)
    return pl.pallas_call(
        body, name=name, out_shape=out_shape, in_specs=[hbm] * n, out_specs=[hbm] * n,
        scratch_shapes=[pltpu.SemaphoreType.DMA((n * n_peers,)), pltpu.SemaphoreType.DMA((n * n_peers,)),
                        pltpu.SemaphoreType.DMA((n,))],
    )(*xs)


def _other_chips(xi, yi):
    return [(1 - xi, yi), (xi, 1 - yi), (1 - xi, 1 - yi)]


def _gather_two_level(xs, *, name):
    n = len(xs)

    def body(*refs):
        x_refs, o_refs = refs[:n], refs[n:2 * n]
        send_sems, recv_sems, local_sems = refs[2 * n:]
        xi, yi, ci = lax.axis_index("x"), lax.axis_index("y"), lax.axis_index("c")
        me, sibling = (xi, yi, ci), (xi, yi, 1 - ci)
        chips = _other_chips(xi, yi)

        def slot(px, py, pc):
            return 4 * px + 2 * py + pc

        def copy(a, k, block, to, from_input=False):
            return pltpu.make_async_remote_copy(
                src_ref=x_refs[a] if from_input else o_refs[a].at[slot(*block)], dst_ref=o_refs[a].at[slot(*block)],
                send_sem=send_sems.at[7 * a + k], recv_sem=recv_sems.at[7 * a + k],
                device_id=to, device_id_type=pl.DeviceIdType.MESH)

        local = [pltpu.make_async_copy(x_refs[a], o_refs[a].at[slot(*me)], local_sems.at[a]) for a in range(n)]
        for cp in local:
            cp.start()
        sends = []
        for a in range(n):
            sends.append(copy(a, 0, me, sibling, from_input=True))
            sends += [copy(a, 1 + j, me, (*chip, ci), from_input=True) for j, chip in enumerate(chips)]
        for cp in sends:
            cp.start()
        for j, chip in enumerate(chips):
            for a in range(n):
                copy(a, 1 + j, (*chip, ci), me).wait_recv()
                sends.append(copy(a, 4 + j, (*chip, ci), sibling))
                sends[-1].start()
        for a in range(n):
            copy(a, 0, sibling, me).wait_recv()
            for j, chip in enumerate(chips):
                copy(a, 4 + j, (*chip, 1 - ci), me).wait_recv()
        for cp in sends:
            cp.wait_send()
        for cp in local:
            cp.wait()

    hbm = pl.BlockSpec(memory_space=pltpu.HBM)
    return pl.pallas_call(
        body, name=name, out_shape=[jax.ShapeDtypeStruct((N_DEV,) + x.shape, x.dtype) for x in xs],
        in_specs=[hbm] * n, out_specs=[hbm] * n,
        scratch_shapes=[pltpu.SemaphoreType.DMA((7 * n,)), pltpu.SemaphoreType.DMA((7 * n,)),
                        pltpu.SemaphoreType.DMA((n,))],
    )(*xs)


def _sibling_swap(xs, *, name):
    n = len(xs)

    def body(*refs):
        x_refs, o_refs = refs[:n], refs[n:2 * n]
        send_sems, recv_sems = refs[2 * n:]
        xi, yi, ci = lax.axis_index("x"), lax.axis_index("y"), lax.axis_index("c")
        copies = [pltpu.make_async_remote_copy(
            src_ref=x_refs[a].at[1 - ci], dst_ref=o_refs[a], send_sem=send_sems.at[a], recv_sem=recv_sems.at[a],
            device_id=(xi, yi, 1 - ci), device_id_type=pl.DeviceIdType.MESH) for a in range(n)]
        for cp in copies:
            cp.start()
        for cp in copies:
            cp.wait()

    hbm = pl.BlockSpec(memory_space=pltpu.HBM)
    return pl.pallas_call(
        body, name=name, out_shape=[jax.ShapeDtypeStruct(x.shape[1:], x.dtype) for x in xs],
        in_specs=[hbm] * n, out_specs=[hbm] * n,
        scratch_shapes=[pltpu.SemaphoreType.DMA((n,)), pltpu.SemaphoreType.DMA((n,))],
    )(*xs)


def _chip_all_to_all(xs, *, name):
    n = len(xs)

    def body(*refs):
        x_refs, o_refs = refs[:n], refs[n:2 * n]
        send_sems, recv_sems, local_sems = refs[2 * n:]
        xi, yi, ci = lax.axis_index("x"), lax.axis_index("y"), lax.axis_index("c")
        mine = 2 * xi + yi
        chips = _other_chips(xi, yi)

        def copy(a, j, src_slot, dst_slot, chip):
            return pltpu.make_async_remote_copy(
                src_ref=x_refs[a].at[src_slot], dst_ref=o_refs[a].at[dst_slot], send_sem=send_sems.at[3 * a + j],
                recv_sem=recv_sems.at[3 * a + j], device_id=(*chip, ci), device_id_type=pl.DeviceIdType.MESH)

        local = [pltpu.make_async_copy(x_refs[a].at[mine], o_refs[a].at[mine], local_sems.at[a]) for a in range(n)]
        for cp in local:
            cp.start()
        sends = [copy(a, j, 2 * chip[0] + chip[1], mine, chip) for j, chip in enumerate(chips) for a in range(n)]
        for cp in sends:
            cp.start()
        for j, chip in enumerate(chips):
            for a in range(n):
                copy(a, j, mine, 2 * chip[0] + chip[1], chip).wait_recv()
        for cp in sends:
            cp.wait_send()
        for cp in local:
            cp.wait()

    hbm = pl.BlockSpec(memory_space=pltpu.HBM)
    return pl.pallas_call(
        body, name=name, out_shape=[jax.ShapeDtypeStruct(x.shape, x.dtype) for x in xs],
        in_specs=[hbm] * n, out_specs=[hbm] * n,
        scratch_shapes=[pltpu.SemaphoreType.DMA((3 * n,)), pltpu.SemaphoreType.DMA((3 * n,)),
                        pltpu.SemaphoreType.DMA((n,))],
    )(*xs)


def _pair_add(a, b, *, name):
    rows, cols = a.shape
    lanes = -(-cols // 128) * 128
    tr = _tile(rows, max(16, (512 * 1024) // lanes), 16)

    def body(a_ref, b_ref, o_ref):
        o_ref[...] = (a_ref[...].astype(F32) + b_ref[...].astype(F32)).astype(BF16)

    spec = pl.BlockSpec((tr, cols), lambda i: (i, 0))
    return pl.pallas_call(
        body, name=name, grid=(rows // tr,), in_specs=[spec, spec], out_specs=spec,
        out_shape=jax.ShapeDtypeStruct((rows, cols), BF16), compiler_params=_cparams(1),
    )(a, b)


def _adamw(gs, w, m, v, *, name):
    n_slots = gs.shape[0]
    rows, cols = w.shape
    lanes = -(-cols // 128) * 128
    tr = _tile(rows, max(16, (128 * 1024) // lanes), 16 if gs.dtype == BF16 else 8)

    def body(g_ref, w_ref, m_ref, v_ref, go_ref, d_ref, mo_ref, vo_ref):
        g = g_ref[0].astype(F32)
        for j in range(1, n_slots):
            g = g + g_ref[j].astype(F32)
        m2 = ADAM_B1 * m_ref[...] + (1.0 - ADAM_B1) * g
        v2 = ADAM_B2 * v_ref[...] + (1.0 - ADAM_B2) * (g * g)
        m_hat = m2 / (1.0 - ADAM_B1 ** ADAM_STEP)
        v_hat = v2 / (1.0 - ADAM_B2 ** ADAM_STEP)
        go_ref[...] = g
        d_ref[...] = -ADAM_LR * (m_hat / (jnp.sqrt(v_hat) + ADAM_EPS) + ADAM_WD * w_ref[...])
        mo_ref[...] = m2
        vo_ref[...] = v2

    spec = pl.BlockSpec((tr, cols), lambda i: (i, 0))
    sds = jax.ShapeDtypeStruct((rows, cols), F32)
    return pl.pallas_call(
        body, name=name, grid=(rows // tr,),
        in_specs=[pl.BlockSpec((n_slots, tr, cols), lambda i: (0, i, 0)), spec, spec, spec],
        out_specs=[spec, spec, spec, spec], out_shape=[sds, sds, sds, sds], compiler_params=_cparams(1),
    )(gs, w, m, v)


SHARDED = (("meta_tokens", 1), ("w_in", 2), ("ssd_conv_w", 2), ("mla_w_uq", 2), ("mla_w_ukv", 2),
           ("w_out", 1), ("ffn_w_up", 2), ("ffn_conv_w", 2), ("ffn_w_down", 1))
BIG = ("w_in", "w_out", "ffn_w_up", "ffn_w_down")
SMALL = ("meta_tokens", "ssd_conv_w", "mla_w_uq", "mla_w_ukv", "ffn_conv_w")
REPLICATED = ("norm_mix_g", "ssd_conv_b", "ssd_dt_bias", "ssd_a_log", "ssd_d", "ssd_norm_g", "sb_norm_g",
              "mla_q_norm_g", "mla_kv_norm_g", "mla_norm_g", "norm_ffn_g", "ffn_conv_b", "final_norm_g")
WEIGHTS = ("meta_tokens", "norm_mix_g", "w_in", "ssd_conv_w", "ssd_conv_b", "ssd_dt_bias", "ssd_a_log", "ssd_d",
           "ssd_norm_g", "sb_norm_g", "mla_q_norm_g", "mla_kv_norm_g", "mla_w_uq", "mla_w_ukv", "mla_norm_g",
           "w_out", "norm_ffn_g", "ffn_w_up", "ffn_conv_w", "ffn_conv_b", "ffn_w_down", "final_norm_g")


def _flat_pack(arrays, dtype, align):
    flat = jnp.concatenate([a.reshape(-1).astype(dtype) for a in arrays])
    pad = (-flat.shape[0]) % align
    return jnp.pad(flat, (0, pad)).reshape(-1, 128)


def _pieces(full, axis):
    shp = full.shape
    t = full.reshape(shp[:axis] + (N_DEV, shp[axis] // N_DEV) + shp[axis + 1:])
    return jnp.moveaxis(t, axis, 0).reshape(N_DEV, -1)


def _unpieces(p8, shard_shape, axis):
    t = p8.reshape((N_DEV,) + shard_shape)
    t = jnp.moveaxis(t, 0, axis)
    return t.reshape(shard_shape[:axis] + (N_DEV * shard_shape[axis],) + shard_shape[axis + 1:])


def _split_core_chip(full, axis):
    shp = full.shape
    t = full.reshape(shp[:axis] + (4, 2, shp[axis] // N_DEV) + shp[axis + 1:])
    return jnp.moveaxis(t, (axis + 1, axis), (0, 1))


def _merge_blocks(b8, axis):
    shard = b8.shape[1:]
    t = jnp.moveaxis(b8, 0, axis)
    return t.reshape(shard[:axis] + (N_DEV * shard[axis],) + shard[axis + 1:])


def _gather_weights(shards):
    axes = dict(SHARDED)
    got = _gather_two_level([shards[n].astype(BF16) for n in BIG], name="gather_big")
    full = {n: _merge_blocks(b8, axes[n]) for n, b8 in zip(BIG, got)}
    packed = _flat_pack([shards[n] for n in SMALL], F32, 8 * 128)
    got = _exchange([packed], gather=True, name="gather_small")[0].reshape(N_DEV, -1)
    off = 0
    for n in SMALL:
        size = math.prod(shards[n].shape)
        full[n] = _unpieces(got[:, off:off + size], shards[n].shape, axes[n])
        off += size
    return full


def _pad_cols(a, width):
    return jnp.pad(a, ((0, 0), (0, width - a.shape[1])))


def _w_in_padded(w):
    kr = w[:, 2632:2664]
    return jnp.concatenate([
        w[:, 0:512], w[:, 512:1536], w[:, 1544:2312], _pad_cols(w[:, 2312:2504], 256), w[:, 2504:2632],
        _pad_cols(kr[:, :16], 64), _pad_cols(kr[:, 16:], 64), _pad_cols(w[:, 1536:1544], 128),
        jnp.zeros((w.shape[0], 128), w.dtype)], axis=1)


def _w_in_unpadded(wp):
    return jnp.concatenate([
        wp[:, 0:512], wp[:, 512:1536], wp[:, OFF_DT:OFF_DT + 8], wp[:, 1536:2304], wp[:, OFF_QA:OFF_QA + 192],
        wp[:, OFF_CKV:OFF_CKV + 128], wp[:, OFF_KR:OFF_KR + 16], wp[:, OFF_KR + 64:OFF_KR + 80]], axis=1)


def _w_uq_perm(w):
    t = w.reshape(MLA_Q_RANK, MLA_HEADS, MLA_NOPE + MLA_ROPE)
    out = jnp.concatenate([t[:, :, :64].reshape(MLA_Q_RANK, 256), t[:, :, 64:80].reshape(MLA_Q_RANK, 64),
                           t[:, :, 80:96].reshape(MLA_Q_RANK, 64)], axis=1)
    return jnp.pad(out, ((0, 256 - MLA_Q_RANK), (0, 0)))


def _w_uq_unperm(wp):
    wp = wp[:MLA_Q_RANK]
    t = jnp.concatenate([wp[:, :256].reshape(MLA_Q_RANK, 4, 64), wp[:, 256:320].reshape(MLA_Q_RANK, 4, 16),
                         wp[:, 320:384].reshape(MLA_Q_RANK, 4, 16)], axis=2)
    return t.reshape(MLA_Q_RANK, 4 * 96)


def _w_ukv_perm(w):
    t = w.reshape(MLA_KV_RANK, MLA_HEADS, 128)
    return jnp.concatenate([t[:, :, :64].reshape(MLA_KV_RANK, 256), t[:, :, 64:].reshape(MLA_KV_RANK, 256)], axis=1)


def _w_ukv_unperm(wp):
    t = jnp.concatenate([wp[:, :256].reshape(MLA_KV_RANK, 4, 64), wp[:, 256:].reshape(MLA_KV_RANK, 4, 64)], axis=2)
    return t.reshape(MLA_KV_RANK, 512)


def _heads(a, hd):
    return jnp.moveaxis(a.reshape(a.shape[0], -1, hd), 1, 0)


def _unheads(a):
    return jnp.moveaxis(a, 0, 1).reshape(a.shape[1], -1)


def _row(v, width=None):
    v = v.reshape(1, -1)
    return v if width is None else _pad_cols(v, width)


def _rope_tables(rows):
    pos = jnp.arange(rows, dtype=F32) - float(N_PAD)
    inv = 1.0 / (ROPE_BASE ** (jnp.arange(0, MLA_ROPE, 2, dtype=F32) / MLA_ROPE))
    ang = pos[:, None] * inv[None, :]
    cos = jnp.tile(jnp.cos(ang), (1, 8))
    sin = jnp.tile(jnp.sin(ang), (1, 4))
    return cos, jnp.concatenate([-sin, sin], axis=1)


def _layer_fwd(h, p, cos_t, sin_t, tag):
    s = {"h_in": h}
    hn = _rmsnorm_fwd(h, p["norm_mix_g"], width=D_MODEL, name=tag + "norm_mix")
    u = _matmul(hn, p["w_in"], name=tag + "in_proj")
    s["hn"], s["u"] = hn, u

    xbc_in = u[:, OFF_XBC:OFF_XBC + SSD_XBC]
    pre = _dwconv_fwd(xbc_in, p["ssd_conv_w"], p["ssd_conv_b"], taps=SSD_CONV, name=tag + "ssd_conv")
    dtr = u[:, OFF_DT:OFF_DT + 128]
    y_ssd, states = _ssd_fwd(pre, dtr, p["dt_bias"], p["a_row"], p["d_exp"], name=tag + "ssd_core")
    zgate = u[:, OFF_Z:OFF_Z + SSD_WIDTH]
    yn_ssd = _rmsnorm_fwd(y_ssd, p["ssd_norm_g"], width=SSD_WIDTH, z=zgate, name=tag + "ssd_norm")
    s.update(xbc_in=xbc_in, pre=pre, dtr=dtr, y_ssd=y_ssd, states=states, zgate=zgate)

    q_sb = _heads(u[:, OFF_QSB:OFF_QSB + SB_WIDTH], SB_HEAD_DIM)
    k_sb = _heads(u[:, OFF_KSB:OFF_KSB + SB_WIDTH], SB_HEAD_DIM).astype(BF16)
    v_sb = _heads(u[:, OFF_VSB:OFF_VSB + SB_WIDTH], SB_HEAD_DIM).astype(BF16)
    o_sb, u_tot = _sb_fwd(q_sb, k_sb, v_sb, name=tag + "sb_attn")
    o_sb_flat = _unheads(o_sb)
    yn_sb = _rmsnorm_fwd(o_sb_flat, p["sb_norm_g"], width=SB_WIDTH, name=tag + "sb_norm")
    s.update(q_sb=q_sb, k_sb=k_sb, v_sb=v_sb, u_tot=u_tot, o_sb_flat=o_sb_flat)

    qa = u[:, OFF_QA:OFF_QA + 256]
    ckv = u[:, OFF_CKV:OFF_CKV + 128]
    qa_n = _rmsnorm_fwd(qa, p["mla_q_norm_g"], width=MLA_Q_RANK, name=tag + "mla_qnorm")
    ckv_n = _rmsnorm_fwd(ckv, p["mla_kv_norm_g"], width=MLA_KV_RANK, name=tag + "mla_kvnorm")
    qf = _matmul(qa_n, p["mla_w_uq"], name=tag + "mla_uq")
    kvf = _matmul(ckv_n, p["mla_w_ukv"], name=tag + "mla_ukv")
    q_rope = _rope(qf[:, 256:384], cos_t, sin_t, name=tag + "rope_q")
    k_rope = _rope(u[:, OFF_KR:OFF_KR + 128], cos_t, sin_t, name=tag + "rope_k")
    rows = h.shape[0]
    zpad = jnp.zeros((MLA_HEADS, rows, 32), F32)
    qh = jnp.concatenate([_heads(qf[:, :256], 64), _heads(q_rope[:, :64], 16), _heads(q_rope[:, 64:], 16), zpad], axis=2)
    kr_b = jnp.broadcast_to(jnp.concatenate([k_rope[:, 0:16], k_rope[:, 64:80]], axis=1)[None], (MLA_HEADS, rows, 32))
    kh = jnp.concatenate([_heads(kvf[:, :256], 64), kr_b, zpad], axis=2).astype(BF16)
    vh = _heads(kvf[:, 256:], 64).astype(BF16)
    v_ones = jnp.concatenate([vh, jnp.ones_like(vh)], axis=2)
    o_mla, lse = _mla_fwd(qh, kh, v_ones, name=tag + "mla_attn")
    o_mla_flat = _unheads(o_mla)
    yn_mla = _rmsnorm_fwd(o_mla_flat, p["mla_norm_g"], width=256, name=tag + "mla_norm")
    s.update(qa=qa, ckv=ckv, qa_n=qa_n, ckv_n=ckv_n, qh=qh, kh=kh, vh=vh, o_mla=o_mla, lse=lse,
             o_mla_flat=o_mla_flat)

    mix = jnp.concatenate([yn_ssd, yn_sb, yn_mla], axis=1)
    h_mid = _matmul(mix, p["w_out"], res=h, mask_pad=True, name=tag + "out_proj")
    hn2 = _rmsnorm_fwd(h_mid, p["norm_ffn_g"], width=D_MODEL, name=tag + "norm_ffn")
    up = _matmul(hn2, p["ffn_w_up"], tn=1408, name=tag + "ffn_up")
    act = _ffn_conv_gate_fwd(up, p["ffn_conv_w"], p["ffn_conv_b"], taps=FFN_CONV, name=tag + "ffn_conv_gate")
    h_out = _matmul(act, p["ffn_w_down"], res=h_mid, mask_pad=True, tk=1408, name=tag + "ffn_down")
    s.update(mix=mix, h_mid=h_mid, hn2=hn2, up=up, act=act)
    return h_out, s


def _layer_bwd(dh_out, p, s, cos_t, sin_t, tag):
    g = {}
    rows = dh_out.shape[0]
    dact = _matmul(dh_out, p["ffn_w_down"], tb=True, tn=1408, name=tag + "b_down_dx")
    g["ffn_w_down"] = _matmul(s["act"], dh_out, ta=True, tm=1408, tk=640, name=tag + "b_down_dw")
    dup1, dup2, dcw1, dcw2, dcb1, dcb2 = _ffn_conv_gate_bwd(
        dact, s["up"], p["ffn_conv_w"], p["ffn_conv_b"], taps=FFN_CONV, name=tag + "b_ffn_conv_gate")
    g["ffn_conv_w"] = jnp.concatenate([dcw1[:FFN_CONV], dcw2[:FFN_CONV]], axis=1)
    g["ffn_conv_b"] = jnp.concatenate([dcb1[0], dcb2[0]])
    w_up1, w_up2 = p["ffn_w_up"][:, :D_FF], p["ffn_w_up"][:, D_FF:]
    dhn2 = _matmul(dup1, w_up1, tb=True, tk=1408, name=tag + "b_up_dx1")
    dhn2 = _matmul(dup2, w_up2, tb=True, tk=1408, res=dhn2, name=tag + "b_up_dx2")
    g["ffn_w_up"] = jnp.concatenate([_matmul(s["hn2"], dup1, ta=True, tn=1408, tk=640, name=tag + "b_up_dw1"),
                                     _matmul(s["hn2"], dup2, ta=True, tn=1408, tk=640, name=tag + "b_up_dw2")], axis=1)
    dh_mid, _, dg = _rmsnorm_bwd(s["h_mid"], p["norm_ffn_g"], dhn2, width=D_MODEL, res=dh_out, mask_pad=True,
                                 name=tag + "b_norm_ffn")
    g["norm_ffn_g"] = dg[0]

    dmix = _matmul(dh_mid, p["w_out"], tb=True, name=tag + "b_out_dx")
    g["w_out"] = _matmul(s["mix"], dh_mid, ta=True, tk=640, name=tag + "b_out_dw")

    dy_ssd, dz, dg = _rmsnorm_bwd(s["y_ssd"], p["ssd_norm_g"], dmix[:, :SSD_WIDTH], width=SSD_WIDTH, z=s["zgate"],
                                  name=tag + "b_ssd_norm")
    g["ssd_norm_g"] = dg[0]
    dpre, ddtr, dbias, da, dd = _ssd_bwd(s["pre"], s["dtr"], p["dt_bias"], p["a_row"], p["d_exp"], s["states"],
                                         dy_ssd, name=tag + "b_ssd_core")
    g["ssd_dt_bias"] = dbias[0, :8]
    g["ssd_a_log"] = da[0, :8] * p["a_row"][0, :8]
    g["ssd_d"] = dd.reshape(8, 64).sum(axis=1)
    dxbc_in, dcw, dcb_ = _dwconv_bwd(dpre, s["xbc_in"], p["ssd_conv_w"], taps=SSD_CONV, name=tag + "b_ssd_conv")
    g["ssd_conv_w"], g["ssd_conv_b"] = dcw[:SSD_CONV], dcb_[0]

    do_sb_flat, _, dg = _rmsnorm_bwd(s["o_sb_flat"], p["sb_norm_g"], dmix[:, 512:768], width=SB_WIDTH,
                                     name=tag + "b_sb_norm")
    g["sb_norm_g"] = dg[0]
    dq_sb, dk_sb, dv_sb = _sb_bwd(s["q_sb"], s["k_sb"], s["v_sb"], _heads(do_sb_flat, SB_HEAD_DIM), s["u_tot"],
                                  name=tag + "b_sb_attn")

    do_mla_flat, _, dg = _rmsnorm_bwd(s["o_mla_flat"], p["mla_norm_g"], dmix[:, 768:1024], width=256,
                                      name=tag + "b_mla_norm")
    g["mla_norm_g"] = dg[0]
    dqh, dkh, dvh = _mla_bwd(s["qh"], s["kh"], s["vh"], s["o_mla"], s["lse"], _heads(do_mla_flat, 64),
                             name=tag + "b_mla_attn")
    dq_rope_in = jnp.concatenate([_unheads(dqh[:, :, 64:80]), _unheads(dqh[:, :, 80:96])], axis=1)
    dq_r = _rope(dq_rope_in, cos_t, sin_t, transpose=True, name=tag + "b_rope_q")
    dqf = jnp.concatenate([_unheads(dqh[:, :, :64]), dq_r], axis=1)
    dkr_sum = jnp.sum(dkh[:, :, 64:96], axis=0)
    dk_rope_in = jnp.concatenate([_pad_cols(dkr_sum[:, :16], 64), _pad_cols(dkr_sum[:, 16:], 64)], axis=1)
    dkr = _rope(dk_rope_in, cos_t, sin_t, transpose=True, name=tag + "b_rope_k")
    dkvf = jnp.concatenate([_unheads(dkh[:, :, :64]), _unheads(dvh)], axis=1)
    dqa_n = _matmul(dqf, p["mla_w_uq"], tb=True, name=tag + "b_uq_dx")
    g["mla_w_uq"] = _matmul(s["qa_n"], dqf, ta=True, tk=640, name=tag + "b_uq_dw")
    dckv_n = _matmul(dkvf, p["mla_w_ukv"], tb=True, name=tag + "b_ukv_dx")
    g["mla_w_ukv"] = _matmul(s["ckv_n"], dkvf, ta=True, tk=640, name=tag + "b_ukv_dw")
    dqa, _, dg = _rmsnorm_bwd(s["qa"], p["mla_q_norm_g"], dqa_n, width=MLA_Q_RANK, name=tag + "b_mla_qnorm")
    g["mla_q_norm_g"] = dg[0, :MLA_Q_RANK]
    dckv, _, dg = _rmsnorm_bwd(s["ckv"], p["mla_kv_norm_g"], dckv_n, width=MLA_KV_RANK, name=tag + "b_mla_kvnorm")
    g["mla_kv_norm_g"] = dg[0]

    du = jnp.concatenate([dz, dxbc_in, _unheads(dq_sb), _unheads(dk_sb), _unheads(dv_sb), dqa, dckv, dkr, ddtr,
                          jnp.zeros((rows, 128), F32)], axis=1)
    dhn = _matmul(du, p["w_in"], tb=True, name=tag + "b_in_dx")
    g["w_in"] = _matmul(s["hn"], du, ta=True, tk=640, name=tag + "b_in_dw")
    dh_in, _, dg = _rmsnorm_bwd(s["h_in"], p["norm_mix_g"], dhn, width=D_MODEL, res=dh_mid, mask_pad=True,
                                name=tag + "b_norm_mix")
    g["norm_mix_g"] = dg[0]
    return dh_in, g


def _prepare_layer(full, rep, l):
    a_row = _row(-jnp.exp(rep["ssd_a_log"][l]), 128)
    return {
        "norm_mix_g": _row(rep["norm_mix_g"][l]),
        "w_in": _w_in_padded(full["w_in"][l]),
        "ssd_conv_w": jnp.pad(full["ssd_conv_w"][l], ((0, HALO - SSD_CONV), (0, 0))),
        "ssd_conv_b": _row(rep["ssd_conv_b"][l]),
        "dt_bias": _row(rep["ssd_dt_bias"][l], 128),
        "a_row": a_row,
        "d_exp": _row(jnp.repeat(rep["ssd_d"][l], 64)),
        "ssd_norm_g": _row(rep["ssd_norm_g"][l]),
        "sb_norm_g": _row(rep["sb_norm_g"][l]),
        "mla_q_norm_g": _row(rep["mla_q_norm_g"][l], 256),
        "mla_kv_norm_g": _row(rep["mla_kv_norm_g"][l]),
        "mla_w_uq": _w_uq_perm(full["mla_w_uq"][l]),
        "mla_w_ukv": _w_ukv_perm(full["mla_w_ukv"][l]),
        "mla_norm_g": _row(rep["mla_norm_g"][l]),
        "w_out": full["w_out"][l],
        "norm_ffn_g": _row(rep["norm_ffn_g"][l]),
        "ffn_w_up": full["ffn_w_up"][l],
        "ffn_conv_w": jnp.pad(full["ffn_conv_w"][l], ((0, HALO - FFN_CONV), (0, 0))),
        "ffn_conv_b": _row(rep["ffn_conv_b"][l]),
        "ffn_w_down": full["ffn_w_down"][l],
    }


def _layer_grads_to_full(g):
    out = dict(g)
    out["w_in"] = _w_in_unpadded(g["w_in"])
    out["mla_w_uq"] = _w_uq_unperm(g["mla_w_uq"])
    out["mla_w_ukv"] = _w_ukv_unperm(g["mla_w_ukv"])
    return out


def kernel(x, meta_tokens, norm_mix_g, w_in, ssd_conv_w, ssd_conv_b, ssd_dt_bias, ssd_a_log, ssd_d, ssd_norm_g, sb_norm_g, mla_q_norm_g, mla_kv_norm_g, mla_w_uq, mla_w_ukv, mla_norm_g, w_out, norm_ffn_g, ffn_w_up, ffn_conv_w, ffn_conv_b, ffn_w_down, final_norm_g, loss_target, m_meta_tokens, m_norm_mix_g, m_w_in, m_ssd_conv_w, m_ssd_conv_b, m_ssd_dt_bias, m_ssd_a_log, m_ssd_d, m_ssd_norm_g, m_sb_norm_g, m_mla_q_norm_g, m_mla_kv_norm_g, m_mla_w_uq, m_mla_w_ukv, m_mla_norm_g, m_w_out, m_norm_ffn_g, m_ffn_w_up, m_ffn_conv_w, m_ffn_conv_b, m_ffn_w_down, m_final_norm_g, v_meta_tokens, v_norm_mix_g, v_w_in, v_ssd_conv_w, v_ssd_conv_b, v_ssd_dt_bias, v_ssd_a_log, v_ssd_d, v_ssd_norm_g, v_sb_norm_g, v_mla_q_norm_g, v_mla_kv_norm_g, v_mla_w_uq, v_mla_w_ukv, v_mla_norm_g, v_w_out, v_norm_ffn_g, v_ffn_w_up, v_ffn_conv_w, v_ffn_conv_b, v_ffn_w_down, v_final_norm_g):
    w = dict(meta_tokens=meta_tokens, norm_mix_g=norm_mix_g, w_in=w_in, ssd_conv_w=ssd_conv_w, ssd_conv_b=ssd_conv_b,
             ssd_dt_bias=ssd_dt_bias, ssd_a_log=ssd_a_log, ssd_d=ssd_d, ssd_norm_g=ssd_norm_g, sb_norm_g=sb_norm_g,
             mla_q_norm_g=mla_q_norm_g, mla_kv_norm_g=mla_kv_norm_g, mla_w_uq=mla_w_uq, mla_w_ukv=mla_w_ukv,
             mla_norm_g=mla_norm_g, w_out=w_out, norm_ffn_g=norm_ffn_g, ffn_w_up=ffn_w_up, ffn_conv_w=ffn_conv_w,
             ffn_conv_b=ffn_conv_b, ffn_w_down=ffn_w_down, final_norm_g=final_norm_g)
    mom = dict(meta_tokens=m_meta_tokens, norm_mix_g=m_norm_mix_g, w_in=m_w_in, ssd_conv_w=m_ssd_conv_w,
               ssd_conv_b=m_ssd_conv_b, ssd_dt_bias=m_ssd_dt_bias, ssd_a_log=m_ssd_a_log, ssd_d=m_ssd_d,
               ssd_norm_g=m_ssd_norm_g, sb_norm_g=m_sb_norm_g, mla_q_norm_g=m_mla_q_norm_g,
               mla_kv_norm_g=m_mla_kv_norm_g, mla_w_uq=m_mla_w_uq, mla_w_ukv=m_mla_w_ukv, mla_norm_g=m_mla_norm_g,
               w_out=m_w_out, norm_ffn_g=m_norm_ffn_g, ffn_w_up=m_ffn_w_up, ffn_conv_w=m_ffn_conv_w,
               ffn_conv_b=m_ffn_conv_b, ffn_w_down=m_ffn_w_down, final_norm_g=m_final_norm_g)
    vel = dict(meta_tokens=v_meta_tokens, norm_mix_g=v_norm_mix_g, w_in=v_w_in, ssd_conv_w=v_ssd_conv_w,
               ssd_conv_b=v_ssd_conv_b, ssd_dt_bias=v_ssd_dt_bias, ssd_a_log=v_ssd_a_log, ssd_d=v_ssd_d,
               ssd_norm_g=v_ssd_norm_g, sb_norm_g=v_sb_norm_g, mla_q_norm_g=v_mla_q_norm_g,
               mla_kv_norm_g=v_mla_kv_norm_g, mla_w_uq=v_mla_w_uq, mla_w_ukv=v_mla_w_ukv, mla_norm_g=v_mla_norm_g,
               w_out=v_w_out, norm_ffn_g=v_norm_ffn_g, ffn_w_up=v_ffn_w_up, ffn_conv_w=v_ffn_conv_w,
               ffn_conv_b=v_ffn_conv_b, ffn_w_down=v_ffn_w_down, final_norm_g=v_final_norm_g)

    full = _gather_weights({n: w[n] for n, _ in SHARDED})
    layers = [_prepare_layer(full, w, l) for l in range(DEPTH)]

    seq = x.shape[1]
    rows = BLOCK + seq
    cos_t, sin_t = _rope_tables(rows)
    h = jnp.concatenate([jnp.zeros((N_PAD, D_MODEL), F32), full["meta_tokens"], x[0]], axis=0)

    saved = []
    for l in range(DEPTH):
        h, s = _layer_fwd(h, layers[l], cos_t, sin_t, "l%d_" % l)
        saved.append(s)
    dh, dg_final, loss_part = _final_loss(h, _row(final_norm_g), loss_target[0], name="final_loss")
    loss = lax.psum(loss_part[0, 0], ("x", "y", "c"))

    layer_grads = [None] * DEPTH
    for l in reversed(range(DEPTH)):
        dh, g = _layer_bwd(dh, layers[l], saved[l], cos_t, sin_t, "l%d_" % l)
        layer_grads[l] = _layer_grads_to_full(g)
    grad_x = dh[BLOCK:][None]

    partial = {n: jnp.stack([layer_grads[l][n] for l in range(DEPTH)]) for n in layer_grads[0]}
    partial["meta_tokens"] = dh[N_PAD:BLOCK]
    partial["final_norm_g"] = dg_final[0]

    results = [dict(), dict(), dict(), dict()]
    axes = dict(SHARDED)

    core = lax.axis_index("c")
    halves = [_split_core_chip(partial[n], axes[n]).astype(BF16) for n in BIG]
    theirs = _sibling_swap(halves, name="grad_sibling_swap")
    chip_sums = []
    for n, h2, t4 in zip(BIG, halves, theirs):
        view = (4 * math.prod(w[n].shape[:-1]), w[n].shape[-1])
        mine = lax.dynamic_index_in_dim(h2, core, 0, keepdims=False)
        chip_sums.append(_pair_add(mine.reshape(view), t4.reshape(view), name="grad_pair_add_" + n).reshape(t4.shape))
    got_big = _chip_all_to_all(chip_sums, name="grad_chip_all_to_all")
    for n, g4 in zip(BIG, got_big):
        shp = w[n].shape
        view = (math.prod(shp[:-1]), shp[-1])
        outs = _adamw(g4.reshape((4,) + view), w[n].reshape(view), mom[n].reshape(view), vel[n].reshape(view),
                      name="adamw_" + n)
        for kind in range(4):
            results[kind][n] = outs[kind].reshape(shp)

    send = jnp.concatenate([_pieces(partial[n], axes[n]) for n in SMALL], axis=1)
    pad = (-send.shape[1]) % (8 * 128)
    send = jnp.pad(send, ((0, 0), (0, pad))).reshape(N_DEV, -1, 128)
    got = _exchange([send], gather=False, name="grad_all_to_all_small")[0]
    pack = lambda d: _flat_pack([d[n] for n in SMALL], F32, 8 * 128)
    sh_out = _adamw(got, pack(w), pack(mom), pack(vel), name="adamw_small")

    rep_g = _flat_pack([partial[n] for n in REPLICATED], F32, 8 * 128)
    got_r = _exchange([rep_g], gather=True, name="grad_all_gather")[0]
    packr = lambda d: _flat_pack([d[n] for n in REPLICATED], F32, 8 * 128)
    rep_out = _adamw(got_r, packr(w), packr(mom), packr(vel), name="adamw_replicated")

    for names, outs in ((list(SMALL), sh_out), (list(REPLICATED), rep_out)):
        off = 0
        for n in names:
            size = math.prod(w[n].shape)
            for kind in range(4):
                results[kind][n] = outs[kind].reshape(-1)[off:off + size].reshape(w[n].shape)
            off += size

    return (loss, grad_x, *[results[0][n] for n in WEIGHTS], *[results[1][n] for n in WEIGHTS],
            *[results[2][n] for n in WEIGHTS], *[results[3][n] for n in WEIGHTS])
```

```python
import math

import jax
import jax.numpy as jnp
from jax import lax
from jax.experimental import pallas as pl
from jax.experimental.pallas import tpu as pltpu

F32 = jnp.float32
BF16 = jnp.bfloat16

D_MODEL = 1024
DEPTH = 2
N_META = 16
BLOCK = 128
N_PAD = BLOCK - N_META
EPS = 1e-6
SSD_WIDTH = 512
SSD_XBC = 1024
SSD_CONV = 4
SB_WIDTH = 256
SB_HEAD_DIM = 64
MLA_HEADS = 4
MLA_NOPE = 64
MLA_ROPE = 32
MLA_Q_RANK = 192
MLA_KV_RANK = 128
ROPE_BASE = 10000.0
D_FF = 2816
FFN_CONV = 3
IN_COLS = 2664
N_DEV = 8

ADAM_LR = 0.001
ADAM_B1 = 0.9
ADAM_B2 = 0.999
ADAM_EPS = 1e-08
ADAM_WD = 0.01
ADAM_STEP = 10

U_COLS = 3072
OFF_Z, OFF_XBC, OFF_QSB, OFF_KSB, OFF_VSB, OFF_QA, OFF_CKV, OFF_KR, OFF_DT = (
    0, 512, 1536, 1792, 2048, 2304, 2560, 2688, 2816)

V7X_VMEM_BYTES = 64 * 1024 * 1024
VMEM_LIMIT = (V7X_VMEM_BYTES * 7) // 8
NEG_BIG = -1e30


def _cparams(n_axes):
    return pltpu.CompilerParams(dimension_semantics=("arbitrary",) * n_axes, vmem_limit_bytes=VMEM_LIMIT)


def _tile(n, target, align):
    best = None
    for d in range(align, min(n, target) + 1, align):
        if n % d == 0:
            best = d
    return n if best is None else best


def _sigmoid(x):
    return 1.0 / (1.0 + jnp.exp(-x))


def _softplus(x):
    return jnp.maximum(x, 0.0) + jnp.log(1.0 + jnp.exp(-jnp.abs(x)))


def _dot(a, b):
    return jnp.dot(a, b, preferred_element_type=F32)


def _dot_nt(a, b):
    return lax.dot_general(a, b, (((1,), (1,)), ((), ())), preferred_element_type=F32)


def _hilo(x):
    hi = x.astype(BF16)
    lo = (x - hi.astype(F32)).astype(BF16)
    return hi, lo


def _hilo_dot_l(x, m):
    hi, lo = _hilo(x)
    return _dot(hi, m) + _dot(lo, m)


def _hilo_dot_r(m, x):
    hi, lo = _hilo(x)
    return _dot(m, hi) + _dot(m, lo)


def _hilo_dot_nt(x, m):
    hi, lo = _hilo(x)
    return _dot_nt(hi, m) + _dot_nt(lo, m)


def _ones_where(cond):
    return jnp.where(cond, 1.0, 0.0).astype(BF16)


def _matmul(a, b, *, name, ta=False, tb=False, out_dtype=F32, res=None, mask_pad=False,
            tm=640, tn=1024, tk=1024):
    m_dim = a.shape[1] if ta else a.shape[0]
    k_dim = a.shape[0] if ta else a.shape[1]
    n_dim = b.shape[0] if tb else b.shape[1]
    assert (b.shape[1] if tb else b.shape[0]) == k_dim
    tm = _tile(m_dim, tm, 128)
    tn = _tile(n_dim, tn, 128)
    tk = _tile(k_dim, tk, 128)
    nk = k_dim // tk
    dn = (((0 if ta else 1,), (1 if tb else 0,)), ((), ()))

    use_scratch = nk > 1 and out_dtype != F32

    def body(*refs):
        refs = list(refs)
        a_ref, b_ref = refs[0], refs[1]
        r_ref = refs[2] if res is not None else None
        o_ref = refs[3] if res is not None else refs[2]
        acc = refs[-1] if use_scratch else o_ref
        k = pl.program_id(2)
        part = lax.dot_general(a_ref[...].astype(BF16), b_ref[...].astype(BF16), dn, preferred_element_type=F32)

        def finish(r):
            if res is not None:
                r = r + r_ref[...].astype(F32)
            if mask_pad:
                rows = pl.program_id(0) * tm + lax.broadcasted_iota(jnp.int32, (tm, 1), 0)
                r = jnp.where(rows >= N_PAD, r, 0.0)
            o_ref[...] = r.astype(out_dtype)

        if nk == 1:
            finish(part)
        else:
            @pl.when(k == 0)
            def _():
                acc[...] = part

            @pl.when(jnp.logical_and(k > 0, k < nk - 1))
            def _():
                acc[...] += part

            @pl.when(k == nk - 1)
            def _():
                finish(acc[...] + part)

    a_spec = pl.BlockSpec((tk, tm), lambda i, j, k: (k, i)) if ta else pl.BlockSpec((tm, tk), lambda i, j, k: (i, k))
    b_spec = pl.BlockSpec((tn, tk), lambda i, j, k: (j, k)) if tb else pl.BlockSpec((tk, tn), lambda i, j, k: (k, j))
    o_spec = pl.BlockSpec((tm, tn), lambda i, j, k: (i, j))
    in_specs = [a_spec, b_spec]
    args = [a, b]
    if res is not None:
        in_specs.append(o_spec)
        args.append(res)
    return pl.pallas_call(
        body, name=name, grid=(m_dim // tm, n_dim // tn, nk),
        in_specs=in_specs, out_specs=o_spec,
        out_shape=jax.ShapeDtypeStruct((m_dim, n_dim), out_dtype),
        scratch_shapes=[pltpu.VMEM((tm, tn), F32)] if use_scratch else [],
        compiler_params=_cparams(3),
    )(*args)


def _rmsnorm_fwd(x, g, *, width, name, z=None, out_dtype=None):
    out_dtype = BF16 if out_dtype is None else out_dtype
    rows, w = x.shape
    tr = _tile(rows, 640, 128)
    inv_w = 1.0 / width

    def body(*refs):
        if z is not None:
            x_ref, z_ref, g_ref, o_ref = refs
        else:
            x_ref, g_ref, o_ref = refs
        t = x_ref[...].astype(F32)
        if z is not None:
            zz = z_ref[...]
            t = t * (zz * _sigmoid(zz))
        ms = jnp.sum(t * t, axis=-1, keepdims=True) * inv_w
        o_ref[...] = ((t * lax.rsqrt(ms + EPS)) * g_ref[...]).astype(out_dtype)

    row_spec = pl.BlockSpec((tr, w), lambda i: (i, 0))
    g_spec = pl.BlockSpec((1, w), lambda i: (0, 0))
    in_specs = [row_spec] + ([row_spec] if z is not None else []) + [g_spec]
    args = [x] + ([z] if z is not None else []) + [g]
    return pl.pallas_call(
        body, name=name, grid=(rows // tr,), in_specs=in_specs, out_specs=row_spec,
        out_shape=jax.ShapeDtypeStruct((rows, w), out_dtype), compiler_params=_cparams(1),
    )(*args)


def _rmsnorm_bwd(x, g, dout, *, width, name, z=None, res=None, mask_pad=False):
    rows, w = x.shape
    tr = _tile(rows, 640, 128)
    inv_w = 1.0 / width

    def body(*refs):
        refs = list(refs)
        x_ref = refs.pop(0)
        z_ref = refs.pop(0) if z is not None else None
        g_ref = refs.pop(0)
        do_ref = refs.pop(0)
        r_ref = refs.pop(0) if res is not None else None
        dx_ref = refs.pop(0)
        dz_ref = refs.pop(0) if z is not None else None
        dg_ref = refs.pop(0)
        i = pl.program_id(0)

        @pl.when(i == 0)
        def _():
            dg_ref[...] = jnp.zeros_like(dg_ref)

        xv = x_ref[...].astype(F32)
        t = xv
        if z is not None:
            zz = z_ref[...]
            sig = _sigmoid(zz)
            sl = zz * sig
            t = xv * sl
        ms = jnp.sum(t * t, axis=-1, keepdims=True) * inv_w
        rstd = lax.rsqrt(ms + EPS)
        xhat = t * rstd
        do = do_ref[...].astype(F32)
        dxh = do * g_ref[...]
        c = jnp.sum(dxh * xhat, axis=-1, keepdims=True) * inv_w
        dt = rstd * (dxh - xhat * c)
        dg_ref[...] += jnp.sum(do * xhat, axis=0, keepdims=True)
        if z is not None:
            dz_ref[...] = dt * xv * (sig * (1.0 + zz * (1.0 - sig)))
            dx = dt * sl
        else:
            dx = dt
        if res is not None:
            dx = dx + r_ref[...]
        if mask_pad:
            rws = i * tr + lax.broadcasted_iota(jnp.int32, (tr, 1), 0)
            dx = jnp.where(rws >= N_PAD, dx, 0.0)
        dx_ref[...] = dx

    row_spec = pl.BlockSpec((tr, w), lambda i: (i, 0))
    g_spec = pl.BlockSpec((1, w), lambda i: (0, 0))
    in_specs = [row_spec] + ([row_spec] if z is not None else []) + [g_spec, row_spec] + (
        [row_spec] if res is not None else [])
    args = [x] + ([z] if z is not None else []) + [g, dout] + ([res] if res is not None else [])
    out_specs = [row_spec] + ([row_spec] if z is not None else []) + [g_spec]
    out_shape = [jax.ShapeDtypeStruct((rows, w), F32)] + (
        [jax.ShapeDtypeStruct((rows, w), F32)] if z is not None else []) + [jax.ShapeDtypeStruct((1, w), F32)]
    outs = pl.pallas_call(
        body, name=name, grid=(rows // tr,), in_specs=in_specs, out_specs=out_specs,
        out_shape=out_shape, compiler_params=_cparams(1),
    )(*args)
    if z is not None:
        return outs[0], outs[1], outs[2]
    return outs[0], None, outs[1]


def _final_loss(h, g, target, *, name):
    rows, w = h.shape
    nb = rows // BLOCK
    inv_w = 1.0 / w

    def body(h_ref, g_ref, t_ref, dh_ref, dg_ref, loss_ref):
        i = pl.program_id(0)

        @pl.when(i == 0)
        def _():
            dg_ref[...] = jnp.zeros_like(dg_ref)
            loss_ref[...] = jnp.zeros_like(loss_ref)

        xv = h_ref[...]
        ms = jnp.sum(xv * xv, axis=-1, keepdims=True) * inv_w
        rstd = lax.rsqrt(ms + EPS)
        xhat = xv * rstd
        gv = g_ref[...]
        err = jnp.where(i >= 1, xhat * gv - t_ref[...], 0.0)
        loss_ref[...] += (0.5 * inv_w) * jnp.sum(err * err)
        do = err * inv_w
        dxh = do * gv
        c = jnp.sum(dxh * xhat, axis=-1, keepdims=True) * inv_w
        dh_ref[...] = rstd * (dxh - xhat * c)
        dg_ref[...] += jnp.sum(do * xhat, axis=0, keepdims=True)

    row_spec = pl.BlockSpec((BLOCK, w), lambda i: (i, 0))
    g_spec = pl.BlockSpec((1, w), lambda i: (0, 0))
    return pl.pallas_call(
        body, name=name, grid=(nb,),
        in_specs=[row_spec, g_spec, pl.BlockSpec((BLOCK, w), lambda i: (jnp.maximum(i - 1, 0), 0))],
        out_specs=[row_spec, g_spec, pl.BlockSpec((1, 128), lambda i: (0, 0))],
        out_shape=[jax.ShapeDtypeStruct((rows, w), F32), jax.ShapeDtypeStruct((1, w), F32),
                   jax.ShapeDtypeStruct((1, 128), F32)],
        compiler_params=_cparams(1),
    )(h, g, target)


HALO = 8


def _dwconv_fwd(u, w8, b, *, taps, name):
    rows, ch = u.shape
    tb = _tile(rows, 640, 128)
    tc = _tile(ch, 512, 128)
    hb = tb // HALO

    def body(u_ref, h_ref, w_ref, b_ref, o_ref, buf):
        i = pl.program_id(0)
        buf[0:HALO, :] = jnp.where(i > 0, h_ref[...], 0.0)
        buf[HALO:HALO + tb, :] = u_ref[...]
        acc = jnp.broadcast_to(b_ref[...], (tb, tc))
        for k in range(taps):
            acc = acc + w_ref[k:k + 1, :] * buf[pl.ds(HALO - (taps - 1) + k, tb), :]
        o_ref[...] = acc

    return pl.pallas_call(
        body, name=name, grid=(rows // tb, ch // tc),
        in_specs=[pl.BlockSpec((tb, tc), lambda i, j: (i, j)),
                  pl.BlockSpec((HALO, tc), lambda i, j: (jnp.maximum(i * hb - 1, 0), j)),
                  pl.BlockSpec((HALO, tc), lambda i, j: (0, j)),
                  pl.BlockSpec((1, tc), lambda i, j: (0, j))],
        out_specs=pl.BlockSpec((tb, tc), lambda i, j: (i, j)),
        out_shape=jax.ShapeDtypeStruct((rows, ch), F32),
        scratch_shapes=[pltpu.VMEM((tb + HALO, tc), F32)],
        compiler_params=_cparams(2),
    )(u, u, w8, b)


def _dwconv_bwd(dpre, u, w8, *, taps, name):
    rows, ch = u.shape
    tb = _tile(rows, 640, 128)
    tc = _tile(ch, 512, 128)
    hb = tb // HALO
    nb = rows // tb
    last_halo = rows // HALO - 1

    def body(d_ref, dn_ref, u_ref, up_ref, w_ref, du_ref, dw_ref, db_ref, bufd, bufu):
        i = pl.program_id(1)

        @pl.when(i == 0)
        def _():
            dw_ref[...] = jnp.zeros_like(dw_ref)
            db_ref[...] = jnp.zeros_like(db_ref)

        d = d_ref[...]
        bufd[0:tb, :] = d
        bufd[tb:tb + HALO, :] = jnp.where(i < nb - 1, dn_ref[...], 0.0)
        bufu[0:HALO, :] = jnp.where(i > 0, up_ref[...], 0.0)
        bufu[HALO:HALO + tb, :] = u_ref[...]
        acc = jnp.zeros((tb, tc), F32)
        for k in range(taps):
            acc = acc + w_ref[k:k + 1, :] * bufd[pl.ds(taps - 1 - k, tb), :]
        du_ref[...] = acc
        for k in range(taps):
            dw_ref[k:k + 1, :] += jnp.sum(d * bufu[pl.ds(HALO - (taps - 1) + k, tb), :], axis=0, keepdims=True)
        db_ref[...] += jnp.sum(d, axis=0, keepdims=True)

    return pl.pallas_call(
        body, name=name, grid=(ch // tc, nb),
        in_specs=[pl.BlockSpec((tb, tc), lambda j, i: (i, j)),
                  pl.BlockSpec((HALO, tc), lambda j, i: (jnp.minimum((i + 1) * hb, last_halo), j)),
                  pl.BlockSpec((tb, tc), lambda j, i: (i, j)),
                  pl.BlockSpec((HALO, tc), lambda j, i: (jnp.maximum(i * hb - 1, 0), j)),
                  pl.BlockSpec((HALO, tc), lambda j, i: (0, j))],
        out_specs=[pl.BlockSpec((tb, tc), lambda j, i: (i, j)),
                   pl.BlockSpec((HALO, tc), lambda j, i: (0, j)),
                   pl.BlockSpec((1, tc), lambda j, i: (0, j))],
        out_shape=[jax.ShapeDtypeStruct((rows, ch), F32), jax.ShapeDtypeStruct((HALO, ch), F32),
                   jax.ShapeDtypeStruct((1, ch), F32)],
        scratch_shapes=[pltpu.VMEM((tb + HALO, tc), F32), pltpu.VMEM((tb + HALO, tc), F32)],
        compiler_params=_cparams(2),
    )(dpre, dpre, u, u, w8)


def _ffn_conv_gate_fwd(up, w8, b, *, taps, name):
    rows, c2 = up.shape
    f = c2 // 2
    tb = _tile(rows, 640, 128)
    tc = _tile(f, 512, 128)
    nct = f // tc
    hb = tb // HALO

    def body(u1_ref, u2_ref, h1_ref, h2_ref, w1_ref, w2_ref, b1_ref, b2_ref, o_ref, buf1, buf2):
        i = pl.program_id(0)
        pre = []
        for u_ref, h_ref, w_ref, b_ref, buf in ((u1_ref, h1_ref, w1_ref, b1_ref, buf1),
                                                 (u2_ref, h2_ref, w2_ref, b2_ref, buf2)):
            buf[0:HALO, :] = jnp.where(i > 0, h_ref[...], 0.0)
            buf[HALO:HALO + tb, :] = u_ref[...]
            acc = jnp.broadcast_to(b_ref[...], (tb, tc))
            for k in range(taps):
                acc = acc + w_ref[k:k + 1, :] * buf[pl.ds(HALO - (taps - 1) + k, tb), :]
            pre.append(acc)
        o_ref[...] = (pre[0] * _sigmoid(pre[0]) * pre[1]).astype(BF16)

    main = lambda off: pl.BlockSpec((tb, tc), lambda i, j: (i, j + off))
    halo = lambda off: pl.BlockSpec((HALO, tc), lambda i, j: (jnp.maximum(i * hb - 1, 0), j + off))
    wrow = lambda off: pl.BlockSpec((HALO, tc), lambda i, j: (0, j + off))
    brow = lambda off: pl.BlockSpec((1, tc), lambda i, j: (0, j + off))
    return pl.pallas_call(
        body, name=name, grid=(rows // tb, nct),
        in_specs=[main(0), main(nct), halo(0), halo(nct), wrow(0), wrow(nct), brow(0), brow(nct)],
        out_specs=pl.BlockSpec((tb, tc), lambda i, j: (i, j)),
        out_shape=jax.ShapeDtypeStruct((rows, f), BF16),
        scratch_shapes=[pltpu.VMEM((tb + HALO, tc), F32), pltpu.VMEM((tb + HALO, tc), F32)],
        compiler_params=_cparams(2),
    )(up, up, up, up, w8, w8, b, b)


def _ffn_conv_gate_bwd(dact, up, w8, b, *, taps, name):
    rows, c2 = up.shape
    f = c2 // 2
    tb = _tile(rows, 640, 128)
    tc = _tile(f, 512, 128)
    nct = f // tc
    hb = tb // HALO
    nb = rows // tb
    last_halo = rows // HALO - 1
    ext = tb + HALO

    def body(d_ref, dn_ref, u1_ref, u2_ref, p1_ref, p2_ref, n1_ref, n2_ref, w1_ref, w2_ref, b1_ref, b2_ref,
             du1_ref, du2_ref, dw1_ref, dw2_ref, db1_ref, db2_ref, bufu1, bufu2, bufd1, bufd2):
        i = pl.program_id(1)

        @pl.when(i == 0)
        def _():
            for r in (dw1_ref, dw2_ref, db1_ref, db2_ref):
                r[...] = jnp.zeros_like(r)

        has_next = i < nb - 1
        pre = []
        for u_ref, p_ref, n_ref, w_ref, b_ref, buf in ((u1_ref, p1_ref, n1_ref, w1_ref, b1_ref, bufu1),
                                                       (u2_ref, p2_ref, n2_ref, w2_ref, b2_ref, bufu2)):
            buf[0:HALO, :] = jnp.where(i > 0, p_ref[...], 0.0)
            buf[HALO:HALO + tb, :] = u_ref[...]
            buf[HALO + tb:HALO + ext, :] = jnp.where(has_next, n_ref[...], 0.0)
            acc = jnp.broadcast_to(b_ref[...], (ext, tc))
            for k in range(taps):
                acc = acc + w_ref[k:k + 1, :] * buf[pl.ds(HALO - (taps - 1) + k, ext), :]
            pre.append(acc)
        d_ext = jnp.concatenate([d_ref[...], jnp.where(has_next, dn_ref[...], 0.0)], axis=0)
        sig = _sigmoid(pre[0])
        bufd1[...] = d_ext * pre[1] * (sig * (1.0 + pre[0] * (1.0 - sig)))
        bufd2[...] = d_ext * (pre[0] * sig)
        for w_ref, bufd, bufu, du_ref, dw_ref, db_ref in ((w1_ref, bufd1, bufu1, du1_ref, dw1_ref, db1_ref),
                                                          (w2_ref, bufd2, bufu2, du2_ref, dw2_ref, db2_ref)):
            acc = jnp.zeros((tb, tc), F32)
            for k in range(taps):
                acc = acc + w_ref[k:k + 1, :] * bufd[pl.ds(taps - 1 - k, tb), :]
            du_ref[...] = acc.astype(BF16)
            dmain = bufd[0:tb, :]
            for k in range(taps):
                dw_ref[k:k + 1, :] += jnp.sum(dmain * bufu[pl.ds(HALO - (taps - 1) + k, tb), :], axis=0, keepdims=True)
            db_ref[...] += jnp.sum(dmain, axis=0, keepdims=True)

    main = lambda off: pl.BlockSpec((tb, tc), lambda j, i: (i, j + off))
    prev = lambda off: pl.BlockSpec((HALO, tc), lambda j, i: (jnp.maximum(i * hb - 1, 0), j + off))
    nxt = lambda off: pl.BlockSpec((HALO, tc), lambda j, i: (jnp.minimum((i + 1) * hb, last_halo), j + off))
    wrow = lambda off: pl.BlockSpec((HALO, tc), lambda j, i: (0, j + off))
    brow = lambda off: pl.BlockSpec((1, tc), lambda j, i: (0, j + off))
    half = jax.ShapeDtypeStruct((rows, f), BF16)
    return pl.pallas_call(
        body, name=name, grid=(nct, nb),
        in_specs=[main(0), nxt(0), main(0), main(nct), prev(0), prev(nct), nxt(0), nxt(nct),
                  wrow(0), wrow(nct), brow(0), brow(nct)],
        out_specs=[main(0), main(0), wrow(0), wrow(0), brow(0), brow(0)],
        out_shape=[half, half, jax.ShapeDtypeStruct((HALO, f), F32), jax.ShapeDtypeStruct((HALO, f), F32),
                   jax.ShapeDtypeStruct((1, f), F32), jax.ShapeDtypeStruct((1, f), F32)],
        scratch_shapes=[pltpu.VMEM((ext + HALO, tc), F32), pltpu.VMEM((ext + HALO, tc), F32),
                        pltpu.VMEM((ext, tc), F32), pltpu.VMEM((ext, tc), F32)],
        compiler_params=_cparams(2),
    )(dact, dact, up, up, up, up, up, up, w8, w8, b, b)


def _rope(xr, cos_t, sin_t, *, name, transpose=False):
    rows, w = xr.shape
    tr = _tile(rows, 640, 128)

    def body(x_ref, c_ref, s_ref, o_ref):
        xv = x_ref[...]
        if transpose:
            o_ref[...] = xv * c_ref[...] + pltpu.roll(xv * s_ref[...], 64, 1)
        else:
            o_ref[...] = xv * c_ref[...] + pltpu.roll(xv, 64, 1) * s_ref[...]

    spec = pl.BlockSpec((tr, w), lambda i: (i, 0))
    return pl.pallas_call(
        body, name=name, grid=(rows // tr,), in_specs=[spec, spec, spec], out_specs=spec,
        out_shape=jax.ShapeDtypeStruct((rows, w), F32), compiler_params=_cparams(1),
    )(xr, cos_t, sin_t)


ATT_TQ = 640
ATT_GROUP = 4


def _grouped_loop(lo, hi, step, carry, *, descending=False):
    n = jnp.maximum(hi - lo, 0)
    n_groups = lax.div(n, jnp.int32(ATT_GROUP))
    rest = n - n_groups * ATT_GROUP
    n_pairs = rest >> 1
    n_single = rest & 1
    if descending:
        top = hi - 1 - ATT_GROUP * n_groups
        carry = lax.fori_loop(
            0, n_groups, lambda t, c: step([hi - 1 - ATT_GROUP * t - b for b in range(ATT_GROUP)], c), carry)
        carry = lax.fori_loop(0, n_pairs, lambda t, c: step([top - 2 * t, top - 2 * t - 1], c), carry)
        return lax.fori_loop(0, n_single, lambda t, c: step([lo], c), carry)
    base = lo + ATT_GROUP * n_groups
    carry = lax.fori_loop(0, n_groups, lambda t, c: step([lo + ATT_GROUP * t + b for b in range(ATT_GROUP)], c), carry)
    carry = lax.fori_loop(0, n_pairs, lambda t, c: step([base + 2 * t, base + 2 * t + 1], c), carry)
    return lax.fori_loop(0, n_single, lambda t, c: step([hi - 1], c), carry)


def _tile_iotas(rows=BLOCK):
    r_i = lax.broadcasted_iota(jnp.int32, (rows, BLOCK), 0)
    c_i = lax.broadcasted_iota(jnp.int32, (rows, BLOCK), 1)
    return r_i, c_i


def _key_ranges(i, tq, first=0):
    n_blocks = ((i + 1) * tq + (BLOCK - 1)) >> 7
    first_diag = jnp.maximum((i * tq) >> 7, first)
    return first_diag, n_blocks


def _dot_tn(a, b):
    return lax.dot_general(a, b, (((0,), (0,)), ((), ())), preferred_element_type=F32)


def _cumsum_rhs(pred):
    r = lax.broadcasted_iota(jnp.int32, (BLOCK, 2 * BLOCK), 0)
    c = lax.broadcasted_iota(jnp.int32, (BLOCK, 2 * BLOCK), 1)
    return _ones_where((c >= BLOCK) | pred(r, c))


def _cumsum_dot(x, rhs):
    r = _dot(x.astype(BF16), rhs)
    return r[:, :BLOCK], r[:, BLOCK:]


def _sb_fwd(q, k, v, *, name):
    nh, rows, hd = q.shape
    tq = _tile(rows, ATT_TQ, 8)
    scale = SB_HEAD_DIM ** -0.5

    def body(q_ref, k_ref, v_ref, o_ref, u_ref):
        i = pl.program_id(1)
        r_i, c_i = _tile_iotas(tq)
        m_after = _cumsum_rhs(lambda j, s: j > s)
        qb = (q_ref[0] * scale).astype(BF16)
        rowpos = i * tq + r_i
        first_diag, n_blocks = _key_ranges(i, tq)

        def step(js, carry, masked):
            acc, cu = carry
            offs = [pl.multiple_of(j * BLOCK, BLOCK) for j in js]
            zs = [_dot_nt(qb, k_ref[0, pl.ds(off, BLOCK), :].astype(BF16)) for off in offs]
            sp = [_softplus(z) for z in zs]
            if masked:
                masks = [((off + c_i) < rowpos) & ((off + c_i) >= N_PAD) for off in offs]
                cs = [_cumsum_dot(jnp.where(m_, s, 0.0), m_after) for m_, s in zip(masks, sp)]
            else:
                cs = [_cumsum_dot(s, m_after) for s in sp]
            wgt = []
            for b in range(len(js)):
                w_ = jnp.exp(zs[b] - sp[b] - (cu + cs[b][0]))
                wgt.append(jnp.where(masks[b], w_, 0.0) if masked else w_)
                cu = cu + cs[b][1]
            for b, off in enumerate(offs):
                acc = acc + _dot(wgt[b].astype(BF16), v_ref[0, pl.ds(off, BLOCK), :].astype(BF16))
            return acc, cu

        carry = (jnp.zeros((tq, hd), F32), jnp.zeros((tq, BLOCK), F32))
        carry = _grouped_loop(first_diag, n_blocks, lambda js, c: step(js, c, True), carry, descending=True)
        acc, cu = _grouped_loop(0, first_diag, lambda js, c: step(js, c, False), carry, descending=True)
        o_ref[0] = acc
        u_ref[0] = -cu

    blk = pl.BlockSpec((1, tq, hd), lambda h, i: (h, i, 0))
    full = pl.BlockSpec((1, rows, hd), lambda h, i: (h, 0, 0))
    return pl.pallas_call(
        body, name=name, grid=(nh, rows // tq), in_specs=[blk, full, full],
        out_specs=[blk, pl.BlockSpec((1, tq, 128), lambda h, i: (h, i, 0))],
        out_shape=[jax.ShapeDtypeStruct((nh, rows, hd), F32), jax.ShapeDtypeStruct((nh, rows, 128), F32)],
        compiler_params=_cparams(2),
    )(q, k, v)


def _sb_bwd(q, k, v, do, u_tot, *, name):
    nh, rows, hd = q.shape
    tq = _tile(rows, ATT_TQ, 8)
    scale = SB_HEAD_DIM ** -0.5

    def body(q_ref, k_ref, v_ref, do_ref, u_ref, dq_ref, dk_ref, dv_ref):
        i = pl.program_id(1)

        @pl.when(i == 0)
        def _():
            dk_ref[...] = jnp.zeros_like(dk_ref)
            dv_ref[...] = jnp.zeros_like(dv_ref)

        r_i, c_i = _tile_iotas(tq)
        m_incl = _cumsum_rhs(lambda j, s: j <= s)
        m_excl = _cumsum_rhs(lambda j, s: j < s)
        qb = (q_ref[0] * scale).astype(BF16)
        dob = do_ref[0].astype(BF16)
        rowpos = i * tq + r_i
        first_diag, n_blocks = _key_ranges(i, tq)

        def step(js, carry, masked):
            dq, rem, cg = carry
            nb = range(len(js))
            offs = [pl.multiple_of(j * BLOCK, BLOCK) for j in js]
            kbs = [k_ref[0, pl.ds(off, BLOCK), :].astype(BF16) for off in offs]
            vbs = [v_ref[0, pl.ds(off, BLOCK), :].astype(BF16) for off in offs]
            zs = [_dot_nt(qb, kb) for kb in kbs]
            dws = [_dot_nt(dob, vb) for vb in vbs]
            sp = [_softplus(z) for z in zs]
            sig = [jnp.exp(zs[b] - sp[b]) for b in nb]
            if masked:
                masks = [((off + c_i) < rowpos) & ((off + c_i) >= N_PAD) for off in offs]
                cs = [_cumsum_dot(jnp.where(masks[b], sp[b], 0.0), m_incl) for b in nb]
            else:
                cs = [_cumsum_dot(sp[b], m_incl) for b in nb]
            wgt, gg = [], []
            for b in nb:
                w_ = jnp.exp(jnp.minimum(zs[b] - sp[b] - (rem - cs[b][0]), 0.0))
                wgt.append(jnp.where(masks[b], w_, 0.0) if masked else w_)
                gg.append(wgt[b] * dws[b])
                rem = rem - cs[b][1]
            gs = [_cumsum_dot(g_, m_excl) for g_ in gg]
            dzb = []
            for b in nb:
                dz = gg[b] * (1.0 - sig[b]) - sig[b] * (cg + gs[b][0])
                dzb.append((jnp.where(masks[b], dz, 0.0) if masked else dz).astype(BF16))
                cg = cg + gs[b][1]
            for b, off in enumerate(offs):
                dq = dq + _dot(dzb[b], kbs[b])
                dk_ref[0, pl.ds(off, BLOCK), :] += _dot_tn(dzb[b], qb)
                dv_ref[0, pl.ds(off, BLOCK), :] += _dot_tn(wgt[b].astype(BF16), dob)
            return dq, rem, cg

        carry = (jnp.zeros((tq, hd), F32), -u_ref[0], jnp.zeros((tq, BLOCK), F32))
        carry = _grouped_loop(0, first_diag, lambda js, c: step(js, c, False), carry)
        dq, _, _ = _grouped_loop(first_diag, n_blocks, lambda js, c: step(js, c, True), carry)
        dq_ref[0] = dq * scale

    blk = pl.BlockSpec((1, tq, hd), lambda h, i: (h, i, 0))
    full = pl.BlockSpec((1, rows, hd), lambda h, i: (h, 0, 0))
    ublk = pl.BlockSpec((1, tq, 128), lambda h, i: (h, i, 0))
    sds = jax.ShapeDtypeStruct((nh, rows, hd), F32)
    return pl.pallas_call(
        body, name=name, grid=(nh, rows // tq), in_specs=[blk, full, full, blk, ublk],
        out_specs=[blk, full, full], out_shape=[sds, sds, sds], compiler_params=_cparams(2),
    )(q, k, v, do, u_tot)


def _mla_fwd(q, k, v_ones, *, name):
    nh, rows, dk = q.shape
    dv = v_ones.shape[2] // 2
    tq = _tile(rows, ATT_TQ, 8)
    scale = (MLA_NOPE + MLA_ROPE) ** -0.5

    def body(q_ref, k_ref, v_ref, o_ref, lse_ref):
        i = pl.program_id(1)
        r_i, c_i = _tile_iotas(tq)
        qb = q_ref[0].astype(BF16)
        rowpos = i * tq + r_i
        first_diag, n_blocks = _key_ranges(i, tq, 1)

        def step(js, carry, masked):
            m, acc = carry
            offs = [pl.multiple_of(j * BLOCK, BLOCK) for j in js]
            ss = [_dot_nt(qb, k_ref[0, pl.ds(off, BLOCK), :].astype(BF16)) * scale for off in offs]
            if masked:
                ss = [jnp.where(((off + c_i) <= rowpos) & ((off + c_i) >= N_PAD), s, NEG_BIG)
                      for off, s in zip(offs, ss)]
            m_new = m
            for s in ss:
                m_new = jnp.maximum(m_new, jnp.max(s, axis=1, keepdims=True))
            ps = [jnp.exp(s - m_new).astype(BF16) for s in ss]
            acc = jnp.exp(m - m_new) * acc
            for off, p in zip(offs, ps):
                acc = acc + _dot(p, v_ref[0, pl.ds(off, BLOCK), :].astype(BF16))
            return m_new, acc

        carry = (jnp.full((tq, 1), NEG_BIG, F32), jnp.zeros((tq, 2 * dv), F32))
        carry = step([0], carry, True)
        carry = _grouped_loop(1, first_diag, lambda js, c: step(js, c, False), carry)
        m, acc = _grouped_loop(first_diag, n_blocks, lambda js, c: step(js, c, True), carry)
        o_ref[0] = (acc / pltpu.roll(acc, dv, 1))[:, :dv]
        lse_ref[0] = m + jnp.log(jnp.where(c_i >= dv, acc, 1.0))

    qblk = pl.BlockSpec((1, tq, dk), lambda h, i: (h, i, 0))
    kfull = pl.BlockSpec((1, rows, dk), lambda h, i: (h, 0, 0))
    return pl.pallas_call(
        body, name=name, grid=(nh, rows // tq), in_specs=[qblk, kfull, kfull],
        out_specs=[pl.BlockSpec((1, tq, dv), lambda h, i: (h, i, 0)),
                   pl.BlockSpec((1, tq, 128), lambda h, i: (h, i, 0))],
        out_shape=[jax.ShapeDtypeStruct((nh, rows, dv), F32), jax.ShapeDtypeStruct((nh, rows, 128), F32)],
        compiler_params=_cparams(2),
    )(q, k, v_ones)


def _mla_bwd(q, k, v, o, lse, do, *, name):
    nh, rows, dk = q.shape
    dv = v.shape[2]
    tq = _tile(rows, ATT_TQ, 8)
    scale = (MLA_NOPE + MLA_ROPE) ** -0.5

    def body(q_ref, k_ref, v_ref, o_ref, lse_ref, do_ref, dq_ref, dk_ref, dv_ref):
        i = pl.program_id(1)

        @pl.when(i == 0)
        def _():
            dk_ref[...] = jnp.zeros_like(dk_ref)
            dv_ref[...] = jnp.zeros_like(dv_ref)

        r_i, c_i = _tile_iotas(tq)
        qb = q_ref[0].astype(BF16)
        dov = do_ref[0]
        dob = dov.astype(BF16)
        delta = jnp.sum(dov * o_ref[0], axis=1, keepdims=True)
        lse = lse_ref[0][:, BLOCK - 1:BLOCK]
        rowpos = i * tq + r_i
        first_diag, n_blocks = _key_ranges(i, tq, 1)

        def step(js, dq, masked):
            nb = range(len(js))
            offs = [pl.multiple_of(j * BLOCK, BLOCK) for j in js]
            kbs = [k_ref[0, pl.ds(off, BLOCK), :].astype(BF16) for off in offs]
            ss = [_dot_nt(qb, kb) for kb in kbs]
            dps = [_dot_nt(dob, v_ref[0, pl.ds(off, BLOCK), :].astype(BF16)) for off in offs]
            ps = [jnp.exp(jnp.minimum(s * scale - lse, 0.0)) for s in ss]
            if masked:
                ps = [jnp.where(((off + c_i) <= rowpos) & ((off + c_i) >= N_PAD), p, 0.0) for off, p in zip(offs, ps)]
            dss = [(ps[b] * (dps[b] - delta) * scale).astype(BF16) for b in nb]
            for b, off in enumerate(offs):
                dq = dq + _dot(dss[b], kbs[b])
                dk_ref[0, pl.ds(off, BLOCK), :] += _dot_tn(dss[b], qb)
                dv_ref[0, pl.ds(off, BLOCK), :] += _dot_tn(ps[b].astype(BF16), dob)
            return dq

        dq = step([0], jnp.zeros((tq, dk), F32), True)
        dq = _grouped_loop(1, first_diag, lambda js, c: step(js, c, False), dq)
        dq_ref[0] = _grouped_loop(first_diag, n_blocks, lambda js, c: step(js, c, True), dq)

    qblk = pl.BlockSpec((1, tq, dk), lambda h, i: (h, i, 0))
    vblk = pl.BlockSpec((1, tq, dv), lambda h, i: (h, i, 0))
    lblk = pl.BlockSpec((1, tq, 128), lambda h, i: (h, i, 0))
    kfull = pl.BlockSpec((1, rows, dk), lambda h, i: (h, 0, 0))
    vfull = pl.BlockSpec((1, rows, dv), lambda h, i: (h, 0, 0))
    return pl.pallas_call(
        body, name=name, grid=(nh, rows // tq), in_specs=[qblk, kfull, vfull, vblk, lblk, vblk],
        out_specs=[qblk, kfull, vfull],
        out_shape=[jax.ShapeDtypeStruct((nh, rows, dk), F32), jax.ShapeDtypeStruct((nh, rows, dk), F32),
                   jax.ShapeDtypeStruct((nh, rows, dv), F32)],
        compiler_params=_cparams(2),
    )(q, k, v, o, lse, do)


def _ssd_consts():
    r_i, c_i = _tile_iotas()
    eh = lax.broadcasted_iota(jnp.int32, (BLOCK, SSD_WIDTH), 0)
    ec = lax.broadcasted_iota(jnp.int32, (BLOCK, SSD_WIDTH), 1)
    expand = _ones_where(lax.shift_right_logical(ec, 6) == eh)
    return r_i, c_i, expand


def _ssd_common(pre_v, dtr_v, bias_v, a_v, chunk, r_i, c_i, expand):
    lower = r_i >= c_i
    sig_pre = _sigmoid(pre_v)
    xbc = pre_v * sig_pre
    xs = xbc[:, :SSD_WIDTH]
    valid = (chunk * BLOCK + lax.broadcasted_iota(jnp.int32, (BLOCK, 1), 0)) >= N_PAD
    dt_in = dtr_v + bias_v
    dtv = jnp.where(valid, _softplus(dt_in), 0.0)
    d_a = dtv * a_v
    acs = _hilo_dot_r(_ones_where(lower), d_a)
    acs_t = acs.T
    dt_exp = _hilo_dot_l(dtv, expand)
    acs_exp = _hilo_dot_l(acs, expand)
    a_last = acs_exp[BLOCK - 1:BLOCK, :]
    ea = jnp.exp(acs_exp)
    e_l = jnp.exp(a_last - acs_exp)
    ea_l = jnp.exp(a_last)
    return lower, sig_pre, xbc, xs, valid, dt_in, dtv, acs, acs_t, dt_exp, ea, e_l, ea_l


def _decay(acs, acs_t, h, lower):
    col = acs[:, h:h + 1]
    row = acs_t[h:h + 1, :]
    return jnp.where(lower, jnp.exp(jnp.minimum(col - row, 0.0)), 0.0)


def _ssd_fwd(pre, dtr, bias_row, a_row, d_exp, *, name):
    rows = pre.shape[0]
    nc = rows // BLOCK

    def body(pre_ref, dtr_ref, bias_ref, a_ref, dexp_ref, y_ref, st_ref, state):
        c = pl.program_id(0)

        @pl.when(c == 0)
        def _():
            state[...] = jnp.zeros_like(state)

        r_i, c_i, expand = _ssd_consts()
        lane_lo = c_i < 64
        (lower, _, xbc, xs, _, _, _, acs, acs_t, dt_exp, ea, e_l, ea_l) = _ssd_common(
            pre_ref[...], dtr_ref[...], bias_ref[...], a_ref[...], c, r_i, c_i, expand)
        xin = xs * dt_exp
        for g in range(2):
            bg = xbc[:, 512 + 128 * g:640 + 128 * g]
            cg = xbc[:, 768 + 128 * g:896 + 128 * g]
            bb = bg.astype(BF16)
            cbf = cg.astype(BF16)
            cb = _dot_nt(cbf, bb)
            bt = bg.T.astype(BF16)
            for pp in range(2):
                p = 2 * g + pp
                sl = slice(128 * p, 128 * p + 128)
                xp = xin[:, sl]
                xb = xp.astype(BF16)
                rs = [_dot((cb * _decay(acs, acs_t, 2 * p + hh, lower)).astype(BF16), xb) for hh in range(2)]
                ydiag = jnp.where(lane_lo, rs[0], rs[1])
                s_in = state[p]
                st_ref[0, p] = s_in
                yoff = ea[:, sl] * _dot(cbf, s_in.astype(BF16))
                y_ref[:, sl] = ydiag + yoff + xs[:, sl] * dexp_ref[:, sl]
                state[p] = ea_l[:, sl] * s_in + _dot(bt, (xp * e_l[:, sl]).astype(BF16))

    vec = pl.BlockSpec((1, 128), lambda c: (0, 0))
    return pl.pallas_call(
        body, name=name, grid=(nc,),
        in_specs=[pl.BlockSpec((BLOCK, SSD_XBC), lambda c: (c, 0)),
                  pl.BlockSpec((BLOCK, 128), lambda c: (c, 0)), vec, vec,
                  pl.BlockSpec((1, SSD_WIDTH), lambda c: (0, 0))],
        out_specs=[pl.BlockSpec((BLOCK, SSD_WIDTH), lambda c: (c, 0)),
                   pl.BlockSpec((1, 4, 128, 128), lambda c: (c, 0, 0, 0))],
        out_shape=[jax.ShapeDtypeStruct((rows, SSD_WIDTH), F32), jax.ShapeDtypeStruct((nc, 4, 128, 128), F32)],
        scratch_shapes=[pltpu.VMEM((4, 128, 128), F32)],
        compiler_params=_cparams(1),
    )(pre, dtr, bias_row, a_row, d_exp)


def _ssd_bwd(pre, dtr, bias_row, a_row, d_exp, states, dy, *, name):
    rows = pre.shape[0]
    nc = rows // BLOCK

    def body(pre_ref, dtr_ref, bias_ref, a_ref, dexp_ref, st_ref, dy_ref,
             dpre_ref, ddtr_ref, dbias_ref, da_ref, dd_ref, dstate, q_buf, dx_buf):
        step = pl.program_id(0)
        c = nc - 1 - step

        @pl.when(step == 0)
        def _():
            dstate[...] = jnp.zeros_like(dstate)
            dbias_ref[...] = jnp.zeros_like(dbias_ref)
            da_ref[...] = jnp.zeros_like(da_ref)
            dd_ref[...] = jnp.zeros_like(dd_ref)

        r_i, c_i, expand = _ssd_consts()
        lane_lo = c_i < 64
        last_row = r_i == BLOCK - 1
        pre_v = pre_ref[...]
        (lower, sig_pre, xbc, xs, valid, dt_in, dtv, acs, acs_t, dt_exp, ea, e_l, ea_l) = _ssd_common(
            pre_v, dtr_ref[...], bias_ref[...], a_ref[...], c, r_i, c_i, expand)
        dsilu = sig_pre * (1.0 + pre_v * (1.0 - sig_pre))
        xin = xs * dt_exp
        dyv = dy_ref[...]
        d_acs_diag = jnp.zeros((BLOCK, BLOCK), F32)
        for g in range(2):
            bg = xbc[:, 512 + 128 * g:640 + 128 * g]
            cg = xbc[:, 768 + 128 * g:896 + 128 * g]
            bb = bg.astype(BF16)
            cbf = cg.astype(BF16)
            cb = _dot_nt(cbf, bb)
            ct = cg.T.astype(BF16)
            dcb = jnp.zeros((BLOCK, BLOCK), F32)
            dbg = jnp.zeros((BLOCK, BLOCK), F32)
            dcg = jnp.zeros((BLOCK, BLOCK), F32)
            for pp in range(2):
                p = 2 * g + pp
                sl = slice(128 * p, 128 * p + 128)
                xp = xin[:, sl]
                xb = xp.astype(BF16)
                dyp = dyv[:, sl]
                dyb = dyp.astype(BF16)
                dxs_ = []
                for hh in range(2):
                    dec = _decay(acs, acs_t, 2 * p + hh, lower)
                    wm = cb * dec
                    dxs_.append(_dot(wm.T.astype(BF16), dyb))
                    half = lane_lo if hh == 0 else jnp.logical_not(lane_lo)
                    dwm = _dot_nt(jnp.where(half, dyp, 0.0).astype(BF16), xb)
                    dcb = dcb + dwm * dec
                    dseg = dwm * wm
                    dcol = jnp.sum(dseg, axis=1, keepdims=True) - jnp.sum(dseg.T, axis=1, keepdims=True)
                    d_acs_diag = jnp.where(c_i == 2 * p + hh, dcol, d_acs_diag)
                dxdiag =jnp.where(lane_lo, dxs_[0], dxs_[1])
                s_in = st_ref[0, p]
                sb = s_in.astype(BF16)
                ds_out = dstate[p]
                dsb = ds_out.astype(BF16)
                yoff = ea[:, sl] * _dot(cbf, sb)
                dxst = e_l[:, sl] * _dot(bb, dsb)
                dxp = dxdiag + dxst
                dye = dyp * ea[:, sl]
                dyeb = dye.astype(BF16)
                qp = dyp * yoff - xp * dxst
                lastv = (jnp.sum(xp * dxst, axis=0, keepdims=True)
                         + ea_l[:, sl] * jnp.sum(ds_out * s_in, axis=0, keepdims=True))
                q_buf[:, sl] = jnp.where(last_row, qp + lastv, qp)
                dx_buf[:, sl] = dxp
                dcg = dcg + _dot_nt(dyeb, sb)
                dbg = dbg + _dot_nt((xp * e_l[:, sl]).astype(BF16), dsb)
                dstate[p] = ea_l[:, sl] * ds_out + _dot(ct, dyeb)
            dcg = dcg + _dot(dcb.astype(BF16), bb)
            dbg = dbg + _dot(dcb.T.astype(BF16), cbf)
            bsl = slice(512 + 128 * g, 640 + 128 * g)
            csl = slice(768 + 128 * g, 896 + 128 * g)
            dpre_ref[:, bsl] = dbg * dsilu[:, bsl]
            dpre_ref[:, csl] = dcg * dsilu[:, csl]
        dxall = dx_buf[...]
        dpre_ref[:, :SSD_WIDTH] = (dyv * dexp_ref[...] + dxall * dt_exp) * dsilu[:, :SSD_WIDTH]
        dd_ref[...] += jnp.sum(dyv * xs, axis=0, keepdims=True)
        d_acs = d_acs_diag + _hilo_dot_nt(q_buf[...], expand)
        dd_a = _hilo_dot_r(_ones_where(r_i <= c_i), d_acs)
        ddt = dd_a * a_ref[...] + _hilo_dot_nt(dxall * xs, expand)
        ddt = jnp.where(valid, ddt, 0.0)
        da_ref[...] += jnp.sum(dd_a * dtv, axis=0, keepdims=True)
        ddtr = ddt * _sigmoid(dt_in)
        ddtr_ref[...] = ddtr
        dbias_ref[...] += jnp.sum(ddtr, axis=0, keepdims=True)

    vec = pl.BlockSpec((1, 128), lambda s: (0, 0))
    wide = pl.BlockSpec((1, SSD_WIDTH), lambda s: (0, 0))
    rev = lambda s: (nc - 1 - s, 0)
    return pl.pallas_call(
        body, name=name, grid=(nc,),
        in_specs=[pl.BlockSpec((BLOCK, SSD_XBC), rev), pl.BlockSpec((BLOCK, 128), rev), vec, vec, wide,
                  pl.BlockSpec((1, 4, 128, 128), lambda s: (nc - 1 - s, 0, 0, 0)),
                  pl.BlockSpec((BLOCK, SSD_WIDTH), rev)],
        out_specs=[pl.BlockSpec((BLOCK, SSD_XBC), rev), pl.BlockSpec((BLOCK, 128), rev), vec, vec, wide],
        out_shape=[jax.ShapeDtypeStruct((rows, SSD_XBC), F32), jax.ShapeDtypeStruct((rows, 128), F32),
                   jax.ShapeDtypeStruct((1, 128), F32), jax.ShapeDtypeStruct((1, 128), F32),
                   jax.ShapeDtypeStruct((1, SSD_WIDTH), F32)],
        scratch_shapes=[pltpu.VMEM((4, 128, 128), F32), pltpu.VMEM((BLOCK, SSD_WIDTH), F32),
                        pltpu.VMEM((BLOCK, SSD_WIDTH), F32)],
        compiler_params=_cparams(1),
    )(pre, dtr, bias_row, a_row, d_exp, states, dy)


def _peer(xi, yi, ci, k):
    px = (1 - xi) if (k >> 2) & 1 else xi
    py = (1 - yi) if (k >> 1) & 1 else yi
    pc = (1 - ci) if k & 1 else ci
    return (px, py, pc), 4 * px + 2 * py + pc


def _exchange(xs, *, gather, name):
    n = len(xs)
    n_peers = N_DEV - 1
    out_shape = [jax.ShapeDtypeStruct((N_DEV,) + x.shape if gather else x.shape, x.dtype) for x in xs]

    def body(*refs):
        x_refs, o_refs = refs[:n], refs[n:2 * n]
        send_sems, recv_sems, local_sems = refs[2 * n:]
        xi, yi, ci = lax.axis_index("x"), lax.axis_index("y"), lax.axis_index("c")
        me = 4 * xi + 2 * yi + ci

        def copy(a, k, src_idx, dst_idx, peer):
            src = x_refs[a] if gather else x_refs[a].at[src_idx]
            return pltpu.make_async_remote_copy(
                src_ref=src, dst_ref=o_refs[a].at[dst_idx], send_sem=send_sems.at[a * n_peers + k - 1],
                recv_sem=recv_sems.at[a * n_peers + k - 1], device_id=peer, device_id_type=pl.DeviceIdType.MESH)

        local = [pltpu.make_async_copy(x_refs[a] if gather else x_refs[a].at[me], o_refs[a].at[me], local_sems.at[a])
                 for a in range(n)]
        for cp in local:
            cp.start()
        sends = []
        for k in range(1, N_DEV):
            peer, pidx = _peer(xi, yi, ci, k)
            for a in range(n):
                sends.append(copy(a, k, pidx, me, peer))
                sends[-1].start()
        for k in range(1, N_DEV):
            peer, pidx = _peer(xi, yi, ci, k)
            for a in range(n):
                copy(a, k, pidx, pidx, peer).wait_recv()
        for cp in sends:
            cp.wait_send()
        for cp in local:
            cp.wait()

    hbm = pl.BlockSpec(memory_space=pltpu.HBM)
    return pl.pallas_call(
        body, name=name, out_shape=out_shape, in_specs=[hbm] * n, out_specs=[hbm] * n,
        scratch_shapes=[pltpu.SemaphoreType.DMA((n * n_peers,)), pltpu.SemaphoreType.DMA((n * n_peers,)),
                        pltpu.SemaphoreType.DMA((n,))],
    )(*xs)


def _other_chips(xi, yi):
    return [(1 - xi, yi), (xi, 1 - yi), (1 - xi, 1 - yi)]


def _gather_two_level(xs, *, name):
    n = len(xs)

    def body(*refs):
        x_refs, o_refs = refs[:n], refs[n:2 * n]
        send_sems, recv_sems, local_sems = refs[2 * n:]
        xi, yi, ci = lax.axis_index("x"), lax.axis_index("y"), lax.axis_index("c")
        me, sibling = (xi, yi, ci), (xi, yi, 1 - ci)
        chips = _other_chips(xi, yi)

        def slot(px, py, pc):
            return 4 * px + 2 * py + pc

        def copy(a, k, block, to, from_input=False):
            return pltpu.make_async_remote_copy(
                src_ref=x_refs[a] if from_input else o_refs[a].at[slot(*block)], dst_ref=o_refs[a].at[slot(*block)],
                send_sem=send_sems.at[7 * a + k], recv_sem=recv_sems.at[7 * a + k],
                device_id=to, device_id_type=pl.DeviceIdType.MESH)

        local = [pltpu.make_async_copy(x_refs[a], o_refs[a].at[slot(*me)], local_sems.at[a]) for a in range(n)]
        for cp in local:
            cp.start()
        sends = []
        for a in range(n):
            sends.append(copy(a, 0, me, sibling, from_input=True))
            sends += [copy(a, 1 + j, me, (*chip, ci), from_input=True) for j, chip in enumerate(chips)]
        for cp in sends:
            cp.start()
        for j, chip in enumerate(chips):
            for a in range(n):
                copy(a, 1 + j, (*chip, ci), me).wait_recv()
                sends.append(copy(a, 4 + j, (*chip, ci), sibling))
                sends[-1].start()
        for a in range(n):
            copy(a, 0, sibling, me).wait_recv()
            for j, chip in enumerate(chips):
                copy(a, 4 + j, (*chip, 1 - ci), me).wait_recv()
        for cp in sends:
            cp.wait_send()
        for cp in local:
            cp.wait()

    hbm = pl.BlockSpec(memory_space=pltpu.HBM)
    return pl.pallas_call(
        body, name=name, out_shape=[jax.ShapeDtypeStruct((N_DEV,) + x.shape, x.dtype) for x in xs],
        in_specs=[hbm] * n, out_specs=[hbm] * n,
        scratch_shapes=[pltpu.SemaphoreType.DMA((7 * n,)), pltpu.SemaphoreType.DMA((7 * n,)),
                        pltpu.SemaphoreType.DMA((n,))],
    )(*xs)


def _sibling_swap(xs, *, name):
    n = len(xs)

    def body(*refs):
        x_refs, o_refs = refs[:n], refs[n:2 * n]
        send_sems, recv_sems = refs[2 * n:]
        xi, yi, ci = lax.axis_index("x"), lax.axis_index("y"), lax.axis_index("c")
        copies = [pltpu.make_async_remote_copy(
            src_ref=x_refs[a].at[1 - ci], dst_ref=o_refs[a], send_sem=send_sems.at[a], recv_sem=recv_sems.at[a],
            device_id=(xi, yi, 1 - ci), device_id_type=pl.DeviceIdType.MESH) for a in range(n)]
        for cp in copies:
            cp.start()
        for cp in copies:
            cp.wait()

    hbm = pl.BlockSpec(memory_space=pltpu.HBM)
    return pl.pallas_call(
        body, name=name, out_shape=[jax.ShapeDtypeStruct(x.shape[1:], x.dtype) for x in xs],
        in_specs=[hbm] * n, out_specs=[hbm] * n,
        scratch_shapes=[pltpu.SemaphoreType.DMA((n,)), pltpu.SemaphoreType.DMA((n,))],
    )(*xs)


def _chip_all_to_all(xs, *, name):
    n = len(xs)

    def body(*refs):
        x_refs, o_refs = refs[:n], refs[n:2 * n]
        send_sems, recv_sems, local_sems = refs[2 * n:]
        xi, yi, ci = lax.axis_index("x"), lax.axis_index("y"), lax.axis_index("c")
        mine = 2 * xi + yi
        chips = _other_chips(xi, yi)

        def copy(a, j, src_slot, dst_slot, chip):
            return pltpu.make_async_remote_copy(
                src_ref=x_refs[a].at[src_slot], dst_ref=o_refs[a].at[dst_slot], send_sem=send_sems.at[3 * a + j],
                recv_sem=recv_sems.at[3 * a + j], device_id=(*chip, ci), device_id_type=pl.DeviceIdType.MESH)

        local = [pltpu.make_async_copy(x_refs[a].at[mine], o_refs[a].at[mine], local_sems.at[a]) for a in range(n)]
        for cp in local:
            cp.start()
        sends = [copy(a, j, 2 * chip[0] + chip[1], mine, chip) for j, chip in enumerate(chips) for a in range(n)]
        for cp in sends:
            cp.start()
        for j, chip in enumerate(chips):
            for a in range(n):
                copy(a, j, mine, 2 * chip[0] + chip[1], chip).wait_recv()
        for cp in sends:
            cp.wait_send()
        for cp in local:
            cp.wait()

    hbm = pl.BlockSpec(memory_space=pltpu.HBM)
    return pl.pallas_call(
        body, name=name, out_shape=[jax.ShapeDtypeStruct(x.shape, x.dtype) for x in xs],
        in_specs=[hbm] * n, out_specs=[hbm] * n,
        scratch_shapes=[pltpu.SemaphoreType.DMA((3 * n,)), pltpu.SemaphoreType.DMA((3 * n,)),
                        pltpu.SemaphoreType.DMA((n,))],
    )(*xs)


def _pair_add(a, b, *, name):
    rows, cols = a.shape
    lanes = -(-cols // 128) * 128
    tr = _tile(rows, max(16, (512 * 1024) // lanes), 16)

    def body(a_ref, b_ref, o_ref):
        o_ref[...] = (a_ref[...].astype(F32) + b_ref[...].astype(F32)).astype(BF16)

    spec = pl.BlockSpec((tr, cols), lambda i: (i, 0))
    return pl.pallas_call(
        body, name=name, grid=(rows // tr,), in_specs=[spec, spec], out_specs=spec,
        out_shape=jax.ShapeDtypeStruct((rows, cols), BF16), compiler_params=_cparams(1),
    )(a, b)


def _adamw(gs, w, m, v, *, name):
    n_slots = gs.shape[0]
    rows, cols = w.shape
    lanes = -(-cols // 128) * 128
    tr = _tile(rows, max(16, (128 * 1024) // lanes), 16 if gs.dtype == BF16 else 8)

    def body(g_ref, w_ref, m_ref, v_ref, go_ref, d_ref, mo_ref, vo_ref):
        g = g_ref[0].astype(F32)
        for j in range(1, n_slots):
            g = g + g_ref[j].astype(F32)
        m2 = ADAM_B1 * m_ref[...] + (1.0 - ADAM_B1) * g
        v2 = ADAM_B2 * v_ref[...] + (1.0 - ADAM_B2) * (g * g)
        m_hat = m2 / (1.0 - ADAM_B1 ** ADAM_STEP)
        v_hat = v2 / (1.0 - ADAM_B2 ** ADAM_STEP)
        go_ref[...] = g
        d_ref[...] = -ADAM_LR * (m_hat / (jnp.sqrt(v_hat) + ADAM_EPS) + ADAM_WD * w_ref[...])
        mo_ref[...] = m2
        vo_ref[...] = v2

    spec = pl.BlockSpec((tr, cols), lambda i: (i, 0))
    sds = jax.ShapeDtypeStruct((rows, cols), F32)
    return pl.pallas_call(
        body, name=name, grid=(rows // tr,),
        in_specs=[pl.BlockSpec((n_slots, tr, cols), lambda i: (0, i, 0)), spec, spec, spec],
        out_specs=[spec, spec, spec, spec], out_shape=[sds, sds, sds, sds], compiler_params=_cparams(1),
    )(gs, w, m, v)


SHARDED = (("meta_tokens", 1), ("w_in", 2), ("ssd_conv_w", 2), ("mla_w_uq", 2), ("mla_w_ukv", 2),
           ("w_out", 1), ("ffn_w_up", 2), ("ffn_conv_w", 2), ("ffn_w_down", 1))
BIG = ("w_in", "w_out", "ffn_w_up", "ffn_w_down")
SMALL = ("meta_tokens", "ssd_conv_w", "mla_w_uq", "mla_w_ukv", "ffn_conv_w")
REPLICATED = ("norm_mix_g", "ssd_conv_b", "ssd_dt_bias", "ssd_a_log", "ssd_d", "ssd_norm_g", "sb_norm_g",
              "mla_q_norm_g", "mla_kv_norm_g", "mla_norm_g", "norm_ffn_g", "ffn_conv_b", "final_norm_g")
WEIGHTS = ("meta_tokens", "norm_mix_g", "w_in", "ssd_conv_w", "ssd_conv_b", "ssd_dt_bias", "ssd_a_log", "ssd_d",
           "ssd_norm_g", "sb_norm_g", "mla_q_norm_g", "mla_kv_norm_g", "mla_w_uq", "mla_w_ukv", "mla_norm_g",
           "w_out", "norm_ffn_g", "ffn_w_up", "ffn_conv_w", "ffn_conv_b", "ffn_w_down", "final_norm_g")


def _flat_pack(arrays, dtype, align):
    flat = jnp.concatenate([a.reshape(-1).astype(dtype) for a in arrays])
    pad = (-flat.shape[0]) % align
    return jnp.pad(flat, (0, pad)).reshape(-1, 128)


def _pieces(full, axis):
    shp = full.shape
    t = full.reshape(shp[:axis] + (N_DEV, shp[axis] // N_DEV) + shp[axis + 1:])
    return jnp.moveaxis(t, axis, 0).reshape(N_DEV, -1)


def _unpieces(p8, shard_shape, axis):
    t = p8.reshape((N_DEV,) + shard_shape)
    t = jnp.moveaxis(t, 0, axis)
    return t.reshape(shard_shape[:axis] + (N_DEV * shard_shape[axis],) + shard_shape[axis + 1:])


def _split_core_chip(full, axis):
    shp = full.shape
    t = full.reshape(shp[:axis] + (4, 2, shp[axis] // N_DEV) + shp[axis + 1:])
    return jnp.moveaxis(t, (axis + 1, axis), (0, 1))


def _merge_blocks(b8, axis):
    shard = b8.shape[1:]
    t = jnp.moveaxis(b8, 0, axis)
    return t.reshape(shard[:axis] + (N_DEV * shard[axis],) + shard[axis + 1:])


def _gather_weights(shards):
    axes = dict(SHARDED)
    got = _gather_two_level([shards[n].astype(BF16) for n in BIG], name="gather_big")
    full = {n: _merge_blocks(b8, axes[n]) for n, b8 in zip(BIG, got)}
    packed = _flat_pack([shards[n] for n in SMALL], F32, 8 * 128)
    got = _exchange([packed], gather=True, name="gather_small")[0].reshape(N_DEV, -1)
    off = 0
    for n in SMALL:
        size = math.prod(shards[n].shape)
        full[n] = _unpieces(got[:, off:off + size], shards[n].shape, axes[n])
        off += size
    return full


def _pad_cols(a, width):
    return jnp.pad(a, ((0, 0), (0, width - a.shape[1])))


def _w_in_padded(w):
    kr = w[:, 2632:2664]
    return jnp.concatenate([
        w[:, 0:512], w[:, 512:1536], w[:, 1544:2312], _pad_cols(w[:, 2312:2504], 256), w[:, 2504:2632],
        _pad_cols(kr[:, :16], 64), _pad_cols(kr[:, 16:], 64), _pad_cols(w[:, 1536:1544], 128),
        jnp.zeros((w.shape[0], 128), w.dtype)], axis=1)


def _w_in_unpadded(wp):
    return jnp.concatenate([
        wp[:, 0:512], wp[:, 512:1536], wp[:, OFF_DT:OFF_DT + 8], wp[:, 1536:2304], wp[:, OFF_QA:OFF_QA + 192],
        wp[:, OFF_CKV:OFF_CKV + 128], wp[:, OFF_KR:OFF_KR + 16], wp[:, OFF_KR + 64:OFF_KR + 80]], axis=1)


def _w_uq_perm(w):
    t = w.reshape(MLA_Q_RANK, MLA_HEADS, MLA_NOPE + MLA_ROPE)
    out = jnp.concatenate([t[:, :, :64].reshape(MLA_Q_RANK, 256), t[:, :, 64:80].reshape(MLA_Q_RANK, 64),
                           t[:, :, 80:96].reshape(MLA_Q_RANK, 64)], axis=1)
    return jnp.pad(out, ((0, 256 - MLA_Q_RANK), (0, 0)))


def _w_uq_unperm(wp):
    wp = wp[:MLA_Q_RANK]
    t = jnp.concatenate([wp[:, :256].reshape(MLA_Q_RANK, 4, 64), wp[:, 256:320].reshape(MLA_Q_RANK, 4, 16),
                         wp[:, 320:384].reshape(MLA_Q_RANK, 4, 16)], axis=2)
    return t.reshape(MLA_Q_RANK, 4 * 96)


def _w_ukv_perm(w):
    t = w.reshape(MLA_KV_RANK, MLA_HEADS, 128)
    return jnp.concatenate([t[:, :, :64].reshape(MLA_KV_RANK, 256), t[:, :, 64:].reshape(MLA_KV_RANK, 256)], axis=1)


def _w_ukv_unperm(wp):
    t = jnp.concatenate([wp[:, :256].reshape(MLA_KV_RANK, 4, 64), wp[:, 256:].reshape(MLA_KV_RANK, 4, 64)], axis=2)
    return t.reshape(MLA_KV_RANK, 512)


def _heads(a, hd):
    return jnp.moveaxis(a.reshape(a.shape[0], -1, hd), 1, 0)


def _unheads(a):
    return jnp.moveaxis(a, 0, 1).reshape(a.shape[1], -1)


def _row(v, width=None):
    v = v.reshape(1, -1)
    return v if width is None else _pad_cols(v, width)


def _rope_tables(rows):
    pos = jnp.arange(rows, dtype=F32) - float(N_PAD)
    inv = 1.0 / (ROPE_BASE ** (jnp.arange(0, MLA_ROPE, 2, dtype=F32) / MLA_ROPE))
    ang = pos[:, None] * inv[None, :]
    cos = jnp.tile(jnp.cos(ang), (1, 8))
    sin = jnp.tile(jnp.sin(ang), (1, 4))
    return cos, jnp.concatenate([-sin, sin], axis=1)


def _layer_fwd(h, p, cos_t, sin_t, tag):
    s = {"h_in": h}
    hn = _rmsnorm_fwd(h, p["norm_mix_g"], width=D_MODEL, name=tag + "norm_mix")
    u = _matmul(hn, p["w_in"], name=tag + "in_proj")
    s["hn"], s["u"] = hn, u

    xbc_in = u[:, OFF_XBC:OFF_XBC + SSD_XBC]
    pre = _dwconv_fwd(xbc_in, p["ssd_conv_w"], p["ssd_conv_b"], taps=SSD_CONV, name=tag + "ssd_conv")
    dtr = u[:, OFF_DT:OFF_DT + 128]
    y_ssd, states = _ssd_fwd(pre, dtr, p["dt_bias"], p["a_row"], p["d_exp"], name=tag + "ssd_core")
    zgate = u[:, OFF_Z:OFF_Z + SSD_WIDTH]
    yn_ssd = _rmsnorm_fwd(y_ssd, p["ssd_norm_g"], width=SSD_WIDTH, z=zgate, name=tag + "ssd_norm")
    s.update(xbc_in=xbc_in, pre=pre, dtr=dtr, y_ssd=y_ssd, states=states, zgate=zgate)

    q_sb = _heads(u[:, OFF_QSB:OFF_QSB + SB_WIDTH], SB_HEAD_DIM)
    k_sb = _heads(u[:, OFF_KSB:OFF_KSB + SB_WIDTH], SB_HEAD_DIM).astype(BF16)
    v_sb = _heads(u[:, OFF_VSB:OFF_VSB + SB_WIDTH], SB_HEAD_DIM).astype(BF16)
    o_sb, u_tot = _sb_fwd(q_sb, k_sb, v_sb, name=tag + "sb_attn")
    o_sb_flat = _unheads(o_sb)
    yn_sb = _rmsnorm_fwd(o_sb_flat, p["sb_norm_g"], width=SB_WIDTH, name=tag + "sb_norm")
    s.update(q_sb=q_sb, k_sb=k_sb, v_sb=v_sb, u_tot=u_tot, o_sb_flat=o_sb_flat)

    qa = u[:, OFF_QA:OFF_QA + 256]
    ckv = u[:, OFF_CKV:OFF_CKV + 128]
    qa_n = _rmsnorm_fwd(qa, p["mla_q_norm_g"], width=MLA_Q_RANK, name=tag + "mla_qnorm")
    ckv_n = _rmsnorm_fwd(ckv, p["mla_kv_norm_g"], width=MLA_KV_RANK, name=tag + "mla_kvnorm")
    qf = _matmul(qa_n, p["mla_w_uq"], name=tag + "mla_uq")
    kvf = _matmul(ckv_n, p["mla_w_ukv"], name=tag + "mla_ukv")
    q_rope = _rope(qf[:, 256:384], cos_t, sin_t, name=tag + "rope_q")
    k_rope = _rope(u[:, OFF_KR:OFF_KR + 128], cos_t, sin_t, name=tag + "rope_k")
    rows = h.shape[0]
    zpad = jnp.zeros((MLA_HEADS, rows, 32), F32)
    qh = jnp.concatenate([_heads(qf[:, :256], 64), _heads(q_rope[:, :64], 16), _heads(q_rope[:, 64:], 16), zpad], axis=2)
    kr_b = jnp.broadcast_to(jnp.concatenate([k_rope[:, 0:16], k_rope[:, 64:80]], axis=1)[None], (MLA_HEADS, rows, 32))
    kh = jnp.concatenate([_heads(kvf[:, :256], 64), kr_b, zpad], axis=2).astype(BF16)
    vh = _heads(kvf[:, 256:], 64).astype(BF16)
    v_ones = jnp.concatenate([vh, jnp.ones_like(vh)], axis=2)
    o_mla, lse = _mla_fwd(qh, kh, v_ones, name=tag + "mla_attn")
    o_mla_flat = _unheads(o_mla)
    yn_mla = _rmsnorm_fwd(o_mla_flat, p["mla_norm_g"], width=256, name=tag + "mla_norm")
    s.update(qa=qa, ckv=ckv, qa_n=qa_n, ckv_n=ckv_n, qh=qh, kh=kh, vh=vh, o_mla=o_mla, lse=lse,
             o_mla_flat=o_mla_flat)

    mix = jnp.concatenate([yn_ssd, yn_sb, yn_mla], axis=1)
    h_mid = _matmul(mix, p["w_out"], res=h, mask_pad=True, name=tag + "out_proj")
    hn2 = _rmsnorm_fwd(h_mid, p["norm_ffn_g"], width=D_MODEL, name=tag + "norm_ffn")
    up = _matmul(hn2, p["ffn_w_up"], tn=1408, name=tag + "ffn_up")
    act = _ffn_conv_gate_fwd(up, p["ffn_conv_w"], p["ffn_conv_b"], taps=FFN_CONV, name=tag + "ffn_conv_gate")
    h_out = _matmul(act, p["ffn_w_down"], res=h_mid, mask_pad=True, tk=1408, name=tag + "ffn_down")
    s.update(mix=mix, h_mid=h_mid, hn2=hn2, up=up, act=act)
    return h_out, s


def _layer_bwd(dh_out, p, s, cos_t, sin_t, tag):
    g = {}
    rows = dh_out.shape[0]
    dact = _matmul(dh_out, p["ffn_w_down"], tb=True, tn=1408, name=tag + "b_down_dx")
    g["ffn_w_down"] = _matmul(s["act"], dh_out, ta=True, tm=1408, tk=640, name=tag + "b_down_dw")
    dup1, dup2, dcw1, dcw2, dcb1, dcb2 = _ffn_conv_gate_bwd(
        dact, s["up"], p["ffn_conv_w"], p["ffn_conv_b"], taps=FFN_CONV, name=tag + "b_ffn_conv_gate")
    g["ffn_conv_w"] = jnp.concatenate([dcw1[:FFN_CONV], dcw2[:FFN_CONV]], axis=1)
    g["ffn_conv_b"] = jnp.concatenate([dcb1[0], dcb2[0]])
    w_up1, w_up2 = p["ffn_w_up"][:, :D_FF], p["ffn_w_up"][:, D_FF:]
    dhn2 = _matmul(dup1, w_up1, tb=True, tk=1408, name=tag + "b_up_dx1")
    dhn2 = _matmul(dup2, w_up2, tb=True, tk=1408, res=dhn2, name=tag + "b_up_dx2")
    g["ffn_w_up"] = jnp.concatenate([_matmul(s["hn2"], dup1, ta=True, tn=1408, tk=640, name=tag + "b_up_dw1"),
                                     _matmul(s["hn2"], dup2, ta=True, tn=1408, tk=640, name=tag + "b_up_dw2")], axis=1)
    dh_mid, _, dg = _rmsnorm_bwd(s["h_mid"], p["norm_ffn_g"], dhn2, width=D_MODEL, res=dh_out, mask_pad=True,
                                 name=tag + "b_norm_ffn")
    g["norm_ffn_g"] = dg[0]

    dmix = _matmul(dh_mid, p["w_out"], tb=True, name=tag + "b_out_dx")
    g["w_out"] = _matmul(s["mix"], dh_mid, ta=True, tk=640, name=tag + "b_out_dw")

    dy_ssd, dz, dg = _rmsnorm_bwd(s["y_ssd"], p["ssd_norm_g"], dmix[:, :SSD_WIDTH], width=SSD_WIDTH, z=s["zgate"],
                                  name=tag + "b_ssd_norm")
    g["ssd_norm_g"] = dg[0]
    dpre, ddtr, dbias, da, dd = _ssd_bwd(s["pre"], s["dtr"], p["dt_bias"], p["a_row"], p["d_exp"], s["states"],
                                         dy_ssd, name=tag + "b_ssd_core")
    g["ssd_dt_bias"] = dbias[0, :8]
    g["ssd_a_log"] = da[0, :8] * p["a_row"][0, :8]
    g["ssd_d"] = dd.reshape(8, 64).sum(axis=1)
    dxbc_in, dcw, dcb_ = _dwconv_bwd(dpre, s["xbc_in"], p["ssd_conv_w"], taps=SSD_CONV, name=tag + "b_ssd_conv")
    g["ssd_conv_w"], g["ssd_conv_b"] = dcw[:SSD_CONV], dcb_[0]

    do_sb_flat, _, dg = _rmsnorm_bwd(s["o_sb_flat"], p["sb_norm_g"], dmix[:, 512:768], width=SB_WIDTH,
                                     name=tag + "b_sb_norm")
    g["sb_norm_g"] = dg[0]
    dq_sb, dk_sb, dv_sb = _sb_bwd(s["q_sb"], s["k_sb"], s["v_sb"], _heads(do_sb_flat, SB_HEAD_DIM), s["u_tot"],
                                  name=tag + "b_sb_attn")

    do_mla_flat, _, dg = _rmsnorm_bwd(s["o_mla_flat"], p["mla_norm_g"], dmix[:, 768:1024], width=256,
                                      name=tag + "b_mla_norm")
    g["mla_norm_g"] = dg[0]
    dqh, dkh, dvh = _mla_bwd(s["qh"], s["kh"], s["vh"], s["o_mla"], s["lse"], _heads(do_mla_flat, 64),
                             name=tag + "b_mla_attn")
    dq_rope_in = jnp.concatenate([_unheads(dqh[:, :, 64:80]), _unheads(dqh[:, :, 80:96])], axis=1)
    dq_r = _rope(dq_rope_in, cos_t, sin_t, transpose=True, name=tag + "b_rope_q")
    dqf = jnp.concatenate([_unheads(dqh[:, :, :64]), dq_r], axis=1)
    dkr_sum = jnp.sum(dkh[:, :, 64:96], axis=0)
    dk_rope_in = jnp.concatenate([_pad_cols(dkr_sum[:, :16], 64), _pad_cols(dkr_sum[:, 16:], 64)], axis=1)
    dkr = _rope(dk_rope_in, cos_t, sin_t, transpose=True, name=tag + "b_rope_k")
    dkvf = jnp.concatenate([_unheads(dkh[:, :, :64]), _unheads(dvh)], axis=1)
    dqa_n = _matmul(dqf, p["mla_w_uq"], tb=True, name=tag + "b_uq_dx")
    g["mla_w_uq"] = _matmul(s["qa_n"], dqf, ta=True, tk=640, name=tag + "b_uq_dw")
    dckv_n = _matmul(dkvf, p["mla_w_ukv"], tb=True, name=tag + "b_ukv_dx")
    g["mla_w_ukv"] = _matmul(s["ckv_n"], dkvf, ta=True, tk=640, name=tag + "b_ukv_dw")
    dqa, _, dg = _rmsnorm_bwd(s["qa"], p["mla_q_norm_g"], dqa_n, width=MLA_Q_RANK, name=tag + "b_mla_qnorm")
    g["mla_q_norm_g"] = dg[0, :MLA_Q_RANK]
    dckv, _, dg = _rmsnorm_bwd(s["ckv"], p["mla_kv_norm_g"], dckv_n, width=MLA_KV_RANK, name=tag + "b_mla_kvnorm")
    g["mla_kv_norm_g"] = dg[0]

    du = jnp.concatenate([dz, dxbc_in, _unheads(dq_sb), _unheads(dk_sb), _unheads(dv_sb), dqa, dckv, dkr, ddtr,
                          jnp.zeros((rows, 128), F32)], axis=1).astype(BF16)
    dhn = _matmul(du, p["w_in"], tb=True, name=tag + "b_in_dx")
    g["w_in"] = _matmul(s["hn"], du, ta=True, tk=640, name=tag + "b_in_dw")
    dh_in, _, dg = _rmsnorm_bwd(s["h_in"], p["norm_mix_g"], dhn, width=D_MODEL, res=dh_mid, mask_pad=True,
                                name=tag + "b_norm_mix")
    g["norm_mix_g"] = dg[0]
    return dh_in, g


def _prepare_layer(full, rep, l):
    a_row = _row(-jnp.exp(rep["ssd_a_log"][l]), 128)
    return {
        "norm_mix_g": _row(rep["norm_mix_g"][l]),
        "w_in": _w_in_padded(full["w_in"][l]),
        "ssd_conv_w": jnp.pad(full["ssd_conv_w"][l], ((0, HALO - SSD_CONV), (0, 0))),
        "ssd_conv_b": _row(rep["ssd_conv_b"][l]),
        "dt_bias": _row(rep["ssd_dt_bias"][l], 128),
        "a_row": a_row,
        "d_exp": _row(jnp.repeat(rep["ssd_d"][l], 64)),
        "ssd_norm_g": _row(rep["ssd_norm_g"][l]),
        "sb_norm_g": _row(rep["sb_norm_g"][l]),
        "mla_q_norm_g": _row(rep["mla_q_norm_g"][l], 256),
        "mla_kv_norm_g": _row(rep["mla_kv_norm_g"][l]),
        "mla_w_uq": _w_uq_perm(full["mla_w_uq"][l]),
        "mla_w_ukv": _w_ukv_perm(full["mla_w_ukv"][l]),
        "mla_norm_g": _row(rep["mla_norm_g"][l]),
        "w_out": full["w_out"][l],
        "norm_ffn_g": _row(rep["norm_ffn_g"][l]),
        "ffn_w_up": full["ffn_w_up"][l],
        "ffn_conv_w": jnp.pad(full["ffn_conv_w"][l], ((0, HALO - FFN_CONV), (0, 0))),
        "ffn_conv_b": _row(rep["ffn_conv_b"][l]),
        "ffn_w_down": full["ffn_w_down"][l],
    }


def _layer_grads_to_full(g):
    out = dict(g)
    out["w_in"] = _w_in_unpadded(g["w_in"])
    out["mla_w_uq"] = _w_uq_unperm(g["mla_w_uq"])
    out["mla_w_ukv"] = _w_ukv_unperm(g["mla_w_ukv"])
    return out


def kernel(x, meta_tokens, norm_mix_g, w_in, ssd_conv_w, ssd_conv_b, ssd_dt_bias, ssd_a_log, ssd_d, ssd_norm_g, sb_norm_g, mla_q_norm_g, mla_kv_norm_g, mla_w_uq, mla_w_ukv, mla_norm_g, w_out, norm_ffn_g, ffn_w_up, ffn_conv_w, ffn_conv_b, ffn_w_down, final_norm_g, loss_target, m_meta_tokens, m_norm_mix_g, m_w_in, m_ssd_conv_w, m_ssd_conv_b, m_ssd_dt_bias, m_ssd_a_log, m_ssd_d, m_ssd_norm_g, m_sb_norm_g, m_mla_q_norm_g, m_mla_kv_norm_g, m_mla_w_uq, m_mla_w_ukv, m_mla_norm_g, m_w_out, m_norm_ffn_g, m_ffn_w_up, m_ffn_conv_w, m_ffn_conv_b, m_ffn_w_down, m_final_norm_g, v_meta_tokens, v_norm_mix_g, v_w_in, v_ssd_conv_w, v_ssd_conv_b, v_ssd_dt_bias, v_ssd_a_log, v_ssd_d, v_ssd_norm_g, v_sb_norm_g, v_mla_q_norm_g, v_mla_kv_norm_g, v_mla_w_uq, v_mla_w_ukv, v_mla_norm_g, v_w_out, v_norm_ffn_g, v_ffn_w_up, v_ffn_conv_w, v_ffn_conv_b, v_ffn_w_down, v_final_norm_g):
    w = dict(meta_tokens=meta_tokens, norm_mix_g=norm_mix_g, w_in=w_in, ssd_conv_w=ssd_conv_w, ssd_conv_b=ssd_conv_b,
             ssd_dt_bias=ssd_dt_bias, ssd_a_log=ssd_a_log, ssd_d=ssd_d, ssd_norm_g=ssd_norm_g, sb_norm_g=sb_norm_g,
             mla_q_norm_g=mla_q_norm_g, mla_kv_norm_g=mla_kv_norm_g, mla_w_uq=mla_w_uq, mla_w_ukv=mla_w_ukv,
             mla_norm_g=mla_norm_g, w_out=w_out, norm_ffn_g=norm_ffn_g, ffn_w_up=ffn_w_up, ffn_conv_w=ffn_conv_w,
             ffn_conv_b=ffn_conv_b, ffn_w_down=ffn_w_down, final_norm_g=final_norm_g)
    mom = dict(meta_tokens=m_meta_tokens, norm_mix_g=m_norm_mix_g, w_in=m_w_in, ssd_conv_w=m_ssd_conv_w,
               ssd_conv_b=m_ssd_conv_b, ssd_dt_bias=m_ssd_dt_bias, ssd_a_log=m_ssd_a_log, ssd_d=m_ssd_d,
               ssd_norm_g=m_ssd_norm_g, sb_norm_g=m_sb_norm_g, mla_q_norm_g=m_mla_q_norm_g,
               mla_kv_norm_g=m_mla_kv_norm_g, mla_w_uq=m_mla_w_uq, mla_w_ukv=m_mla_w_ukv, mla_norm_g=m_mla_norm_g,
               w_out=m_w_out, norm_ffn_g=m_norm_ffn_g, ffn_w_up=m_ffn_w_up, ffn_conv_w=m_ffn_conv_w,
               ffn_conv_b=m_ffn_conv_b, ffn_w_down=m_ffn_w_down, final_norm_g=m_final_norm_g)
    vel = dict(meta_tokens=v_meta_tokens, norm_mix_g=v_norm_mix_g, w_in=v_w_in, ssd_conv_w=v_ssd_conv_w,
               ssd_conv_b=v_ssd_conv_b, ssd_dt_bias=v_ssd_dt_bias, ssd_a_log=v_ssd_a_log, ssd_d=v_ssd_d,
               ssd_norm_g=v_ssd_norm_g, sb_norm_g=v_sb_norm_g, mla_q_norm_g=v_mla_q_norm_g,
               mla_kv_norm_g=v_mla_kv_norm_g, mla_w_uq=v_mla_w_uq, mla_w_ukv=v_mla_w_ukv, mla_norm_g=v_mla_norm_g,
               w_out=v_w_out, norm_ffn_g=v_norm_ffn_g, ffn_w_up=v_ffn_w_up, ffn_conv_w=v_ffn_conv_w,
               ffn_conv_b=v_ffn_conv_b, ffn_w_down=v_ffn_w_down, final_norm_g=v_final_norm_g)

    full = _gather_weights({n: w[n] for n, _ in SHARDED})
    layers = [_prepare_layer(full, w, l) for l in range(DEPTH)]

    seq = x.shape[1]
    rows = BLOCK + seq
    cos_t, sin_t = _rope_tables(rows)
    h = jnp.concatenate([jnp.zeros((N_PAD, D_MODEL), F32), full["meta_tokens"], x[0]], axis=0)

    saved = []
    for l in range(DEPTH):
        h, s = _layer_fwd(h, layers[l], cos_t, sin_t, "l%d_" % l)
        saved.append(s)
    dh, dg_final, loss_part = _final_loss(h, _row(final_norm_g), loss_target[0], name="final_loss")
    loss = lax.psum(loss_part[0, 0], ("x", "y", "c"))

    layer_grads = [None] * DEPTH
    for l in reversed(range(DEPTH)):
        dh, g = _layer_bwd(dh, layers[l], saved[l], cos_t, sin_t, "l%d_" % l)
        layer_grads[l] = _layer_grads_to_full(g)
    grad_x = dh[BLOCK:][None]

    partial = {n: jnp.stack([layer_grads[l][n] for l in range(DEPTH)]) for n in layer_grads[0]}
    partial["meta_tokens"] = dh[N_PAD:BLOCK]
    partial["final_norm_g"] = dg_final[0]

    results = [dict(), dict(), dict(), dict()]
    axes = dict(SHARDED)

    core = lax.axis_index("c")
    halves = [_split_core_chip(partial[n], axes[n]).astype(BF16) for n in BIG]
    theirs = _sibling_swap(halves, name="grad_sibling_swap")
    chip_sums = []
    for n, h2, t4 in zip(BIG, halves, theirs):
        view = (4 * math.prod(w[n].shape[:-1]), w[n].shape[-1])
        mine = lax.dynamic_index_in_dim(h2, core, 0, keepdims=False)
        chip_sums.append(_pair_add(mine.reshape(view), t4.reshape(view), name="grad_pair_add_" + n).reshape(t4.shape))
    got_big = _chip_all_to_all(chip_sums, name="grad_chip_all_to_all")
    for n, g4 in zip(BIG, got_big):
        shp = w[n].shape
        view = (math.prod(shp[:-1]), shp[-1])
        outs = _adamw(g4.reshape((4,) + view), w[n].reshape(view), mom[n].reshape(view), vel[n].reshape(view),
                      name="adamw_" + n)
        for kind in range(4):
            results[kind][n] = outs[kind].reshape(shp)

    send = jnp.concatenate([_pieces(partial[n], axes[n]) for n in SMALL], axis=1)
    pad = (-send.shape[1]) % (8 * 128)
    send = jnp.pad(send, ((0, 0), (0, pad))).reshape(N_DEV, -1, 128)
    got = _exchange([send], gather=False, name="grad_all_to_all_small")[0]
    pack = lambda d: _flat_pack([d[n] for n in SMALL], F32, 8 * 128)
    sh_out = _adamw(got, pack(w), pack(mom), pack(vel), name="adamw_small")

    rep_g = _flat_pack([partial[n] for n in REPLICATED], F32, 8 * 128)
    got_r = _exchange([rep_g], gather=True, name="grad_all_gather")[0]
    packr = lambda d: _flat_pack([d[n] for n in REPLICATED], F32, 8 * 128)
    rep_out = _adamw(got_r, packr(w), packr(mom), packr(vel), name="adamw_replicated")

    for names, outs in ((list(SMALL), sh_out), (list(REPLICATED), rep_out)):
        off = 0
        for n in names:
            size = math.prod(w[n].shape)
            for kind in range(4):
                results[kind][n] = outs[kind].reshape(-1)[off:off + size].reshape(w[n].shape)
            off += size

    return (loss, grad_x, *[results[0][n] for n in WEIGHTS], *[results[1][n] for n in WEIGHTS],
            *[results[2][n] for n in WEIGHTS], *[results[3][n] for n in WEIGHTS])
```

```python
import math

import jax
import jax.numpy as jnp
from jax import lax
from jax.experimental import pallas as pl
from jax.experimental.pallas import tpu as pltpu

F32 = jnp.float32
BF16 = jnp.bfloat16

D_MODEL = 1024
DEPTH = 2
N_META = 16
BLOCK = 128
N_PAD = BLOCK - N_META
EPS = 1e-6
SSD_WIDTH = 512
SSD_XBC = 1024
SSD_CONV = 4
SB_WIDTH = 256
SB_HEAD_DIM = 64
MLA_HEADS = 4
MLA_NOPE = 64
MLA_ROPE = 32
MLA_Q_RANK = 192
MLA_KV_RANK = 128
ROPE_BASE = 10000.0
D_FF = 2816
FFN_CONV = 3
IN_COLS = 2664
N_DEV = 8

ADAM_LR = 0.001
ADAM_B1 = 0.9
ADAM_B2 = 0.999
ADAM_EPS = 1e-08
ADAM_WD = 0.01
ADAM_STEP = 10

U_COLS = 3072
OFF_Z, OFF_XBC, OFF_QSB, OFF_KSB, OFF_VSB, OFF_QA, OFF_CKV, OFF_KR, OFF_DT = (
    0, 512, 1536, 1792, 2048, 2304, 2560, 2688, 2816)

V7X_VMEM_BYTES = 64 * 1024 * 1024
VMEM_LIMIT = (V7X_VMEM_BYTES * 7) // 8
NEG_BIG = -1e30


def _cparams(n_axes):
    return pltpu.CompilerParams(dimension_semantics=("arbitrary",) * n_axes, vmem_limit_bytes=VMEM_LIMIT)


def _tile(n, target, align):
    best = None
    for d in range(align, min(n, target) + 1, align):
        if n % d == 0:
            best = d
    return n if best is None else best


def _sigmoid(x):
    return 1.0 / (1.0 + jnp.exp(-x))


def _softplus(x):
    return jnp.maximum(x, 0.0) + jnp.log(1.0 + jnp.exp(-jnp.abs(x)))


def _dot(a, b):
    return jnp.dot(a, b, preferred_element_type=F32)


def _dot_nt(a, b):
    return lax.dot_general(a, b, (((1,), (1,)), ((), ())), preferred_element_type=F32)


def _hilo(x):
    hi = x.astype(BF16)
    lo = (x - hi.astype(F32)).astype(BF16)
    return hi, lo


def _hilo_dot_l(x, m):
    hi, lo = _hilo(x)
    return _dot(hi, m) + _dot(lo, m)


def _hilo_dot_r(m, x):
    hi, lo = _hilo(x)
    return _dot(m, hi) + _dot(m, lo)


def _hilo_dot_nt(x, m):
    hi, lo = _hilo(x)
    return _dot_nt(hi, m) + _dot_nt(lo, m)


def _ones_where(cond):
    return jnp.where(cond, 1.0, 0.0).astype(BF16)


def _matmul(a, b, *, name, ta=False, tb=False, out_dtype=F32, res=None, mask_pad=False,
            tm=640, tn=1024, tk=1024):
    m_dim = a.shape[1] if ta else a.shape[0]
    k_dim = a.shape[0] if ta else a.shape[1]
    n_dim = b.shape[0] if tb else b.shape[1]
    assert (b.shape[1] if tb else b.shape[0]) == k_dim
    tm = _tile(m_dim, tm, 128)
    tn = _tile(n_dim, tn, 128)
    tk = _tile(k_dim, tk, 128)
    nk = k_dim // tk
    dn = (((0 if ta else 1,), (1 if tb else 0,)), ((), ()))

    use_scratch = nk > 1 and out_dtype != F32

    def body(*refs):
        refs = list(refs)
        a_ref, b_ref = refs[0], refs[1]
        r_ref = refs[2] if res is not None else None
        o_ref = refs[3] if res is not None else refs[2]
        acc = refs[-1] if use_scratch else o_ref
        k = pl.program_id(2)
        part = lax.dot_general(a_ref[...].astype(BF16), b_ref[...].astype(BF16), dn, preferred_element_type=F32)

        def finish(r):
            if res is not None:
                r = r + r_ref[...].astype(F32)
            if mask_pad:
                rows = pl.program_id(0) * tm + lax.broadcasted_iota(jnp.int32, (tm, 1), 0)
                r = jnp.where(rows >= N_PAD, r, 0.0)
            o_ref[...] = r.astype(out_dtype)

        if nk == 1:
            finish(part)
        else:
            @pl.when(k == 0)
            def _():
                acc[...] = part

            @pl.when(jnp.logical_and(k > 0, k < nk - 1))
            def _():
                acc[...] += part

            @pl.when(k == nk - 1)
            def _():
                finish(acc[...] + part)

    a_spec = pl.BlockSpec((tk, tm), lambda i, j, k: (k, i)) if ta else pl.BlockSpec((tm, tk), lambda i, j, k: (i, k))
    b_spec = pl.BlockSpec((tn, tk), lambda i, j, k: (j, k)) if tb else pl.BlockSpec((tk, tn), lambda i, j, k: (k, j))
    o_spec = pl.BlockSpec((tm, tn), lambda i, j, k: (i, j))
    in_specs = [a_spec, b_spec]
    args = [a, b]
    if res is not None:
        in_specs.append(o_spec)
        args.append(res)
    return pl.pallas_call(
        body, name=name, grid=(m_dim // tm, n_dim // tn, nk),
        in_specs=in_specs, out_specs=o_spec,
        out_shape=jax.ShapeDtypeStruct((m_dim, n_dim), out_dtype),
        scratch_shapes=[pltpu.VMEM((tm, tn), F32)] if use_scratch else [],
        compiler_params=_cparams(3),
    )(*args)


def _rmsnorm_fwd(x, g, *, width, name, z=None, out_dtype=None):
    out_dtype = BF16 if out_dtype is None else out_dtype
    rows, w = x.shape
    tr = _tile(rows, 640, 128)
    inv_w = 1.0 / width

    def body(*refs):
        if z is not None:
            x_ref, z_ref, g_ref, o_ref = refs
        else:
            x_ref, g_ref, o_ref = refs
        t = x_ref[...].astype(F32)
        if z is not None:
            zz = z_ref[...]
            t = t * (zz * _sigmoid(zz))
        ms = jnp.sum(t * t, axis=-1, keepdims=True) * inv_w
        o_ref[...] = ((t * lax.rsqrt(ms + EPS)) * g_ref[...]).astype(out_dtype)

    row_spec = pl.BlockSpec((tr, w), lambda i: (i, 0))
    g_spec = pl.BlockSpec((1, w), lambda i: (0, 0))
    in_specs = [row_spec] + ([row_spec] if z is not None else []) + [g_spec]
    args = [x] + ([z] if z is not None else []) + [g]
    return pl.pallas_call(
        body, name=name, grid=(rows // tr,), in_specs=in_specs, out_specs=row_spec,
        out_shape=jax.ShapeDtypeStruct((rows, w), out_dtype), compiler_params=_cparams(1),
    )(*args)


def _rmsnorm_bwd(x, g, dout, *, width, name, z=None, res=None, mask_pad=False):
    rows, w = x.shape
    tr = _tile(rows, 640, 128)
    inv_w = 1.0 / width

    def body(*refs):
        refs = list(refs)
        x_ref = refs.pop(0)
        z_ref = refs.pop(0) if z is not None else None
        g_ref = refs.pop(0)
        do_ref = refs.pop(0)
        r_ref = refs.pop(0) if res is not None else None
        dx_ref = refs.pop(0)
        dz_ref = refs.pop(0) if z is not None else None
        dg_ref = refs.pop(0)
        i = pl.program_id(0)

        @pl.when(i == 0)
        def _():
            dg_ref[...] = jnp.zeros_like(dg_ref)

        xv = x_ref[...].astype(F32)
        t = xv
        if z is not None:
            zz = z_ref[...]
            sig = _sigmoid(zz)
            sl = zz * sig
            t = xv * sl
        ms = jnp.sum(t * t, axis=-1, keepdims=True) * inv_w
        rstd = lax.rsqrt(ms + EPS)
        xhat = t * rstd
        do = do_ref[...].astype(F32)
        dxh = do * g_ref[...]
        c = jnp.sum(dxh * xhat, axis=-1, keepdims=True) * inv_w
        dt = rstd * (dxh - xhat * c)
        dg_ref[...] += jnp.sum(do * xhat, axis=0, keepdims=True)
        if z is not None:
            dz_ref[...] = dt * xv * (sig * (1.0 + zz * (1.0 - sig)))
            dx = dt * sl
        else:
            dx = dt
        if res is not None:
            dx = dx + r_ref[...]
        if mask_pad:
            rws = i * tr + lax.broadcasted_iota(jnp.int32, (tr, 1), 0)
            dx = jnp.where(rws >= N_PAD, dx, 0.0)
        dx_ref[...] = dx

    row_spec = pl.BlockSpec((tr, w), lambda i: (i, 0))
    g_spec = pl.BlockSpec((1, w), lambda i: (0, 0))
    in_specs = [row_spec] + ([row_spec] if z is not None else []) + [g_spec, row_spec] + (
        [row_spec] if res is not None else [])
    args = [x] + ([z] if z is not None else []) + [g, dout] + ([res] if res is not None else [])
    out_specs = [row_spec] + ([row_spec] if z is not None else []) + [g_spec]
    out_shape = [jax.ShapeDtypeStruct((rows, w), F32)] + (
        [jax.ShapeDtypeStruct((rows, w), F32)] if z is not None else []) + [jax.ShapeDtypeStruct((1, w), F32)]
    outs = pl.pallas_call(
        body, name=name, grid=(rows // tr,), in_specs=in_specs, out_specs=out_specs,
        out_shape=out_shape, compiler_params=_cparams(1),
    )(*args)
    if z is not None:
        return outs[0], outs[1], outs[2]
    return outs[0], None, outs[1]


def _final_loss(h, g, target, *, name):
    rows, w = h.shape
    nb = rows // BLOCK
    inv_w = 1.0 / w

    def body(h_ref, g_ref, t_ref, dh_ref, dg_ref, loss_ref):
        i = pl.program_id(0)

        @pl.when(i == 0)
        def _():
            dg_ref[...] = jnp.zeros_like(dg_ref)
            loss_ref[...] = jnp.zeros_like(loss_ref)

        xv = h_ref[...]
        ms = jnp.sum(xv * xv, axis=-1, keepdims=True) * inv_w
        rstd = lax.rsqrt(ms + EPS)
        xhat = xv * rstd
        gv = g_ref[...]
        err = jnp.where(i >= 1, xhat * gv - t_ref[...], 0.0)
        loss_ref[...] += (0.5 * inv_w) * jnp.sum(err * err)
        do = err * inv_w
        dxh = do * gv
        c = jnp.sum(dxh * xhat, axis=-1, keepdims=True) * inv_w
        dh_ref[...] = rstd * (dxh - xhat * c)
        dg_ref[...] += jnp.sum(do * xhat, axis=0, keepdims=True)

    row_spec = pl.BlockSpec((BLOCK, w), lambda i: (i, 0))
    g_spec = pl.BlockSpec((1, w), lambda i: (0, 0))
    return pl.pallas_call(
        body, name=name, grid=(nb,),
        in_specs=[row_spec, g_spec, pl.BlockSpec((BLOCK, w), lambda i: (jnp.maximum(i - 1, 0), 0))],
        out_specs=[row_spec, g_spec, pl.BlockSpec((1, 128), lambda i: (0, 0))],
        out_shape=[jax.ShapeDtypeStruct((rows, w), F32), jax.ShapeDtypeStruct((1, w), F32),
                   jax.ShapeDtypeStruct((1, 128), F32)],
        compiler_params=_cparams(1),
    )(h, g, target)


HALO = 8


def _dwconv_fwd(u, w8, b, *, taps, name):
    rows, ch = u.shape
    tb = _tile(rows, 640, 128)
    tc = _tile(ch, 512, 128)
    hb = tb // HALO

    def body(u_ref, h_ref, w_ref, b_ref, o_ref, buf):
        i = pl.program_id(0)
        buf[0:HALO, :] = jnp.where(i > 0, h_ref[...], 0.0)
        buf[HALO:HALO + tb, :] = u_ref[...]
        acc = jnp.broadcast_to(b_ref[...], (tb, tc))
        for k in range(taps):
            acc = acc + w_ref[k:k + 1, :] * buf[pl.ds(HALO - (taps - 1) + k, tb), :]
        o_ref[...] = acc

    return pl.pallas_call(
        body, name=name, grid=(rows // tb, ch // tc),
        in_specs=[pl.BlockSpec((tb, tc), lambda i, j: (i, j)),
                  pl.BlockSpec((HALO, tc), lambda i, j: (jnp.maximum(i * hb - 1, 0), j)),
                  pl.BlockSpec((HALO, tc), lambda i, j: (0, j)),
                  pl.BlockSpec((1, tc), lambda i, j: (0, j))],
        out_specs=pl.BlockSpec((tb, tc), lambda i, j: (i, j)),
        out_shape=jax.ShapeDtypeStruct((rows, ch), F32),
        scratch_shapes=[pltpu.VMEM((tb + HALO, tc), F32)],
        compiler_params=_cparams(2),
    )(u, u, w8, b)


def _dwconv_bwd(dpre, u, w8, *, taps, name):
    rows, ch = u.shape
    tb = _tile(rows, 640, 128)
    tc = _tile(ch, 512, 128)
    hb = tb // HALO
    nb = rows // tb
    last_halo = rows // HALO - 1

    def body(d_ref, dn_ref, u_ref, up_ref, w_ref, du_ref, dw_ref, db_ref, bufd, bufu):
        i = pl.program_id(1)

        @pl.when(i == 0)
        def _():
            dw_ref[...] = jnp.zeros_like(dw_ref)
            db_ref[...] = jnp.zeros_like(db_ref)

        d = d_ref[...]
        bufd[0:tb, :] = d
        bufd[tb:tb + HALO, :] = jnp.where(i < nb - 1, dn_ref[...], 0.0)
        bufu[0:HALO, :] = jnp.where(i > 0, up_ref[...], 0.0)
        bufu[HALO:HALO + tb, :] = u_ref[...]
        acc = jnp.zeros((tb, tc), F32)
        for k in range(taps):
            acc = acc + w_ref[k:k + 1, :] * bufd[pl.ds(taps - 1 - k, tb), :]
        du_ref[...] = acc
        for k in range(taps):
            dw_ref[k:k + 1, :] += jnp.sum(d * bufu[pl.ds(HALO - (taps - 1) + k, tb), :], axis=0, keepdims=True)
        db_ref[...] += jnp.sum(d, axis=0, keepdims=True)

    return pl.pallas_call(
        body, name=name, grid=(ch // tc, nb),
        in_specs=[pl.BlockSpec((tb, tc), lambda j, i: (i, j)),
                  pl.BlockSpec((HALO, tc), lambda j, i: (jnp.minimum((i + 1) * hb, last_halo), j)),
                  pl.BlockSpec((tb, tc), lambda j, i: (i, j)),
                  pl.BlockSpec((HALO, tc), lambda j, i: (jnp.maximum(i * hb - 1, 0), j)),
                  pl.BlockSpec((HALO, tc), lambda j, i: (0, j))],
        out_specs=[pl.BlockSpec((tb, tc), lambda j, i: (i, j)),
                   pl.BlockSpec((HALO, tc), lambda j, i: (0, j)),
                   pl.BlockSpec((1, tc), lambda j, i: (0, j))],
        out_shape=[jax.ShapeDtypeStruct((rows, ch), F32), jax.ShapeDtypeStruct((HALO, ch), F32),
                   jax.ShapeDtypeStruct((1, ch), F32)],
        scratch_shapes=[pltpu.VMEM((tb + HALO, tc), F32), pltpu.VMEM((tb + HALO, tc), F32)],
        compiler_params=_cparams(2),
    )(dpre, dpre, u, u, w8)


def _ffn_conv_gate_fwd(up, w8, b, *, taps, name):
    rows, c2 = up.shape
    f = c2 // 2
    tb = _tile(rows, 640, 128)
    tc = _tile(f, 512, 128)
    nct = f // tc
    hb = tb // HALO

    def body(u1_ref, u2_ref, h1_ref, h2_ref, w1_ref, w2_ref, b1_ref, b2_ref, o_ref, buf1, buf2):
        i = pl.program_id(0)
        pre = []
        for u_ref, h_ref, w_ref, b_ref, buf in ((u1_ref, h1_ref, w1_ref, b1_ref, buf1),
                                                 (u2_ref, h2_ref, w2_ref, b2_ref, buf2)):
            buf[0:HALO, :] = jnp.where(i > 0, h_ref[...], 0.0)
            buf[HALO:HALO + tb, :] = u_ref[...]
            acc = jnp.broadcast_to(b_ref[...], (tb, tc))
            for k in range(taps):
                acc = acc + w_ref[k:k + 1, :] * buf[pl.ds(HALO - (taps - 1) + k, tb), :]
            pre.append(acc)
        o_ref[...] = (pre[0] * _sigmoid(pre[0]) * pre[1]).astype(BF16)

    main = lambda off: pl.BlockSpec((tb, tc), lambda i, j: (i, j + off))
    halo = lambda off: pl.BlockSpec((HALO, tc), lambda i, j: (jnp.maximum(i * hb - 1, 0), j + off))
    wrow = lambda off: pl.BlockSpec((HALO, tc), lambda i, j: (0, j + off))
    brow = lambda off: pl.BlockSpec((1, tc), lambda i, j: (0, j + off))
    return pl.pallas_call(
        body, name=name, grid=(rows // tb, nct),
        in_specs=[main(0), main(nct), halo(0), halo(nct), wrow(0), wrow(nct), brow(0), brow(nct)],
        out_specs=pl.BlockSpec((tb, tc), lambda i, j: (i, j)),
        out_shape=jax.ShapeDtypeStruct((rows, f), BF16),
        scratch_shapes=[pltpu.VMEM((tb + HALO, tc), F32), pltpu.VMEM((tb + HALO, tc), F32)],
        compiler_params=_cparams(2),
    )(up, up, up, up, w8, w8, b, b)


def _ffn_conv_gate_bwd(dact, up, w8, b, *, taps, name):
    rows, c2 = up.shape
    f = c2 // 2
    tb = _tile(rows, 640, 128)
    tc = _tile(f, 512, 128)
    nct = f // tc
    hb = tb // HALO
    nb = rows // tb
    last_halo = rows // HALO - 1
    ext = tb + HALO

    def body(d_ref, dn_ref, u1_ref, u2_ref, p1_ref, p2_ref, n1_ref, n2_ref, w1_ref, w2_ref, b1_ref, b2_ref,
             du1_ref, du2_ref, dw1_ref, dw2_ref, db1_ref, db2_ref, bufu1, bufu2, bufd1, bufd2):
        i = pl.program_id(1)

        @pl.when(i == 0)
        def _():
            for r in (dw1_ref, dw2_ref, db1_ref, db2_ref):
                r[...] = jnp.zeros_like(r)

        has_next = i < nb - 1
        pre = []
        for u_ref, p_ref, n_ref, w_ref, b_ref, buf in ((u1_ref, p1_ref, n1_ref, w1_ref, b1_ref, bufu1),
                                                       (u2_ref, p2_ref, n2_ref, w2_ref, b2_ref, bufu2)):
            buf[0:HALO, :] = jnp.where(i > 0, p_ref[...], 0.0)
            buf[HALO:HALO + tb, :] = u_ref[...]
            buf[HALO + tb:HALO + ext, :] = jnp.where(has_next, n_ref[...], 0.0)
            acc = jnp.broadcast_to(b_ref[...], (ext, tc))
            for k in range(taps):
                acc = acc + w_ref[k:k + 1, :] * buf[pl.ds(HALO - (taps - 1) + k, ext), :]
            pre.append(acc)
        d_ext = jnp.concatenate([d_ref[...], jnp.where(has_next, dn_ref[...], 0.0)], axis=0)
        sig = _sigmoid(pre[0])
        bufd1[...] = d_ext * pre[1] * (sig * (1.0 + pre[0] * (1.0 - sig)))
        bufd2[...] = d_ext * (pre[0] * sig)
        for w_ref, bufd, bufu, du_ref, dw_ref, db_ref in ((w1_ref, bufd1, bufu1, du1_ref, dw1_ref, db1_ref),
                                                          (w2_ref, bufd2, bufu2, du2_ref, dw2_ref, db2_ref)):
            acc = jnp.zeros((tb, tc), F32)
            for k in range(taps):
                acc = acc + w_ref[k:k + 1, :] * bufd[pl.ds(taps - 1 - k, tb), :]
            du_ref[...] = acc.astype(BF16)
            dmain = bufd[0:tb, :]
            for k in range(taps):
                dw_ref[k:k + 1, :] += jnp.sum(dmain * bufu[pl.ds(HALO - (taps - 1) + k, tb), :], axis=0, keepdims=True)
            db_ref[...] += jnp.sum(dmain, axis=0, keepdims=True)

    main = lambda off: pl.BlockSpec((tb, tc), lambda j, i: (i, j + off))
    prev = lambda off: pl.BlockSpec((HALO, tc), lambda j, i: (jnp.maximum(i * hb - 1, 0), j + off))
    nxt = lambda off: pl.BlockSpec((HALO, tc), lambda j, i: (jnp.minimum((i + 1) * hb, last_halo), j + off))
    wrow = lambda off: pl.BlockSpec((HALO, tc), lambda j, i: (0, j + off))
    brow = lambda off: pl.BlockSpec((1, tc), lambda j, i: (0, j + off))
    half = jax.ShapeDtypeStruct((rows, f), BF16)
    return pl.pallas_call(
        body, name=name, grid=(nct, nb),
        in_specs=[main(0), nxt(0), main(0), main(nct), prev(0), prev(nct), nxt(0), nxt(nct),
                  wrow(0), wrow(nct), brow(0), brow(nct)],
        out_specs=[main(0), main(0), wrow(0), wrow(0), brow(0), brow(0)],
        out_shape=[half, half, jax.ShapeDtypeStruct((HALO, f), F32), jax.ShapeDtypeStruct((HALO, f), F32),
                   jax.ShapeDtypeStruct((1, f), F32), jax.ShapeDtypeStruct((1, f), F32)],
        scratch_shapes=[pltpu.VMEM((ext + HALO, tc), F32), pltpu.VMEM((ext + HALO, tc), F32),
                        pltpu.VMEM((ext, tc), F32), pltpu.VMEM((ext, tc), F32)],
        compiler_params=_cparams(2),
    )(dact, dact, up, up, up, up, up, up, w8, w8, b, b)


def _rope(xr, cos_t, sin_t, *, name, transpose=False):
    rows, w = xr.shape
    tr = _tile(rows, 640, 128)

    def body(x_ref, c_ref, s_ref, o_ref):
        xv = x_ref[...]
        if transpose:
            o_ref[...] = xv * c_ref[...] + pltpu.roll(xv * s_ref[...], 64, 1)
        else:
            o_ref[...] = xv * c_ref[...] + pltpu.roll(xv, 64, 1) * s_ref[...]

    spec = pl.BlockSpec((tr, w), lambda i: (i, 0))
    return pl.pallas_call(
        body, name=name, grid=(rows // tr,), in_specs=[spec, spec, spec], out_specs=spec,
        out_shape=jax.ShapeDtypeStruct((rows, w), F32), compiler_params=_cparams(1),
    )(xr, cos_t, sin_t)


ATT_TQ = 640
GROUPS_WIDE = (8, 4, 2, 1)
GROUPS = (4, 2, 1)


def _grouped_loop(lo, hi, step, carry, *, sizes, descending=False):
    n = jnp.maximum(hi - lo, 0)
    done = 0
    for g in sizes:
        count = lax.div(n - done, jnp.int32(g))

        def body(t, c, g=g, done=done):
            if descending:
                return step([hi - 1 - done - g * t - b for b in range(g)], c)
            return step([lo + done + g * t + b for b in range(g)], c)

        carry = lax.fori_loop(0, count, body, carry)
        done = done + count * g
    return carry


def _tile_iotas(rows=BLOCK):
    r_i = lax.broadcasted_iota(jnp.int32, (rows, BLOCK), 0)
    c_i = lax.broadcasted_iota(jnp.int32, (rows, BLOCK), 1)
    return r_i, c_i


def _key_ranges(i, tq, first=0):
    n_blocks = ((i + 1) * tq + (BLOCK - 1)) >> 7
    first_diag = jnp.maximum((i * tq) >> 7, first)
    return first_diag, n_blocks


def _dot_tn(a, b):
    return lax.dot_general(a, b, (((0,), (0,)), ((), ())), preferred_element_type=F32)


def _cumsum_rhs(pred):
    r = lax.broadcasted_iota(jnp.int32, (BLOCK, 2 * BLOCK), 0)
    c = lax.broadcasted_iota(jnp.int32, (BLOCK, 2 * BLOCK), 1)
    return _ones_where((c >= BLOCK) | pred(r, c))


def _cumsum_dot(x, rhs):
    r = _dot(x.astype(BF16), rhs)
    return r[:, :BLOCK], r[:, BLOCK:]


def _sb_fwd(q, k, v, *, name):
    nh, rows, hd = q.shape
    tq = _tile(rows, ATT_TQ, 8)
    scale = SB_HEAD_DIM ** -0.5

    def body(q_ref, k_ref, v_ref, o_ref, u_ref):
        i = pl.program_id(1)
        r_i, c_i = _tile_iotas(tq)
        m_after = _cumsum_rhs(lambda j, s: j > s)
        qb = (q_ref[0] * scale).astype(BF16)
        rowpos = i * tq + r_i
        first_diag, n_blocks = _key_ranges(i, tq)

        def step(js, carry, masked):
            acc, cu = carry
            offs = [pl.multiple_of(j * BLOCK, BLOCK) for j in js]
            zs = [_dot_nt(qb, k_ref[0, pl.ds(off, BLOCK), :].astype(BF16)) for off in offs]
            sp = [_softplus(z) for z in zs]
            if masked:
                masks = [((off + c_i) < rowpos) & ((off + c_i) >= N_PAD) for off in offs]
                cs = [_cumsum_dot(jnp.where(m_, s, 0.0), m_after) for m_, s in zip(masks, sp)]
            else:
                cs = [_cumsum_dot(s, m_after) for s in sp]
            wgt = []
            for b in range(len(js)):
                w_ = jnp.exp(zs[b] - sp[b] - (cu + cs[b][0]))
                wgt.append(jnp.where(masks[b], w_, 0.0) if masked else w_)
                cu = cu + cs[b][1]
            for b, off in enumerate(offs):
                acc = acc + _dot(wgt[b].astype(BF16), v_ref[0, pl.ds(off, BLOCK), :].astype(BF16))
            return acc, cu

        carry = (jnp.zeros((tq, hd), F32), jnp.zeros((tq, BLOCK), F32))
        carry = _grouped_loop(first_diag, n_blocks, lambda js, c: step(js, c, True), carry, sizes=GROUPS,
                              descending=True)
        acc, cu = _grouped_loop(0, first_diag, lambda js, c: step(js, c, False), carry, sizes=GROUPS_WIDE,
                                descending=True)
        o_ref[0] = acc
        u_ref[0] = -cu

    blk = pl.BlockSpec((1, tq, hd), lambda h, i: (h, i, 0))
    full = pl.BlockSpec((1, rows, hd), lambda h, i: (h, 0, 0))
    return pl.pallas_call(
        body, name=name, grid=(nh, rows // tq), in_specs=[blk, full, full],
        out_specs=[blk, pl.BlockSpec((1, tq, 128), lambda h, i: (h, i, 0))],
        out_shape=[jax.ShapeDtypeStruct((nh, rows, hd), F32), jax.ShapeDtypeStruct((nh, rows, 128), F32)],
        compiler_params=_cparams(2),
    )(q, k, v)


def _sb_bwd(q, k, v, do, u_tot, *, name):
    nh, rows, hd = q.shape
    tq = _tile(rows, ATT_TQ, 8)
    scale = SB_HEAD_DIM ** -0.5

    def body(q_ref, k_ref, v_ref, do_ref, u_ref, dq_ref, dk_ref, dv_ref):
        i = pl.program_id(1)

        @pl.when(i == 0)
        def _():
            dk_ref[...] = jnp.zeros_like(dk_ref)
            dv_ref[...] = jnp.zeros_like(dv_ref)

        r_i, c_i = _tile_iotas(tq)
        m_incl = _cumsum_rhs(lambda j, s: j <= s)
        m_excl = _cumsum_rhs(lambda j, s: j < s)
        qb = (q_ref[0] * scale).astype(BF16)
        dob = do_ref[0].astype(BF16)
        rowpos = i * tq + r_i
        first_diag, n_blocks = _key_ranges(i, tq)

        def step(js, carry, masked):
            dq, rem, cg = carry
            nb = range(len(js))
            offs = [pl.multiple_of(j * BLOCK, BLOCK) for j in js]
            kbs = [k_ref[0, pl.ds(off, BLOCK), :].astype(BF16) for off in offs]
            vbs = [v_ref[0, pl.ds(off, BLOCK), :].astype(BF16) for off in offs]
            zs = [_dot_nt(qb, kb) for kb in kbs]
            dws = [_dot_nt(dob, vb) for vb in vbs]
            sp = [_softplus(z) for z in zs]
            sig = [jnp.exp(zs[b] - sp[b]) for b in nb]
            if masked:
                masks = [((off + c_i) < rowpos) & ((off + c_i) >= N_PAD) for off in offs]
                cs = [_cumsum_dot(jnp.where(masks[b], sp[b], 0.0), m_incl) for b in nb]
            else:
                cs = [_cumsum_dot(sp[b], m_incl) for b in nb]
            wgt, gg = [], []
            for b in nb:
                w_ = jnp.exp(jnp.minimum(zs[b] - sp[b] - (rem - cs[b][0]), 0.0))
                wgt.append(jnp.where(masks[b], w_, 0.0) if masked else w_)
                gg.append(wgt[b] * dws[b])
                rem = rem - cs[b][1]
            gs = [_cumsum_dot(g_, m_excl) for g_ in gg]
            dzb = []
            for b in nb:
                dz = gg[b] * (1.0 - sig[b]) - sig[b] * (cg + gs[b][0])
                dzb.append((jnp.where(masks[b], dz, 0.0) if masked else dz).astype(BF16))
                cg = cg + gs[b][1]
            for b, off in enumerate(offs):
                dq = dq + _dot(dzb[b], kbs[b])
                dk_ref[0, pl.ds(off, BLOCK), :] += _dot_tn(dzb[b], qb)
                dv_ref[0, pl.ds(off, BLOCK), :] += _dot_tn(wgt[b].astype(BF16), dob)
            return dq, rem, cg

        carry = (jnp.zeros((tq, hd), F32), -u_ref[0], jnp.zeros((tq, BLOCK), F32))
        carry = _grouped_loop(0, first_diag, lambda js, c: step(js, c, False), carry, sizes=GROUPS)
        dq, _, _ = _grouped_loop(first_diag, n_blocks, lambda js, c: step(js, c, True), carry, sizes=GROUPS)
        dq_ref[0] = dq * scale

    blk = pl.BlockSpec((1, tq, hd), lambda h, i: (h, i, 0))
    full = pl.BlockSpec((1, rows, hd), lambda h, i: (h, 0, 0))
    ublk = pl.BlockSpec((1, tq, 128), lambda h, i: (h, i, 0))
    sds = jax.ShapeDtypeStruct((nh, rows, hd), F32)
    return pl.pallas_call(
        body, name=name, grid=(nh, rows // tq), in_specs=[blk, full, full, blk, ublk],
        out_specs=[blk, full, full], out_shape=[sds, sds, sds], compiler_params=_cparams(2),
    )(q, k, v, do, u_tot)


def _mla_fwd(q, k, v_ones, *, name):
    nh, rows, dk = q.shape
    dv = v_ones.shape[2] // 2
    tq = _tile(rows, ATT_TQ, 8)
    scale = (MLA_NOPE + MLA_ROPE) ** -0.5

    def body(q_ref, k_ref, v_ref, o_ref, lse_ref):
        i = pl.program_id(1)
        r_i, c_i = _tile_iotas(tq)
        qb = q_ref[0].astype(BF16)
        rowpos = i * tq + r_i
        first_diag, n_blocks = _key_ranges(i, tq, 1)

        def step(js, carry, masked):
            m, acc = carry
            offs = [pl.multiple_of(j * BLOCK, BLOCK) for j in js]
            ss = [_dot_nt(qb, k_ref[0, pl.ds(off, BLOCK), :].astype(BF16)) * scale for off in offs]
            if masked:
                ss = [jnp.where(((off + c_i) <= rowpos) & ((off + c_i) >= N_PAD), s, NEG_BIG)
                      for off, s in zip(offs, ss)]
            m_new = m
            for s in ss:
                m_new = jnp.maximum(m_new, jnp.max(s, axis=1, keepdims=True))
            ps = [jnp.exp(s - m_new).astype(BF16) for s in ss]
            acc = jnp.exp(m - m_new) * acc
            for off, p in zip(offs, ps):
                acc = acc + _dot(p, v_ref[0, pl.ds(off, BLOCK), :].astype(BF16))
            return m_new, acc

        carry = (jnp.full((tq, 1), NEG_BIG, F32), jnp.zeros((tq, 2 * dv), F32))
        carry = step([0], carry, True)
        carry = _grouped_loop(1, first_diag, lambda js, c: step(js, c, False), carry, sizes=GROUPS_WIDE)
        m, acc = _grouped_loop(first_diag, n_blocks, lambda js, c: step(js, c, True), carry, sizes=GROUPS)
        o_ref[0] = (acc / pltpu.roll(acc, dv, 1))[:, :dv]
        lse_ref[0] = m + jnp.log(jnp.where(c_i >= dv, acc, 1.0))

    qblk = pl.BlockSpec((1, tq, dk), lambda h, i: (h, i, 0))
    kfull = pl.BlockSpec((1, rows, dk), lambda h, i: (h, 0, 0))
    return pl.pallas_call(
        body, name=name, grid=(nh, rows // tq), in_specs=[qblk, kfull, kfull],
        out_specs=[pl.BlockSpec((1, tq, dv), lambda h, i: (h, i, 0)),
                   pl.BlockSpec((1, tq, 128), lambda h, i: (h, i, 0))],
        out_shape=[jax.ShapeDtypeStruct((nh, rows, dv), F32), jax.ShapeDtypeStruct((nh, rows, 128), F32)],
        compiler_params=_cparams(2),
    )(q, k, v_ones)


def _mla_bwd(q, k, v, o, lse, do, *, name):
    nh, rows, dk = q.shape
    dv = v.shape[2]
    tq = _tile(rows, ATT_TQ, 8)
    scale = (MLA_NOPE + MLA_ROPE) ** -0.5

    def body(q_ref, k_ref, v_ref, o_ref, lse_ref, do_ref, dq_ref, dk_ref, dv_ref):
        i = pl.program_id(1)

        @pl.when(i == 0)
        def _():
            dk_ref[...] = jnp.zeros_like(dk_ref)
            dv_ref[...] = jnp.zeros_like(dv_ref)

        r_i, c_i = _tile_iotas(tq)
        qb = q_ref[0].astype(BF16)
        dov = do_ref[0]
        dob = dov.astype(BF16)
        delta = jnp.sum(dov * o_ref[0], axis=1, keepdims=True)
        lse = lse_ref[0][:, BLOCK - 1:BLOCK]
        rowpos = i * tq + r_i
        first_diag, n_blocks = _key_ranges(i, tq, 1)

        def step(js, dq, masked):
            nb = range(len(js))
            offs = [pl.multiple_of(j * BLOCK, BLOCK) for j in js]
            kbs = [k_ref[0, pl.ds(off, BLOCK), :].astype(BF16) for off in offs]
            ss = [_dot_nt(qb, kb) for kb in kbs]
            dps = [_dot_nt(dob, v_ref[0, pl.ds(off, BLOCK), :].astype(BF16)) for off in offs]
            ps = [jnp.exp(jnp.minimum(s * scale - lse, 0.0)) for s in ss]
            if masked:
                ps = [jnp.where(((off + c_i) <= rowpos) & ((off + c_i) >= N_PAD), p, 0.0) for off, p in zip(offs, ps)]
            dss = [(ps[b] * (dps[b] - delta) * scale).astype(BF16) for b in nb]
            for b, off in enumerate(offs):
                dq = dq + _dot(dss[b], kbs[b])
                dk_ref[0, pl.ds(off, BLOCK), :] += _dot_tn(dss[b], qb)
                dv_ref[0, pl.ds(off, BLOCK), :] += _dot_tn(ps[b].astype(BF16), dob)
            return dq

        dq = step([0], jnp.zeros((tq, dk), F32), True)
        dq = _grouped_loop(1, first_diag, lambda js, c: step(js, c, False), dq, sizes=GROUPS_WIDE)
        dq_ref[0] = _grouped_loop(first_diag, n_blocks, lambda js, c: step(js, c, True), dq, sizes=GROUPS)

    qblk = pl.BlockSpec((1, tq, dk), lambda h, i: (h, i, 0))
    vblk = pl.BlockSpec((1, tq, dv), lambda h, i: (h, i, 0))
    lblk = pl.BlockSpec((1, tq, 128), lambda h, i: (h, i, 0))
    kfull = pl.BlockSpec((1, rows, dk), lambda h, i: (h, 0, 0))
    vfull = pl.BlockSpec((1, rows, dv), lambda h, i: (h, 0, 0))
    return pl.pallas_call(
        body, name=name, grid=(nh, rows // tq), in_specs=[qblk, kfull, vfull, vblk, lblk, vblk],
        out_specs=[qblk, kfull, vfull],
        out_shape=[jax.ShapeDtypeStruct((nh, rows, dk), F32), jax.ShapeDtypeStruct((nh, rows, dk), F32),
                   jax.ShapeDtypeStruct((nh, rows, dv), F32)],
        compiler_params=_cparams(2),
    )(q, k, v, o, lse, do)


def _ssd_consts():
    r_i, c_i = _tile_iotas()
    eh = lax.broadcasted_iota(jnp.int32, (BLOCK, SSD_WIDTH), 0)
    ec = lax.broadcasted_iota(jnp.int32, (BLOCK, SSD_WIDTH), 1)
    expand = _ones_where(lax.shift_right_logical(ec, 6) == eh)
    return r_i, c_i, expand


def _ssd_common(pre_v, dtr_v, bias_v, a_v, chunk, r_i, c_i, expand):
    lower = r_i >= c_i
    sig_pre = _sigmoid(pre_v)
    xbc = pre_v * sig_pre
    xs = xbc[:, :SSD_WIDTH]
    valid = (chunk * BLOCK + lax.broadcasted_iota(jnp.int32, (BLOCK, 1), 0)) >= N_PAD
    dt_in = dtr_v + bias_v
    dtv = jnp.where(valid, _softplus(dt_in), 0.0)
    d_a = dtv * a_v
    acs = _hilo_dot_r(_ones_where(lower), d_a)
    acs_t = acs.T
    dt_exp = _hilo_dot_l(dtv, expand)
    acs_exp = _hilo_dot_l(acs, expand)
    a_last = acs_exp[BLOCK - 1:BLOCK, :]
    ea = jnp.exp(acs_exp)
    e_l = jnp.exp(a_last - acs_exp)
    ea_l = jnp.exp(a_last)
    return lower, sig_pre, xbc, xs, valid, dt_in, dtv, acs, acs_t, dt_exp, ea, e_l, ea_l


def _decay(acs, acs_t, h, lower):
    col = acs[:, h:h + 1]
    row = acs_t[h:h + 1, :]
    return jnp.where(lower, jnp.exp(jnp.minimum(col - row, 0.0)), 0.0)


def _ssd_fwd(pre, dtr, bias_row, a_row, d_exp, *, name):
    rows = pre.shape[0]
    nc = rows // BLOCK

    def body(pre_ref, dtr_ref, bias_ref, a_ref, dexp_ref, y_ref, st_ref, state):
        c = pl.program_id(0)

        @pl.when(c == 0)
        def _():
            state[...] = jnp.zeros_like(state)

        r_i, c_i, expand = _ssd_consts()
        lane_lo = c_i < 64
        (lower, _, xbc, xs, _, _, _, acs, acs_t, dt_exp, ea, e_l, ea_l) = _ssd_common(
            pre_ref[...], dtr_ref[...], bias_ref[...], a_ref[...], c, r_i, c_i, expand)
        xin = xs * dt_exp
        for g in range(2):
            bg = xbc[:, 512 + 128 * g:640 + 128 * g]
            cg = xbc[:, 768 + 128 * g:896 + 128 * g]
            bb = bg.astype(BF16)
            cbf = cg.astype(BF16)
            cb = _dot_nt(cbf, bb)
            bt = bg.T.astype(BF16)
            for pp in range(2):
                p = 2 * g + pp
                sl = slice(128 * p, 128 * p + 128)
                xp = xin[:, sl]
                xb = xp.astype(BF16)
                rs = [_dot((cb * _decay(acs, acs_t, 2 * p + hh, lower)).astype(BF16), xb) for hh in range(2)]
                ydiag = jnp.where(lane_lo, rs[0], rs[1])
                s_in = state[p]
                st_ref[0, p] = s_in
                yoff = ea[:, sl] * _dot(cbf, s_in.astype(BF16))
                y_ref[:, sl] = ydiag + yoff + xs[:, sl] * dexp_ref[:, sl]
                state[p] = ea_l[:, sl] * s_in + _dot(bt, (xp * e_l[:, sl]).astype(BF16))

    vec = pl.BlockSpec((1, 128), lambda c: (0, 0))
    return pl.pallas_call(
        body, name=name, grid=(nc,),
        in_specs=[pl.BlockSpec((BLOCK, SSD_XBC), lambda c: (c, 0)),
                  pl.BlockSpec((BLOCK, 128), lambda c: (c, 0)), vec, vec,
                  pl.BlockSpec((1, SSD_WIDTH), lambda c: (0, 0))],
        out_specs=[pl.BlockSpec((BLOCK, SSD_WIDTH), lambda c: (c, 0)),
                   pl.BlockSpec((1, 4, 128, 128), lambda c: (c, 0, 0, 0))],
        out_shape=[jax.ShapeDtypeStruct((rows, SSD_WIDTH), F32), jax.ShapeDtypeStruct((nc, 4, 128, 128), F32)],
        scratch_shapes=[pltpu.VMEM((4, 128, 128), F32)],
        compiler_params=_cparams(1),
    )(pre, dtr, bias_row, a_row, d_exp)


def _ssd_bwd(pre, dtr, bias_row, a_row, d_exp, states, dy, *, name):
    rows = pre.shape[0]
    nc = rows // BLOCK

    def body(pre_ref, dtr_ref, bias_ref, a_ref, dexp_ref, st_ref, dy_ref,
             dpre_ref, ddtr_ref, dbias_ref, da_ref, dd_ref, dstate, q_buf, dx_buf):
        step = pl.program_id(0)
        c = nc - 1 - step

        @pl.when(step == 0)
        def _():
            dstate[...] = jnp.zeros_like(dstate)
            dbias_ref[...] = jnp.zeros_like(dbias_ref)
            da_ref[...] = jnp.zeros_like(da_ref)
            dd_ref[...] = jnp.zeros_like(dd_ref)

        r_i, c_i, expand = _ssd_consts()
        lane_lo = c_i < 64
        last_row = r_i == BLOCK - 1
        pre_v = pre_ref[...]
        (lower, sig_pre, xbc, xs, valid, dt_in, dtv, acs, acs_t, dt_exp, ea, e_l, ea_l) = _ssd_common(
            pre_v, dtr_ref[...], bias_ref[...], a_ref[...], c, r_i, c_i, expand)
        dsilu = sig_pre * (1.0 + pre_v * (1.0 - sig_pre))
        xin = xs * dt_exp
        dyv = dy_ref[...]
        d_acs_diag = jnp.zeros((BLOCK, BLOCK), F32)
        for g in range(2):
            bg = xbc[:, 512 + 128 * g:640 + 128 * g]
            cg = xbc[:, 768 + 128 * g:896 + 128 * g]
            bb = bg.astype(BF16)
            cbf = cg.astype(BF16)
            cb = _dot_nt(cbf, bb)
            ct = cg.T.astype(BF16)
            dcb = jnp.zeros((BLOCK, BLOCK), F32)
            dbg = jnp.zeros((BLOCK, BLOCK), F32)
            dcg = jnp.zeros((BLOCK, BLOCK), F32)
            for pp in range(2):
                p = 2 * g + pp
                sl = slice(128 * p, 128 * p + 128)
                xp = xin[:, sl]
                xb = xp.astype(BF16)
                dyp = dyv[:, sl]
                dyb = dyp.astype(BF16)
                dxs_ = []
                for hh in range(2):
                    dec = _decay(acs, acs_t, 2 * p + hh, lower)
                    wm = cb * dec
                    dxs_.append(_dot(wm.T.astype(BF16), dyb))
                    half = lane_lo if hh == 0 else jnp.logical_not(lane_lo)
                    dwm = _dot_nt(jnp.where(half, dyp, 0.0).astype(BF16), xb)
                    dcb = dcb + dwm * dec
                    dseg = dwm * wm
                    dcol = jnp.sum(dseg, axis=1, keepdims=True) - jnp.sum(dseg.T, axis=1, keepdims=True)
                    d_acs_diag = jnp.where(c_i == 2 * p + hh, dcol, d_acs_diag)
                dxdiag =jnp.where(lane_lo, dxs_[0], dxs_[1])
                s_in = st_ref[0, p]
                sb = s_in.astype(BF16)
                ds_out = dstate[p]
                dsb = ds_out.astype(BF16)
                yoff = ea[:, sl] * _dot(cbf, sb)
                dxst = e_l[:, sl] * _dot(bb, dsb)
                dxp = dxdiag + dxst
                dye = dyp * ea[:, sl]
                dyeb = dye.astype(BF16)
                qp = dyp * yoff - xp * dxst
                lastv = (jnp.sum(xp * dxst, axis=0, keepdims=True)
                         + ea_l[:, sl] * jnp.sum(ds_out * s_in, axis=0, keepdims=True))
                q_buf[:, sl] = jnp.where(last_row, qp + lastv, qp)
                dx_buf[:, sl] = dxp
                dcg = dcg + _dot_nt(dyeb, sb)
                dbg = dbg + _dot_nt((xp * e_l[:, sl]).astype(BF16), dsb)
                dstate[p] = ea_l[:, sl] * ds_out + _dot(ct, dyeb)
            dcg = dcg + _dot(dcb.astype(BF16), bb)
            dbg = dbg + _dot(dcb.T.astype(BF16), cbf)
            bsl = slice(512 + 128 * g, 640 + 128 * g)
            csl = slice(768 + 128 * g, 896 + 128 * g)
            dpre_ref[:, bsl] = dbg * dsilu[:, bsl]
            dpre_ref[:, csl] = dcg * dsilu[:, csl]
        dxall = dx_buf[...]
        dpre_ref[:, :SSD_WIDTH] = (dyv * dexp_ref[...] + dxall * dt_exp) * dsilu[:, :SSD_WIDTH]
        dd_ref[...] += jnp.sum(dyv * xs, axis=0, keepdims=True)
        d_acs = d_acs_diag + _hilo_dot_nt(q_buf[...], expand)
        dd_a = _hilo_dot_r(_ones_where(r_i <= c_i), d_acs)
        ddt = dd_a * a_ref[...] + _hilo_dot_nt(dxall * xs, expand)
        ddt = jnp.where(valid, ddt, 0.0)
        da_ref[...] += jnp.sum(dd_a * dtv, axis=0, keepdims=True)
        ddtr = ddt * _sigmoid(dt_in)
        ddtr_ref[...] = ddtr
        dbias_ref[...] += jnp.sum(ddtr, axis=0, keepdims=True)

    vec = pl.BlockSpec((1, 128), lambda s: (0, 0))
    wide = pl.BlockSpec((1, SSD_WIDTH), lambda s: (0, 0))
    rev = lambda s: (nc - 1 - s, 0)
    return pl.pallas_call(
        body, name=name, grid=(nc,),
        in_specs=[pl.BlockSpec((BLOCK, SSD_XBC), rev), pl.BlockSpec((BLOCK, 128), rev), vec, vec, wide,
                  pl.BlockSpec((1, 4, 128, 128), lambda s: (nc - 1 - s, 0, 0, 0)),
                  pl.BlockSpec((BLOCK, SSD_WIDTH), rev)],
        out_specs=[pl.BlockSpec((BLOCK, SSD_XBC), rev), pl.BlockSpec((BLOCK, 128), rev), vec, vec, wide],
        out_shape=[jax.ShapeDtypeStruct((rows, SSD_XBC), F32), jax.ShapeDtypeStruct((rows, 128), F32),
                   jax.ShapeDtypeStruct((1, 128), F32), jax.ShapeDtypeStruct((1, 128), F32),
                   jax.ShapeDtypeStruct((1, SSD_WIDTH), F32)],
        scratch_shapes=[pltpu.VMEM((4, 128, 128), F32), pltpu.VMEM((BLOCK, SSD_WIDTH), F32),
                        pltpu.VMEM((BLOCK, SSD_WIDTH), F32)],
        compiler_params=_cparams(1),
    )(pre, dtr, bias_row, a_row, d_exp, states, dy)


def _peer(xi, yi, ci, k):
    px = (1 - xi) if (k >> 2) & 1 else xi
    py = (1 - yi) if (k >> 1) & 1 else yi
    pc = (1 - ci) if k & 1 else ci
    return (px, py, pc), 4 * px + 2 * py + pc


def _exchange(xs, *, gather, name):
    n = len(xs)
    n_peers = N_DEV - 1
    out_shape = [jax.ShapeDtypeStruct((N_DEV,) + x.shape if gather else x.shape, x.dtype) for x in xs]

    def body(*refs):
        x_refs, o_refs = refs[:n], refs[n:2 * n]
        send_sems, recv_sems, local_sems = refs[2 * n:]
        xi, yi, ci = lax.axis_index("x"), lax.axis_index("y"), lax.axis_index("c")
        me = 4 * xi + 2 * yi + ci

        def copy(a, k, src_idx, dst_idx, peer):
            src = x_refs[a] if gather else x_refs[a].at[src_idx]
            return pltpu.make_async_remote_copy(
                src_ref=src, dst_ref=o_refs[a].at[dst_idx], send_sem=send_sems.at[a * n_peers + k - 1],
                recv_sem=recv_sems.at[a * n_peers + k - 1], device_id=peer, device_id_type=pl.DeviceIdType.MESH)

        local = [pltpu.make_async_copy(x_refs[a] if gather else x_refs[a].at[me], o_refs[a].at[me], local_sems.at[a])
                 for a in range(n)]
        for cp in local:
            cp.start()
        sends = []
        for k in range(1, N_DEV):
            peer, pidx = _peer(xi, yi, ci, k)
            for a in range(n):
                sends.append(copy(a, k, pidx, me, peer))
                sends[-1].start()
        for k in range(1, N_DEV):
            peer, pidx = _peer(xi, yi, ci, k)
            for a in range(n):
                copy(a, k, pidx, pidx, peer).wait_recv()
        for cp in sends:
            cp.wait_send()
        for cp in local:
            cp.wait()

    hbm = pl.BlockSpec(memory_space=pltpu.HBM)
    return pl.pallas_call(
        body, name=name, out_shape=out_shape, in_specs=[hbm] * n, out_specs=[hbm] * n,
        scratch_shapes=[pltpu.SemaphoreType.DMA((n * n_peers,)), pltpu.SemaphoreType.DMA((n * n_peers,)),
                        pltpu.SemaphoreType.DMA((n,))],
    )(*xs)


def _other_chips(xi, yi):
    return [(1 - xi, yi), (xi, 1 - yi), (1 - xi, 1 - yi)]


def _gather_two_level(xs, *, name):
    n = len(xs)

    def body(*refs):
        x_refs, o_refs = refs[:n], refs[n:2 * n]
        send_sems, recv_sems, local_sems = refs[2 * n:]
        xi, yi, ci = lax.axis_index("x"), lax.axis_index("y"), lax.axis_index("c")
        me, sibling = (xi, yi, ci), (xi, yi, 1 - ci)
        chips = _other_chips(xi, yi)

        def slot(px, py, pc):
            return 4 * px + 2 * py + pc

        def copy(a, k, block, to, from_input=False):
            return pltpu.make_async_remote_copy(
                src_ref=x_refs[a] if from_input else o_refs[a].at[slot(*block)], dst_ref=o_refs[a].at[slot(*block)],
                send_sem=send_sems.at[7 * a + k], recv_sem=recv_sems.at[7 * a + k],
                device_id=to, device_id_type=pl.DeviceIdType.MESH)

        local = [pltpu.make_async_copy(x_refs[a], o_refs[a].at[slot(*me)], local_sems.at[a]) for a in range(n)]
        for cp in local:
            cp.start()
        sends = []
        for a in range(n):
            sends.append(copy(a, 0, me, sibling, from_input=True))
            sends += [copy(a, 1 + j, me, (*chip, ci), from_input=True) for j, chip in enumerate(chips)]
        for cp in sends:
            cp.start()
        for j, chip in enumerate(chips):
            for a in range(n):
                copy(a, 1 + j, (*chip, ci), me).wait_recv()
                sends.append(copy(a, 4 + j, (*chip, ci), sibling))
                sends[-1].start()
        for a in range(n):
            copy(a, 0, sibling, me).wait_recv()
            for j, chip in enumerate(chips):
                copy(a, 4 + j, (*chip, 1 - ci), me).wait_recv()
        for cp in sends:
            cp.wait_send()
        for cp in local:
            cp.wait()

    hbm = pl.BlockSpec(memory_space=pltpu.HBM)
    return pl.pallas_call(
        body, name=name, out_shape=[jax.ShapeDtypeStruct((N_DEV,) + x.shape, x.dtype) for x in xs],
        in_specs=[hbm] * n, out_specs=[hbm] * n,
        scratch_shapes=[pltpu.SemaphoreType.DMA((7 * n,)), pltpu.SemaphoreType.DMA((7 * n,)),
                        pltpu.SemaphoreType.DMA((n,))],
    )(*xs)


def _sibling_swap(xs, *, name):
    n = len(xs)

    def body(*refs):
        x_refs, o_refs = refs[:n], refs[n:2 * n]
        send_sems, recv_sems = refs[2 * n:]
        xi, yi, ci = lax.axis_index("x"), lax.axis_index("y"), lax.axis_index("c")
        copies = [pltpu.make_async_remote_copy(
            src_ref=x_refs[a].at[1 - ci], dst_ref=o_refs[a], send_sem=send_sems.at[a], recv_sem=recv_sems.at[a],
            device_id=(xi, yi, 1 - ci), device_id_type=pl.DeviceIdType.MESH) for a in range(n)]
        for cp in copies:
            cp.start()
        for cp in copies:
            cp.wait()

    hbm = pl.BlockSpec(memory_space=pltpu.HBM)
    return pl.pallas_call(
        body, name=name, out_shape=[jax.ShapeDtypeStruct(x.shape[1:], x.dtype) for x in xs],
        in_specs=[hbm] * n, out_specs=[hbm] * n,
        scratch_shapes=[pltpu.SemaphoreType.DMA((n,)), pltpu.SemaphoreType.DMA((n,))],
    )(*xs)


def _chip_all_to_all(xs, *, name):
    n = len(xs)

    def body(*refs):
        x_refs, o_refs = refs[:n], refs[n:2 * n]
        send_sems, recv_sems, local_sems = refs[2 * n:]
        xi, yi, ci = lax.axis_index("x"), lax.axis_index("y"), lax.axis_index("c")
        mine = 2 * xi + yi
        chips = _other_chips(xi, yi)

        def copy(a, j, src_slot, dst_slot, chip):
            return pltpu.make_async_remote_copy(
                src_ref=x_refs[a].at[src_slot], dst_ref=o_refs[a].at[dst_slot], send_sem=send_sems.at[3 * a + j],
                recv_sem=recv_sems.at[3 * a + j], device_id=(*chip, ci), device_id_type=pl.DeviceIdType.MESH)

        local = [pltpu.make_async_copy(x_refs[a].at[mine], o_refs[a].at[mine], local_sems.at[a]) for a in range(n)]
        for cp in local:
            cp.start()
        sends = [copy(a, j, 2 * chip[0] + chip[1], mine, chip) for j, chip in enumerate(chips) for a in range(n)]
        for cp in sends:
            cp.start()
        for j, chip in enumerate(chips):
            for a in range(n):
                copy(a, j, mine, 2 * chip[0] + chip[1], chip).wait_recv()
        for cp in sends:
            cp.wait_send()
        for cp in local:
            cp.wait()

    hbm = pl.BlockSpec(memory_space=pltpu.HBM)
    return pl.pallas_call(
        body, name=name, out_shape=[jax.ShapeDtypeStruct(x.shape, x.dtype) for x in xs],
        in_specs=[hbm] * n, out_specs=[hbm] * n,
        scratch_shapes=[pltpu.SemaphoreType.DMA((3 * n,)), pltpu.SemaphoreType.DMA((3 * n,)),
                        pltpu.SemaphoreType.DMA((n,))],
    )(*xs)


def _pair_add(a, b, *, name):
    rows, cols = a.shape
    lanes = -(-cols // 128) * 128
    tr = _tile(rows, max(16, (512 * 1024) // lanes), 16)

    def body(a_ref, b_ref, o_ref):
        o_ref[...] = (a_ref[...].astype(F32) + b_ref[...].astype(F32)).astype(BF16)

    spec = pl.BlockSpec((tr, cols), lambda i: (i, 0))
    return pl.pallas_call(
        body, name=name, grid=(rows // tr,), in_specs=[spec, spec], out_specs=spec,
        out_shape=jax.ShapeDtypeStruct((rows, cols), BF16), compiler_params=_cparams(1),
    )(a, b)


def _adamw(gs, w, m, v, *, name):
    n_slots = gs.shape[0]
    rows, cols = w.shape
    lanes = -(-cols // 128) * 128
    tr = _tile(rows, max(16, (128 * 1024) // lanes), 16 if gs.dtype == BF16 else 8)

    def body(g_ref, w_ref, m_ref, v_ref, go_ref, d_ref, mo_ref, vo_ref):
        g = g_ref[0].astype(F32)
        for j in range(1, n_slots):
            g = g + g_ref[j].astype(F32)
        m2 = ADAM_B1 * m_ref[...] + (1.0 - ADAM_B1) * g
        v2 = ADAM_B2 * v_ref[...] + (1.0 - ADAM_B2) * (g * g)
        m_hat = m2 / (1.0 - ADAM_B1 ** ADAM_STEP)
        v_hat = v2 / (1.0 - ADAM_B2 ** ADAM_STEP)
        go_ref[...] = g
        d_ref[...] = -ADAM_LR * (m_hat / (jnp.sqrt(v_hat) + ADAM_EPS) + ADAM_WD * w_ref[...])
        mo_ref[...] = m2
        vo_ref[...] = v2

    spec = pl.BlockSpec((tr, cols), lambda i: (i, 0))
    sds = jax.ShapeDtypeStruct((rows, cols), F32)
    return pl.pallas_call(
        body, name=name, grid=(rows // tr,),
        in_specs=[pl.BlockSpec((n_slots, tr, cols), lambda i: (0, i, 0)), spec, spec, spec],
        out_specs=[spec, spec, spec, spec], out_shape=[sds, sds, sds, sds], compiler_params=_cparams(1),
    )(gs, w, m, v)


SHARDED = (("meta_tokens", 1), ("w_in", 2), ("ssd_conv_w", 2), ("mla_w_uq", 2), ("mla_w_ukv", 2),
           ("w_out", 1), ("ffn_w_up", 2), ("ffn_conv_w", 2), ("ffn_w_down", 1))
BIG = ("w_in", "w_out", "ffn_w_up", "ffn_w_down")
SMALL = ("meta_tokens", "ssd_conv_w", "mla_w_uq", "mla_w_ukv", "ffn_conv_w")
REPLICATED = ("norm_mix_g", "ssd_conv_b", "ssd_dt_bias", "ssd_a_log", "ssd_d", "ssd_norm_g", "sb_norm_g",
              "mla_q_norm_g", "mla_kv_norm_g", "mla_norm_g", "norm_ffn_g", "ffn_conv_b", "final_norm_g")
WEIGHTS = ("meta_tokens", "norm_mix_g", "w_in", "ssd_conv_w", "ssd_conv_b", "ssd_dt_bias", "ssd_a_log", "ssd_d",
           "ssd_norm_g", "sb_norm_g", "mla_q_norm_g", "mla_kv_norm_g", "mla_w_uq", "mla_w_ukv", "mla_norm_g",
           "w_out", "norm_ffn_g", "ffn_w_up", "ffn_conv_w", "ffn_conv_b", "ffn_w_down", "final_norm_g")


def _flat_pack(arrays, dtype, align):
    flat = jnp.concatenate([a.reshape(-1).astype(dtype) for a in arrays])
    pad = (-flat.shape[0]) % align
    return jnp.pad(flat, (0, pad)).reshape(-1, 128)


def _pieces(full, axis):
    shp = full.shape
    t = full.reshape(shp[:axis] + (N_DEV, shp[axis] // N_DEV) + shp[axis + 1:])
    return jnp.moveaxis(t, axis, 0).reshape(N_DEV, -1)


def _unpieces(p8, shard_shape, axis):
    t = p8.reshape((N_DEV,) + shard_shape)
    t = jnp.moveaxis(t, 0, axis)
    return t.reshape(shard_shape[:axis] + (N_DEV * shard_shape[axis],) + shard_shape[axis + 1:])


def _split_core_chip(full, axis):
    shp = full.shape
    t = full.reshape(shp[:axis] + (4, 2, shp[axis] // N_DEV) + shp[axis + 1:])
    return jnp.moveaxis(t, (axis + 1, axis), (0, 1))


def _merge_blocks(b8, axis):
    shard = b8.shape[1:]
    t = jnp.moveaxis(b8, 0, axis)
    return t.reshape(shard[:axis] + (N_DEV * shard[axis],) + shard[axis + 1:])


def _gather_weights(shards):
    axes = dict(SHARDED)
    got = _gather_two_level([shards[n].astype(BF16) for n in BIG], name="gather_big")
    full = {n: _merge_blocks(b8, axes[n]) for n, b8 in zip(BIG, got)}
    packed = _flat_pack([shards[n] for n in SMALL], F32, 8 * 128)
    got = _exchange([packed], gather=True, name="gather_small")[0].reshape(N_DEV, -1)
    off = 0
    for n in SMALL:
        size = math.prod(shards[n].shape)
        full[n] = _unpieces(got[:, off:off + size], shards[n].shape, axes[n])
        off += size
    return full


def _pad_cols(a, width):
    return jnp.pad(a, ((0, 0), (0, width - a.shape[1])))


def _w_in_padded(w):
    kr = w[:, 2632:2664]
    return jnp.concatenate([
        w[:, 0:512], w[:, 512:1536], w[:, 1544:2312], _pad_cols(w[:, 2312:2504], 256), w[:, 2504:2632],
        _pad_cols(kr[:, :16], 64), _pad_cols(kr[:, 16:], 64), _pad_cols(w[:, 1536:1544], 128),
        jnp.zeros((w.shape[0], 128), w.dtype)], axis=1)


def _w_in_unpadded(wp):
    return jnp.concatenate([
        wp[:, 0:512], wp[:, 512:1536], wp[:, OFF_DT:OFF_DT + 8], wp[:, 1536:2304], wp[:, OFF_QA:OFF_QA + 192],
        wp[:, OFF_CKV:OFF_CKV + 128], wp[:, OFF_KR:OFF_KR + 16], wp[:, OFF_KR + 64:OFF_KR + 80]], axis=1)


def _w_uq_perm(w):
    t = w.reshape(MLA_Q_RANK, MLA_HEADS, MLA_NOPE + MLA_ROPE)
    out = jnp.concatenate([t[:, :, :64].reshape(MLA_Q_RANK, 256), t[:, :, 64:80].reshape(MLA_Q_RANK, 64),
                           t[:, :, 80:96].reshape(MLA_Q_RANK, 64)], axis=1)
    return jnp.pad(out, ((0, 256 - MLA_Q_RANK), (0, 0)))


def _w_uq_unperm(wp):
    wp = wp[:MLA_Q_RANK]
    t = jnp.concatenate([wp[:, :256].reshape(MLA_Q_RANK, 4, 64), wp[:, 256:320].reshape(MLA_Q_RANK, 4, 16),
                         wp[:, 320:384].reshape(MLA_Q_RANK, 4, 16)], axis=2)
    return t.reshape(MLA_Q_RANK, 4 * 96)


def _w_ukv_perm(w):
    t = w.reshape(MLA_KV_RANK, MLA_HEADS, 128)
    return jnp.concatenate([t[:, :, :64].reshape(MLA_KV_RANK, 256), t[:, :, 64:].reshape(MLA_KV_RANK, 256)], axis=1)


def _w_ukv_unperm(wp):
    t = jnp.concatenate([wp[:, :256].reshape(MLA_KV_RANK, 4, 64), wp[:, 256:].reshape(MLA_KV_RANK, 4, 64)], axis=2)
    return t.reshape(MLA_KV_RANK, 512)


def _heads(a, hd):
    return jnp.moveaxis(a.reshape(a.shape[0], -1, hd), 1, 0)


def _unheads(a):
    return jnp.moveaxis(a, 0, 1).reshape(a.shape[1], -1)


def _row(v, width=None):
    v = v.reshape(1, -1)
    return v if width is None else _pad_cols(v, width)


def _rope_tables(rows):
    pos = jnp.arange(rows, dtype=F32) - float(N_PAD)
    inv = 1.0 / (ROPE_BASE ** (jnp.arange(0, MLA_ROPE, 2, dtype=F32) / MLA_ROPE))
    ang = pos[:, None] * inv[None, :]
    cos = jnp.tile(jnp.cos(ang), (1, 8))
    sin = jnp.tile(jnp.sin(ang), (1, 4))
    return cos, jnp.concatenate([-sin, sin], axis=1)


def _layer_fwd(h, p, cos_t, sin_t, tag):
    s = {"h_in": h}
    hn = _rmsnorm_fwd(h, p["norm_mix_g"], width=D_MODEL, name=tag + "norm_mix")
    u = _matmul(hn, p["w_in"], name=tag + "in_proj")
    s["hn"], s["u"] = hn, u

    xbc_in = u[:, OFF_XBC:OFF_XBC + SSD_XBC]
    pre = _dwconv_fwd(xbc_in, p["ssd_conv_w"], p["ssd_conv_b"], taps=SSD_CONV, name=tag + "ssd_conv")
    dtr = u[:, OFF_DT:OFF_DT + 128]
    y_ssd, states = _ssd_fwd(pre, dtr, p["dt_bias"], p["a_row"], p["d_exp"], name=tag + "ssd_core")
    zgate = u[:, OFF_Z:OFF_Z + SSD_WIDTH]
    yn_ssd = _rmsnorm_fwd(y_ssd, p["ssd_norm_g"], width=SSD_WIDTH, z=zgate, name=tag + "ssd_norm")
    s.update(xbc_in=xbc_in, pre=pre, dtr=dtr, y_ssd=y_ssd, states=states, zgate=zgate)

    q_sb = _heads(u[:, OFF_QSB:OFF_QSB + SB_WIDTH], SB_HEAD_DIM)
    k_sb = _heads(u[:, OFF_KSB:OFF_KSB + SB_WIDTH], SB_HEAD_DIM).astype(BF16)
    v_sb = _heads(u[:, OFF_VSB:OFF_VSB + SB_WIDTH], SB_HEAD_DIM).astype(BF16)
    o_sb, u_tot = _sb_fwd(q_sb, k_sb, v_sb, name=tag + "sb_attn")
    o_sb_flat = _unheads(o_sb)
    yn_sb = _rmsnorm_fwd(o_sb_flat, p["sb_norm_g"], width=SB_WIDTH, name=tag + "sb_norm")
    s.update(q_sb=q_sb, k_sb=k_sb, v_sb=v_sb, u_tot=u_tot, o_sb_flat=o_sb_flat)

    qa = u[:, OFF_QA:OFF_QA + 256]
    ckv = u[:, OFF_CKV:OFF_CKV + 128]
    qa_n = _rmsnorm_fwd(qa, p["mla_q_norm_g"], width=MLA_Q_RANK, name=tag + "mla_qnorm")
    ckv_n = _rmsnorm_fwd(ckv, p["mla_kv_norm_g"], width=MLA_KV_RANK, name=tag + "mla_kvnorm")
    qf = _matmul(qa_n, p["mla_w_uq"], name=tag + "mla_uq")
    kvf = _matmul(ckv_n, p["mla_w_ukv"], name=tag + "mla_ukv")
    q_rope = _rope(qf[:, 256:384], cos_t, sin_t, name=tag + "rope_q")
    k_rope = _rope(u[:, OFF_KR:OFF_KR + 128], cos_t, sin_t, name=tag + "rope_k")
    rows = h.shape[0]
    zpad = jnp.zeros((MLA_HEADS, rows, 32), F32)
    qh = jnp.concatenate([_heads(qf[:, :256], 64), _heads(q_rope[:, :64], 16), _heads(q_rope[:, 64:], 16), zpad], axis=2)
    kr_b = jnp.broadcast_to(jnp.concatenate([k_rope[:, 0:16], k_rope[:, 64:80]], axis=1)[None], (MLA_HEADS, rows, 32))
    kh = jnp.concatenate([_heads(kvf[:, :256], 64), kr_b, zpad], axis=2).astype(BF16)
    vh = _heads(kvf[:, 256:], 64).astype(BF16)
    v_ones = jnp.concatenate([vh, jnp.ones_like(vh)], axis=2)
    o_mla, lse = _mla_fwd(qh, kh, v_ones, name=tag + "mla_attn")
    o_mla_flat = _unheads(o_mla)
    yn_mla = _rmsnorm_fwd(o_mla_flat, p["mla_norm_g"], width=256, name=tag + "mla_norm")
    s.update(qa=qa, ckv=ckv, qa_n=qa_n, ckv_n=ckv_n, qh=qh, kh=kh, vh=vh, o_mla=o_mla, lse=lse,
             o_mla_flat=o_mla_flat)

    mix = jnp.concatenate([yn_ssd, yn_sb, yn_mla], axis=1)
    h_mid = _matmul(mix, p["w_out"], res=h, mask_pad=True, name=tag + "out_proj")
    hn2 = _rmsnorm_fwd(h_mid, p["norm_ffn_g"], width=D_MODEL, name=tag + "norm_ffn")
    up = _matmul(hn2, p["ffn_w_up"], tn=1408, name=tag + "ffn_up")
    act = _ffn_conv_gate_fwd(up, p["ffn_conv_w"], p["ffn_conv_b"], taps=FFN_CONV, name=tag + "ffn_conv_gate")
    h_out = _matmul(act, p["ffn_w_down"], res=h_mid, mask_pad=True, tk=1408, name=tag + "ffn_down")
    s.update(mix=mix, h_mid=h_mid, hn2=hn2, up=up, act=act)
    return h_out, s


def _layer_bwd(dh_out, p, s, cos_t, sin_t, tag):
    g = {}
    rows = dh_out.shape[0]
    dact = _matmul(dh_out, p["ffn_w_down"], tb=True, tn=1408, name=tag + "b_down_dx")
    g["ffn_w_down"] = _matmul(s["act"], dh_out, ta=True, tm=1408, tk=640, name=tag + "b_down_dw")
    dup1, dup2, dcw1, dcw2, dcb1, dcb2 = _ffn_conv_gate_bwd(
        dact, s["up"], p["ffn_conv_w"], p["ffn_conv_b"], taps=FFN_CONV, name=tag + "b_ffn_conv_gate")
    g["ffn_conv_w"] = jnp.concatenate([dcw1[:FFN_CONV], dcw2[:FFN_CONV]], axis=1)
    g["ffn_conv_b"] = jnp.concatenate([dcb1[0], dcb2[0]])
    w_up1, w_up2 = p["ffn_w_up"][:, :D_FF], p["ffn_w_up"][:, D_FF:]
    dhn2 = _matmul(dup1, w_up1, tb=True, tk=1408, name=tag + "b_up_dx1")
    dhn2 = _matmul(dup2, w_up2, tb=True, tk=1408, res=dhn2, name=tag + "b_up_dx2")
    g["ffn_w_up"] = jnp.concatenate([_matmul(s["hn2"], dup1, ta=True, tn=1408, tk=640, name=tag + "b_up_dw1"),
                                     _matmul(s["hn2"], dup2, ta=True, tn=1408, tk=640, name=tag + "b_up_dw2")], axis=1)
    dh_mid, _, dg = _rmsnorm_bwd(s["h_mid"], p["norm_ffn_g"], dhn2, width=D_MODEL, res=dh_out, mask_pad=True,
                                 name=tag + "b_norm_ffn")
    g["norm_ffn_g"] = dg[0]

    dmix = _matmul(dh_mid, p["w_out"], tb=True, name=tag + "b_out_dx")
    g["w_out"] = _matmul(s["mix"], dh_mid, ta=True, tk=640, name=tag + "b_out_dw")

    dy_ssd, dz, dg = _rmsnorm_bwd(s["y_ssd"], p["ssd_norm_g"], dmix[:, :SSD_WIDTH], width=SSD_WIDTH, z=s["zgate"],
                                  name=tag + "b_ssd_norm")
    g["ssd_norm_g"] = dg[0]
    dpre, ddtr, dbias, da, dd = _ssd_bwd(s["pre"], s["dtr"], p["dt_bias"], p["a_row"], p["d_exp"], s["states"],
                                         dy_ssd, name=tag + "b_ssd_core")
    g["ssd_dt_bias"] = dbias[0, :8]
    g["ssd_a_log"] = da[0, :8] * p["a_row"][0, :8]
    g["ssd_d"] = dd.reshape(8, 64).sum(axis=1)
    dxbc_in, dcw, dcb_ = _dwconv_bwd(dpre, s["xbc_in"], p["ssd_conv_w"], taps=SSD_CONV, name=tag + "b_ssd_conv")
    g["ssd_conv_w"], g["ssd_conv_b"] = dcw[:SSD_CONV], dcb_[0]

    do_sb_flat, _, dg = _rmsnorm_bwd(s["o_sb_flat"], p["sb_norm_g"], dmix[:, 512:768], width=SB_WIDTH,
                                     name=tag + "b_sb_norm")
    g["sb_norm_g"] = dg[0]
    dq_sb, dk_sb, dv_sb = _sb_bwd(s["q_sb"], s["k_sb"], s["v_sb"], _heads(do_sb_flat, SB_HEAD_DIM), s["u_tot"],
                                  name=tag + "b_sb_attn")

    do_mla_flat, _, dg = _rmsnorm_bwd(s["o_mla_flat"], p["mla_norm_g"], dmix[:, 768:1024], width=256,
                                      name=tag + "b_mla_norm")
    g["mla_norm_g"] = dg[0]
    dqh, dkh, dvh = _mla_bwd(s["qh"], s["kh"], s["vh"], s["o_mla"], s["lse"], _heads(do_mla_flat, 64),
                             name=tag + "b_mla_attn")
    dq_rope_in = jnp.concatenate([_unheads(dqh[:, :, 64:80]), _unheads(dqh[:, :, 80:96])], axis=1)
    dq_r = _rope(dq_rope_in, cos_t, sin_t, transpose=True, name=tag + "b_rope_q")
    dqf = jnp.concatenate([_unheads(dqh[:, :, :64]), dq_r], axis=1)
    dkr_sum = jnp.sum(dkh[:, :, 64:96], axis=0)
    dk_rope_in = jnp.concatenate([_pad_cols(dkr_sum[:, :16], 64), _pad_cols(dkr_sum[:, 16:], 64)], axis=1)
    dkr = _rope(dk_rope_in, cos_t, sin_t, transpose=True, name=tag + "b_rope_k")
    dkvf = jnp.concatenate([_unheads(dkh[:, :, :64]), _unheads(dvh)], axis=1)
    dqa_n = _matmul(dqf, p["mla_w_uq"], tb=True, name=tag + "b_uq_dx")
    g["mla_w_uq"] = _matmul(s["qa_n"], dqf, ta=True, tk=640, name=tag + "b_uq_dw")
    dckv_n = _matmul(dkvf, p["mla_w_ukv"], tb=True, name=tag + "b_ukv_dx")
    g["mla_w_ukv"] = _matmul(s["ckv_n"], dkvf, ta=True, tk=640, name=tag + "b_ukv_dw")
    dqa, _, dg = _rmsnorm_bwd(s["qa"], p["mla_q_norm_g"], dqa_n, width=MLA_Q_RANK, name=tag + "b_mla_qnorm")
    g["mla_q_norm_g"] = dg[0, :MLA_Q_RANK]
    dckv, _, dg = _rmsnorm_bwd(s["ckv"], p["mla_kv_norm_g"], dckv_n, width=MLA_KV_RANK, name=tag + "b_mla_kvnorm")
    g["mla_kv_norm_g"] = dg[0]

    du = jnp.concatenate([dz, dxbc_in, _unheads(dq_sb), _unheads(dk_sb), _unheads(dv_sb), dqa, dckv, dkr, ddtr,
                          jnp.zeros((rows, 128), F32)], axis=1)
    dhn = _matmul(du, p["w_in"], tb=True, name=tag + "b_in_dx")
    g["w_in"] = _matmul(s["hn"], du, ta=True, tk=640, name=tag + "b_in_dw")
    dh_in, _, dg = _rmsnorm_bwd(s["h_in"], p["norm_mix_g"], dhn, width=D_MODEL, res=dh_mid, mask_pad=True,
                                name=tag + "b_norm_mix")
    g["norm_mix_g"] = dg[0]
    return dh_in, g


def _prepare_layer(full, rep, l):
    a_row = _row(-jnp.exp(rep["ssd_a_log"][l]), 128)
    return {
        "norm_mix_g": _row(rep["norm_mix_g"][l]),
        "w_in": _w_in_padded(full["w_in"][l]),
        "ssd_conv_w": jnp.pad(full["ssd_conv_w"][l], ((0, HALO - SSD_CONV), (0, 0))),
        "ssd_conv_b": _row(rep["ssd_conv_b"][l]),
        "dt_bias": _row(rep["ssd_dt_bias"][l], 128),
        "a_row": a_row,
        "d_exp": _row(jnp.repeat(rep["ssd_d"][l], 64)),
        "ssd_norm_g": _row(rep["ssd_norm_g"][l]),
        "sb_norm_g": _row(rep["sb_norm_g"][l]),
        "mla_q_norm_g": _row(rep["mla_q_norm_g"][l], 256),
        "mla_kv_norm_g": _row(rep["mla_kv_norm_g"][l]),
        "mla_w_uq": _w_uq_perm(full["mla_w_uq"][l]),
        "mla_w_ukv": _w_ukv_perm(full["mla_w_ukv"][l]),
        "mla_norm_g": _row(rep["mla_norm_g"][l]),
        "w_out": full["w_out"][l],
        "norm_ffn_g": _row(rep["norm_ffn_g"][l]),
        "ffn_w_up": full["ffn_w_up"][l],
        "ffn_conv_w": jnp.pad(full["ffn_conv_w"][l], ((0, HALO - FFN_CONV), (0, 0))),
        "ffn_conv_b": _row(rep["ffn_conv_b"][l]),
        "ffn_w_down": full["ffn_w_down"][l],
    }


def _layer_grads_to_full(g):
    out = dict(g)
    out["w_in"] = _w_in_unpadded(g["w_in"])
    out["mla_w_uq"] = _w_uq_unperm(g["mla_w_uq"])
    out["mla_w_ukv"] = _w_ukv_unperm(g["mla_w_ukv"])
    return out


def kernel(x, meta_tokens, norm_mix_g, w_in, ssd_conv_w, ssd_conv_b, ssd_dt_bias, ssd_a_log, ssd_d, ssd_norm_g, sb_norm_g, mla_q_norm_g, mla_kv_norm_g, mla_w_uq, mla_w_ukv, mla_norm_g, w_out, norm_ffn_g, ffn_w_up, ffn_conv_w, ffn_conv_b, ffn_w_down, final_norm_g, loss_target, m_meta_tokens, m_norm_mix_g, m_w_in, m_ssd_conv_w, m_ssd_conv_b, m_ssd_dt_bias, m_ssd_a_log, m_ssd_d, m_ssd_norm_g, m_sb_norm_g, m_mla_q_norm_g, m_mla_kv_norm_g, m_mla_w_uq, m_mla_w_ukv, m_mla_norm_g, m_w_out, m_norm_ffn_g, m_ffn_w_up, m_ffn_conv_w, m_ffn_conv_b, m_ffn_w_down, m_final_norm_g, v_meta_tokens, v_norm_mix_g, v_w_in, v_ssd_conv_w, v_ssd_conv_b, v_ssd_dt_bias, v_ssd_a_log, v_ssd_d, v_ssd_norm_g, v_sb_norm_g, v_mla_q_norm_g, v_mla_kv_norm_g, v_mla_w_uq, v_mla_w_ukv, v_mla_norm_g, v_w_out, v_norm_ffn_g, v_ffn_w_up, v_ffn_conv_w, v_ffn_conv_b, v_ffn_w_down, v_final_norm_g):
    w = dict(meta_tokens=meta_tokens, norm_mix_g=norm_mix_g, w_in=w_in, ssd_conv_w=ssd_conv_w, ssd_conv_b=ssd_conv_b,
             ssd_dt_bias=ssd_dt_bias, ssd_a_log=ssd_a_log, ssd_d=ssd_d, ssd_norm_g=ssd_norm_g, sb_norm_g=sb_norm_g,
             mla_q_norm_g=mla_q_norm_g, mla_kv_norm_g=mla_kv_norm_g, mla_w_uq=mla_w_uq, mla_w_ukv=mla_w_ukv,
             mla_norm_g=mla_norm_g, w_out=w_out, norm_ffn_g=norm_ffn_g, ffn_w_up=ffn_w_up, ffn_conv_w=ffn_conv_w,
             ffn_conv_b=ffn_conv_b, ffn_w_down=ffn_w_down, final_norm_g=final_norm_g)
    mom = dict(meta_tokens=m_meta_tokens, norm_mix_g=m_norm_mix_g, w_in=m_w_in, ssd_conv_w=m_ssd_conv_w,
               ssd_conv_b=m_ssd_conv_b, ssd_dt_bias=m_ssd_dt_bias, ssd_a_log=m_ssd_a_log, ssd_d=m_ssd_d,
               ssd_norm_g=m_ssd_norm_g, sb_norm_g=m_sb_norm_g, mla_q_norm_g=m_mla_q_norm_g,
               mla_kv_norm_g=m_mla_kv_norm_g, mla_w_uq=m_mla_w_uq, mla_w_ukv=m_mla_w_ukv, mla_norm_g=m_mla_norm_g,
               w_out=m_w_out, norm_ffn_g=m_norm_ffn_g, ffn_w_up=m_ffn_w_up, ffn_conv_w=m_ffn_conv_w,
               ffn_conv_b=m_ffn_conv_b, ffn_w_down=m_ffn_w_down, final_norm_g=m_final_norm_g)
    vel = dict(meta_tokens=v_meta_tokens, norm_mix_g=v_norm_mix_g, w_in=v_w_in, ssd_conv_w=v_ssd_conv_w,
               ssd_conv_b=v_ssd_conv_b, ssd_dt_bias=v_ssd_dt_bias, ssd_a_log=v_ssd_a_log, ssd_d=v_ssd_d,
               ssd_norm_g=v_ssd_norm_g, sb_norm_g=v_sb_norm_g, mla_q_norm_g=v_mla_q_norm_g,
               mla_kv_norm_g=v_mla_kv_norm_g, mla_w_uq=v_mla_w_uq, mla_w_ukv=v_mla_w_ukv, mla_norm_g=v_mla_norm_g,
               w_out=v_w_out, norm_ffn_g=v_norm_ffn_g, ffn_w_up=v_ffn_w_up, ffn_conv_w=v_ffn_conv_w,
               ffn_conv_b=v_ffn_conv_b, ffn_w_down=v_ffn_w_down, final_norm_g=v_final_norm_g)

    full = _gather_weights({n: w[n] for n, _ in SHARDED})
    layers = [_prepare_layer(full, w, l) for l in range(DEPTH)]

    seq = x.shape[1]
    rows = BLOCK + seq
    cos_t, sin_t = _rope_tables(rows)
    h = jnp.concatenate([jnp.zeros((N_PAD, D_MODEL), F32), full["meta_tokens"], x[0]], axis=0)

    saved = []
    for l in range(DEPTH):
        h, s = _layer_fwd(h, layers[l], cos_t, sin_t, "l%d_" % l)
        saved.append(s)
    dh, dg_final, loss_part = _final_loss(h, _row(final_norm_g), loss_target[0], name="final_loss")
    loss = lax.psum(loss_part[0, 0], ("x", "y", "c"))

    layer_grads = [None] * DEPTH
    for l in reversed(range(DEPTH)):
        dh, g = _layer_bwd(dh, layers[l], saved[l], cos_t, sin_t, "l%d_" % l)
        layer_grads[l] = _layer_grads_to_full(g)
    grad_x = dh[BLOCK:][None]

    partial = {n: jnp.stack([layer_grads[l][n] for l in range(DEPTH)]) for n in layer_grads[0]}
    partial["meta_tokens"] = dh[N_PAD:BLOCK]
    partial["final_norm_g"] = dg_final[0]

    results = [dict(), dict(), dict(), dict()]
    axes = dict(SHARDED)

    core = lax.axis_index("c")
    halves = [_split_core_chip(partial[n], axes[n]).astype(BF16) for n in BIG]
    theirs = _sibling_swap(halves, name="grad_sibling_swap")
    chip_sums = []
    for n, h2, t4 in zip(BIG, halves, theirs):
        view = (4 * math.prod(w[n].shape[:-1]), w[n].shape[-1])
        mine = lax.dynamic_index_in_dim(h2, core, 0, keepdims=False)
        chip_sums.append(_pair_add(mine.reshape(view), t4.reshape(view), name="grad_pair_add_" + n).reshape(t4.shape))
    got_big = _chip_all_to_all(chip_sums, name="grad_chip_all_to_all")
    for n, g4 in zip(BIG, got_big):
        shp = w[n].shape
        view = (math.prod(shp[:-1]), shp[-1])
        outs = _adamw(g4.reshape((4,) + view), w[n].reshape(view), mom[n].reshape(view), vel[n].reshape(view),
                      name="adamw_" + n)
        for kind in range(4):
            results[kind][n] = outs[kind].reshape(shp)

    send = jnp.concatenate([_pieces(partial[n], axes[n]) for n in SMALL], axis=1)
    pad = (-send.shape[1]) % (8 * 128)
    send = jnp.pad(send, ((0, 0), (0, pad))).reshape(N_DEV, -1, 128)
    got = _exchange([send], gather=False, name="grad_all_to_all_small")[0]
    pack = lambda d: _flat_pack([d[n] for n in SMALL], F32, 8 * 128)
    sh_out = _adamw(got, pack(w), pack(mom), pack(vel), name="adamw_small")

    rep_g = _flat_pack([partial[n] for n in REPLICATED], F32, 8 * 128)
    got_r = _exchange([rep_g], gather=True, name="grad_all_gather")[0]
    packr = lambda d: _flat_pack([d[n] for n in REPLICATED], F32, 8 * 128)
    rep_out = _adamw(got_r, packr(w), packr(mom), packr(vel), name="adamw_replicated")

    for names, outs in ((list(SMALL), sh_out), (list(REPLICATED), rep_out)):
        off = 0
        for n in names:
            size = math.prod(w[n].shape)
            for kind in range(4):
                results[kind][n] = outs[kind].reshape(-1)[off:off + size].reshape(w[n].shape)
            off += size

    return (loss, grad_x, *[results[0][n] for n in WEIGHTS], *[results[1][n] for n in WEIGHTS],
            *[results[2][n] for n in WEIGHTS], *[results[3][n] for n in WEIGHTS])
```

```python
import math

import jax
import jax.numpy as jnp
from jax import lax
from jax.experimental import pallas as pl
from jax.experimental.pallas import tpu as pltpu

F32 = jnp.float32
BF16 = jnp.bfloat16

D_MODEL = 1024
DEPTH = 2
N_META = 16
BLOCK = 128
N_PAD = BLOCK - N_META
EPS = 1e-6
SSD_WIDTH = 512
SSD_XBC = 1024
SSD_CONV = 4
SB_WIDTH = 256
SB_HEAD_DIM = 64
MLA_HEADS = 4
MLA_NOPE = 64
MLA_ROPE = 32
MLA_Q_RANK = 192
MLA_KV_RANK = 128
ROPE_BASE = 10000.0
D_FF = 2816
FFN_CONV = 3
IN_COLS = 2664
N_DEV = 8

ADAM_LR = 0.001
ADAM_B1 = 0.9
ADAM_B2 = 0.999
ADAM_EPS = 1e-08
ADAM_WD = 0.01
ADAM_STEP = 10

U_COLS = 3072
OFF_Z, OFF_XBC, OFF_QSB, OFF_KSB, OFF_VSB, OFF_QA, OFF_CKV, OFF_KR, OFF_DT = (
    0, 512, 1536, 1792, 2048, 2304, 2560, 2688, 2816)

V7X_VMEM_BYTES = 64 * 1024 * 1024
VMEM_LIMIT = (V7X_VMEM_BYTES * 7) // 8
NEG_BIG = -1e30


def _cparams(n_axes):
    return pltpu.CompilerParams(dimension_semantics=("arbitrary",) * n_axes, vmem_limit_bytes=VMEM_LIMIT)


def _tile(n, target, align):
    best = None
    for d in range(align, min(n, target) + 1, align):
        if n % d == 0:
            best = d
    return n if best is None else best


def _sigmoid(x):
    return 1.0 / (1.0 + jnp.exp(-x))


def _softplus(x):
    return jnp.maximum(x, 0.0) + jnp.log(1.0 + jnp.exp(-jnp.abs(x)))


def _dot(a, b):
    return jnp.dot(a, b, preferred_element_type=F32)


def _dot_nt(a, b):
    return lax.dot_general(a, b, (((1,), (1,)), ((), ())), preferred_element_type=F32)


def _hilo(x):
    hi = x.astype(BF16)
    lo = (x - hi.astype(F32)).astype(BF16)
    return hi, lo


def _hilo_dot_l(x, m):
    hi, lo = _hilo(x)
    return _dot(hi, m) + _dot(lo, m)


def _hilo_dot_r(m, x):
    hi, lo = _hilo(x)
    return _dot(m, hi) + _dot(m, lo)


def _hilo_dot_nt(x, m):
    hi, lo = _hilo(x)
    return _dot_nt(hi, m) + _dot_nt(lo, m)


def _ones_where(cond):
    return jnp.where(cond, 1.0, 0.0).astype(BF16)


def _matmul(a, b, *, name, ta=False, tb=False, out_dtype=F32, res=None, mask_pad=False,
            tm=640, tn=1024, tk=1024):
    m_dim = a.shape[1] if ta else a.shape[0]
    k_dim = a.shape[0] if ta else a.shape[1]
    n_dim = b.shape[0] if tb else b.shape[1]
    assert (b.shape[1] if tb else b.shape[0]) == k_dim
    tm = _tile(m_dim, tm, 128)
    tn = _tile(n_dim, tn, 128)
    tk = _tile(k_dim, tk, 128)
    nk = k_dim // tk
    dn = (((0 if ta else 1,), (1 if tb else 0,)), ((), ()))

    use_scratch = nk > 1 and out_dtype != F32

    def body(*refs):
        refs = list(refs)
        a_ref, b_ref = refs[0], refs[1]
        r_ref = refs[2] if res is not None else None
        o_ref = refs[3] if res is not None else refs[2]
        acc = refs[-1] if use_scratch else o_ref
        k = pl.program_id(2)
        part = lax.dot_general(a_ref[...].astype(BF16), b_ref[...].astype(BF16), dn, preferred_element_type=F32)

        def finish(r):
            if res is not None:
                r = r + r_ref[...].astype(F32)
            if mask_pad:
                rows = pl.program_id(0) * tm + lax.broadcasted_iota(jnp.int32, (tm, 1), 0)
                r = jnp.where(rows >= N_PAD, r, 0.0)
            o_ref[...] = r.astype(out_dtype)

        if nk == 1:
            finish(part)
        else:
            @pl.when(k == 0)
            def _():
                acc[...] = part

            @pl.when(jnp.logical_and(k > 0, k < nk - 1))
            def _():
                acc[...] += part

            @pl.when(k == nk - 1)
            def _():
                finish(acc[...] + part)

    a_spec = pl.BlockSpec((tk, tm), lambda i, j, k: (k, i)) if ta else pl.BlockSpec((tm, tk), lambda i, j, k: (i, k))
    b_spec = pl.BlockSpec((tn, tk), lambda i, j, k: (j, k)) if tb else pl.BlockSpec((tk, tn), lambda i, j, k: (k, j))
    o_spec = pl.BlockSpec((tm, tn), lambda i, j, k: (i, j))
    in_specs = [a_spec, b_spec]
    args = [a, b]
    if res is not None:
        in_specs.append(o_spec)
        args.append(res)
    return pl.pallas_call(
        body, name=name, grid=(m_dim // tm, n_dim // tn, nk),
        in_specs=in_specs, out_specs=o_spec,
        out_shape=jax.ShapeDtypeStruct((m_dim, n_dim), out_dtype),
        scratch_shapes=[pltpu.VMEM((tm, tn), F32)] if use_scratch else [],
        compiler_params=_cparams(3),
    )(*args)


def _rmsnorm_fwd(x, g, *, width, name, z=None, out_dtype=None):
    out_dtype = BF16 if out_dtype is None else out_dtype
    rows, w = x.shape
    tr = _tile(rows, 640, 128)
    inv_w = 1.0 / width

    def body(*refs):
        if z is not None:
            x_ref, z_ref, g_ref, o_ref = refs
        else:
            x_ref, g_ref, o_ref = refs
        t = x_ref[...].astype(F32)
        if z is not None:
            zz = z_ref[...]
            t = t * (zz * _sigmoid(zz))
        ms = jnp.sum(t * t, axis=-1, keepdims=True) * inv_w
        o_ref[...] = ((t * lax.rsqrt(ms + EPS)) * g_ref[...]).astype(out_dtype)

    row_spec = pl.BlockSpec((tr, w), lambda i: (i, 0))
    g_spec = pl.BlockSpec((1, w), lambda i: (0, 0))
    in_specs = [row_spec] + ([row_spec] if z is not None else []) + [g_spec]
    args = [x] + ([z] if z is not None else []) + [g]
    return pl.pallas_call(
        body, name=name, grid=(rows // tr,), in_specs=in_specs, out_specs=row_spec,
        out_shape=jax.ShapeDtypeStruct((rows, w), out_dtype), compiler_params=_cparams(1),
    )(*args)


def _rmsnorm_bwd(x, g, dout, *, width, name, z=None, res=None, mask_pad=False):
    rows, w = x.shape
    tr = _tile(rows, 640, 128)
    inv_w = 1.0 / width

    def body(*refs):
        refs = list(refs)
        x_ref = refs.pop(0)
        z_ref = refs.pop(0) if z is not None else None
        g_ref = refs.pop(0)
        do_ref = refs.pop(0)
        r_ref = refs.pop(0) if res is not None else None
        dx_ref = refs.pop(0)
        dz_ref = refs.pop(0) if z is not None else None
        dg_ref = refs.pop(0)
        i = pl.program_id(0)

        @pl.when(i == 0)
        def _():
            dg_ref[...] = jnp.zeros_like(dg_ref)

        xv = x_ref[...].astype(F32)
        t = xv
        if z is not None:
            zz = z_ref[...]
            sig = _sigmoid(zz)
            sl = zz * sig
            t = xv * sl
        ms = jnp.sum(t * t, axis=-1, keepdims=True) * inv_w
        rstd = lax.rsqrt(ms + EPS)
        xhat = t * rstd
        do = do_ref[...].astype(F32)
        dxh = do * g_ref[...]
        c = jnp.sum(dxh * xhat, axis=-1, keepdims=True) * inv_w
        dt = rstd * (dxh - xhat * c)
        dg_ref[...] += jnp.sum(do * xhat, axis=0, keepdims=True)
        if z is not None:
            dz_ref[...] = dt * xv * (sig * (1.0 + zz * (1.0 - sig)))
            dx = dt * sl
        else:
            dx = dt
        if res is not None:
            dx = dx + r_ref[...]
        if mask_pad:
            rws = i * tr + lax.broadcasted_iota(jnp.int32, (tr, 1), 0)
            dx = jnp.where(rws >= N_PAD, dx, 0.0)
        dx_ref[...] = dx

    row_spec = pl.BlockSpec((tr, w), lambda i: (i, 0))
    g_spec = pl.BlockSpec((1, w), lambda i: (0, 0))
    in_specs = [row_spec] + ([row_spec] if z is not None else []) + [g_spec, row_spec] + (
        [row_spec] if res is not None else [])
    args = [x] + ([z] if z is not None else []) + [g, dout] + ([res] if res is not None else [])
    out_specs = [row_spec] + ([row_spec] if z is not None else []) + [g_spec]
    out_shape = [jax.ShapeDtypeStruct((rows, w), F32)] + (
        [jax.ShapeDtypeStruct((rows, w), F32)] if z is not None else []) + [jax.ShapeDtypeStruct((1, w), F32)]
    outs = pl.pallas_call(
        body, name=name, grid=(rows // tr,), in_specs=in_specs, out_specs=out_specs,
        out_shape=out_shape, compiler_params=_cparams(1),
    )(*args)
    if z is not None:
        return outs[0], outs[1], outs[2]
    return outs[0], None, outs[1]


def _final_loss(h, g, target, *, name):
    rows, w = h.shape
    nb = rows // BLOCK
    inv_w = 1.0 / w

    def body(h_ref, g_ref, t_ref, dh_ref, dg_ref, loss_ref):
        i = pl.program_id(0)

        @pl.when(i == 0)
        def _():
            dg_ref[...] = jnp.zeros_like(dg_ref)
            loss_ref[...] = jnp.zeros_like(loss_ref)

        xv = h_ref[...]
        ms = jnp.sum(xv * xv, axis=-1, keepdims=True) * inv_w
        rstd = lax.rsqrt(ms + EPS)
        xhat = xv * rstd
        gv = g_ref[...]
        err = jnp.where(i >= 1, xhat * gv - t_ref[...], 0.0)
        loss_ref[...] += (0.5 * inv_w) * jnp.sum(err * err)
        do = err * inv_w
        dxh = do * gv
        c = jnp.sum(dxh * xhat, axis=-1, keepdims=True) * inv_w
        dh_ref[...] = rstd * (dxh - xhat * c)
        dg_ref[...] += jnp.sum(do * xhat, axis=0, keepdims=True)

    row_spec = pl.BlockSpec((BLOCK, w), lambda i: (i, 0))
    g_spec = pl.BlockSpec((1, w), lambda i: (0, 0))
    return pl.pallas_call(
        body, name=name, grid=(nb,),
        in_specs=[row_spec, g_spec, pl.BlockSpec((BLOCK, w), lambda i: (jnp.maximum(i - 1, 0), 0))],
        out_specs=[row_spec, g_spec, pl.BlockSpec((1, 128), lambda i: (0, 0))],
        out_shape=[jax.ShapeDtypeStruct((rows, w), F32), jax.ShapeDtypeStruct((1, w), F32),
                   jax.ShapeDtypeStruct((1, 128), F32)],
        compiler_params=_cparams(1),
    )(h, g, target)


HALO = 8


def _dwconv_fwd(u, w8, b, *, taps, name):
    rows, ch = u.shape
    tb = _tile(rows, 640, 128)
    tc = _tile(ch, 512, 128)
    hb = tb // HALO

    def body(u_ref, h_ref, w_ref, b_ref, o_ref, buf):
        i = pl.program_id(0)
        buf[0:HALO, :] = jnp.where(i > 0, h_ref[...], 0.0)
        buf[HALO:HALO + tb, :] = u_ref[...]
        acc = jnp.broadcast_to(b_ref[...], (tb, tc))
        for k in range(taps):
            acc = acc + w_ref[k:k + 1, :] * buf[pl.ds(HALO - (taps - 1) + k, tb), :]
        o_ref[...] = acc

    return pl.pallas_call(
        body, name=name, grid=(rows // tb, ch // tc),
        in_specs=[pl.BlockSpec((tb, tc), lambda i, j: (i, j)),
                  pl.BlockSpec((HALO, tc), lambda i, j: (jnp.maximum(i * hb - 1, 0), j)),
                  pl.BlockSpec((HALO, tc), lambda i, j: (0, j)),
                  pl.BlockSpec((1, tc), lambda i, j: (0, j))],
        out_specs=pl.BlockSpec((tb, tc), lambda i, j: (i, j)),
        out_shape=jax.ShapeDtypeStruct((rows, ch), F32),
        scratch_shapes=[pltpu.VMEM((tb + HALO, tc), F32)],
        compiler_params=_cparams(2),
    )(u, u, w8, b)


def _dwconv_bwd(dpre, u, w8, *, taps, name):
    rows, ch = u.shape
    tb = _tile(rows, 640, 128)
    tc = _tile(ch, 512, 128)
    hb = tb // HALO
    nb = rows // tb
    last_halo = rows // HALO - 1

    def body(d_ref, dn_ref, u_ref, up_ref, w_ref, du_ref, dw_ref, db_ref, bufd, bufu):
        i = pl.program_id(1)

        @pl.when(i == 0)
        def _():
            dw_ref[...] = jnp.zeros_like(dw_ref)
            db_ref[...] = jnp.zeros_like(db_ref)

        d = d_ref[...]
        bufd[0:tb, :] = d
        bufd[tb:tb + HALO, :] = jnp.where(i < nb - 1, dn_ref[...], 0.0)
        bufu[0:HALO, :] = jnp.where(i > 0, up_ref[...], 0.0)
        bufu[HALO:HALO + tb, :] = u_ref[...]
        acc = jnp.zeros((tb, tc), F32)
        for k in range(taps):
            acc = acc + w_ref[k:k + 1, :] * bufd[pl.ds(taps - 1 - k, tb), :]
        du_ref[...] = acc
        for k in range(taps):
            dw_ref[k:k + 1, :] += jnp.sum(d * bufu[pl.ds(HALO - (taps - 1) + k, tb), :], axis=0, keepdims=True)
        db_ref[...] += jnp.sum(d, axis=0, keepdims=True)

    return pl.pallas_call(
        body, name=name, grid=(ch // tc, nb),
        in_specs=[pl.BlockSpec((tb, tc), lambda j, i: (i, j)),
                  pl.BlockSpec((HALO, tc), lambda j, i: (jnp.minimum((i + 1) * hb, last_halo), j)),
                  pl.BlockSpec((tb, tc), lambda j, i: (i, j)),
                  pl.BlockSpec((HALO, tc), lambda j, i: (jnp.maximum(i * hb - 1, 0), j)),
                  pl.BlockSpec((HALO, tc), lambda j, i: (0, j))],
        out_specs=[pl.BlockSpec((tb, tc), lambda j, i: (i, j)),
                   pl.BlockSpec((HALO, tc), lambda j, i: (0, j)),
                   pl.BlockSpec((1, tc), lambda j, i: (0, j))],
        out_shape=[jax.ShapeDtypeStruct((rows, ch), F32), jax.ShapeDtypeStruct((HALO, ch), F32),
                   jax.ShapeDtypeStruct((1, ch), F32)],
        scratch_shapes=[pltpu.VMEM((tb + HALO, tc), F32), pltpu.VMEM((tb + HALO, tc), F32)],
        compiler_params=_cparams(2),
    )(dpre, dpre, u, u, w8)


def _ffn_conv_gate_fwd(up, w8, b, *, taps, name):
    rows, c2 = up.shape
    f = c2 // 2
    tb = _tile(rows, 640, 128)
    tc = _tile(f, 512, 128)
    nct = f // tc
    hb = tb // HALO

    def body(u1_ref, u2_ref, h1_ref, h2_ref, w1_ref, w2_ref, b1_ref, b2_ref, o_ref, buf1, buf2):
        i = pl.program_id(0)
        pre = []
        for u_ref, h_ref, w_ref, b_ref, buf in ((u1_ref, h1_ref, w1_ref, b1_ref, buf1),
                                                 (u2_ref, h2_ref, w2_ref, b2_ref, buf2)):
            buf[0:HALO, :] = jnp.where(i > 0, h_ref[...], 0.0)
            buf[HALO:HALO + tb, :] = u_ref[...]
            acc = jnp.broadcast_to(b_ref[...], (tb, tc))
            for k in range(taps):
                acc = acc + w_ref[k:k + 1, :] * buf[pl.ds(HALO - (taps - 1) + k, tb), :]
            pre.append(acc)
        o_ref[...] = (pre[0] * _sigmoid(pre[0]) * pre[1]).astype(BF16)

    main = lambda off: pl.BlockSpec((tb, tc), lambda i, j: (i, j + off))
    halo = lambda off: pl.BlockSpec((HALO, tc), lambda i, j: (jnp.maximum(i * hb - 1, 0), j + off))
    wrow = lambda off: pl.BlockSpec((HALO, tc), lambda i, j: (0, j + off))
    brow = lambda off: pl.BlockSpec((1, tc), lambda i, j: (0, j + off))
    return pl.pallas_call(
        body, name=name, grid=(rows // tb, nct),
        in_specs=[main(0), main(nct), halo(0), halo(nct), wrow(0), wrow(nct), brow(0), brow(nct)],
        out_specs=pl.BlockSpec((tb, tc), lambda i, j: (i, j)),
        out_shape=jax.ShapeDtypeStruct((rows, f), BF16),
        scratch_shapes=[pltpu.VMEM((tb + HALO, tc), F32), pltpu.VMEM((tb + HALO, tc), F32)],
        compiler_params=_cparams(2),
    )(up, up, up, up, w8, w8, b, b)


def _ffn_conv_gate_bwd(dact, up, w8, b, *, taps, name):
    rows, c2 = up.shape
    f = c2 // 2
    tb = _tile(rows, 640, 128)
    tc = _tile(f, 512, 128)
    nct = f // tc
    hb = tb // HALO
    nb = rows // tb
    last_halo = rows // HALO - 1
    ext = tb + HALO

    def body(d_ref, dn_ref, u1_ref, u2_ref, p1_ref, p2_ref, n1_ref, n2_ref, w1_ref, w2_ref, b1_ref, b2_ref,
             du1_ref, du2_ref, dw1_ref, dw2_ref, db1_ref, db2_ref, bufu1, bufu2, bufd1, bufd2):
        i = pl.program_id(1)

        @pl.when(i == 0)
        def _():
            for r in (dw1_ref, dw2_ref, db1_ref, db2_ref):
                r[...] = jnp.zeros_like(r)

        has_next = i < nb - 1
        pre = []
        for u_ref, p_ref, n_ref, w_ref, b_ref, buf in ((u1_ref, p1_ref, n1_ref, w1_ref, b1_ref, bufu1),
                                                       (u2_ref, p2_ref, n2_ref, w2_ref, b2_ref, bufu2)):
            buf[0:HALO, :] = jnp.where(i > 0, p_ref[...], 0.0)
            buf[HALO:HALO + tb, :] = u_ref[...]
            buf[HALO + tb:HALO + ext, :] = jnp.where(has_next, n_ref[...], 0.0)
            acc = jnp.broadcast_to(b_ref[...], (ext, tc))
            for k in range(taps):
                acc = acc + w_ref[k:k + 1, :] * buf[pl.ds(HALO - (taps - 1) + k, ext), :]
            pre.append(acc)
        d_ext = jnp.concatenate([d_ref[...], jnp.where(has_next, dn_ref[...], 0.0)], axis=0)
        sig = _sigmoid(pre[0])
        bufd1[...] = d_ext * pre[1] * (sig * (1.0 + pre[0] * (1.0 - sig)))
        bufd2[...] = d_ext * (pre[0] * sig)
        for w_ref, bufd, bufu, du_ref, dw_ref, db_ref in ((w1_ref, bufd1, bufu1, du1_ref, dw1_ref, db1_ref),
                                                          (w2_ref, bufd2, bufu2, du2_ref, dw2_ref, db2_ref)):
            acc = jnp.zeros((tb, tc), F32)
            for k in range(taps):
                acc = acc + w_ref[k:k + 1, :] * bufd[pl.ds(taps - 1 - k, tb), :]
            du_ref[...] = acc.astype(BF16)
            dmain = bufd[0:tb, :]
            for k in range(taps):
                dw_ref[k:k + 1, :] += jnp.sum(dmain * bufu[pl.ds(HALO - (taps - 1) + k, tb), :], axis=0, keepdims=True)
            db_ref[...] += jnp.sum(dmain, axis=0, keepdims=True)

    main = lambda off: pl.BlockSpec((tb, tc), lambda j, i: (i, j + off))
    prev = lambda off: pl.BlockSpec((HALO, tc), lambda j, i: (jnp.maximum(i * hb - 1, 0), j + off))
    nxt = lambda off: pl.BlockSpec((HALO, tc), lambda j, i: (jnp.minimum((i + 1) * hb, last_halo), j + off))
    wrow = lambda off: pl.BlockSpec((HALO, tc), lambda j, i: (0, j + off))
    brow = lambda off: pl.BlockSpec((1, tc), lambda j, i: (0, j + off))
    half = jax.ShapeDtypeStruct((rows, f), BF16)
    return pl.pallas_call(
        body, name=name, grid=(nct, nb),
        in_specs=[main(0), nxt(0), main(0), main(nct), prev(0), prev(nct), nxt(0), nxt(nct),
                  wrow(0), wrow(nct), brow(0), brow(nct)],
        out_specs=[main(0), main(0), wrow(0), wrow(0), brow(0), brow(0)],
        out_shape=[half, half, jax.ShapeDtypeStruct((HALO, f), F32), jax.ShapeDtypeStruct((HALO, f), F32),
                   jax.ShapeDtypeStruct((1, f), F32), jax.ShapeDtypeStruct((1, f), F32)],
        scratch_shapes=[pltpu.VMEM((ext + HALO, tc), F32), pltpu.VMEM((ext + HALO, tc), F32),
                        pltpu.VMEM((ext, tc), F32), pltpu.VMEM((ext, tc), F32)],
        compiler_params=_cparams(2),
    )(dact, dact, up, up, up, up, up, up, w8, w8, b, b)


def _rope(xr, cos_t, sin_t, *, name, transpose=False):
    rows, w = xr.shape
    tr = _tile(rows, 640, 128)

    def body(x_ref, c_ref, s_ref, o_ref):
        xv = x_ref[...]
        if transpose:
            o_ref[...] = xv * c_ref[...] + pltpu.roll(xv * s_ref[...], 64, 1)
        else:
            o_ref[...] = xv * c_ref[...] + pltpu.roll(xv, 64, 1) * s_ref[...]

    spec = pl.BlockSpec((tr, w), lambda i: (i, 0))
    return pl.pallas_call(
        body, name=name, grid=(rows // tr,), in_specs=[spec, spec, spec], out_specs=spec,
        out_shape=jax.ShapeDtypeStruct((rows, w), F32), compiler_params=_cparams(1),
    )(xr, cos_t, sin_t)


ATT_TQ = 640
GROUPS_WIDE = (8, 4, 2, 1)
GROUPS = (4, 2, 1)


def _grouped_loop(lo, hi, step, carry, *, sizes, descending=False):
    n = jnp.maximum(hi - lo, 0)
    done = 0
    for g in sizes:
        count = lax.div(n - done, jnp.int32(g))

        def body(t, c, g=g, done=done):
            if descending:
                return step([hi - 1 - done - g * t - b for b in range(g)], c)
            return step([lo + done + g * t + b for b in range(g)], c)

        carry = lax.fori_loop(0, count, body, carry)
        done = done + count * g
    return carry


def _tile_iotas(rows=BLOCK):
    r_i = lax.broadcasted_iota(jnp.int32, (rows, BLOCK), 0)
    c_i = lax.broadcasted_iota(jnp.int32, (rows, BLOCK), 1)
    return r_i, c_i


def _key_ranges(i, tq, first=0):
    n_blocks = ((i + 1) * tq + (BLOCK - 1)) >> 7
    first_diag = jnp.maximum((i * tq) >> 7, first)
    return first_diag, n_blocks


def _dot_tn(a, b):
    return lax.dot_general(a, b, (((0,), (0,)), ((), ())), preferred_element_type=F32)


def _cumsum_rhs(pred):
    r = lax.broadcasted_iota(jnp.int32, (BLOCK, 2 * BLOCK), 0)
    c = lax.broadcasted_iota(jnp.int32, (BLOCK, 2 * BLOCK), 1)
    return _ones_where((c >= BLOCK) | pred(r, c))


def _cumsum_dot(x, rhs):
    r = _dot(x.astype(BF16), rhs)
    return r[:, :BLOCK], r[:, BLOCK:]


def _sb_fwd(q, k, v, *, name):
    nh, rows, hd = q.shape
    tq = _tile(rows, ATT_TQ, 8)
    scale = SB_HEAD_DIM ** -0.5

    def body(q_ref, k_ref, v_ref, o_ref, u_ref):
        i = pl.program_id(1)
        r_i, c_i = _tile_iotas(tq)
        m_after = _cumsum_rhs(lambda j, s: j > s)
        qb = (q_ref[0] * scale).astype(BF16)
        rowpos = i * tq + r_i
        first_diag, n_blocks = _key_ranges(i, tq)

        def step(js, carry, masked):
            acc, cu = carry
            offs = [pl.multiple_of(j * BLOCK, BLOCK) for j in js]
            zs = [_dot_nt(qb, k_ref[0, pl.ds(off, BLOCK), :].astype(BF16)) for off in offs]
            sp = [_softplus(z) for z in zs]
            if masked:
                masks = [((off + c_i) < rowpos) & ((off + c_i) >= N_PAD) for off in offs]
                cs = [_cumsum_dot(jnp.where(m_, s, 0.0), m_after) for m_, s in zip(masks, sp)]
            else:
                cs = [_cumsum_dot(s, m_after) for s in sp]
            wgt = []
            for b in range(len(js)):
                w_ = jnp.exp(zs[b] - sp[b] - (cu + cs[b][0]))
                wgt.append(jnp.where(masks[b], w_, 0.0) if masked else w_)
                cu = cu + cs[b][1]
            for b, off in enumerate(offs):
                acc = acc + _dot(wgt[b].astype(BF16), v_ref[0, pl.ds(off, BLOCK), :].astype(BF16))
            return acc, cu

        carry = (jnp.zeros((tq, hd), F32), jnp.zeros((tq, BLOCK), F32))
        carry = _grouped_loop(first_diag, n_blocks, lambda js, c: step(js, c, True), carry, sizes=GROUPS,
                              descending=True)
        acc, cu = _grouped_loop(0, first_diag, lambda js, c: step(js, c, False), carry, sizes=GROUPS_WIDE,
                                descending=True)
        o_ref[0] = acc
        u_ref[0] = -cu

    blk = pl.BlockSpec((1, tq, hd), lambda h, i: (h, i, 0))
    full = pl.BlockSpec((1, rows, hd), lambda h, i: (h, 0, 0))
    return pl.pallas_call(
        body, name=name, grid=(nh, rows // tq), in_specs=[blk, full, full],
        out_specs=[blk, pl.BlockSpec((1, tq, 128), lambda h, i: (h, i, 0))],
        out_shape=[jax.ShapeDtypeStruct((nh, rows, hd), F32), jax.ShapeDtypeStruct((nh, rows, 128), F32)],
        compiler_params=_cparams(2),
    )(q, k, v)


def _sb_bwd(q, k, v, do, u_tot, q_t, do_t, *, name):
    nh, rows, hd = q.shape
    tq = _tile(rows, ATT_TQ, 8)
    scale = SB_HEAD_DIM ** -0.5

    def body(q_ref, k_ref, v_ref, do_ref, u_ref, qt_ref, dot_ref, dq_ref, dk_ref, dv_ref):
        i = pl.program_id(1)

        @pl.when(i == 0)
        def _():
            dk_ref[...] = jnp.zeros_like(dk_ref)
            dv_ref[...] = jnp.zeros_like(dv_ref)

        r_i, c_i = _tile_iotas(tq)
        qtb = qt_ref[0]
        dotb = dot_ref[0]
        r_b, c_b = _tile_iotas()
        t_incl = _ones_where(r_b <= c_b)

        def prefix(x):
            r = _dot(x.astype(BF16), t_incl)
            return r, jnp.broadcast_to(r[:, BLOCK - 1:BLOCK], r.shape)

        qb = (q_ref[0] * scale).astype(BF16)
        dob = do_ref[0].astype(BF16)
        rowpos = i * tq + r_i
        first_diag, n_blocks = _key_ranges(i, tq)

        def step(js, carry, masked):
            dq, rem, cg = carry
            nb = range(len(js))
            offs = [pl.multiple_of(j * BLOCK, BLOCK) for j in js]
            kbs = [k_ref[0, pl.ds(off, BLOCK), :].astype(BF16) for off in offs]
            vbs = [v_ref[0, pl.ds(off, BLOCK), :].astype(BF16) for off in offs]
            zs = [_dot_nt(qb, kb) for kb in kbs]
            dws = [_dot_nt(dob, vb) for vb in vbs]
            sp = [_softplus(z) for z in zs]
            sig = [jnp.exp(zs[b] - sp[b]) for b in nb]
            if masked:
                masks = [((off + c_i) < rowpos) & ((off + c_i) >= N_PAD) for off in offs]
                cs = [prefix(jnp.where(masks[b], sp[b], 0.0)) for b in nb]
            else:
                cs = [prefix(sp[b]) for b in nb]
            wgt, gg = [], []
            for b in nb:
                w_ = jnp.exp(jnp.minimum(zs[b] - sp[b] - (rem - cs[b][0]), 0.0))
                wgt.append(jnp.where(masks[b], w_, 0.0) if masked else w_)
                gg.append(wgt[b] * dws[b])
                rem = rem - cs[b][1]
            gs = [prefix(g_) for g_ in gg]
            dzb = []
            for b in nb:
                dz = gg[b] * (1.0 - sig[b]) - sig[b] * (cg + (gs[b][0] - gg[b]))
                dzb.append((jnp.where(masks[b], dz, 0.0) if masked else dz).astype(BF16))
                cg = cg + gs[b][1]
            for b, off in enumerate(offs):
                dq = dq + _dot(dzb[b], kbs[b])
                dk_ref[0, :, pl.ds(off, BLOCK)] += _dot(qtb, dzb[b])
                dv_ref[0, :, pl.ds(off, BLOCK)] += _dot(dotb, wgt[b].astype(BF16))
            return dq, rem, cg

        carry = (jnp.zeros((tq, hd), F32), -u_ref[0], jnp.zeros((tq, BLOCK), F32))
        carry = _grouped_loop(0, first_diag, lambda js, c: step(js, c, False), carry, sizes=GROUPS)
        dq, _, _ = _grouped_loop(first_diag, n_blocks, lambda js, c: step(js, c, True), carry, sizes=GROUPS)
        dq_ref[0] = dq * scale

    blk = pl.BlockSpec((1, tq, hd), lambda h, i: (h, i, 0))
    full = pl.BlockSpec((1, rows, hd), lambda h, i: (h, 0, 0))
    ublk = pl.BlockSpec((1, tq, 128), lambda h, i: (h, i, 0))
    tblk = pl.BlockSpec((1, hd, tq), lambda h, i: (h, 0, i))
    tfull = pl.BlockSpec((1, hd, rows), lambda h, i: (h, 0, 0))
    sds = jax.ShapeDtypeStruct((nh, rows, hd), F32)
    sds_t = jax.ShapeDtypeStruct((nh, hd, rows), F32)
    return pl.pallas_call(
        body, name=name, grid=(nh, rows // tq), in_specs=[blk, full, full, blk, ublk, tblk, tblk],
        out_specs=[blk, tfull, tfull], out_shape=[sds, sds_t, sds_t], compiler_params=_cparams(2),
    )(q, k, v, do, u_tot, q_t, do_t)


def _mla_fwd(q, k, v_ones, *, name):
    nh, rows, dk = q.shape
    dv = v_ones.shape[2] // 2
    tq = _tile(rows, ATT_TQ, 8)
    scale = (MLA_NOPE + MLA_ROPE) ** -0.5

    def body(q_ref, k_ref, v_ref, o_ref, lse_ref):
        i = pl.program_id(1)
        r_i, c_i = _tile_iotas(tq)
        qb = q_ref[0].astype(BF16)
        rowpos = i * tq + r_i
        first_diag, n_blocks = _key_ranges(i, tq, 1)

        def step(js, carry, masked):
            m, acc = carry
            offs = [pl.multiple_of(j * BLOCK, BLOCK) for j in js]
            ss = [_dot_nt(qb, k_ref[0, pl.ds(off, BLOCK), :].astype(BF16)) * scale for off in offs]
            if masked:
                ss = [jnp.where(((off + c_i) <= rowpos) & ((off + c_i) >= N_PAD), s, NEG_BIG)
                      for off, s in zip(offs, ss)]
            m_new = m
            for s in ss:
                m_new = jnp.maximum(m_new, jnp.max(s, axis=1, keepdims=True))
            ps = [jnp.exp(s - m_new).astype(BF16) for s in ss]
            acc = jnp.exp(m - m_new) * acc
            for off, p in zip(offs, ps):
                acc = acc + _dot(p, v_ref[0, pl.ds(off, BLOCK), :].astype(BF16))
            return m_new, acc

        carry = (jnp.full((tq, 1), NEG_BIG, F32), jnp.zeros((tq, 2 * dv), F32))
        carry = step([0], carry, True)
        carry = _grouped_loop(1, first_diag, lambda js, c: step(js, c, False), carry, sizes=GROUPS_WIDE)
        m, acc = _grouped_loop(first_diag, n_blocks, lambda js, c: step(js, c, True), carry, sizes=GROUPS)
        o_ref[0] = (acc / pltpu.roll(acc, dv, 1))[:, :dv]
        lse_ref[0] = m + jnp.log(jnp.where(c_i >= dv, acc, 1.0))

    qblk = pl.BlockSpec((1, tq, dk), lambda h, i: (h, i, 0))
    kfull = pl.BlockSpec((1, rows, dk), lambda h, i: (h, 0, 0))
    return pl.pallas_call(
        body, name=name, grid=(nh, rows // tq), in_specs=[qblk, kfull, kfull],
        out_specs=[pl.BlockSpec((1, tq, dv), lambda h, i: (h, i, 0)),
                   pl.BlockSpec((1, tq, 128), lambda h, i: (h, i, 0))],
        out_shape=[jax.ShapeDtypeStruct((nh, rows, dv), F32), jax.ShapeDtypeStruct((nh, rows, 128), F32)],
        compiler_params=_cparams(2),
    )(q, k, v_ones)


def _mla_bwd(q, k, v, o, lse, do, *, name):
    nh, rows, dk = q.shape
    dv = v.shape[2]
    tq = _tile(rows, ATT_TQ, 8)
    scale = (MLA_NOPE + MLA_ROPE) ** -0.5

    def body(q_ref, k_ref, v_ref, o_ref, lse_ref, do_ref, dq_ref, dk_ref, dv_ref):
        i = pl.program_id(1)

        @pl.when(i == 0)
        def _():
            dk_ref[...] = jnp.zeros_like(dk_ref)
            dv_ref[...] = jnp.zeros_like(dv_ref)

        r_i, c_i = _tile_iotas(tq)
        qb = q_ref[0].astype(BF16)
        dov = do_ref[0]
        dob = dov.astype(BF16)
        delta = jnp.sum(dov * o_ref[0], axis=1, keepdims=True)
        lse = lse_ref[0][:, BLOCK - 1:BLOCK]
        rowpos = i * tq + r_i
        first_diag, n_blocks = _key_ranges(i, tq, 1)

        def step(js, dq, masked):
            nb = range(len(js))
            offs = [pl.multiple_of(j * BLOCK, BLOCK) for j in js]
            kbs = [k_ref[0, pl.ds(off, BLOCK), :].astype(BF16) for off in offs]
            ss = [_dot_nt(qb, kb) for kb in kbs]
            dps = [_dot_nt(dob, v_ref[0, pl.ds(off, BLOCK), :].astype(BF16)) for off in offs]
            ps = [jnp.exp(jnp.minimum(s * scale - lse, 0.0)) for s in ss]
            if masked:
                ps = [jnp.where(((off + c_i) <= rowpos) & ((off + c_i) >= N_PAD), p, 0.0) for off, p in zip(offs, ps)]
            dss = [(ps[b] * (dps[b] - delta) * scale).astype(BF16) for b in nb]
            for b, off in enumerate(offs):
                dq = dq + _dot(dss[b], kbs[b])
                dk_ref[0, pl.ds(off, BLOCK), :] += _dot_tn(dss[b], qb)
                dv_ref[0, pl.ds(off, BLOCK), :] += _dot_tn(ps[b].astype(BF16), dob)
            return dq

        dq = step([0], jnp.zeros((tq, dk), F32), True)
        dq = _grouped_loop(1, first_diag, lambda js, c: step(js, c, False), dq, sizes=GROUPS_WIDE)
        dq_ref[0] = _grouped_loop(first_diag, n_blocks, lambda js, c: step(js, c, True), dq, sizes=GROUPS)

    qblk = pl.BlockSpec((1, tq, dk), lambda h, i: (h, i, 0))
    vblk = pl.BlockSpec((1, tq, dv), lambda h, i: (h, i, 0))
    lblk = pl.BlockSpec((1, tq, 128), lambda h, i: (h, i, 0))
    kfull = pl.BlockSpec((1, rows, dk), lambda h, i: (h, 0, 0))
    vfull = pl.BlockSpec((1, rows, dv), lambda h, i: (h, 0, 0))
    return pl.pallas_call(
        body, name=name, grid=(nh, rows // tq), in_specs=[qblk, kfull, vfull, vblk, lblk, vblk],
        out_specs=[qblk, kfull, vfull],
        out_shape=[jax.ShapeDtypeStruct((nh, rows, dk), F32), jax.ShapeDtypeStruct((nh, rows, dk), F32),
                   jax.ShapeDtypeStruct((nh, rows, dv), F32)],
        compiler_params=_cparams(2),
    )(q, k, v, o, lse, do)


def _ssd_consts():
    r_i, c_i = _tile_iotas()
    eh = lax.broadcasted_iota(jnp.int32, (BLOCK, SSD_WIDTH), 0)
    ec = lax.broadcasted_iota(jnp.int32, (BLOCK, SSD_WIDTH), 1)
    expand = _ones_where(lax.shift_right_logical(ec, 6) == eh)
    return r_i, c_i, expand


def _ssd_common(pre_v, dtr_v, bias_v, a_v, chunk, r_i, c_i, expand):
    lower = r_i >= c_i
    sig_pre = _sigmoid(pre_v)
    xbc = pre_v * sig_pre
    xs = xbc[:, :SSD_WIDTH]
    valid = (chunk * BLOCK + lax.broadcasted_iota(jnp.int32, (BLOCK, 1), 0)) >= N_PAD
    dt_in = dtr_v + bias_v
    dtv = jnp.where(valid, _softplus(dt_in), 0.0)
    d_a = dtv * a_v
    acs = _hilo_dot_r(_ones_where(lower), d_a)
    acs_t = acs.T
    dt_exp = _hilo_dot_l(dtv, expand)
    acs_exp = _hilo_dot_l(acs, expand)
    a_last = acs_exp[BLOCK - 1:BLOCK, :]
    ea = jnp.exp(acs_exp)
    e_l = jnp.exp(a_last - acs_exp)
    ea_l = jnp.exp(a_last)
    return lower, sig_pre, xbc, xs, valid, dt_in, dtv, acs, acs_t, dt_exp, ea, e_l, ea_l


def _decay(acs, acs_t, h, lower):
    col = acs[:, h:h + 1]
    row = acs_t[h:h + 1, :]
    return jnp.where(lower, jnp.exp(jnp.minimum(col - row, 0.0)), 0.0)


def _ssd_fwd(pre, dtr, bias_row, a_row, d_exp, *, name):
    rows = pre.shape[0]
    nc = rows // BLOCK

    def body(pre_ref, dtr_ref, bias_ref, a_ref, dexp_ref, y_ref, st_ref, state):
        c = pl.program_id(0)

        @pl.when(c == 0)
        def _():
            state[...] = jnp.zeros_like(state)

        r_i, c_i, expand = _ssd_consts()
        lane_lo = c_i < 64
        (lower, _, xbc, xs, _, _, _, acs, acs_t, dt_exp, ea, e_l, ea_l) = _ssd_common(
            pre_ref[...], dtr_ref[...], bias_ref[...], a_ref[...], c, r_i, c_i, expand)
        xin = xs * dt_exp
        for g in range(2):
            bg = xbc[:, 512 + 128 * g:640 + 128 * g]
            cg = xbc[:, 768 + 128 * g:896 + 128 * g]
            bb = bg.astype(BF16)
            cbf = cg.astype(BF16)
            cb = _dot_nt(cbf, bb)
            bt = bg.T.astype(BF16)
            for pp in range(2):
                p = 2 * g + pp
                sl = slice(128 * p, 128 * p + 128)
                xp = xin[:, sl]
                xb = xp.astype(BF16)
                rs = [_dot((cb * _decay(acs, acs_t, 2 * p + hh, lower)).astype(BF16), xb) for hh in range(2)]
                ydiag = jnp.where(lane_lo, rs[0], rs[1])
                s_in = state[p]
                st_ref[0, p] = s_in
                yoff = ea[:, sl] * _dot(cbf, s_in.astype(BF16))
                y_ref[:, sl] = ydiag + yoff + xs[:, sl] * dexp_ref[:, sl]
                state[p] = ea_l[:, sl] * s_in + _dot(bt, (xp * e_l[:, sl]).astype(BF16))

    vec = pl.BlockSpec((1, 128), lambda c: (0, 0))
    return pl.pallas_call(
        body, name=name, grid=(nc,),
        in_specs=[pl.BlockSpec((BLOCK, SSD_XBC), lambda c: (c, 0)),
                  pl.BlockSpec((BLOCK, 128), lambda c: (c, 0)), vec, vec,
                  pl.BlockSpec((1, SSD_WIDTH), lambda c: (0, 0))],
        out_specs=[pl.BlockSpec((BLOCK, SSD_WIDTH), lambda c: (c, 0)),
                   pl.BlockSpec((1, 4, 128, 128), lambda c: (c, 0, 0, 0))],
        out_shape=[jax.ShapeDtypeStruct((rows, SSD_WIDTH), F32), jax.ShapeDtypeStruct((nc, 4, 128, 128), F32)],
        scratch_shapes=[pltpu.VMEM((4, 128, 128), F32)],
        compiler_params=_cparams(1),
    )(pre, dtr, bias_row, a_row, d_exp)


def _ssd_bwd(pre, dtr, bias_row, a_row, d_exp, states, dy, *, name):
    rows = pre.shape[0]
    nc = rows // BLOCK

    def body(pre_ref, dtr_ref, bias_ref, a_ref, dexp_ref, st_ref, dy_ref,
             dpre_ref, ddtr_ref, dbias_ref, da_ref, dd_ref, dstate, q_buf, dx_buf):
        step = pl.program_id(0)
        c = nc - 1 - step

        @pl.when(step == 0)
        def _():
            dstate[...] = jnp.zeros_like(dstate)
            dbias_ref[...] = jnp.zeros_like(dbias_ref)
            da_ref[...] = jnp.zeros_like(da_ref)
            dd_ref[...] = jnp.zeros_like(dd_ref)

        r_i, c_i, expand = _ssd_consts()
        lane_lo = c_i < 64
        last_row = r_i == BLOCK - 1
        pre_v = pre_ref[...]
        (lower, sig_pre, xbc, xs, valid, dt_in, dtv, acs, acs_t, dt_exp, ea, e_l, ea_l) = _ssd_common(
            pre_v, dtr_ref[...], bias_ref[...], a_ref[...], c, r_i, c_i, expand)
        dsilu = sig_pre * (1.0 + pre_v * (1.0 - sig_pre))
        xin = xs * dt_exp
        dyv = dy_ref[...]
        d_acs_diag = jnp.zeros((BLOCK, BLOCK), F32)
        for g in range(2):
            bg = xbc[:, 512 + 128 * g:640 + 128 * g]
            cg = xbc[:, 768 + 128 * g:896 + 128 * g]
            bb = bg.astype(BF16)
            cbf = cg.astype(BF16)
            cb = _dot_nt(cbf, bb)
            ct = cg.T.astype(BF16)
            dcb = jnp.zeros((BLOCK, BLOCK), F32)
            dbg = jnp.zeros((BLOCK, BLOCK), F32)
            dcg = jnp.zeros((BLOCK, BLOCK), F32)
            for pp in range(2):
                p = 2 * g + pp
                sl = slice(128 * p, 128 * p + 128)
                xp = xin[:, sl]
                xb = xp.astype(BF16)
                dyp = dyv[:, sl]
                dyb = dyp.astype(BF16)
                dxs_ = []
                for hh in range(2):
                    dec = _decay(acs, acs_t, 2 * p + hh, lower)
                    wm = cb * dec
                    dxs_.append(_dot(wm.T.astype(BF16), dyb))
                    half = lane_lo if hh == 0 else jnp.logical_not(lane_lo)
                    dwm = _dot_nt(jnp.where(half, dyp, 0.0).astype(BF16), xb)
                    dcb = dcb + dwm * dec
                    dseg = dwm * wm
                    dcol = jnp.sum(dseg, axis=1, keepdims=True) - jnp.sum(dseg.T, axis=1, keepdims=True)
                    d_acs_diag = jnp.where(c_i == 2 * p + hh, dcol, d_acs_diag)
                dxdiag =jnp.where(lane_lo, dxs_[0], dxs_[1])
                s_in = st_ref[0, p]
                sb = s_in.astype(BF16)
                ds_out = dstate[p]
                dsb = ds_out.astype(BF16)
                yoff = ea[:, sl] * _dot(cbf, sb)
                dxst = e_l[:, sl] * _dot(bb, dsb)
                dxp = dxdiag + dxst
                dye = dyp * ea[:, sl]
                dyeb = dye.astype(BF16)
                qp = dyp * yoff - xp * dxst
                lastv = (jnp.sum(xp * dxst, axis=0, keepdims=True)
                         + ea_l[:, sl] * jnp.sum(ds_out * s_in, axis=0, keepdims=True))
                q_buf[:, sl] = jnp.where(last_row, qp + lastv, qp)
                dx_buf[:, sl] = dxp
                dcg = dcg + _dot_nt(dyeb, sb)
                dbg = dbg + _dot_nt((xp * e_l[:, sl]).astype(BF16), dsb)
                dstate[p] = ea_l[:, sl] * ds_out + _dot(ct, dyeb)
            dcg = dcg + _dot(dcb.astype(BF16), bb)
            dbg = dbg + _dot(dcb.T.astype(BF16), cbf)
            bsl = slice(512 + 128 * g, 640 + 128 * g)
            csl = slice(768 + 128 * g, 896 + 128 * g)
            dpre_ref[:, bsl] = dbg * dsilu[:, bsl]
            dpre_ref[:, csl] = dcg * dsilu[:, csl]
        dxall = dx_buf[...]
        dpre_ref[:, :SSD_WIDTH] = (dyv * dexp_ref[...] + dxall * dt_exp) * dsilu[:, :SSD_WIDTH]
        dd_ref[...] += jnp.sum(dyv * xs, axis=0, keepdims=True)
        d_acs = d_acs_diag + _hilo_dot_nt(q_buf[...], expand)
        dd_a = _hilo_dot_r(_ones_where(r_i <= c_i), d_acs)
        ddt = dd_a * a_ref[...] + _hilo_dot_nt(dxall * xs, expand)
        ddt = jnp.where(valid, ddt, 0.0)
        da_ref[...] += jnp.sum(dd_a * dtv, axis=0, keepdims=True)
        ddtr = ddt * _sigmoid(dt_in)
        ddtr_ref[...] = ddtr
        dbias_ref[...] += jnp.sum(ddtr, axis=0, keepdims=True)

    vec = pl.BlockSpec((1, 128), lambda s: (0, 0))
    wide = pl.BlockSpec((1, SSD_WIDTH), lambda s: (0, 0))
    rev = lambda s: (nc - 1 - s, 0)
    return pl.pallas_call(
        body, name=name, grid=(nc,),
        in_specs=[pl.BlockSpec((BLOCK, SSD_XBC), rev), pl.BlockSpec((BLOCK, 128), rev), vec, vec, wide,
                  pl.BlockSpec((1, 4, 128, 128), lambda s: (nc - 1 - s, 0, 0, 0)),
                  pl.BlockSpec((BLOCK, SSD_WIDTH), rev)],
        out_specs=[pl.BlockSpec((BLOCK, SSD_XBC), rev), pl.BlockSpec((BLOCK, 128), rev), vec, vec, wide],
        out_shape=[jax.ShapeDtypeStruct((rows, SSD_XBC), F32), jax.ShapeDtypeStruct((rows, 128), F32),
                   jax.ShapeDtypeStruct((1, 128), F32), jax.ShapeDtypeStruct((1, 128), F32),
                   jax.ShapeDtypeStruct((1, SSD_WIDTH), F32)],
        scratch_shapes=[pltpu.VMEM((4, 128, 128), F32), pltpu.VMEM((BLOCK, SSD_WIDTH), F32),
                        pltpu.VMEM((BLOCK, SSD_WIDTH), F32)],
        compiler_params=_cparams(1),
    )(pre, dtr, bias_row, a_row, d_exp, states, dy)


def _peer(xi, yi, ci, k):
    px = (1 - xi) if (k >> 2) & 1 else xi
    py = (1 - yi) if (k >> 1) & 1 else yi
    pc = (1 - ci) if k & 1 else ci
    return (px, py, pc), 4 * px + 2 * py + pc


def _exchange(xs, *, gather, name):
    n = len(xs)
    n_peers = N_DEV - 1
    out_shape = [jax.ShapeDtypeStruct((N_DEV,) + x.shape if gather else x.shape, x.dtype) for x in xs]

    def body(*refs):
        x_refs, o_refs = refs[:n], refs[n:2 * n]
        send_sems, recv_sems, local_sems = refs[2 * n:]
        xi, yi, ci = lax.axis_index("x"), lax.axis_index("y"), lax.axis_index("c")
        me = 4 * xi + 2 * yi + ci

        def copy(a, k, src_idx, dst_idx, peer):
            src = x_refs[a] if gather else x_refs[a].at[src_idx]
            return pltpu.make_async_remote_copy(
                src_ref=src, dst_ref=o_refs[a].at[dst_idx], send_sem=send_sems.at[a * n_peers + k - 1],
                recv_sem=recv_sems.at[a * n_peers + k - 1], device_id=peer, device_id_type=pl.DeviceIdType.MESH)

        local = [pltpu.make_async_copy(x_refs[a] if gather else x_refs[a].at[me], o_refs[a].at[me], local_sems.at[a])
                 for a in range(n)]
        for cp in local:
            cp.start()
        sends = []
        for k in range(1, N_DEV):
            peer, pidx = _peer(xi, yi, ci, k)
            for a in range(n):
                sends.append(copy(a, k, pidx, me, peer))
                sends[-1].start()
        for k in range(1, N_DEV):
            peer, pidx = _peer(xi, yi, ci, k)
            for a in range(n):
                copy(a, k, pidx, pidx, peer).wait_recv()
        for cp in sends:
            cp.wait_send()
        for cp in local:
            cp.wait()

    hbm = pl.BlockSpec(memory_space=pltpu.HBM)
    return pl.pallas_call(
        body, name=name, out_shape=out_shape, in_specs=[hbm] * n, out_specs=[hbm] * n,
        scratch_shapes=[pltpu.SemaphoreType.DMA((n * n_peers,)), pltpu.SemaphoreType.DMA((n * n_peers,)),
                        pltpu.SemaphoreType.DMA((n,))],
    )(*xs)


def _other_chips(xi, yi):
    return [(1 - xi, yi), (xi, 1 - yi), (1 - xi, 1 - yi)]


def _gather_two_level(xs, *, name):
    n = len(xs)

    def body(*refs):
        x_refs, o_refs = refs[:n], refs[n:2 * n]
        send_sems, recv_sems, local_sems = refs[2 * n:]
        xi, yi, ci = lax.axis_index("x"), lax.axis_index("y"), lax.axis_index("c")
        me, sibling = (xi, yi, ci), (xi, yi, 1 - ci)
        chips = _other_chips(xi, yi)

        def slot(px, py, pc):
            return 4 * px + 2 * py + pc

        def copy(a, k, block, to, from_input=False):
            return pltpu.make_async_remote_copy(
                src_ref=x_refs[a] if from_input else o_refs[a].at[slot(*block)], dst_ref=o_refs[a].at[slot(*block)],
                send_sem=send_sems.at[7 * a + k], recv_sem=recv_sems.at[7 * a + k],
                device_id=to, device_id_type=pl.DeviceIdType.MESH)

        local = [pltpu.make_async_copy(x_refs[a], o_refs[a].at[slot(*me)], local_sems.at[a]) for a in range(n)]
        for cp in local:
            cp.start()
        sends = []
        for a in range(n):
            sends.append(copy(a, 0, me, sibling, from_input=True))
            sends += [copy(a, 1 + j, me, (*chip, ci), from_input=True) for j, chip in enumerate(chips)]
        for cp in sends:
            cp.start()
        for j, chip in enumerate(chips):
            for a in range(n):
                copy(a, 1 + j, (*chip, ci), me).wait_recv()
                sends.append(copy(a, 4 + j, (*chip, ci), sibling))
                sends[-1].start()
        for a in range(n):
            copy(a, 0, sibling, me).wait_recv()
            for j, chip in enumerate(chips):
                copy(a, 4 + j, (*chip, 1 - ci), me).wait_recv()
        for cp in sends:
            cp.wait_send()
        for cp in local:
            cp.wait()

    hbm = pl.BlockSpec(memory_space=pltpu.HBM)
    return pl.pallas_call(
        body, name=name, out_shape=[jax.ShapeDtypeStruct((N_DEV,) + x.shape, x.dtype) for x in xs],
        in_specs=[hbm] * n, out_specs=[hbm] * n,
        scratch_shapes=[pltpu.SemaphoreType.DMA((7 * n,)), pltpu.SemaphoreType.DMA((7 * n,)),
                        pltpu.SemaphoreType.DMA((n,))],
    )(*xs)


def _sibling_swap(xs, *, name):
    n = len(xs)

    def body(*refs):
        x_refs, o_refs = refs[:n], refs[n:2 * n]
        send_sems, recv_sems = refs[2 * n:]
        xi, yi, ci = lax.axis_index("x"), lax.axis_index("y"), lax.axis_index("c")
        copies = [pltpu.make_async_remote_copy(
            src_ref=x_refs[a].at[1 - ci], dst_ref=o_refs[a], send_sem=send_sems.at[a], recv_sem=recv_sems.at[a],
            device_id=(xi, yi, 1 - ci), device_id_type=pl.DeviceIdType.MESH) for a in range(n)]
        for cp in copies:
            cp.start()
        for cp in copies:
            cp.wait()

    hbm = pl.BlockSpec(memory_space=pltpu.HBM)
    return pl.pallas_call(
        body, name=name, out_shape=[jax.ShapeDtypeStruct(x.shape[1:], x.dtype) for x in xs],
        in_specs=[hbm] * n, out_specs=[hbm] * n,
        scratch_shapes=[pltpu.SemaphoreType.DMA((n,)), pltpu.SemaphoreType.DMA((n,))],
    )(*xs)


def _chip_all_to_all(xs, *, name):
    n = len(xs)

    def body(*refs):
        x_refs, o_refs = refs[:n], refs[n:2 * n]
        send_sems, recv_sems, local_sems = refs[2 * n:]
        xi, yi, ci = lax.axis_index("x"), lax.axis_index("y"), lax.axis_index("c")
        mine = 2 * xi + yi
        chips = _other_chips(xi, yi)

        def copy(a, j, src_slot, dst_slot, chip):
            return pltpu.make_async_remote_copy(
                src_ref=x_refs[a].at[src_slot], dst_ref=o_refs[a].at[dst_slot], send_sem=send_sems.at[3 * a + j],
                recv_sem=recv_sems.at[3 * a + j], device_id=(*chip, ci), device_id_type=pl.DeviceIdType.MESH)

        local = [pltpu.make_async_copy(x_refs[a].at[mine], o_refs[a].at[mine], local_sems.at[a]) for a in range(n)]
        for cp in local:
            cp.start()
        sends = [copy(a, j, 2 * chip[0] + chip[1], mine, chip) for j, chip in enumerate(chips) for a in range(n)]
        for cp in sends:
            cp.start()
        for j, chip in enumerate(chips):
            for a in range(n):
                copy(a, j, mine, 2 * chip[0] + chip[1], chip).wait_recv()
        for cp in sends:
            cp.wait_send()
        for cp in local:
            cp.wait()

    hbm = pl.BlockSpec(memory_space=pltpu.HBM)
    return pl.pallas_call(
        body, name=name, out_shape=[jax.ShapeDtypeStruct(x.shape, x.dtype) for x in xs],
        in_specs=[hbm] * n, out_specs=[hbm] * n,
        scratch_shapes=[pltpu.SemaphoreType.DMA((3 * n,)), pltpu.SemaphoreType.DMA((3 * n,)),
                        pltpu.SemaphoreType.DMA((n,))],
    )(*xs)


def _pair_add(a, b, *, name):
    rows, cols = a.shape
    lanes = -(-cols // 128) * 128
    tr = _tile(rows, max(16, (512 * 1024) // lanes), 16)

    def body(a_ref, b_ref, o_ref):
        o_ref[...] = (a_ref[...].astype(F32) + b_ref[...].astype(F32)).astype(BF16)

    spec = pl.BlockSpec((tr, cols), lambda i: (i, 0))
    return pl.pallas_call(
        body, name=name, grid=(rows // tr,), in_specs=[spec, spec], out_specs=spec,
        out_shape=jax.ShapeDtypeStruct((rows, cols), BF16), compiler_params=_cparams(1),
    )(a, b)


def _adamw(gs, w, m, v, *, name):
    n_slots = gs.shape[0]
    rows, cols = w.shape
    lanes = -(-cols // 128) * 128
    tr = _tile(rows, max(16, (128 * 1024) // lanes), 16 if gs.dtype == BF16 else 8)

    def body(g_ref, w_ref, m_ref, v_ref, go_ref, d_ref, mo_ref, vo_ref):
        g = g_ref[0].astype(F32)
        for j in range(1, n_slots):
            g = g + g_ref[j].astype(F32)
        m2 = ADAM_B1 * m_ref[...] + (1.0 - ADAM_B1) * g
        v2 = ADAM_B2 * v_ref[...] + (1.0 - ADAM_B2) * (g * g)
        m_hat = m2 / (1.0 - ADAM_B1 ** ADAM_STEP)
        v_hat = v2 / (1.0 - ADAM_B2 ** ADAM_STEP)
        go_ref[...] = g
        d_ref[...] = -ADAM_LR * (m_hat / (jnp.sqrt(v_hat) + ADAM_EPS) + ADAM_WD * w_ref[...])
        mo_ref[...] = m2
        vo_ref[...] = v2

    spec = pl.BlockSpec((tr, cols), lambda i: (i, 0))
    sds = jax.ShapeDtypeStruct((rows, cols), F32)
    return pl.pallas_call(
        body, name=name, grid=(rows // tr,),
        in_specs=[pl.BlockSpec((n_slots, tr, cols), lambda i: (0, i, 0)), spec, spec, spec],
        out_specs=[spec, spec, spec, spec], out_shape=[sds, sds, sds, sds], compiler_params=_cparams(1),
    )(gs, w, m, v)


SHARDED = (("meta_tokens", 1), ("w_in", 2), ("ssd_conv_w", 2), ("mla_w_uq", 2), ("mla_w_ukv", 2),
           ("w_out", 1), ("ffn_w_up", 2), ("ffn_conv_w", 2), ("ffn_w_down", 1))
BIG = ("w_in", "w_out", "ffn_w_up", "ffn_w_down")
SMALL = ("meta_tokens", "ssd_conv_w", "mla_w_uq", "mla_w_ukv", "ffn_conv_w")
REPLICATED = ("norm_mix_g", "ssd_conv_b", "ssd_dt_bias", "ssd_a_log", "ssd_d", "ssd_norm_g", "sb_norm_g",
              "mla_q_norm_g", "mla_kv_norm_g", "mla_norm_g", "norm_ffn_g", "ffn_conv_b", "final_norm_g")
WEIGHTS = ("meta_tokens", "norm_mix_g", "w_in", "ssd_conv_w", "ssd_conv_b", "ssd_dt_bias", "ssd_a_log", "ssd_d",
           "ssd_norm_g", "sb_norm_g", "mla_q_norm_g", "mla_kv_norm_g", "mla_w_uq", "mla_w_ukv", "mla_norm_g",
           "w_out", "norm_ffn_g", "ffn_w_up", "ffn_conv_w", "ffn_conv_b", "ffn_w_down", "final_norm_g")


def _flat_pack(arrays, dtype, align):
    flat = jnp.concatenate([a.reshape(-1).astype(dtype) for a in arrays])
    pad = (-flat.shape[0]) % align
    return jnp.pad(flat, (0, pad)).reshape(-1, 128)


def _pieces(full, axis):
    shp = full.shape
    t = full.reshape(shp[:axis] + (N_DEV, shp[axis] // N_DEV) + shp[axis + 1:])
    return jnp.moveaxis(t, axis, 0).reshape(N_DEV, -1)


def _unpieces(p8, shard_shape, axis):
    t = p8.reshape((N_DEV,) + shard_shape)
    t = jnp.moveaxis(t, 0, axis)
    return t.reshape(shard_shape[:axis] + (N_DEV * shard_shape[axis],) + shard_shape[axis + 1:])


def _split_core_chip(full, axis):
    shp = full.shape
    t = full.reshape(shp[:axis] + (4, 2, shp[axis] // N_DEV) + shp[axis + 1:])
    return jnp.moveaxis(t, (axis + 1, axis), (0, 1))


def _merge_blocks(b8, axis):
    shard = b8.shape[1:]
    t = jnp.moveaxis(b8, 0, axis)
    return t.reshape(shard[:axis] + (N_DEV * shard[axis],) + shard[axis + 1:])


def _gather_weights(shards):
    axes = dict(SHARDED)
    got = _gather_two_level([shards[n].astype(BF16) for n in BIG], name="gather_big")
    full = {n: _merge_blocks(b8, axes[n]) for n, b8 in zip(BIG, got)}
    packed = _flat_pack([shards[n] for n in SMALL], F32, 8 * 128)
    got = _exchange([packed], gather=True, name="gather_small")[0].reshape(N_DEV, -1)
    off = 0
    for n in SMALL:
        size = math.prod(shards[n].shape)
        full[n] = _unpieces(got[:, off:off + size], shards[n].shape, axes[n])
        off += size
    return full


def _pad_cols(a, width):
    return jnp.pad(a, ((0, 0), (0, width - a.shape[1])))


def _w_in_padded(w):
    kr = w[:, 2632:2664]
    return jnp.concatenate([
        w[:, 0:512], w[:, 512:1536], w[:, 1544:2312], _pad_cols(w[:, 2312:2504], 256), w[:, 2504:2632],
        _pad_cols(kr[:, :16], 64), _pad_cols(kr[:, 16:], 64), _pad_cols(w[:, 1536:1544], 128),
        jnp.zeros((w.shape[0], 128), w.dtype)], axis=1)


def _w_in_unpadded(wp):
    return jnp.concatenate([
        wp[:, 0:512], wp[:, 512:1536], wp[:, OFF_DT:OFF_DT + 8], wp[:, 1536:2304], wp[:, OFF_QA:OFF_QA + 192],
        wp[:, OFF_CKV:OFF_CKV + 128], wp[:, OFF_KR:OFF_KR + 16], wp[:, OFF_KR + 64:OFF_KR + 80]], axis=1)


def _w_uq_perm(w):
    t = w.reshape(MLA_Q_RANK, MLA_HEADS, MLA_NOPE + MLA_ROPE)
    out = jnp.concatenate([t[:, :, :64].reshape(MLA_Q_RANK, 256), t[:, :, 64:80].reshape(MLA_Q_RANK, 64),
                           t[:, :, 80:96].reshape(MLA_Q_RANK, 64)], axis=1)
    return jnp.pad(out, ((0, 256 - MLA_Q_RANK), (0, 0)))


def _w_uq_unperm(wp):
    wp = wp[:MLA_Q_RANK]
    t = jnp.concatenate([wp[:, :256].reshape(MLA_Q_RANK, 4, 64), wp[:, 256:320].reshape(MLA_Q_RANK, 4, 16),
                         wp[:, 320:384].reshape(MLA_Q_RANK, 4, 16)], axis=2)
    return t.reshape(MLA_Q_RANK, 4 * 96)


def _w_ukv_perm(w):
    t = w.reshape(MLA_KV_RANK, MLA_HEADS, 128)
    return jnp.concatenate([t[:, :, :64].reshape(MLA_KV_RANK, 256), t[:, :, 64:].reshape(MLA_KV_RANK, 256)], axis=1)


def _w_ukv_unperm(wp):
    t = jnp.concatenate([wp[:, :256].reshape(MLA_KV_RANK, 4, 64), wp[:, 256:].reshape(MLA_KV_RANK, 4, 64)], axis=2)
    return t.reshape(MLA_KV_RANK, 512)


def _heads(a, hd):
    return jnp.moveaxis(a.reshape(a.shape[0], -1, hd), 1, 0)


def _unheads(a):
    return jnp.moveaxis(a, 0, 1).reshape(a.shape[1], -1)


def _row(v, width=None):
    v = v.reshape(1, -1)
    return v if width is None else _pad_cols(v, width)


def _rope_tables(rows):
    pos = jnp.arange(rows, dtype=F32) - float(N_PAD)
    inv = 1.0 / (ROPE_BASE ** (jnp.arange(0, MLA_ROPE, 2, dtype=F32) / MLA_ROPE))
    ang = pos[:, None] * inv[None, :]
    cos = jnp.tile(jnp.cos(ang), (1, 8))
    sin = jnp.tile(jnp.sin(ang), (1, 4))
    return cos, jnp.concatenate([-sin, sin], axis=1)


def _layer_fwd(h, p, cos_t, sin_t, tag):
    s = {"h_in": h}
    hn = _rmsnorm_fwd(h, p["norm_mix_g"], width=D_MODEL, name=tag + "norm_mix")
    u = _matmul(hn, p["w_in"], name=tag + "in_proj")
    s["hn"], s["u"] = hn, u

    xbc_in = u[:, OFF_XBC:OFF_XBC + SSD_XBC]
    pre = _dwconv_fwd(xbc_in, p["ssd_conv_w"], p["ssd_conv_b"], taps=SSD_CONV, name=tag + "ssd_conv")
    dtr = u[:, OFF_DT:OFF_DT + 128]
    y_ssd, states = _ssd_fwd(pre, dtr, p["dt_bias"], p["a_row"], p["d_exp"], name=tag + "ssd_core")
    zgate = u[:, OFF_Z:OFF_Z + SSD_WIDTH]
    yn_ssd = _rmsnorm_fwd(y_ssd, p["ssd_norm_g"], width=SSD_WIDTH, z=zgate, name=tag + "ssd_norm")
    s.update(xbc_in=xbc_in, pre=pre, dtr=dtr, y_ssd=y_ssd, states=states, zgate=zgate)

    q_sb = _heads(u[:, OFF_QSB:OFF_QSB + SB_WIDTH], SB_HEAD_DIM)
    k_sb = _heads(u[:, OFF_KSB:OFF_KSB + SB_WIDTH], SB_HEAD_DIM).astype(BF16)
    v_sb = _heads(u[:, OFF_VSB:OFF_VSB + SB_WIDTH], SB_HEAD_DIM).astype(BF16)
    o_sb, u_tot = _sb_fwd(q_sb, k_sb, v_sb, name=tag + "sb_attn")
    o_sb_flat = _unheads(o_sb)
    yn_sb = _rmsnorm_fwd(o_sb_flat, p["sb_norm_g"], width=SB_WIDTH, name=tag + "sb_norm")
    s.update(q_sb=q_sb, k_sb=k_sb, v_sb=v_sb, u_tot=u_tot, o_sb_flat=o_sb_flat)

    qa = u[:, OFF_QA:OFF_QA + 256]
    ckv = u[:, OFF_CKV:OFF_CKV + 128]
    qa_n = _rmsnorm_fwd(qa, p["mla_q_norm_g"], width=MLA_Q_RANK, name=tag + "mla_qnorm")
    ckv_n = _rmsnorm_fwd(ckv, p["mla_kv_norm_g"], width=MLA_KV_RANK, name=tag + "mla_kvnorm")
    qf = _matmul(qa_n, p["mla_w_uq"], name=tag + "mla_uq")
    kvf = _matmul(ckv_n, p["mla_w_ukv"], name=tag + "mla_ukv")
    q_rope = _rope(qf[:, 256:384], cos_t, sin_t, name=tag + "rope_q")
    k_rope = _rope(u[:, OFF_KR:OFF_KR + 128], cos_t, sin_t, name=tag + "rope_k")
    rows = h.shape[0]
    zpad = jnp.zeros((MLA_HEADS, rows, 32), F32)
    qh = jnp.concatenate([_heads(qf[:, :256], 64), _heads(q_rope[:, :64], 16), _heads(q_rope[:, 64:], 16), zpad], axis=2)
    kr_b = jnp.broadcast_to(jnp.concatenate([k_rope[:, 0:16], k_rope[:, 64:80]], axis=1)[None], (MLA_HEADS, rows, 32))
    kh = jnp.concatenate([_heads(kvf[:, :256], 64), kr_b, zpad], axis=2).astype(BF16)
    vh = _heads(kvf[:, 256:], 64).astype(BF16)
    v_ones = jnp.concatenate([vh, jnp.ones_like(vh)], axis=2)
    o_mla, lse = _mla_fwd(qh, kh, v_ones, name=tag + "mla_attn")
    o_mla_flat = _unheads(o_mla)
    yn_mla = _rmsnorm_fwd(o_mla_flat, p["mla_norm_g"], width=256, name=tag + "mla_norm")
    s.update(qa=qa, ckv=ckv, qa_n=qa_n, ckv_n=ckv_n, qh=qh, kh=kh, vh=vh, o_mla=o_mla, lse=lse,
             o_mla_flat=o_mla_flat)

    mix = jnp.concatenate([yn_ssd, yn_sb, yn_mla], axis=1)
    h_mid = _matmul(mix, p["w_out"], res=h, mask_pad=True, name=tag + "out_proj")
    hn2 = _rmsnorm_fwd(h_mid, p["norm_ffn_g"], width=D_MODEL, name=tag + "norm_ffn")
    up = _matmul(hn2, p["ffn_w_up"], tn=1408, name=tag + "ffn_up")
    act = _ffn_conv_gate_fwd(up, p["ffn_conv_w"], p["ffn_conv_b"], taps=FFN_CONV, name=tag + "ffn_conv_gate")
    h_out = _matmul(act, p["ffn_w_down"], res=h_mid, mask_pad=True, tk=1408, name=tag + "ffn_down")
    s.update(mix=mix, h_mid=h_mid, hn2=hn2, up=up, act=act)
    return h_out, s


def _layer_bwd(dh_out, p, s, cos_t, sin_t, tag):
    g = {}
    rows = dh_out.shape[0]
    dact = _matmul(dh_out, p["ffn_w_down"], tb=True, tn=1408, name=tag + "b_down_dx")
    g["ffn_w_down"] = _matmul(s["act"], dh_out, ta=True, tm=1408, tk=640, name=tag + "b_down_dw")
    dup1, dup2, dcw1, dcw2, dcb1, dcb2 = _ffn_conv_gate_bwd(
        dact, s["up"], p["ffn_conv_w"], p["ffn_conv_b"], taps=FFN_CONV, name=tag + "b_ffn_conv_gate")
    g["ffn_conv_w"] = jnp.concatenate([dcw1[:FFN_CONV], dcw2[:FFN_CONV]], axis=1)
    g["ffn_conv_b"] = jnp.concatenate([dcb1[0], dcb2[0]])
    w_up1, w_up2 = p["ffn_w_up"][:, :D_FF], p["ffn_w_up"][:, D_FF:]
    dhn2 = _matmul(dup1, w_up1, tb=True, tk=1408, name=tag + "b_up_dx1")
    dhn2 = _matmul(dup2, w_up2, tb=True, tk=1408, res=dhn2, name=tag + "b_up_dx2")
    g["ffn_w_up"] = jnp.concatenate([_matmul(s["hn2"], dup1, ta=True, tn=1408, tk=640, name=tag + "b_up_dw1"),
                                     _matmul(s["hn2"], dup2, ta=True, tn=1408, tk=640, name=tag + "b_up_dw2")], axis=1)
    dh_mid, _, dg = _rmsnorm_bwd(s["h_mid"], p["norm_ffn_g"], dhn2, width=D_MODEL, res=dh_out, mask_pad=True,
                                 name=tag + "b_norm_ffn")
    g["norm_ffn_g"] = dg[0]

    dmix = _matmul(dh_mid, p["w_out"], tb=True, name=tag + "b_out_dx")
    g["w_out"] = _matmul(s["mix"], dh_mid, ta=True, tk=640, name=tag + "b_out_dw")

    dy_ssd, dz, dg = _rmsnorm_bwd(s["y_ssd"], p["ssd_norm_g"], dmix[:, :SSD_WIDTH], width=SSD_WIDTH, z=s["zgate"],
                                  name=tag + "b_ssd_norm")
    g["ssd_norm_g"] = dg[0]
    dpre, ddtr, dbias, da, dd = _ssd_bwd(s["pre"], s["dtr"], p["dt_bias"], p["a_row"], p["d_exp"], s["states"],
                                         dy_ssd, name=tag + "b_ssd_core")
    g["ssd_dt_bias"] = dbias[0, :8]
    g["ssd_a_log"] = da[0, :8] * p["a_row"][0, :8]
    g["ssd_d"] = dd.reshape(8, 64).sum(axis=1)
    dxbc_in, dcw, dcb_ = _dwconv_bwd(dpre, s["xbc_in"], p["ssd_conv_w"], taps=SSD_CONV, name=tag + "b_ssd_conv")
    g["ssd_conv_w"], g["ssd_conv_b"] = dcw[:SSD_CONV], dcb_[0]

    do_sb_flat, _, dg = _rmsnorm_bwd(s["o_sb_flat"], p["sb_norm_g"], dmix[:, 512:768], width=SB_WIDTH,
                                     name=tag + "b_sb_norm")
    g["sb_norm_g"] = dg[0]
    do_sb = _heads(do_sb_flat, SB_HEAD_DIM)
    q_t = jnp.swapaxes(s["q_sb"] * (SB_HEAD_DIM ** -0.5), 1, 2).astype(BF16)
    dq_sb, dk_t, dv_t = _sb_bwd(s["q_sb"], s["k_sb"], s["v_sb"], do_sb, s["u_tot"], q_t,
                                jnp.swapaxes(do_sb, 1, 2).astype(BF16), name=tag + "b_sb_attn")
    dk_sb, dv_sb = jnp.swapaxes(dk_t, 1, 2), jnp.swapaxes(dv_t, 1, 2)

    do_mla_flat, _, dg = _rmsnorm_bwd(s["o_mla_flat"], p["mla_norm_g"], dmix[:, 768:1024], width=256,
                                      name=tag + "b_mla_norm")
    g["mla_norm_g"] = dg[0]
    dqh, dkh, dvh = _mla_bwd(s["qh"], s["kh"], s["vh"], s["o_mla"], s["lse"], _heads(do_mla_flat, 64),
                             name=tag + "b_mla_attn")
    dq_rope_in = jnp.concatenate([_unheads(dqh[:, :, 64:80]), _unheads(dqh[:, :, 80:96])], axis=1)
    dq_r = _rope(dq_rope_in, cos_t, sin_t, transpose=True, name=tag + "b_rope_q")
    dqf = jnp.concatenate([_unheads(dqh[:, :, :64]), dq_r], axis=1)
    dkr_sum = jnp.sum(dkh[:, :, 64:96], axis=0)
    dk_rope_in = jnp.concatenate([_pad_cols(dkr_sum[:, :16], 64), _pad_cols(dkr_sum[:, 16:], 64)], axis=1)
    dkr = _rope(dk_rope_in, cos_t, sin_t, transpose=True, name=tag + "b_rope_k")
    dkvf = jnp.concatenate([_unheads(dkh[:, :, :64]), _unheads(dvh)], axis=1)
    dqa_n = _matmul(dqf, p["mla_w_uq"], tb=True, name=tag + "b_uq_dx")
    g["mla_w_uq"] = _matmul(s["qa_n"], dqf, ta=True, tk=640, name=tag + "b_uq_dw")
    dckv_n = _matmul(dkvf, p["mla_w_ukv"], tb=True, name=tag + "b_ukv_dx")
    g["mla_w_ukv"] = _matmul(s["ckv_n"], dkvf, ta=True, tk=640, name=tag + "b_ukv_dw")
    dqa, _, dg = _rmsnorm_bwd(s["qa"], p["mla_q_norm_g"], dqa_n, width=MLA_Q_RANK, name=tag + "b_mla_qnorm")
    g["mla_q_norm_g"] = dg[0, :MLA_Q_RANK]
    dckv, _, dg = _rmsnorm_bwd(s["ckv"], p["mla_kv_norm_g"], dckv_n, width=MLA_KV_RANK, name=tag + "b_mla_kvnorm")
    g["mla_kv_norm_g"] = dg[0]

    du = jnp.concatenate([dz, dxbc_in, _unheads(dq_sb), _unheads(dk_sb), _unheads(dv_sb), dqa, dckv, dkr, ddtr,
                          jnp.zeros((rows, 128), F32)], axis=1)
    dhn = _matmul(du, p["w_in"], tb=True, name=tag + "b_in_dx")
    g["w_in"] = _matmul(s["hn"], du, ta=True, tk=640, name=tag + "b_in_dw")
    dh_in, _, dg = _rmsnorm_bwd(s["h_in"], p["norm_mix_g"], dhn, width=D_MODEL, res=dh_mid, mask_pad=True,
                                name=tag + "b_norm_mix")
    g["norm_mix_g"] = dg[0]
    return dh_in, g


def _prepare_layer(full, rep, l):
    a_row = _row(-jnp.exp(rep["ssd_a_log"][l]), 128)
    return {
        "norm_mix_g": _row(rep["norm_mix_g"][l]),
        "w_in": _w_in_padded(full["w_in"][l]),
        "ssd_conv_w": jnp.pad(full["ssd_conv_w"][l], ((0, HALO - SSD_CONV), (0, 0))),
        "ssd_conv_b": _row(rep["ssd_conv_b"][l]),
        "dt_bias": _row(rep["ssd_dt_bias"][l], 128),
        "a_row": a_row,
        "d_exp": _row(jnp.repeat(rep["ssd_d"][l], 64)),
        "ssd_norm_g": _row(rep["ssd_norm_g"][l]),
        "sb_norm_g": _row(rep["sb_norm_g"][l]),
        "mla_q_norm_g": _row(rep["mla_q_norm_g"][l], 256),
        "mla_kv_norm_g": _row(rep["mla_kv_norm_g"][l]),
        "mla_w_uq": _w_uq_perm(full["mla_w_uq"][l]),
        "mla_w_ukv": _w_ukv_perm(full["mla_w_ukv"][l]),
        "mla_norm_g": _row(rep["mla_norm_g"][l]),
        "w_out": full["w_out"][l],
        "norm_ffn_g": _row(rep["norm_ffn_g"][l]),
        "ffn_w_up": full["ffn_w_up"][l],
        "ffn_conv_w": jnp.pad(full["ffn_conv_w"][l], ((0, HALO - FFN_CONV), (0, 0))),
        "ffn_conv_b": _row(rep["ffn_conv_b"][l]),
        "ffn_w_down": full["ffn_w_down"][l],
    }


def _layer_grads_to_full(g):
    out = dict(g)
    out["w_in"] = _w_in_unpadded(g["w_in"])
    out["mla_w_uq"] = _w_uq_unperm(g["mla_w_uq"])
    out["mla_w_ukv"] = _w_ukv_unperm(g["mla_w_ukv"])
    return out


def kernel(x, meta_tokens, norm_mix_g, w_in, ssd_conv_w, ssd_conv_b, ssd_dt_bias, ssd_a_log, ssd_d, ssd_norm_g, sb_norm_g, mla_q_norm_g, mla_kv_norm_g, mla_w_uq, mla_w_ukv, mla_norm_g, w_out, norm_ffn_g, ffn_w_up, ffn_conv_w, ffn_conv_b, ffn_w_down, final_norm_g, loss_target, m_meta_tokens, m_norm_mix_g, m_w_in, m_ssd_conv_w, m_ssd_conv_b, m_ssd_dt_bias, m_ssd_a_log, m_ssd_d, m_ssd_norm_g, m_sb_norm_g, m_mla_q_norm_g, m_mla_kv_norm_g, m_mla_w_uq, m_mla_w_ukv, m_mla_norm_g, m_w_out, m_norm_ffn_g, m_ffn_w_up, m_ffn_conv_w, m_ffn_conv_b, m_ffn_w_down, m_final_norm_g, v_meta_tokens, v_norm_mix_g, v_w_in, v_ssd_conv_w, v_ssd_conv_b, v_ssd_dt_bias, v_ssd_a_log, v_ssd_d, v_ssd_norm_g, v_sb_norm_g, v_mla_q_norm_g, v_mla_kv_norm_g, v_mla_w_uq, v_mla_w_ukv, v_mla_norm_g, v_w_out, v_norm_ffn_g, v_ffn_w_up, v_ffn_conv_w, v_ffn_conv_b, v_ffn_w_down, v_final_norm_g):
    w = dict(meta_tokens=meta_tokens, norm_mix_g=norm_mix_g, w_in=w_in, ssd_conv_w=ssd_conv_w, ssd_conv_b=ssd_conv_b,
             ssd_dt_bias=ssd_dt_bias, ssd_a_log=ssd_a_log, ssd_d=ssd_d, ssd_norm_g=ssd_norm_g, sb_norm_g=sb_norm_g,
             mla_q_norm_g=mla_q_norm_g, mla_kv_norm_g=mla_kv_norm_g, mla_w_uq=mla_w_uq, mla_w_ukv=mla_w_ukv,
             mla_norm_g=mla_norm_g, w_out=w_out, norm_ffn_g=norm_ffn_g, ffn_w_up=ffn_w_up, ffn_conv_w=ffn_conv_w,
             ffn_conv_b=ffn_conv_b, ffn_w_down=ffn_w_down, final_norm_g=final_norm_g)
    mom = dict(meta_tokens=m_meta_tokens, norm_mix_g=m_norm_mix_g, w_in=m_w_in, ssd_conv_w=m_ssd_conv_w,
               ssd_conv_b=m_ssd_conv_b, ssd_dt_bias=m_ssd_dt_bias, ssd_a_log=m_ssd_a_log, ssd_d=m_ssd_d,
               ssd_norm_g=m_ssd_norm_g, sb_norm_g=m_sb_norm_g, mla_q_norm_g=m_mla_q_norm_g,
               mla_kv_norm_g=m_mla_kv_norm_g, mla_w_uq=m_mla_w_uq, mla_w_ukv=m_mla_w_ukv, mla_norm_g=m_mla_norm_g,
               w_out=m_w_out, norm_ffn_g=m_norm_ffn_g, ffn_w_up=m_ffn_w_up, ffn_conv_w=m_ffn_conv_w,
               ffn_conv_b=m_ffn_conv_b, ffn_w_down=m_ffn_w_down, final_norm_g=m_final_norm_g)
    vel = dict(meta_tokens=v_meta_tokens, norm_mix_g=v_norm_mix_g, w_in=v_w_in, ssd_conv_w=v_ssd_conv_w,
               ssd_conv_b=v_ssd_conv_b, ssd_dt_bias=v_ssd_dt_bias, ssd_a_log=v_ssd_a_log, ssd_d=v_ssd_d,
               ssd_norm_g=v_ssd_norm_g, sb_norm_g=v_sb_norm_g, mla_q_norm_g=v_mla_q_norm_g,
               mla_kv_norm_g=v_mla_kv_norm_g, mla_w_uq=v_mla_w_uq, mla_w_ukv=v_mla_w_ukv, mla_norm_g=v_mla_norm_g,
               w_out=v_w_out, norm_ffn_g=v_norm_ffn_g, ffn_w_up=v_ffn_w_up, ffn_conv_w=v_ffn_conv_w,
               ffn_conv_b=v_ffn_conv_b, ffn_w_down=v_ffn_w_down, final_norm_g=v_final_norm_g)

    full = _gather_weights({n: w[n] for n, _ in SHARDED})
    layers = [_prepare_layer(full, w, l) for l in range(DEPTH)]

    seq = x.shape[1]
    rows = BLOCK + seq
    cos_t, sin_t = _rope_tables(rows)
    h = jnp.concatenate([jnp.zeros((N_PAD, D_MODEL), F32), full["meta_tokens"], x[0]], axis=0)

    saved = []
    for l in range(DEPTH):
        h, s = _layer_fwd(h, layers[l], cos_t, sin_t, "l%d_" % l)
        saved.append(s)
    dh, dg_final, loss_part = _final_loss(h, _row(final_norm_g), loss_target[0], name="final_loss")
    loss = lax.psum(loss_part[0, 0], ("x", "y", "c"))

    layer_grads = [None] * DEPTH
    for l in reversed(range(DEPTH)):
        dh, g = _layer_bwd(dh, layers[l], saved[l], cos_t, sin_t, "l%d_" % l)
        layer_grads[l] = _layer_grads_to_full(g)
    grad_x = dh[BLOCK:][None]

    partial = {n: jnp.stack([layer_grads[l][n] for l in range(DEPTH)]) for n in layer_grads[0]}
    partial["meta_tokens"] = dh[N_PAD:BLOCK]
    partial["final_norm_g"] = dg_final[0]

    results = [dict(), dict(), dict(), dict()]
    axes = dict(SHARDED)

    core = lax.axis_index("c")
    halves = [_split_core_chip(partial[n], axes[n]).astype(BF16) for n in BIG]
    theirs = _sibling_swap(halves, name="grad_sibling_swap")
    chip_sums = []
    for n, h2, t4 in zip(BIG, halves, theirs):
        view = (4 * math.prod(w[n].shape[:-1]), w[n].shape[-1])
        mine = lax.dynamic_index_in_dim(h2, core, 0, keepdims=False)
        chip_sums.append(_pair_add(mine.reshape(view), t4.reshape(view), name="grad_pair_add_" + n).reshape(t4.shape))
    got_big = _chip_all_to_all(chip_sums, name="grad_chip_all_to_all")
    for n, g4 in zip(BIG, got_big):
        shp = w[n].shape
        view = (math.prod(shp[:-1]), shp[-1])
        outs = _adamw(g4.reshape((4,) + view), w[n].reshape(view), mom[n].reshape(view), vel[n].reshape(view),
                      name="adamw_" + n)
        for kind in range(4):
            results[kind][n] = outs[kind].reshape(shp)

    send = jnp.concatenate([_pieces(partial[n], axes[n]) for n in SMALL], axis=1)
    pad = (-send.shape[1]) % (8 * 128)
    send = jnp.pad(send, ((0, 0), (0, pad))).reshape(N_DEV, -1, 128)
    got = _exchange([send], gather=False, name="grad_all_to_all_small")[0]
    pack = lambda d: _flat_pack([d[n] for n in SMALL], F32, 8 * 128)
    sh_out = _adamw(got, pack(w), pack(mom), pack(vel), name="adamw_small")

    rep_g = _flat_pack([partial[n] for n in REPLICATED], F32, 8 * 128)
    got_r = _exchange([rep_g], gather=True, name="grad_all_gather")[0]
    packr = lambda d: _flat_pack([d[n] for n in REPLICATED], F32, 8 * 128)
    rep_out = _adamw(got_r, packr(w), packr(mom), packr(vel), name="adamw_replicated")

    for names, outs in ((list(SMALL), sh_out), (list(REPLICATED), rep_out)):
        off = 0
        for n in names:
            size = math.prod(w[n].shape)
            for kind in range(4):
                results[kind][n] = outs[kind].reshape(-1)[off:off + size].reshape(w[n].shape)
            off += size

    return (loss, grad_x, *[results[0][n] for n in WEIGHTS], *[results[1][n] for n in WEIGHTS],
            *[results[2][n] for n in WEIGHTS], *[results[3][n] for n in WEIGHTS])
```

```python
import math

import jax
import jax.numpy as jnp
from jax import lax
from jax.experimental import pallas as pl
from jax.experimental.pallas import tpu as pltpu

F32 = jnp.float32
BF16 = jnp.bfloat16

D_MODEL = 1024
DEPTH = 2
N_META = 16
BLOCK = 128
N_PAD = BLOCK - N_META
EPS = 1e-6
SSD_WIDTH = 512
SSD_XBC = 1024
SSD_CONV = 4
SB_WIDTH = 256
SB_HEAD_DIM = 64
MLA_HEADS = 4
MLA_NOPE = 64
MLA_ROPE = 32
MLA_Q_RANK = 192
MLA_KV_RANK = 128
ROPE_BASE = 10000.0
D_FF = 2816
FFN_CONV = 3
IN_COLS = 2664
N_DEV = 8

ADAM_LR = 0.001
ADAM_B1 = 0.9
ADAM_B2 = 0.999
ADAM_EPS = 1e-08
ADAM_WD = 0.01
ADAM_STEP = 10

U_COLS = 3072
OFF_Z, OFF_XBC, OFF_QSB, OFF_KSB, OFF_VSB, OFF_QA, OFF_CKV, OFF_KR, OFF_DT = (
    0, 512, 1536, 1792, 2048, 2304, 2560, 2688, 2816)

V7X_VMEM_BYTES = 64 * 1024 * 1024
VMEM_LIMIT = (V7X_VMEM_BYTES * 7) // 8
NEG_BIG = -1e30


def _cparams(n_axes):
    return pltpu.CompilerParams(dimension_semantics=("arbitrary",) * n_axes, vmem_limit_bytes=VMEM_LIMIT)


def _tile(n, target, align):
    best = None
    for d in range(align, min(n, target) + 1, align):
        if n % d == 0:
            best = d
    return n if best is None else best


def _sigmoid(x):
    return 1.0 / (1.0 + jnp.exp(-x))


def _softplus(x):
    return jnp.maximum(x, 0.0) + jnp.log(1.0 + jnp.exp(-jnp.abs(x)))


def _dot(a, b):
    return jnp.dot(a, b, preferred_element_type=F32)


def _dot_nt(a, b):
    return lax.dot_general(a, b, (((1,), (1,)), ((), ())), preferred_element_type=F32)


def _hilo(x):
    hi = x.astype(BF16)
    lo = (x - hi.astype(F32)).astype(BF16)
    return hi, lo


def _hilo_dot_l(x, m):
    hi, lo = _hilo(x)
    return _dot(hi, m) + _dot(lo, m)


def _hilo_dot_r(m, x):
    hi, lo = _hilo(x)
    return _dot(m, hi) + _dot(m, lo)


def _hilo_dot_nt(x, m):
    hi, lo = _hilo(x)
    return _dot_nt(hi, m) + _dot_nt(lo, m)


def _ones_where(cond):
    return jnp.where(cond, 1.0, 0.0).astype(BF16)


def _matmul(a, b, *, name, ta=False, tb=False, out_dtype=F32, res=None, mask_pad=False,
            tm=640, tn=1024, tk=1024):
    m_dim = a.shape[1] if ta else a.shape[0]
    k_dim = a.shape[0] if ta else a.shape[1]
    n_dim = b.shape[0] if tb else b.shape[1]
    assert (b.shape[1] if tb else b.shape[0]) == k_dim
    tm = _tile(m_dim, tm, 128)
    tn = _tile(n_dim, tn, 128)
    tk = _tile(k_dim, tk, 128)
    nk = k_dim // tk
    dn = (((0 if ta else 1,), (1 if tb else 0,)), ((), ()))

    use_scratch = nk > 1 and out_dtype != F32

    def body(*refs):
        refs = list(refs)
        a_ref, b_ref = refs[0], refs[1]
        r_ref = refs[2] if res is not None else None
        o_ref = refs[3] if res is not None else refs[2]
        acc = refs[-1] if use_scratch else o_ref
        k = pl.program_id(2)
        part = lax.dot_general(a_ref[...].astype(BF16), b_ref[...].astype(BF16), dn, preferred_element_type=F32)

        def finish(r):
            if res is not None:
                r = r + r_ref[...].astype(F32)
            if mask_pad:
                rows = pl.program_id(0) * tm + lax.broadcasted_iota(jnp.int32, (tm, 1), 0)
                r = jnp.where(rows >= N_PAD, r, 0.0)
            o_ref[...] = r.astype(out_dtype)

        if nk == 1:
            finish(part)
        else:
            @pl.when(k == 0)
            def _():
                acc[...] = part

            @pl.when(jnp.logical_and(k > 0, k < nk - 1))
            def _():
                acc[...] += part

            @pl.when(k == nk - 1)
            def _():
                finish(acc[...] + part)

    a_spec = pl.BlockSpec((tk, tm), lambda i, j, k: (k, i)) if ta else pl.BlockSpec((tm, tk), lambda i, j, k: (i, k))
    b_spec = pl.BlockSpec((tn, tk), lambda i, j, k: (j, k)) if tb else pl.BlockSpec((tk, tn), lambda i, j, k: (k, j))
    o_spec = pl.BlockSpec((tm, tn), lambda i, j, k: (i, j))
    in_specs = [a_spec, b_spec]
    args = [a, b]
    if res is not None:
        in_specs.append(o_spec)
        args.append(res)
    return pl.pallas_call(
        body, name=name, grid=(m_dim // tm, n_dim // tn, nk),
        in_specs=in_specs, out_specs=o_spec,
        out_shape=jax.ShapeDtypeStruct((m_dim, n_dim), out_dtype),
        scratch_shapes=[pltpu.VMEM((tm, tn), F32)] if use_scratch else [],
        compiler_params=_cparams(3),
    )(*args)


def _rmsnorm_fwd(x, g, *, width, name, z=None, out_dtype=None):
    out_dtype = BF16 if out_dtype is None else out_dtype
    rows, w = x.shape
    tr = _tile(rows, 640, 128)
    inv_w = 1.0 / width

    def body(*refs):
        if z is not None:
            x_ref, z_ref, g_ref, o_ref = refs
        else:
            x_ref, g_ref, o_ref = refs
        t = x_ref[...].astype(F32)
        if z is not None:
            zz = z_ref[...]
            t = t * (zz * _sigmoid(zz))
        ms = jnp.sum(t * t, axis=-1, keepdims=True) * inv_w
        o_ref[...] = ((t * lax.rsqrt(ms + EPS)) * g_ref[...]).astype(out_dtype)

    row_spec = pl.BlockSpec((tr, w), lambda i: (i, 0))
    g_spec = pl.BlockSpec((1, w), lambda i: (0, 0))
    in_specs = [row_spec] + ([row_spec] if z is not None else []) + [g_spec]
    args = [x] + ([z] if z is not None else []) + [g]
    return pl.pallas_call(
        body, name=name, grid=(rows // tr,), in_specs=in_specs, out_specs=row_spec,
        out_shape=jax.ShapeDtypeStruct((rows, w), out_dtype), compiler_params=_cparams(1),
    )(*args)


def _rmsnorm_bwd(x, g, dout, *, width, name, z=None, res=None, mask_pad=False):
    rows, w = x.shape
    tr = _tile(rows, 640, 128)
    inv_w = 1.0 / width

    def body(*refs):
        refs = list(refs)
        x_ref = refs.pop(0)
        z_ref = refs.pop(0) if z is not None else None
        g_ref = refs.pop(0)
        do_ref = refs.pop(0)
        r_ref = refs.pop(0) if res is not None else None
        dx_ref = refs.pop(0)
        dz_ref = refs.pop(0) if z is not None else None
        dg_ref = refs.pop(0)
        i = pl.program_id(0)

        @pl.when(i == 0)
        def _():
            dg_ref[...] = jnp.zeros_like(dg_ref)

        xv = x_ref[...].astype(F32)
        t = xv
        if z is not None:
            zz = z_ref[...]
            sig = _sigmoid(zz)
            sl = zz * sig
            t = xv * sl
        ms = jnp.sum(t * t, axis=-1, keepdims=True) * inv_w
        rstd = lax.rsqrt(ms + EPS)
        xhat = t * rstd
        do = do_ref[...].astype(F32)
        dxh = do * g_ref[...]
        c = jnp.sum(dxh * xhat, axis=-1, keepdims=True) * inv_w
        dt = rstd * (dxh - xhat * c)
        dg_ref[...] += jnp.sum(do * xhat, axis=0, keepdims=True)
        if z is not None:
            dz_ref[...] = dt * xv * (sig * (1.0 + zz * (1.0 - sig)))
            dx = dt * sl
        else:
            dx = dt
        if res is not None:
            dx = dx + r_ref[...]
        if mask_pad:
            rws = i * tr + lax.broadcasted_iota(jnp.int32, (tr, 1), 0)
            dx = jnp.where(rws >= N_PAD, dx, 0.0)
        dx_ref[...] = dx

    row_spec = pl.BlockSpec((tr, w), lambda i: (i, 0))
    g_spec = pl.BlockSpec((1, w), lambda i: (0, 0))
    in_specs = [row_spec] + ([row_spec] if z is not None else []) + [g_spec, row_spec] + (
        [row_spec] if res is not None else [])
    args = [x] + ([z] if z is not None else []) + [g, dout] + ([res] if res is not None else [])
    out_specs = [row_spec] + ([row_spec] if z is not None else []) + [g_spec]
    out_shape = [jax.ShapeDtypeStruct((rows, w), F32)] + (
        [jax.ShapeDtypeStruct((rows, w), F32)] if z is not None else []) + [jax.ShapeDtypeStruct((1, w), F32)]
    outs = pl.pallas_call(
        body, name=name, grid=(rows // tr,), in_specs=in_specs, out_specs=out_specs,
        out_shape=out_shape, compiler_params=_cparams(1),
    )(*args)
    if z is not None:
        return outs[0], outs[1], outs[2]
    return outs[0], None, outs[1]


def _final_loss(h, g, target, *, name):
    rows, w = h.shape
    nb = rows // BLOCK
    inv_w = 1.0 / w

    def body(h_ref, g_ref, t_ref, dh_ref, dg_ref, loss_ref):
        i = pl.program_id(0)

        @pl.when(i == 0)
        def _():
            dg_ref[...] = jnp.zeros_like(dg_ref)
            loss_ref[...] = jnp.zeros_like(loss_ref)

        xv = h_ref[...]
        ms = jnp.sum(xv * xv, axis=-1, keepdims=True) * inv_w
        rstd = lax.rsqrt(ms + EPS)
        xhat = xv * rstd
        gv = g_ref[...]
        err = jnp.where(i >= 1, xhat * gv - t_ref[...], 0.0)
        loss_ref[...] += (0.5 * inv_w) * jnp.sum(err * err)
        do = err * inv_w
        dxh = do * gv
        c = jnp.sum(dxh * xhat, axis=-1, keepdims=True) * inv_w
        dh_ref[...] = rstd * (dxh - xhat * c)
        dg_ref[...] += jnp.sum(do * xhat, axis=0, keepdims=True)

    row_spec = pl.BlockSpec((BLOCK, w), lambda i: (i, 0))
    g_spec = pl.BlockSpec((1, w), lambda i: (0, 0))
    return pl.pallas_call(
        body, name=name, grid=(nb,),
        in_specs=[row_spec, g_spec, pl.BlockSpec((BLOCK, w), lambda i: (jnp.maximum(i - 1, 0), 0))],
        out_specs=[row_spec, g_spec, pl.BlockSpec((1, 128), lambda i: (0, 0))],
        out_shape=[jax.ShapeDtypeStruct((rows, w), F32), jax.ShapeDtypeStruct((1, w), F32),
                   jax.ShapeDtypeStruct((1, 128), F32)],
        compiler_params=_cparams(1),
    )(h, g, target)


HALO = 8


def _dwconv_fwd(u, w8, b, *, taps, name):
    rows, ch = u.shape
    tb = _tile(rows, 640, 128)
    tc = _tile(ch, 512, 128)
    hb = tb // HALO

    def body(u_ref, h_ref, w_ref, b_ref, o_ref, buf):
        i = pl.program_id(0)
        buf[0:HALO, :] = jnp.where(i > 0, h_ref[...], 0.0)
        buf[HALO:HALO + tb, :] = u_ref[...]
        acc = jnp.broadcast_to(b_ref[...], (tb, tc))
        for k in range(taps):
            acc = acc + w_ref[k:k + 1, :] * buf[pl.ds(HALO - (taps - 1) + k, tb), :]
        o_ref[...] = acc

    return pl.pallas_call(
        body, name=name, grid=(rows // tb, ch // tc),
        in_specs=[pl.BlockSpec((tb, tc), lambda i, j: (i, j)),
                  pl.BlockSpec((HALO, tc), lambda i, j: (jnp.maximum(i * hb - 1, 0), j)),
                  pl.BlockSpec((HALO, tc), lambda i, j: (0, j)),
                  pl.BlockSpec((1, tc), lambda i, j: (0, j))],
        out_specs=pl.BlockSpec((tb, tc), lambda i, j: (i, j)),
        out_shape=jax.ShapeDtypeStruct((rows, ch), F32),
        scratch_shapes=[pltpu.VMEM((tb + HALO, tc), F32)],
        compiler_params=_cparams(2),
    )(u, u, w8, b)


def _dwconv_bwd(dpre, u, w8, *, taps, name):
    rows, ch = u.shape
    tb = _tile(rows, 640, 128)
    tc = _tile(ch, 512, 128)
    hb = tb // HALO
    nb = rows // tb
    last_halo = rows // HALO - 1

    def body(d_ref, dn_ref, u_ref, up_ref, w_ref, du_ref, dw_ref, db_ref, bufd, bufu):
        i = pl.program_id(1)

        @pl.when(i == 0)
        def _():
            dw_ref[...] = jnp.zeros_like(dw_ref)
            db_ref[...] = jnp.zeros_like(db_ref)

        d = d_ref[...]
        bufd[0:tb, :] = d
        bufd[tb:tb + HALO, :] = jnp.where(i < nb - 1, dn_ref[...], 0.0)
        bufu[0:HALO, :] = jnp.where(i > 0, up_ref[...], 0.0)
        bufu[HALO:HALO + tb, :] = u_ref[...]
        acc = jnp.zeros((tb, tc), F32)
        for k in range(taps):
            acc = acc + w_ref[k:k + 1, :] * bufd[pl.ds(taps - 1 - k, tb), :]
        du_ref[...] = acc
        for k in range(taps):
            dw_ref[k:k + 1, :] += jnp.sum(d * bufu[pl.ds(HALO - (taps - 1) + k, tb), :], axis=0, keepdims=True)
        db_ref[...] += jnp.sum(d, axis=0, keepdims=True)

    return pl.pallas_call(
        body, name=name, grid=(ch // tc, nb),
        in_specs=[pl.BlockSpec((tb, tc), lambda j, i: (i, j)),
                  pl.BlockSpec((HALO, tc), lambda j, i: (jnp.minimum((i + 1) * hb, last_halo), j)),
                  pl.BlockSpec((tb, tc), lambda j, i: (i, j)),
                  pl.BlockSpec((HALO, tc), lambda j, i: (jnp.maximum(i * hb - 1, 0), j)),
                  pl.BlockSpec((HALO, tc), lambda j, i: (0, j))],
        out_specs=[pl.BlockSpec((tb, tc), lambda j, i: (i, j)),
                   pl.BlockSpec((HALO, tc), lambda j, i: (0, j)),
                   pl.BlockSpec((1, tc), lambda j, i: (0, j))],
        out_shape=[jax.ShapeDtypeStruct((rows, ch), F32), jax.ShapeDtypeStruct((HALO, ch), F32),
                   jax.ShapeDtypeStruct((1, ch), F32)],
        scratch_shapes=[pltpu.VMEM((tb + HALO, tc), F32), pltpu.VMEM((tb + HALO, tc), F32)],
        compiler_params=_cparams(2),
    )(dpre, dpre, u, u, w8)


def _ffn_conv_gate_fwd(up, w8, b, *, taps, name):
    rows, c2 = up.shape
    f = c2 // 2
    tb = _tile(rows, 640, 128)
    tc = _tile(f, 512, 128)
    nct = f // tc
    hb = tb // HALO

    def body(u1_ref, u2_ref, h1_ref, h2_ref, w1_ref, w2_ref, b1_ref, b2_ref, o_ref, buf1, buf2):
        i = pl.program_id(0)
        pre = []
        for u_ref, h_ref, w_ref, b_ref, buf in ((u1_ref, h1_ref, w1_ref, b1_ref, buf1),
                                                 (u2_ref, h2_ref, w2_ref, b2_ref, buf2)):
            buf[0:HALO, :] = jnp.where(i > 0, h_ref[...], 0.0)
            buf[HALO:HALO + tb, :] = u_ref[...]
            acc = jnp.broadcast_to(b_ref[...], (tb, tc))
            for k in range(taps):
                acc = acc + w_ref[k:k + 1, :] * buf[pl.ds(HALO - (taps - 1) + k, tb), :]
            pre.append(acc)
        o_ref[...] = (pre[0] * _sigmoid(pre[0]) * pre[1]).astype(BF16)

    main = lambda off: pl.BlockSpec((tb, tc), lambda i, j: (i, j + off))
    halo = lambda off: pl.BlockSpec((HALO, tc), lambda i, j: (jnp.maximum(i * hb - 1, 0), j + off))
    wrow = lambda off: pl.BlockSpec((HALO, tc), lambda i, j: (0, j + off))
    brow = lambda off: pl.BlockSpec((1, tc), lambda i, j: (0, j + off))
    return pl.pallas_call(
        body, name=name, grid=(rows // tb, nct),
        in_specs=[main(0), main(nct), halo(0), halo(nct), wrow(0), wrow(nct), brow(0), brow(nct)],
        out_specs=pl.BlockSpec((tb, tc), lambda i, j: (i, j)),
        out_shape=jax.ShapeDtypeStruct((rows, f), BF16),
        scratch_shapes=[pltpu.VMEM((tb + HALO, tc), F32), pltpu.VMEM((tb + HALO, tc), F32)],
        compiler_params=_cparams(2),
    )(up, up, up, up, w8, w8, b, b)


def _ffn_conv_gate_bwd(dact, up, w8, b, *, taps, name):
    rows, c2 = up.shape
    f = c2 // 2
    tb = _tile(rows, 640, 128)
    tc = _tile(f, 512, 128)
    nct = f // tc
    hb = tb // HALO
    nb = rows // tb
    last_halo = rows // HALO - 1
    ext = tb + HALO

    def body(d_ref, dn_ref, u1_ref, u2_ref, p1_ref, p2_ref, n1_ref, n2_ref, w1_ref, w2_ref, b1_ref, b2_ref,
             du1_ref, du2_ref, dw1_ref, dw2_ref, db1_ref, db2_ref, bufu1, bufu2, bufd1, bufd2):
        i = pl.program_id(1)

        @pl.when(i == 0)
        def _():
            for r in (dw1_ref, dw2_ref, db1_ref, db2_ref):
                r[...] = jnp.zeros_like(r)

        has_next = i < nb - 1
        pre = []
        for u_ref, p_ref, n_ref, w_ref, b_ref, buf in ((u1_ref, p1_ref, n1_ref, w1_ref, b1_ref, bufu1),
                                                       (u2_ref, p2_ref, n2_ref, w2_ref, b2_ref, bufu2)):
            buf[0:HALO, :] = jnp.where(i > 0, p_ref[...], 0.0)
            buf[HALO:HALO + tb, :] = u_ref[...]
            buf[HALO + tb:HALO + ext, :] = jnp.where(has_next, n_ref[...], 0.0)
            acc = jnp.broadcast_to(b_ref[...], (ext, tc))
            for k in range(taps):
                acc = acc + w_ref[k:k + 1, :] * buf[pl.ds(HALO - (taps - 1) + k, ext), :]
            pre.append(acc)
        d_ext = jnp.concatenate([d_ref[...], jnp.where(has_next, dn_ref[...], 0.0)], axis=0)
        sig = _sigmoid(pre[0])
        bufd1[...] = d_ext * pre[1] * (sig * (1.0 + pre[0] * (1.0 - sig)))
        bufd2[...] = d_ext * (pre[0] * sig)
        for w_ref, bufd, bufu, du_ref, dw_ref, db_ref in ((w1_ref, bufd1, bufu1, du1_ref, dw1_ref, db1_ref),
                                                          (w2_ref, bufd2, bufu2, du2_ref, dw2_ref, db2_ref)):
            acc = jnp.zeros((tb, tc), F32)
            for k in range(taps):
                acc = acc + w_ref[k:k + 1, :] * bufd[pl.ds(taps - 1 - k, tb), :]
            du_ref[...] = acc.astype(BF16)
            dmain = bufd[0:tb, :]
            for k in range(taps):
                dw_ref[k:k + 1, :] += jnp.sum(dmain * bufu[pl.ds(HALO - (taps - 1) + k, tb), :], axis=0, keepdims=True)
            db_ref[...] += jnp.sum(dmain, axis=0, keepdims=True)

    main = lambda off: pl.BlockSpec((tb, tc), lambda j, i: (i, j + off))
    prev = lambda off: pl.BlockSpec((HALO, tc), lambda j, i: (jnp.maximum(i * hb - 1, 0), j + off))
    nxt = lambda off: pl.BlockSpec((HALO, tc), lambda j, i: (jnp.minimum((i + 1) * hb, last_halo), j + off))
    wrow = lambda off: pl.BlockSpec((HALO, tc), lambda j, i: (0, j + off))
    brow = lambda off: pl.BlockSpec((1, tc), lambda j, i: (0, j + off))
    half = jax.ShapeDtypeStruct((rows, f), BF16)
    return pl.pallas_call(
        body, name=name, grid=(nct, nb),
        in_specs=[main(0), nxt(0), main(0), main(nct), prev(0), prev(nct), nxt(0), nxt(nct),
                  wrow(0), wrow(nct), brow(0), brow(nct)],
        out_specs=[main(0), main(0), wrow(0), wrow(0), brow(0), brow(0)],
        out_shape=[half, half, jax.ShapeDtypeStruct((HALO, f), F32), jax.ShapeDtypeStruct((HALO, f), F32),
                   jax.ShapeDtypeStruct((1, f), F32), jax.ShapeDtypeStruct((1, f), F32)],
        scratch_shapes=[pltpu.VMEM((ext + HALO, tc), F32), pltpu.VMEM((ext + HALO, tc), F32),
                        pltpu.VMEM((ext, tc), F32), pltpu.VMEM((ext, tc), F32)],
        compiler_params=_cparams(2),
    )(dact, dact, up, up, up, up, up, up, w8, w8, b, b)


def _rope(xr, cos_t, sin_t, *, name, transpose=False):
    rows, w = xr.shape
    tr = _tile(rows, 640, 128)

    def body(x_ref, c_ref, s_ref, o_ref):
        xv = x_ref[...]
        if transpose:
            o_ref[...] = xv * c_ref[...] + pltpu.roll(xv * s_ref[...], 64, 1)
        else:
            o_ref[...] = xv * c_ref[...] + pltpu.roll(xv, 64, 1) * s_ref[...]

    spec = pl.BlockSpec((tr, w), lambda i: (i, 0))
    return pl.pallas_call(
        body, name=name, grid=(rows // tr,), in_specs=[spec, spec, spec], out_specs=spec,
        out_shape=jax.ShapeDtypeStruct((rows, w), F32), compiler_params=_cparams(1),
    )(xr, cos_t, sin_t)


ATT_TQ = 640
GROUPS_WIDE = (8, 4, 2, 1)
GROUPS = (4, 2, 1)


def _grouped_loop(lo, hi, step, carry, *, sizes, descending=False):
    n = jnp.maximum(hi - lo, 0)
    done = 0
    for g in sizes:
        count = lax.div(n - done, jnp.int32(g))

        def body(t, c, g=g, done=done):
            if descending:
                return step([hi - 1 - done - g * t - b for b in range(g)], c)
            return step([lo + done + g * t + b for b in range(g)], c)

        carry = lax.fori_loop(0, count, body, carry)
        done = done + count * g
    return carry


def _tile_iotas(rows=BLOCK):
    r_i = lax.broadcasted_iota(jnp.int32, (rows, BLOCK), 0)
    c_i = lax.broadcasted_iota(jnp.int32, (rows, BLOCK), 1)
    return r_i, c_i


def _key_ranges(i, tq, first=0):
    n_blocks = ((i + 1) * tq + (BLOCK - 1)) >> 7
    first_diag = jnp.maximum((i * tq) >> 7, first)
    return first_diag, n_blocks


def _dot_tn(a, b):
    return lax.dot_general(a, b, (((0,), (0,)), ((), ())), preferred_element_type=F32)


def _cumsum_rhs(pred):
    r = lax.broadcasted_iota(jnp.int32, (BLOCK, 2 * BLOCK), 0)
    c = lax.broadcasted_iota(jnp.int32, (BLOCK, 2 * BLOCK), 1)
    return _ones_where((c >= BLOCK) | pred(r, c))


def _cumsum_dot(x, rhs):
    r = _dot(x.astype(BF16), rhs)
    return r[:, :BLOCK], r[:, BLOCK:]


def _sb_fwd(q, k, v, *, name):
    nh, rows, hd = q.shape
    tq = _tile(rows, ATT_TQ, 8)
    scale = SB_HEAD_DIM ** -0.5

    def body(q_ref, k_ref, v_ref, o_ref, u_ref):
        i = pl.program_id(1)
        r_i, c_i = _tile_iotas(tq)
        m_after = _cumsum_rhs(lambda j, s: j > s)
        qb = (q_ref[0] * scale).astype(BF16)
        rowpos = i * tq + r_i
        first_diag, n_blocks = _key_ranges(i, tq)

        def step(js, carry, masked):
            acc, cu = carry
            offs = [pl.multiple_of(j * BLOCK, BLOCK) for j in js]
            zs = [_dot_nt(qb, k_ref[0, pl.ds(off, BLOCK), :].astype(BF16)) for off in offs]
            sp = [_softplus(z) for z in zs]
            if masked:
                masks = [((off + c_i) < rowpos) & ((off + c_i) >= N_PAD) for off in offs]
                cs = [_cumsum_dot(jnp.where(m_, s, 0.0), m_after) for m_, s in zip(masks, sp)]
            else:
                cs = [_cumsum_dot(s, m_after) for s in sp]
            wgt = []
            for b in range(len(js)):
                w_ = jnp.exp(zs[b] - sp[b] - (cu + cs[b][0]))
                wgt.append(jnp.where(masks[b], w_, 0.0) if masked else w_)
                cu = cu + cs[b][1]
            for b, off in enumerate(offs):
                acc = acc + _dot(wgt[b].astype(BF16), v_ref[0, pl.ds(off, BLOCK), :].astype(BF16))
            return acc, cu

        carry = (jnp.zeros((tq, hd), F32), jnp.zeros((tq, BLOCK), F32))
        carry = _grouped_loop(first_diag, n_blocks, lambda js, c: step(js, c, True), carry, sizes=GROUPS,
                              descending=True)
        acc, cu = _grouped_loop(0, first_diag, lambda js, c: step(js, c, False), carry, sizes=GROUPS_WIDE,
                                descending=True)
        o_ref[0] = acc
        u_ref[0] = -cu

    blk = pl.BlockSpec((1, tq, hd), lambda h, i: (h, i, 0))
    full = pl.BlockSpec((1, rows, hd), lambda h, i: (h, 0, 0))
    return pl.pallas_call(
        body, name=name, grid=(nh, rows // tq), in_specs=[blk, full, full],
        out_specs=[blk, pl.BlockSpec((1, tq, 128), lambda h, i: (h, i, 0))],
        out_shape=[jax.ShapeDtypeStruct((nh, rows, hd), F32), jax.ShapeDtypeStruct((nh, rows, 128), F32)],
        compiler_params=_cparams(2),
    )(q, k, v)


def _sb_bwd(q, k, v, do, u_tot, q_t, do_t, *, name):
    nh, rows, hd = q.shape
    tq = _tile(rows, ATT_TQ, 8)
    scale = SB_HEAD_DIM ** -0.5

    def body(q_ref, k_ref, v_ref, do_ref, u_ref, qt_ref, dot_ref, dq_ref, dk_ref, dv_ref):
        i = pl.program_id(1)

        @pl.when(i == 0)
        def _():
            dk_ref[...] = jnp.zeros_like(dk_ref)
            dv_ref[...] = jnp.zeros_like(dv_ref)

        r_i, c_i = _tile_iotas(tq)
        qtb = qt_ref[0]
        dotb = dot_ref[0]
        r_b, c_b = _tile_iotas()
        t_incl = _ones_where(r_b <= c_b)

        def prefix(x):
            r = _dot(x.astype(BF16), t_incl)
            return r, jnp.broadcast_to(r[:, BLOCK - 1:BLOCK], r.shape)

        qb = (q_ref[0] * scale).astype(BF16)
        dob = do_ref[0].astype(BF16)
        rowpos = i * tq + r_i
        first_diag, n_blocks = _key_ranges(i, tq)

        def step(js, carry, masked):
            dq, rem, cg = carry
            nb = range(len(js))
            offs = [pl.multiple_of(j * BLOCK, BLOCK) for j in js]
            kbs = [k_ref[0, pl.ds(off, BLOCK), :].astype(BF16) for off in offs]
            vbs = [v_ref[0, pl.ds(off, BLOCK), :].astype(BF16) for off in offs]
            zs = [_dot_nt(qb, kb) for kb in kbs]
            dws = [_dot_nt(dob, vb) for vb in vbs]
            sp = [_softplus(z) for z in zs]
            sig = [jnp.exp(zs[b] - sp[b]) for b in nb]
            if masked:
                masks = [((off + c_i) < rowpos) & ((off + c_i) >= N_PAD) for off in offs]
                cs = [prefix(jnp.where(masks[b], sp[b], 0.0)) for b in nb]
            else:
                cs = [prefix(sp[b]) for b in nb]
            wgt, gg = [], []
            for b in nb:
                w_ = jnp.exp(jnp.minimum(zs[b] - sp[b] - (rem - cs[b][0]), 0.0))
                wgt.append(jnp.where(masks[b], w_, 0.0) if masked else w_)
                gg.append(wgt[b] * dws[b])
                rem = rem - cs[b][1]
            gs = [prefix(g_) for g_ in gg]
            dzb = []
            for b in nb:
                dz = gg[b] * (1.0 - sig[b]) - sig[b] * (cg + (gs[b][0] - gg[b]))
                dzb.append((jnp.where(masks[b], dz, 0.0) if masked else dz).astype(BF16))
                cg = cg + gs[b][1]
            for b, off in enumerate(offs):
                dq = dq + _dot(dzb[b], kbs[b])
                dk_ref[0, :, pl.ds(off, BLOCK)] += _dot(qtb, dzb[b])
                dv_ref[0, :, pl.ds(off, BLOCK)] += _dot(dotb, wgt[b].astype(BF16))
            return dq, rem, cg

        carry = (jnp.zeros((tq, hd), F32), -u_ref[0], jnp.zeros((tq, BLOCK), F32))
        carry = _grouped_loop(0, first_diag, lambda js, c: step(js, c, False), carry, sizes=GROUPS)
        dq, _, _ = _grouped_loop(first_diag, n_blocks, lambda js, c: step(js, c, True), carry, sizes=GROUPS)
        dq_ref[0] = dq * scale

    blk = pl.BlockSpec((1, tq, hd), lambda h, i: (h, i, 0))
    full = pl.BlockSpec((1, rows, hd), lambda h, i: (h, 0, 0))
    ublk = pl.BlockSpec((1, tq, 128), lambda h, i: (h, i, 0))
    tblk = pl.BlockSpec((1, hd, tq), lambda h, i: (h, 0, i))
    tfull = pl.BlockSpec((1, hd, rows), lambda h, i: (h, 0, 0))
    sds = jax.ShapeDtypeStruct((nh, rows, hd), F32)
    sds_t = jax.ShapeDtypeStruct((nh, hd, rows), F32)
    return pl.pallas_call(
        body, name=name, grid=(nh, rows // tq), in_specs=[blk, full, full, blk, ublk, tblk, tblk],
        out_specs=[blk, tfull, tfull], out_shape=[sds, sds_t, sds_t], compiler_params=_cparams(2),
    )(q, k, v, do, u_tot, q_t, do_t)


def _mla_fwd(q, k, v_ones, *, name):
    nh, rows, dk = q.shape
    dv = v_ones.shape[2] // 2
    tq = _tile(rows, ATT_TQ, 8)
    scale = (MLA_NOPE + MLA_ROPE) ** -0.5

    def body(q_ref, k_ref, v_ref, o_ref, lse_ref):
        i = pl.program_id(1)
        r_i, c_i = _tile_iotas(tq)
        qb = q_ref[0].astype(BF16)
        rowpos = i * tq + r_i
        first_diag, n_blocks = _key_ranges(i, tq, 1)

        def step(js, carry, masked):
            m, acc = carry
            offs = [pl.multiple_of(j * BLOCK, BLOCK) for j in js]
            ss = [_dot_nt(qb, k_ref[0, pl.ds(off, BLOCK), :].astype(BF16)) * scale for off in offs]
            if masked:
                ss = [jnp.where(((off + c_i) <= rowpos) & ((off + c_i) >= N_PAD), s, NEG_BIG)
                      for off, s in zip(offs, ss)]
            m_new = m
            for s in ss:
                m_new = jnp.maximum(m_new, jnp.max(s, axis=1, keepdims=True))
            ps = [jnp.exp(s - m_new).astype(BF16) for s in ss]
            acc = jnp.exp(m - m_new) * acc
            for off, p in zip(offs, ps):
                acc = acc + _dot(p, v_ref[0, pl.ds(off, BLOCK), :].astype(BF16))
            return m_new, acc

        carry = (jnp.full((tq, 1), NEG_BIG, F32), jnp.zeros((tq, 2 * dv), F32))
        carry = step([0], carry, True)
        carry = _grouped_loop(1, first_diag, lambda js, c: step(js, c, False), carry, sizes=GROUPS_WIDE)
        m, acc = _grouped_loop(first_diag, n_blocks, lambda js, c: step(js, c, True), carry, sizes=GROUPS)
        o_ref[0] = (acc / pltpu.roll(acc, dv, 1))[:, :dv]
        lse_ref[0] = m + jnp.log(jnp.where(c_i >= dv, acc, 1.0))

    qblk = pl.BlockSpec((1, tq, dk), lambda h, i: (h, i, 0))
    kfull = pl.BlockSpec((1, rows, dk), lambda h, i: (h, 0, 0))
    return pl.pallas_call(
        body, name=name, grid=(nh, rows // tq), in_specs=[qblk, kfull, kfull],
        out_specs=[pl.BlockSpec((1, tq, dv), lambda h, i: (h, i, 0)),
                   pl.BlockSpec((1, tq, 128), lambda h, i: (h, i, 0))],
        out_shape=[jax.ShapeDtypeStruct((nh, rows, dv), F32), jax.ShapeDtypeStruct((nh, rows, 128), F32)],
        compiler_params=_cparams(2),
    )(q, k, v_ones)


def _mla_bwd(q, k, v, o, lse, do, q_t, do_t, *, name):
    nh, rows, dk = q.shape
    dv = v.shape[2]
    tq = _tile(rows, ATT_TQ, 8)
    scale = (MLA_NOPE + MLA_ROPE) ** -0.5

    def body(q_ref, k_ref, v_ref, o_ref, lse_ref, do_ref, qt_ref, dot_ref, dq_ref, dk_ref, dv_ref):
        i = pl.program_id(1)

        @pl.when(i == 0)
        def _():
            dk_ref[...] = jnp.zeros_like(dk_ref)
            dv_ref[...] = jnp.zeros_like(dv_ref)

        r_i, c_i = _tile_iotas(tq)
        qtb = qt_ref[0]
        dotb = dot_ref[0]
        qb = q_ref[0].astype(BF16)
        dov = do_ref[0]
        dob = dov.astype(BF16)
        delta = jnp.sum(dov * o_ref[0], axis=1, keepdims=True)
        lse = lse_ref[0][:, BLOCK - 1:BLOCK]
        rowpos = i * tq + r_i
        first_diag, n_blocks = _key_ranges(i, tq, 1)

        def step(js, dq, masked):
            nb = range(len(js))
            offs = [pl.multiple_of(j * BLOCK, BLOCK) for j in js]
            kbs = [k_ref[0, pl.ds(off, BLOCK), :].astype(BF16) for off in offs]
            ss = [_dot_nt(qb, kb) for kb in kbs]
            dps = [_dot_nt(dob, v_ref[0, pl.ds(off, BLOCK), :].astype(BF16)) for off in offs]
            ps = [jnp.exp(jnp.minimum(s * scale - lse, 0.0)) for s in ss]
            if masked:
                ps = [jnp.where(((off + c_i) <= rowpos) & ((off + c_i) >= N_PAD), p, 0.0) for off, p in zip(offs, ps)]
            dss = [(ps[b] * (dps[b] - delta) * scale).astype(BF16) for b in nb]
            for b, off in enumerate(offs):
                dq = dq + _dot(dss[b], kbs[b])
                dk_ref[0, :, pl.ds(off, BLOCK)] += _dot(qtb, dss[b])
                dv_ref[0, :, pl.ds(off, BLOCK)] += _dot(dotb, ps[b].astype(BF16))
            return dq

        dq = step([0], jnp.zeros((tq, dk), F32), True)
        dq = _grouped_loop(1, first_diag, lambda js, c: step(js, c, False), dq, sizes=GROUPS_WIDE)
        dq_ref[0] = _grouped_loop(first_diag, n_blocks, lambda js, c: step(js, c, True), dq, sizes=GROUPS)

    qblk = pl.BlockSpec((1, tq, dk), lambda h, i: (h, i, 0))
    vblk = pl.BlockSpec((1, tq, dv), lambda h, i: (h, i, 0))
    lblk = pl.BlockSpec((1, tq, 128), lambda h, i: (h, i, 0))
    kfull = pl.BlockSpec((1, rows, dk), lambda h, i: (h, 0, 0))
    vfull = pl.BlockSpec((1, rows, dv), lambda h, i: (h, 0, 0))
    tspec = lambda d: pl.BlockSpec((1, d, tq), lambda h, i: (h, 0, i))
    tfull = lambda d: pl.BlockSpec((1, d, rows), lambda h, i: (h, 0, 0))
    return pl.pallas_call(
        body, name=name, grid=(nh, rows // tq),
        in_specs=[qblk, kfull, vfull, vblk, lblk, vblk, tspec(dk), tspec(dv)],
        out_specs=[qblk, tfull(dk), tfull(dv)],
        out_shape=[jax.ShapeDtypeStruct((nh, rows, dk), F32), jax.ShapeDtypeStruct((nh, dk, rows), F32),
                   jax.ShapeDtypeStruct((nh, dv, rows), F32)],
        compiler_params=_cparams(2),
    )(q, k, v, o, lse, do, q_t, do_t)


def _ssd_consts():
    r_i, c_i = _tile_iotas()
    eh = lax.broadcasted_iota(jnp.int32, (BLOCK, SSD_WIDTH), 0)
    ec = lax.broadcasted_iota(jnp.int32, (BLOCK, SSD_WIDTH), 1)
    expand = _ones_where(lax.shift_right_logical(ec, 6) == eh)
    return r_i, c_i, expand


def _ssd_common(pre_v, dtr_v, bias_v, a_v, chunk, r_i, c_i, expand):
    lower = r_i >= c_i
    sig_pre = _sigmoid(pre_v)
    xbc = pre_v * sig_pre
    xs = xbc[:, :SSD_WIDTH]
    valid = (chunk * BLOCK + lax.broadcasted_iota(jnp.int32, (BLOCK, 1), 0)) >= N_PAD
    dt_in = dtr_v + bias_v
    dtv = jnp.where(valid, _softplus(dt_in), 0.0)
    d_a = dtv * a_v
    acs = _hilo_dot_r(_ones_where(lower), d_a)
    acs_t = acs.T
    dt_exp = _hilo_dot_l(dtv, expand)
    acs_exp = _hilo_dot_l(acs, expand)
    a_last = acs_exp[BLOCK - 1:BLOCK, :]
    ea = jnp.exp(acs_exp)
    e_l = jnp.exp(a_last - acs_exp)
    ea_l = jnp.exp(a_last)
    return lower, sig_pre, xbc, xs, valid, dt_in, dtv, acs, acs_t, dt_exp, ea, e_l, ea_l


def _decay(acs, acs_t, h, lower):
    col = acs[:, h:h + 1]
    row = acs_t[h:h + 1, :]
    return jnp.where(lower, jnp.exp(jnp.minimum(col - row, 0.0)), 0.0)


def _ssd_fwd(pre, dtr, bias_row, a_row, d_exp, *, name):
    rows = pre.shape[0]
    nc = rows // BLOCK

    def body(pre_ref, dtr_ref, bias_ref, a_ref, dexp_ref, y_ref, st_ref, state):
        c = pl.program_id(0)

        @pl.when(c == 0)
        def _():
            state[...] = jnp.zeros_like(state)

        r_i, c_i, expand = _ssd_consts()
        lane_lo = c_i < 64
        (lower, _, xbc, xs, _, _, _, acs, acs_t, dt_exp, ea, e_l, ea_l) = _ssd_common(
            pre_ref[...], dtr_ref[...], bias_ref[...], a_ref[...], c, r_i, c_i, expand)
        xin = xs * dt_exp
        for g in range(2):
            bg = xbc[:, 512 + 128 * g:640 + 128 * g]
            cg = xbc[:, 768 + 128 * g:896 + 128 * g]
            bb = bg.astype(BF16)
            cbf = cg.astype(BF16)
            cb = _dot_nt(cbf, bb)
            bt = bg.T.astype(BF16)
            for pp in range(2):
                p = 2 * g + pp
                sl = slice(128 * p, 128 * p + 128)
                xp = xin[:, sl]
                xb = xp.astype(BF16)
                rs = [_dot((cb * _decay(acs, acs_t, 2 * p + hh, lower)).astype(BF16), xb) for hh in range(2)]
                ydiag = jnp.where(lane_lo, rs[0], rs[1])
                s_in = state[p]
                st_ref[0, p] = s_in
                yoff = ea[:, sl] * _dot(cbf, s_in.astype(BF16))
                y_ref[:, sl] = ydiag + yoff + xs[:, sl] * dexp_ref[:, sl]
                state[p] = ea_l[:, sl] * s_in + _dot(bt, (xp * e_l[:, sl]).astype(BF16))

    vec = pl.BlockSpec((1, 128), lambda c: (0, 0))
    return pl.pallas_call(
        body, name=name, grid=(nc,),
        in_specs=[pl.BlockSpec((BLOCK, SSD_XBC), lambda c: (c, 0)),
                  pl.BlockSpec((BLOCK, 128), lambda c: (c, 0)), vec, vec,
                  pl.BlockSpec((1, SSD_WIDTH), lambda c: (0, 0))],
        out_specs=[pl.BlockSpec((BLOCK, SSD_WIDTH), lambda c: (c, 0)),
                   pl.BlockSpec((1, 4, 128, 128), lambda c: (c, 0, 0, 0))],
        out_shape=[jax.ShapeDtypeStruct((rows, SSD_WIDTH), F32), jax.ShapeDtypeStruct((nc, 4, 128, 128), F32)],
        scratch_shapes=[pltpu.VMEM((4, 128, 128), F32)],
        compiler_params=_cparams(1),
    )(pre, dtr, bias_row, a_row, d_exp)


def _ssd_bwd(pre, dtr, bias_row, a_row, d_exp, states, dy, *, name):
    rows = pre.shape[0]
    nc = rows // BLOCK

    def body(pre_ref, dtr_ref, bias_ref, a_ref, dexp_ref, st_ref, dy_ref,
             dpre_ref, ddtr_ref, dbias_ref, da_ref, dd_ref, dstate, q_buf, dx_buf):
        step = pl.program_id(0)
        c = nc - 1 - step

        @pl.when(step == 0)
        def _():
            dstate[...] = jnp.zeros_like(dstate)
            dbias_ref[...] = jnp.zeros_like(dbias_ref)
            da_ref[...] = jnp.zeros_like(da_ref)
            dd_ref[...] = jnp.zeros_like(dd_ref)

        r_i, c_i, expand = _ssd_consts()
        lane_lo = c_i < 64
        last_row = r_i == BLOCK - 1
        pre_v = pre_ref[...]
        (lower, sig_pre, xbc, xs, valid, dt_in, dtv, acs, acs_t, dt_exp, ea, e_l, ea_l) = _ssd_common(
            pre_v, dtr_ref[...], bias_ref[...], a_ref[...], c, r_i, c_i, expand)
        dsilu = sig_pre * (1.0 + pre_v * (1.0 - sig_pre))
        xin = xs * dt_exp
        dyv = dy_ref[...]
        d_acs_diag = jnp.zeros((BLOCK, BLOCK), F32)
        for g in range(2):
            bg = xbc[:, 512 + 128 * g:640 + 128 * g]
            cg = xbc[:, 768 + 128 * g:896 + 128 * g]
            bb = bg.astype(BF16)
            cbf = cg.astype(BF16)
            cb = _dot_nt(cbf, bb)
            ct = cg.T.astype(BF16)
            dcb = jnp.zeros((BLOCK, BLOCK), F32)
            dbg = jnp.zeros((BLOCK, BLOCK), F32)
            dcg = jnp.zeros((BLOCK, BLOCK), F32)
            for pp in range(2):
                p = 2 * g + pp
                sl = slice(128 * p, 128 * p + 128)
                xp = xin[:, sl]
                xb = xp.astype(BF16)
                dyp = dyv[:, sl]
                dyb = dyp.astype(BF16)
                dxs_ = []
                for hh in range(2):
                    dec = _decay(acs, acs_t, 2 * p + hh, lower)
                    wm = cb * dec
                    dxs_.append(_dot(wm.T.astype(BF16), dyb))
                    half = lane_lo if hh == 0 else jnp.logical_not(lane_lo)
                    dwm = _dot_nt(jnp.where(half, dyp, 0.0).astype(BF16), xb)
                    dcb = dcb + dwm * dec
                    dseg = dwm * wm
                    dcol = jnp.sum(dseg, axis=1, keepdims=True) - jnp.sum(dseg.T, axis=1, keepdims=True)
                    d_acs_diag = jnp.where(c_i == 2 * p + hh, dcol, d_acs_diag)
                dxdiag =jnp.where(lane_lo, dxs_[0], dxs_[1])
                s_in = st_ref[0, p]
                sb = s_in.astype(BF16)
                ds_out = dstate[p]
                dsb = ds_out.astype(BF16)
                yoff = ea[:, sl] * _dot(cbf, sb)
                dxst = e_l[:, sl] * _dot(bb, dsb)
                dxp = dxdiag + dxst
                dye = dyp * ea[:, sl]
                dyeb = dye.astype(BF16)
                qp = dyp * yoff - xp * dxst
                lastv = (jnp.sum(xp * dxst, axis=0, keepdims=True)
                         + ea_l[:, sl] * jnp.sum(ds_out * s_in, axis=0, keepdims=True))
                q_buf[:, sl] = jnp.where(last_row, qp + lastv, qp)
                dx_buf[:, sl] = dxp
                dcg = dcg + _dot_nt(dyeb, sb)
                dbg = dbg + _dot_nt((xp * e_l[:, sl]).astype(BF16), dsb)
                dstate[p] = ea_l[:, sl] * ds_out + _dot(ct, dyeb)
            dcg = dcg + _dot(dcb.astype(BF16), bb)
            dbg = dbg + _dot(dcb.T.astype(BF16), cbf)
            bsl = slice(512 + 128 * g, 640 + 128 * g)
            csl = slice(768 + 128 * g, 896 + 128 * g)
            dpre_ref[:, bsl] = dbg * dsilu[:, bsl]
            dpre_ref[:, csl] = dcg * dsilu[:, csl]
        dxall = dx_buf[...]
        dpre_ref[:, :SSD_WIDTH] = (dyv * dexp_ref[...] + dxall * dt_exp) * dsilu[:, :SSD_WIDTH]
        dd_ref[...] += jnp.sum(dyv * xs, axis=0, keepdims=True)
        d_acs = d_acs_diag + _hilo_dot_nt(q_buf[...], expand)
        dd_a = _hilo_dot_r(_ones_where(r_i <= c_i), d_acs)
        ddt = dd_a * a_ref[...] + _hilo_dot_nt(dxall * xs, expand)
        ddt = jnp.where(valid, ddt, 0.0)
        da_ref[...] += jnp.sum(dd_a * dtv, axis=0, keepdims=True)
        ddtr = ddt * _sigmoid(dt_in)
        ddtr_ref[...] = ddtr
        dbias_ref[...] += jnp.sum(ddtr, axis=0, keepdims=True)

    vec = pl.BlockSpec((1, 128), lambda s: (0, 0))
    wide = pl.BlockSpec((1, SSD_WIDTH), lambda s: (0, 0))
    rev = lambda s: (nc - 1 - s, 0)
    return pl.pallas_call(
        body, name=name, grid=(nc,),
        in_specs=[pl.BlockSpec((BLOCK, SSD_XBC), rev), pl.BlockSpec((BLOCK, 128), rev), vec, vec, wide,
                  pl.BlockSpec((1, 4, 128, 128), lambda s: (nc - 1 - s, 0, 0, 0)),
                  pl.BlockSpec((BLOCK, SSD_WIDTH), rev)],
        out_specs=[pl.BlockSpec((BLOCK, SSD_XBC), rev), pl.BlockSpec((BLOCK, 128), rev), vec, vec, wide],
        out_shape=[jax.ShapeDtypeStruct((rows, SSD_XBC), F32), jax.ShapeDtypeStruct((rows, 128), F32),
                   jax.ShapeDtypeStruct((1, 128), F32), jax.ShapeDtypeStruct((1, 128), F32),
                   jax.ShapeDtypeStruct((1, SSD_WIDTH), F32)],
        scratch_shapes=[pltpu.VMEM((4, 128, 128), F32), pltpu.VMEM((BLOCK, SSD_WIDTH), F32),
                        pltpu.VMEM((BLOCK, SSD_WIDTH), F32)],
        compiler_params=_cparams(1),
    )(pre, dtr, bias_row, a_row, d_exp, states, dy)


def _peer(xi, yi, ci, k):
    px = (1 - xi) if (k >> 2) & 1 else xi
    py = (1 - yi) if (k >> 1) & 1 else yi
    pc = (1 - ci) if k & 1 else ci
    return (px, py, pc), 4 * px + 2 * py + pc


def _exchange(xs, *, gather, name):
    n = len(xs)
    n_peers = N_DEV - 1
    out_shape = [jax.ShapeDtypeStruct((N_DEV,) + x.shape if gather else x.shape, x.dtype) for x in xs]

    def body(*refs):
        x_refs, o_refs = refs[:n], refs[n:2 * n]
        send_sems, recv_sems, local_sems = refs[2 * n:]
        xi, yi, ci = lax.axis_index("x"), lax.axis_index("y"), lax.axis_index("c")
        me = 4 * xi + 2 * yi + ci

        def copy(a, k, src_idx, dst_idx, peer):
            src = x_refs[a] if gather else x_refs[a].at[src_idx]
            return pltpu.make_async_remote_copy(
                src_ref=src, dst_ref=o_refs[a].at[dst_idx], send_sem=send_sems.at[a * n_peers + k - 1],
                recv_sem=recv_sems.at[a * n_peers + k - 1], device_id=peer, device_id_type=pl.DeviceIdType.MESH)

        local = [pltpu.make_async_copy(x_refs[a] if gather else x_refs[a].at[me], o_refs[a].at[me], local_sems.at[a])
                 for a in range(n)]
        for cp in local:
            cp.start()
        sends = []
        for k in range(1, N_DEV):
            peer, pidx = _peer(xi, yi, ci, k)
            for a in range(n):
                sends.append(copy(a, k, pidx, me, peer))
                sends[-1].start()
        for k in range(1, N_DEV):
            peer, pidx = _peer(xi, yi, ci, k)
            for a in range(n):
                copy(a, k, pidx, pidx, peer).wait_recv()
        for cp in sends:
            cp.wait_send()
        for cp in local:
            cp.wait()

    hbm = pl.BlockSpec(memory_space=pltpu.HBM)
    return pl.pallas_call(
        body, name=name, out_shape=out_shape, in_specs=[hbm] * n, out_specs=[hbm] * n,
        scratch_shapes=[pltpu.SemaphoreType.DMA((n * n_peers,)), pltpu.SemaphoreType.DMA((n * n_peers,)),
                        pltpu.SemaphoreType.DMA((n,))],
    )(*xs)


def _other_chips(xi, yi):
    return [(1 - xi, yi), (xi, 1 - yi), (1 - xi, 1 - yi)]


def _gather_two_level(xs, *, name):
    n = len(xs)

    def body(*refs):
        x_refs, o_refs = refs[:n], refs[n:2 * n]
        send_sems, recv_sems, local_sems = refs[2 * n:]
        xi, yi, ci = lax.axis_index("x"), lax.axis_index("y"), lax.axis_index("c")
        me, sibling = (xi, yi, ci), (xi, yi, 1 - ci)
        chips = _other_chips(xi, yi)

        def slot(px, py, pc):
            return 4 * px + 2 * py + pc

        def copy(a, k, block, to, from_input=False):
            return pltpu.make_async_remote_copy(
                src_ref=x_refs[a] if from_input else o_refs[a].at[slot(*block)], dst_ref=o_refs[a].at[slot(*block)],
                send_sem=send_sems.at[7 * a + k], recv_sem=recv_sems.at[7 * a + k],
                device_id=to, device_id_type=pl.DeviceIdType.MESH)

        local = [pltpu.make_async_copy(x_refs[a], o_refs[a].at[slot(*me)], local_sems.at[a]) for a in range(n)]
        for cp in local:
            cp.start()
        sends = []
        for a in range(n):
            sends.append(copy(a, 0, me, sibling, from_input=True))
            sends += [copy(a, 1 + j, me, (*chip, ci), from_input=True) for j, chip in enumerate(chips)]
        for cp in sends:
            cp.start()
        for j, chip in enumerate(chips):
            for a in range(n):
                copy(a, 1 + j, (*chip, ci), me).wait_recv()
                sends.append(copy(a, 4 + j, (*chip, ci), sibling))
                sends[-1].start()
        for a in range(n):
            copy(a, 0, sibling, me).wait_recv()
            for j, chip in enumerate(chips):
                copy(a, 4 + j, (*chip, 1 - ci), me).wait_recv()
        for cp in sends:
            cp.wait_send()
        for cp in local:
            cp.wait()

    hbm = pl.BlockSpec(memory_space=pltpu.HBM)
    return pl.pallas_call(
        body, name=name, out_shape=[jax.ShapeDtypeStruct((N_DEV,) + x.shape, x.dtype) for x in xs],
        in_specs=[hbm] * n, out_specs=[hbm] * n,
        scratch_shapes=[pltpu.SemaphoreType.DMA((7 * n,)), pltpu.SemaphoreType.DMA((7 * n,)),
                        pltpu.SemaphoreType.DMA((n,))],
    )(*xs)


def _sibling_swap(xs, *, name):
    n = len(xs)

    def body(*refs):
        x_refs, o_refs = refs[:n], refs[n:2 * n]
        send_sems, recv_sems = refs[2 * n:]
        xi, yi, ci = lax.axis_index("x"), lax.axis_index("y"), lax.axis_index("c")
        copies = [pltpu.make_async_remote_copy(
            src_ref=x_refs[a].at[1 - ci], dst_ref=o_refs[a], send_sem=send_sems.at[a], recv_sem=recv_sems.at[a],
            device_id=(xi, yi, 1 - ci), device_id_type=pl.DeviceIdType.MESH) for a in range(n)]
        for cp in copies:
            cp.start()
        for cp in copies:
            cp.wait()

    hbm = pl.BlockSpec(memory_space=pltpu.HBM)
    return pl.pallas_call(
        body, name=name, out_shape=[jax.ShapeDtypeStruct(x.shape[1:], x.dtype) for x in xs],
        in_specs=[hbm] * n, out_specs=[hbm] * n,
        scratch_shapes=[pltpu.SemaphoreType.DMA((n,)), pltpu.SemaphoreType.DMA((n,))],
    )(*xs)


def _chip_all_to_all(xs, *, name):
    n = len(xs)

    def body(*refs):
        x_refs, o_refs = refs[:n], refs[n:2 * n]
        send_sems, recv_sems, local_sems = refs[2 * n:]
        xi, yi, ci = lax.axis_index("x"), lax.axis_index("y"), lax.axis_index("c")
        mine = 2 * xi + yi
        chips = _other_chips(xi, yi)

        def copy(a, j, src_slot, dst_slot, chip):
            return pltpu.make_async_remote_copy(
                src_ref=x_refs[a].at[src_slot], dst_ref=o_refs[a].at[dst_slot], send_sem=send_sems.at[3 * a + j],
                recv_sem=recv_sems.at[3 * a + j], device_id=(*chip, ci), device_id_type=pl.DeviceIdType.MESH)

        local = [pltpu.make_async_copy(x_refs[a].at[mine], o_refs[a].at[mine], local_sems.at[a]) for a in range(n)]
        for cp in local:
            cp.start()
        sends = [copy(a, j, 2 * chip[0] + chip[1], mine, chip) for j, chip in enumerate(chips) for a in range(n)]
        for cp in sends:
            cp.start()
        for j, chip in enumerate(chips):
            for a in range(n):
                copy(a, j, mine, 2 * chip[0] + chip[1], chip).wait_recv()
        for cp in sends:
            cp.wait_send()
        for cp in local:
            cp.wait()

    hbm = pl.BlockSpec(memory_space=pltpu.HBM)
    return pl.pallas_call(
        body, name=name, out_shape=[jax.ShapeDtypeStruct(x.shape, x.dtype) for x in xs],
        in_specs=[hbm] * n, out_specs=[hbm] * n,
        scratch_shapes=[pltpu.SemaphoreType.DMA((3 * n,)), pltpu.SemaphoreType.DMA((3 * n,)),
                        pltpu.SemaphoreType.DMA((n,))],
    )(*xs)


def _pair_add(a, b, *, name):
    rows, cols = a.shape
    lanes = -(-cols // 128) * 128
    tr = _tile(rows, max(16, (512 * 1024) // lanes), 16)

    def body(a_ref, b_ref, o_ref):
        o_ref[...] = (a_ref[...].astype(F32) + b_ref[...].astype(F32)).astype(BF16)

    spec = pl.BlockSpec((tr, cols), lambda i: (i, 0))
    return pl.pallas_call(
        body, name=name, grid=(rows // tr,), in_specs=[spec, spec], out_specs=spec,
        out_shape=jax.ShapeDtypeStruct((rows, cols), BF16), compiler_params=_cparams(1),
    )(a, b)


def _adamw(gs, w, m, v, *, name):
    n_slots = gs.shape[0]
    rows, cols = w.shape
    lanes = -(-cols // 128) * 128
    tr = _tile(rows, max(16, (128 * 1024) // lanes), 16 if gs.dtype == BF16 else 8)

    def body(g_ref, w_ref, m_ref, v_ref, go_ref, d_ref, mo_ref, vo_ref):
        g = g_ref[0].astype(F32)
        for j in range(1, n_slots):
            g = g + g_ref[j].astype(F32)
        m2 = ADAM_B1 * m_ref[...] + (1.0 - ADAM_B1) * g
        v2 = ADAM_B2 * v_ref[...] + (1.0 - ADAM_B2) * (g * g)
        m_hat = m2 / (1.0 - ADAM_B1 ** ADAM_STEP)
        v_hat = v2 / (1.0 - ADAM_B2 ** ADAM_STEP)
        go_ref[...] = g
        d_ref[...] = -ADAM_LR * (m_hat / (jnp.sqrt(v_hat) + ADAM_EPS) + ADAM_WD * w_ref[...])
        mo_ref[...] = m2
        vo_ref[...] = v2

    spec = pl.BlockSpec((tr, cols), lambda i: (i, 0))
    sds = jax.ShapeDtypeStruct((rows, cols), F32)
    return pl.pallas_call(
        body, name=name, grid=(rows // tr,),
        in_specs=[pl.BlockSpec((n_slots, tr, cols), lambda i: (0, i, 0)), spec, spec, spec],
        out_specs=[spec, spec, spec, spec], out_shape=[sds, sds, sds, sds], compiler_params=_cparams(1),
    )(gs, w, m, v)


SHARDED = (("meta_tokens", 1), ("w_in", 2), ("ssd_conv_w", 2), ("mla_w_uq", 2), ("mla_w_ukv", 2),
           ("w_out", 1), ("ffn_w_up", 2), ("ffn_conv_w", 2), ("ffn_w_down", 1))
BIG = ("w_in", "w_out", "ffn_w_up", "ffn_w_down")
SMALL = ("meta_tokens", "ssd_conv_w", "mla_w_uq", "mla_w_ukv", "ffn_conv_w")
REPLICATED = ("norm_mix_g", "ssd_conv_b", "ssd_dt_bias", "ssd_a_log", "ssd_d", "ssd_norm_g", "sb_norm_g",
              "mla_q_norm_g", "mla_kv_norm_g", "mla_norm_g", "norm_ffn_g", "ffn_conv_b", "final_norm_g")
WEIGHTS = ("meta_tokens", "norm_mix_g", "w_in", "ssd_conv_w", "ssd_conv_b", "ssd_dt_bias", "ssd_a_log", "ssd_d",
           "ssd_norm_g", "sb_norm_g", "mla_q_norm_g", "mla_kv_norm_g", "mla_w_uq", "mla_w_ukv", "mla_norm_g",
           "w_out", "norm_ffn_g", "ffn_w_up", "ffn_conv_w", "ffn_conv_b", "ffn_w_down", "final_norm_g")


def _flat_pack(arrays, dtype, align):
    flat = jnp.concatenate([a.reshape(-1).astype(dtype) for a in arrays])
    pad = (-flat.shape[0]) % align
    return jnp.pad(flat, (0, pad)).reshape(-1, 128)


def _pieces(full, axis):
    shp = full.shape
    t = full.reshape(shp[:axis] + (N_DEV, shp[axis] // N_DEV) + shp[axis + 1:])
    return jnp.moveaxis(t, axis, 0).reshape(N_DEV, -1)


def _unpieces(p8, shard_shape, axis):
    t = p8.reshape((N_DEV,) + shard_shape)
    t = jnp.moveaxis(t, 0, axis)
    return t.reshape(shard_shape[:axis] + (N_DEV * shard_shape[axis],) + shard_shape[axis + 1:])


def _split_core_chip(full, axis):
    shp = full.shape
    t = full.reshape(shp[:axis] + (4, 2, shp[axis] // N_DEV) + shp[axis + 1:])
    return jnp.moveaxis(t, (axis + 1, axis), (0, 1))


def _merge_blocks(b8, axis):
    shard = b8.shape[1:]
    t = jnp.moveaxis(b8, 0, axis)
    return t.reshape(shard[:axis] + (N_DEV * shard[axis],) + shard[axis + 1:])


def _gather_weights(shards):
    axes = dict(SHARDED)
    got = _gather_two_level([shards[n].astype(BF16) for n in BIG], name="gather_big")
    full = {n: _merge_blocks(b8, axes[n]) for n, b8 in zip(BIG, got)}
    packed = _flat_pack([shards[n] for n in SMALL], F32, 8 * 128)
    got = _exchange([packed], gather=True, name="gather_small")[0].reshape(N_DEV, -1)
    off = 0
    for n in SMALL:
        size = math.prod(shards[n].shape)
        full[n] = _unpieces(got[:, off:off + size], shards[n].shape, axes[n])
        off += size
    return full


def _pad_cols(a, width):
    return jnp.pad(a, ((0, 0), (0, width - a.shape[1])))


def _w_in_padded(w):
    kr = w[:, 2632:2664]
    return jnp.concatenate([
        w[:, 0:512], w[:, 512:1536], w[:, 1544:2312], _pad_cols(w[:, 2312:2504], 256), w[:, 2504:2632],
        _pad_cols(kr[:, :16], 64), _pad_cols(kr[:, 16:], 64), _pad_cols(w[:, 1536:1544], 128),
        jnp.zeros((w.shape[0], 128), w.dtype)], axis=1)


def _w_in_unpadded(wp):
    return jnp.concatenate([
        wp[:, 0:512], wp[:, 512:1536], wp[:, OFF_DT:OFF_DT + 8], wp[:, 1536:2304], wp[:, OFF_QA:OFF_QA + 192],
        wp[:, OFF_CKV:OFF_CKV + 128], wp[:, OFF_KR:OFF_KR + 16], wp[:, OFF_KR + 64:OFF_KR + 80]], axis=1)


def _w_uq_perm(w):
    t = w.reshape(MLA_Q_RANK, MLA_HEADS, MLA_NOPE + MLA_ROPE)
    out = jnp.concatenate([t[:, :, :64].reshape(MLA_Q_RANK, 256), t[:, :, 64:80].reshape(MLA_Q_RANK, 64),
                           t[:, :, 80:96].reshape(MLA_Q_RANK, 64)], axis=1)
    return jnp.pad(out, ((0, 256 - MLA_Q_RANK), (0, 0)))


def _w_uq_unperm(wp):
    wp = wp[:MLA_Q_RANK]
    t = jnp.concatenate([wp[:, :256].reshape(MLA_Q_RANK, 4, 64), wp[:, 256:320].reshape(MLA_Q_RANK, 4, 16),
                         wp[:, 320:384].reshape(MLA_Q_RANK, 4, 16)], axis=2)
    return t.reshape(MLA_Q_RANK, 4 * 96)


def _w_ukv_perm(w):
    t = w.reshape(MLA_KV_RANK, MLA_HEADS, 128)
    return jnp.concatenate([t[:, :, :64].reshape(MLA_KV_RANK, 256), t[:, :, 64:].reshape(MLA_KV_RANK, 256)], axis=1)


def _w_ukv_unperm(wp):
    t = jnp.concatenate([wp[:, :256].reshape(MLA_KV_RANK, 4, 64), wp[:, 256:].reshape(MLA_KV_RANK, 4, 64)], axis=2)
    return t.reshape(MLA_KV_RANK, 512)


def _heads(a, hd):
    return jnp.moveaxis(a.reshape(a.shape[0], -1, hd), 1, 0)


def _unheads(a):
    return jnp.moveaxis(a, 0, 1).reshape(a.shape[1], -1)


def _row(v, width=None):
    v = v.reshape(1, -1)
    return v if width is None else _pad_cols(v, width)


def _rope_tables(rows):
    pos = jnp.arange(rows, dtype=F32) - float(N_PAD)
    inv = 1.0 / (ROPE_BASE ** (jnp.arange(0, MLA_ROPE, 2, dtype=F32) / MLA_ROPE))
    ang = pos[:, None] * inv[None, :]
    cos = jnp.tile(jnp.cos(ang), (1, 8))
    sin = jnp.tile(jnp.sin(ang), (1, 4))
    return cos, jnp.concatenate([-sin, sin], axis=1)


def _layer_fwd(h, p, cos_t, sin_t, tag):
    s = {"h_in": h}
    hn = _rmsnorm_fwd(h, p["norm_mix_g"], width=D_MODEL, name=tag + "norm_mix")
    u = _matmul(hn, p["w_in"], name=tag + "in_proj")
    s["hn"], s["u"] = hn, u

    xbc_in = u[:, OFF_XBC:OFF_XBC + SSD_XBC]
    pre = _dwconv_fwd(xbc_in, p["ssd_conv_w"], p["ssd_conv_b"], taps=SSD_CONV, name=tag + "ssd_conv")
    dtr = u[:, OFF_DT:OFF_DT + 128]
    y_ssd, states = _ssd_fwd(pre, dtr, p["dt_bias"], p["a_row"], p["d_exp"], name=tag + "ssd_core")
    zgate = u[:, OFF_Z:OFF_Z + SSD_WIDTH]
    yn_ssd = _rmsnorm_fwd(y_ssd, p["ssd_norm_g"], width=SSD_WIDTH, z=zgate, name=tag + "ssd_norm")
    s.update(xbc_in=xbc_in, pre=pre, dtr=dtr, y_ssd=y_ssd, states=states, zgate=zgate)

    q_sb = _heads(u[:, OFF_QSB:OFF_QSB + SB_WIDTH], SB_HEAD_DIM)
    k_sb = _heads(u[:, OFF_KSB:OFF_KSB + SB_WIDTH], SB_HEAD_DIM).astype(BF16)
    v_sb = _heads(u[:, OFF_VSB:OFF_VSB + SB_WIDTH], SB_HEAD_DIM).astype(BF16)
    o_sb, u_tot = _sb_fwd(q_sb, k_sb, v_sb, name=tag + "sb_attn")
    o_sb_flat = _unheads(o_sb)
    yn_sb = _rmsnorm_fwd(o_sb_flat, p["sb_norm_g"], width=SB_WIDTH, name=tag + "sb_norm")
    s.update(q_sb=q_sb, k_sb=k_sb, v_sb=v_sb, u_tot=u_tot, o_sb_flat=o_sb_flat)

    qa = u[:, OFF_QA:OFF_QA + 256]
    ckv = u[:, OFF_CKV:OFF_CKV + 128]
    qa_n = _rmsnorm_fwd(qa, p["mla_q_norm_g"], width=MLA_Q_RANK, name=tag + "mla_qnorm")
    ckv_n = _rmsnorm_fwd(ckv, p["mla_kv_norm_g"], width=MLA_KV_RANK, name=tag + "mla_kvnorm")
    qf = _matmul(qa_n, p["mla_w_uq"], name=tag + "mla_uq")
    kvf = _matmul(ckv_n, p["mla_w_ukv"], name=tag + "mla_ukv")
    q_rope = _rope(qf[:, 256:384], cos_t, sin_t, name=tag + "rope_q")
    k_rope = _rope(u[:, OFF_KR:OFF_KR + 128], cos_t, sin_t, name=tag + "rope_k")
    rows = h.shape[0]
    zpad = jnp.zeros((MLA_HEADS, rows, 32), F32)
    qh = jnp.concatenate([_heads(qf[:, :256], 64), _heads(q_rope[:, :64], 16), _heads(q_rope[:, 64:], 16), zpad], axis=2)
    kr_b = jnp.broadcast_to(jnp.concatenate([k_rope[:, 0:16], k_rope[:, 64:80]], axis=1)[None], (MLA_HEADS, rows, 32))
    kh = jnp.concatenate([_heads(kvf[:, :256], 64), kr_b, zpad], axis=2).astype(BF16)
    vh = _heads(kvf[:, 256:], 64).astype(BF16)
    v_ones = jnp.concatenate([vh, jnp.ones_like(vh)], axis=2)
    o_mla, lse = _mla_fwd(qh, kh, v_ones, name=tag + "mla_attn")
    o_mla_flat = _unheads(o_mla)
    yn_mla = _rmsnorm_fwd(o_mla_flat, p["mla_norm_g"], width=256, name=tag + "mla_norm")
    s.update(qa=qa, ckv=ckv, qa_n=qa_n, ckv_n=ckv_n, qh=qh, kh=kh, vh=vh, o_mla=o_mla, lse=lse,
             o_mla_flat=o_mla_flat)

    mix = jnp.concatenate([yn_ssd, yn_sb, yn_mla], axis=1)
    h_mid = _matmul(mix, p["w_out"], res=h, mask_pad=True, name=tag + "out_proj")
    hn2 = _rmsnorm_fwd(h_mid, p["norm_ffn_g"], width=D_MODEL, name=tag + "norm_ffn")
    up = _matmul(hn2, p["ffn_w_up"], tn=1408, name=tag + "ffn_up")
    act = _ffn_conv_gate_fwd(up, p["ffn_conv_w"], p["ffn_conv_b"], taps=FFN_CONV, name=tag + "ffn_conv_gate")
    h_out = _matmul(act, p["ffn_w_down"], res=h_mid, mask_pad=True, tk=1408, name=tag + "ffn_down")
    s.update(mix=mix, h_mid=h_mid, hn2=hn2, up=up, act=act)
    return h_out, s


def _layer_bwd(dh_out, p, s, cos_t, sin_t, tag):
    g = {}
    rows = dh_out.shape[0]
    dact = _matmul(dh_out, p["ffn_w_down"], tb=True, tn=1408, name=tag + "b_down_dx")
    g["ffn_w_down"] = _matmul(s["act"], dh_out, ta=True, tm=1408, tk=640, name=tag + "b_down_dw")
    dup1, dup2, dcw1, dcw2, dcb1, dcb2 = _ffn_conv_gate_bwd(
        dact, s["up"], p["ffn_conv_w"], p["ffn_conv_b"], taps=FFN_CONV, name=tag + "b_ffn_conv_gate")
    g["ffn_conv_w"] = jnp.concatenate([dcw1[:FFN_CONV], dcw2[:FFN_CONV]], axis=1)
    g["ffn_conv_b"] = jnp.concatenate([dcb1[0], dcb2[0]])
    w_up1, w_up2 = p["ffn_w_up"][:, :D_FF], p["ffn_w_up"][:, D_FF:]
    dhn2 = _matmul(dup1, w_up1, tb=True, tk=1408, name=tag + "b_up_dx1")
    dhn2 = _matmul(dup2, w_up2, tb=True, tk=1408, res=dhn2, name=tag + "b_up_dx2")
    g["ffn_w_up"] = jnp.concatenate([_matmul(s["hn2"], dup1, ta=True, tn=1408, tk=640, name=tag + "b_up_dw1"),
                                     _matmul(s["hn2"], dup2, ta=True, tn=1408, tk=640, name=tag + "b_up_dw2")], axis=1)
    dh_mid, _, dg = _rmsnorm_bwd(s["h_mid"], p["norm_ffn_g"], dhn2, width=D_MODEL, res=dh_out, mask_pad=True,
                                 name=tag + "b_norm_ffn")
    g["norm_ffn_g"] = dg[0]

    dmix = _matmul(dh_mid, p["w_out"], tb=True, name=tag + "b_out_dx")
    g["w_out"] = _matmul(s["mix"], dh_mid, ta=True, tk=640, name=tag + "b_out_dw")

    dy_ssd, dz, dg = _rmsnorm_bwd(s["y_ssd"], p["ssd_norm_g"], dmix[:, :SSD_WIDTH], width=SSD_WIDTH, z=s["zgate"],
                                  name=tag + "b_ssd_norm")
    g["ssd_norm_g"] = dg[0]
    dpre, ddtr, dbias, da, dd = _ssd_bwd(s["pre"], s["dtr"], p["dt_bias"], p["a_row"], p["d_exp"], s["states"],
                                         dy_ssd, name=tag + "b_ssd_core")
    g["ssd_dt_bias"] = dbias[0, :8]
    g["ssd_a_log"] = da[0, :8] * p["a_row"][0, :8]
    g["ssd_d"] = dd.reshape(8, 64).sum(axis=1)
    dxbc_in, dcw, dcb_ = _dwconv_bwd(dpre, s["xbc_in"], p["ssd_conv_w"], taps=SSD_CONV, name=tag + "b_ssd_conv")
    g["ssd_conv_w"], g["ssd_conv_b"] = dcw[:SSD_CONV], dcb_[0]

    do_sb_flat, _, dg = _rmsnorm_bwd(s["o_sb_flat"], p["sb_norm_g"], dmix[:, 512:768], width=SB_WIDTH,
                                     name=tag + "b_sb_norm")
    g["sb_norm_g"] = dg[0]
    do_sb = _heads(do_sb_flat, SB_HEAD_DIM)
    q_t = jnp.swapaxes(s["q_sb"] * (SB_HEAD_DIM ** -0.5), 1, 2).astype(BF16)
    dq_sb, dk_t, dv_t = _sb_bwd(s["q_sb"], s["k_sb"], s["v_sb"], do_sb, s["u_tot"], q_t,
                                jnp.swapaxes(do_sb, 1, 2).astype(BF16), name=tag + "b_sb_attn")
    dk_sb, dv_sb = jnp.swapaxes(dk_t, 1, 2), jnp.swapaxes(dv_t, 1, 2)

    do_mla_flat, _, dg = _rmsnorm_bwd(s["o_mla_flat"], p["mla_norm_g"], dmix[:, 768:1024], width=256,
                                      name=tag + "b_mla_norm")
    g["mla_norm_g"] = dg[0]
    do_mla = _heads(do_mla_flat, 64)
    dqh, dk_t, dv_t = _mla_bwd(s["qh"], s["kh"], s["vh"], s["o_mla"], s["lse"], do_mla,
                               jnp.swapaxes(s["qh"], 1, 2).astype(BF16), jnp.swapaxes(do_mla, 1, 2).astype(BF16),
                               name=tag + "b_mla_attn")
    dkh, dvh = jnp.swapaxes(dk_t, 1, 2), jnp.swapaxes(dv_t, 1, 2)
    dq_rope_in = jnp.concatenate([_unheads(dqh[:, :, 64:80]), _unheads(dqh[:, :, 80:96])], axis=1)
    dq_r = _rope(dq_rope_in, cos_t, sin_t, transpose=True, name=tag + "b_rope_q")
    dqf = jnp.concatenate([_unheads(dqh[:, :, :64]), dq_r], axis=1)
    dkr_sum = jnp.sum(dkh[:, :, 64:96], axis=0)
    dk_rope_in = jnp.concatenate([_pad_cols(dkr_sum[:, :16], 64), _pad_cols(dkr_sum[:, 16:], 64)], axis=1)
    dkr = _rope(dk_rope_in, cos_t, sin_t, transpose=True, name=tag + "b_rope_k")
    dkvf = jnp.concatenate([_unheads(dkh[:, :, :64]), _unheads(dvh)], axis=1)
    dqa_n = _matmul(dqf, p["mla_w_uq"], tb=True, name=tag + "b_uq_dx")
    g["mla_w_uq"] = _matmul(s["qa_n"], dqf, ta=True, tk=640, name=tag + "b_uq_dw")
    dckv_n = _matmul(dkvf, p["mla_w_ukv"], tb=True, name=tag + "b_ukv_dx")
    g["mla_w_ukv"] = _matmul(s["ckv_n"], dkvf, ta=True, tk=640, name=tag + "b_ukv_dw")
    dqa, _, dg = _rmsnorm_bwd(s["qa"], p["mla_q_norm_g"], dqa_n, width=MLA_Q_RANK, name=tag + "b_mla_qnorm")
    g["mla_q_norm_g"] = dg[0, :MLA_Q_RANK]
    dckv, _, dg = _rmsnorm_bwd(s["ckv"], p["mla_kv_norm_g"], dckv_n, width=MLA_KV_RANK, name=tag + "b_mla_kvnorm")
    g["mla_kv_norm_g"] = dg[0]

    du = jnp.concatenate([dz, dxbc_in, _unheads(dq_sb), _unheads(dk_sb), _unheads(dv_sb), dqa, dckv, dkr, ddtr,
                          jnp.zeros((rows, 128), F32)], axis=1)
    dhn = _matmul(du, p["w_in"], tb=True, name=tag + "b_in_dx")
    g["w_in"] = _matmul(s["hn"], du, ta=True, tk=640, name=tag + "b_in_dw")
    dh_in, _, dg = _rmsnorm_bwd(s["h_in"], p["norm_mix_g"], dhn, width=D_MODEL, res=dh_mid, mask_pad=True,
                                name=tag + "b_norm_mix")
    g["norm_mix_g"] = dg[0]
    return dh_in, g


def _prepare_layer(full, rep, l):
    a_row = _row(-jnp.exp(rep["ssd_a_log"][l]), 128)
    return {
        "norm_mix_g": _row(rep["norm_mix_g"][l]),
        "w_in": _w_in_padded(full["w_in"][l]),
        "ssd_conv_w": jnp.pad(full["ssd_conv_w"][l], ((0, HALO - SSD_CONV), (0, 0))),
        "ssd_conv_b": _row(rep["ssd_conv_b"][l]),
        "dt_bias": _row(rep["ssd_dt_bias"][l], 128),
        "a_row": a_row,
        "d_exp": _row(jnp.repeat(rep["ssd_d"][l], 64)),
        "ssd_norm_g": _row(rep["ssd_norm_g"][l]),
        "sb_norm_g": _row(rep["sb_norm_g"][l]),
        "mla_q_norm_g": _row(rep["mla_q_norm_g"][l], 256),
        "mla_kv_norm_g": _row(rep["mla_kv_norm_g"][l]),
        "mla_w_uq": _w_uq_perm(full["mla_w_uq"][l]),
        "mla_w_ukv": _w_ukv_perm(full["mla_w_ukv"][l]),
        "mla_norm_g": _row(rep["mla_norm_g"][l]),
        "w_out": full["w_out"][l],
        "norm_ffn_g": _row(rep["norm_ffn_g"][l]),
        "ffn_w_up": full["ffn_w_up"][l],
        "ffn_conv_w": jnp.pad(full["ffn_conv_w"][l], ((0, HALO - FFN_CONV), (0, 0))),
        "ffn_conv_b": _row(rep["ffn_conv_b"][l]),
        "ffn_w_down": full["ffn_w_down"][l],
    }


def _layer_grads_to_full(g):
    out = dict(g)
    out["w_in"] = _w_in_unpadded(g["w_in"])
    out["mla_w_uq"] = _w_uq_unperm(g["mla_w_uq"])
    out["mla_w_ukv"] = _w_ukv_unperm(g["mla_w_ukv"])
    return out


def kernel(x, meta_tokens, norm_mix_g, w_in, ssd_conv_w, ssd_conv_b, ssd_dt_bias, ssd_a_log, ssd_d, ssd_norm_g, sb_norm_g, mla_q_norm_g, mla_kv_norm_g, mla_w_uq, mla_w_ukv, mla_norm_g, w_out, norm_ffn_g, ffn_w_up, ffn_conv_w, ffn_conv_b, ffn_w_down, final_norm_g, loss_target, m_meta_tokens, m_norm_mix_g, m_w_in, m_ssd_conv_w, m_ssd_conv_b, m_ssd_dt_bias, m_ssd_a_log, m_ssd_d, m_ssd_norm_g, m_sb_norm_g, m_mla_q_norm_g, m_mla_kv_norm_g, m_mla_w_uq, m_mla_w_ukv, m_mla_norm_g, m_w_out, m_norm_ffn_g, m_ffn_w_up, m_ffn_conv_w, m_ffn_conv_b, m_ffn_w_down, m_final_norm_g, v_meta_tokens, v_norm_mix_g, v_w_in, v_ssd_conv_w, v_ssd_conv_b, v_ssd_dt_bias, v_ssd_a_log, v_ssd_d, v_ssd_norm_g, v_sb_norm_g, v_mla_q_norm_g, v_mla_kv_norm_g, v_mla_w_uq, v_mla_w_ukv, v_mla_norm_g, v_w_out, v_norm_ffn_g, v_ffn_w_up, v_ffn_conv_w, v_ffn_conv_b, v_ffn_w_down, v_final_norm_g):
    w = dict(meta_tokens=meta_tokens, norm_mix_g=norm_mix_g, w_in=w_in, ssd_conv_w=ssd_conv_w, ssd_conv_b=ssd_conv_b,
             ssd_dt_bias=ssd_dt_bias, ssd_a_log=ssd_a_log, ssd_d=ssd_d, ssd_norm_g=ssd_norm_g, sb_norm_g=sb_norm_g,
             mla_q_norm_g=mla_q_norm_g, mla_kv_norm_g=mla_kv_norm_g, mla_w_uq=mla_w_uq, mla_w_ukv=mla_w_ukv,
             mla_norm_g=mla_norm_g, w_out=w_out, norm_ffn_g=norm_ffn_g, ffn_w_up=ffn_w_up, ffn_conv_w=ffn_conv_w,
             ffn_conv_b=ffn_conv_b, ffn_w_down=ffn_w_down, final_norm_g=final_norm_g)
    mom = dict(meta_tokens=m_meta_tokens, norm_mix_g=m_norm_mix_g, w_in=m_w_in, ssd_conv_w=m_ssd_conv_w,
               ssd_conv_b=m_ssd_conv_b, ssd_dt_bias=m_ssd_dt_bias, ssd_a_log=m_ssd_a_log, ssd_d=m_ssd_d,
               ssd_norm_g=m_ssd_norm_g, sb_norm_g=m_sb_norm_g, mla_q_norm_g=m_mla_q_norm_g,
               mla_kv_norm_g=m_mla_kv_norm_g, mla_w_uq=m_mla_w_uq, mla_w_ukv=m_mla_w_ukv, mla_norm_g=m_mla_norm_g,
               w_out=m_w_out, norm_ffn_g=m_norm_ffn_g, ffn_w_up=m_ffn_w_up, ffn_conv_w=m_ffn_conv_w,
               ffn_conv_b=m_ffn_conv_b, ffn_w_down=m_ffn_w_down, final_norm_g=m_final_norm_g)
    vel = dict(meta_tokens=v_meta_tokens, norm_mix_g=v_norm_mix_g, w_in=v_w_in, ssd_conv_w=v_ssd_conv_w,
               ssd_conv_b=v_ssd_conv_b, ssd_dt_bias=v_ssd_dt_bias, ssd_a_log=v_ssd_a_log, ssd_d=v_ssd_d,
               ssd_norm_g=v_ssd_norm_g, sb_norm_g=v_sb_norm_g, mla_q_norm_g=v_mla_q_norm_g,
               mla_kv_norm_g=v_mla_kv_norm_g, mla_w_uq=v_mla_w_uq, mla_w_ukv=v_mla_w_ukv, mla_norm_g=v_mla_norm_g,
               w_out=v_w_out, norm_ffn_g=v_norm_ffn_g, ffn_w_up=v_ffn_w_up, ffn_conv_w=v_ffn_conv_w,
               ffn_conv_b=v_ffn_conv_b, ffn_w_down=v_ffn_w_down, final_norm_g=v_final_norm_g)

    full = _gather_weights({n: w[n] for n, _ in SHARDED})
    layers = [_prepare_layer(full, w, l) for l in range(DEPTH)]

    seq = x.shape[1]
    rows = BLOCK + seq
    cos_t, sin_t = _rope_tables(rows)
    h = jnp.concatenate([jnp.zeros((N_PAD, D_MODEL), F32), full["meta_tokens"], x[0]], axis=0)

    saved = []
    for l in range(DEPTH):
        h, s = _layer_fwd(h, layers[l], cos_t, sin_t, "l%d_" % l)
        saved.append(s)
    dh, dg_final, loss_part = _final_loss(h, _row(final_norm_g), loss_target[0], name="final_loss")
    loss = lax.psum(loss_part[0, 0], ("x", "y", "c"))

    layer_grads = [None] * DEPTH
    for l in reversed(range(DEPTH)):
        dh, g = _layer_bwd(dh, layers[l], saved[l], cos_t, sin_t, "l%d_" % l)
        layer_grads[l] = _layer_grads_to_full(g)
    grad_x = dh[BLOCK:][None]

    partial = {n: jnp.stack([layer_grads[l][n] for l in range(DEPTH)]) for n in layer_grads[0]}
    partial["meta_tokens"] = dh[N_PAD:BLOCK]
    partial["final_norm_g"] = dg_final[0]

    results = [dict(), dict(), dict(), dict()]
    axes = dict(SHARDED)

    core = lax.axis_index("c")
    halves = [_split_core_chip(partial[n], axes[n]).astype(BF16) for n in BIG]
    theirs = _sibling_swap(halves, name="grad_sibling_swap")
    chip_sums = []
    for n, h2, t4 in zip(BIG, halves, theirs):
        view = (4 * math.prod(w[n].shape[:-1]), w[n].shape[-1])
        mine = lax.dynamic_index_in_dim(h2, core, 0, keepdims=False)
        chip_sums.append(_pair_add(mine.reshape(view), t4.reshape(view), name="grad_pair_add_" + n).reshape(t4.shape))
    got_big = _chip_all_to_all(chip_sums, name="grad_chip_all_to_all")
    for n, g4 in zip(BIG, got_big):
        shp = w[n].shape
        view = (math.prod(shp[:-1]), shp[-1])
        outs = _adamw(g4.reshape((4,) + view), w[n].reshape(view), mom[n].reshape(view), vel[n].reshape(view),
                      name="adamw_" + n)
        for kind in range(4):
            results[kind][n] = outs[kind].reshape(shp)

    send = jnp.concatenate([_pieces(partial[n], axes[n]) for n in SMALL], axis=1)
    pad = (-send.shape[1]) % (8 * 128)
    send = jnp.pad(send, ((0, 0), (0, pad))).reshape(N_DEV, -1, 128)
    got = _exchange([send], gather=False, name="grad_all_to_all_small")[0]
    pack = lambda d: _flat_pack([d[n] for n in SMALL], F32, 8 * 128)
    sh_out = _adamw(got, pack(w), pack(mom), pack(vel), name="adamw_small")

    rep_g = _flat_pack([partial[n] for n in REPLICATED], F32, 8 * 128)
    got_r = _exchange([rep_g], gather=True, name="grad_all_gather")[0]
    packr = lambda d: _flat_pack([d[n] for n in REPLICATED], F32, 8 * 128)
    rep_out = _adamw(got_r, packr(w), packr(mom), packr(vel), name="adamw_replicated")

    for names, outs in ((list(SMALL), sh_out), (list(REPLICATED), rep_out)):
        off = 0
        for n in names:
            size = math.prod(w[n].shape)
            for kind in range(4):
                results[kind][n] = outs[kind].reshape(-1)[off:off + size].reshape(w[n].shape)
            off += size

    return (loss, grad_x, *[results[0][n] for n in WEIGHTS], *[results[1][n] for n in WEIGHTS],
            *[results[2][n] for n in WEIGHTS], *[results[3][n] for n in WEIGHTS])
```
